```python
import math
import jax
import jax.numpy as jnp
from jax import lax
import numpy as np

D_MODEL = 1024
BATCH = 8
SEQ = 4096
DEPTH = 2

GRID_W = 64
CTX_LEN = 256
ATT_HEADS = 8
ATT_KV_HEADS = 2
ATT_GROUP = ATT_HEADS // ATT_KV_HEADS
HEAD_DIM = 64
WINDOW = 128
ATT_BLOCK = 128
ATT_SCALE = 1.0 / math.sqrt(HEAD_DIM)
ROPE_BASE = 10000.0
AXIS_DIM = HEAD_DIM // 2
RWKV_HEADS = 8
RWKV_HEAD = 64
RWKV_W = RWKV_HEADS * RWKV_HEAD
DECAY_RANK = 32
ICLR_RANK = 32
VRES_RANK = 32
GATE_RANK = 96
LNX_EPS = 64e-5
CONV_W = 512
CONV_K = 31
CONV_LN_EPS = 1e-5
N_BRANCH = 3
D_FF = 2816
N_EXPERTS = 8
TOP_K = 2
D_FF_EXPERT = 3584
N_DENSE = (DEPTH + 1) // 2
N_MOE = DEPTH // 2
NORM_EPS = 1e-6
Q_W = ATT_HEADS * HEAD_DIM
KV_W = ATT_KV_HEADS * HEAD_DIM
OFF_Q = 0
OFF_K = OFF_Q + Q_W
OFF_V = OFF_K + KV_W
OFF_R = OFF_V + KV_W
OFF_CONV = OFF_R + 3 * RWKV_W + 2 * DECAY_RANK + 2 * ICLR_RANK + GATE_RANK
OFF_GATE = OFF_CONV + 2 * CONV_W
IN_COLS = OFF_GATE + N_BRANCH * D_MODEL
RWKV_COLS = OFF_CONV - OFF_R

kernel_name = 'hybrid_diffusion_gqa_rwkv7_conformer_moe'


def rms_norm(x, g):
    x32 = x.astype(jnp.float32)
    y = x32 * lax.rsqrt(jnp.mean(x32 * x32, axis=-1, keepdims=True) + NORM_EPS)
    return (y * g.astype(jnp.float32)).astype(x.dtype)


def axial_rope_tables(rows):
    row = jnp.broadcast_to(jnp.arange(rows)[:, None], (rows, GRID_W)).reshape(-1).astype(jnp.float32)
    col = jnp.broadcast_to(jnp.arange(GRID_W)[None, :], (rows, GRID_W)).reshape(-1).astype(jnp.float32)
    inv = ROPE_BASE ** (-jnp.arange(0, AXIS_DIM, 2, dtype=jnp.float32) / AXIS_DIM)
    ang_r = row[:, None] * inv[None, :]
    ang_c = col[:, None] * inv[None, :]
    return (jnp.cos(ang_r), jnp.sin(ang_r), jnp.cos(ang_c), jnp.sin(ang_c))


def _rotate(x, cos, sin):
    x1, x2 = jnp.split(x, 2, axis=-1)
    return jnp.concatenate([x1 * cos - x2 * sin, x2 * cos + x1 * sin], axis=-1)


def apply_axial_rope(x, rope):
    cr, sr, cc, sc = (t[None, :, None, :].astype(x.dtype) for t in rope)
    return jnp.concatenate([_rotate(x[..., :AXIS_DIM], cr, sr), _rotate(x[..., AXIS_DIM:], cc, sc)], axis=-1)


def windowed_attention(q, k, v, kc, vc, sink):
    B_, T = q.shape[:2]
    L = kc.shape[1]
    nb = T // ATT_BLOCK
    qb = q.reshape(B_, nb, ATT_BLOCK, ATT_KV_HEADS, ATT_GROUP, HEAD_DIM)

    def band(t):
        tp = jnp.pad(t, ((0, 0), (ATT_BLOCK, ATT_BLOCK), (0, 0), (0, 0)))
        tp = tp.reshape(B_, nb + 2, ATT_BLOCK, ATT_KV_HEADS, HEAD_DIM)
        return jnp.concatenate([tp[:, :-2], tp[:, 1:-1], tp[:, 2:]], axis=2)

    kw, vw = band(k), band(v)
    sink32 = sink.astype(jnp.float32)

    def block(args):
        j, qj, kj, vj = args
        qpos = j * ATT_BLOCK + jnp.arange(ATT_BLOCK)
        kpos = (j - 1) * ATT_BLOCK + jnp.arange(3 * ATT_BLOCK)
        mask = (jnp.abs(qpos[:, None] - kpos[None, :]) <= WINDOW) & (kpos >= 0)[None, :] & (kpos < T)[None, :]
        s_loc = jnp.einsum('bqhgd,bkhd->bhgqk', qj, kj, preferred_element_type=jnp.float32) * ATT_SCALE
        s_loc = jnp.where(mask, s_loc, -jnp.inf)
        s_ctx = jnp.einsum('bqhgd,bchd->bhgqc', qj, kc, preferred_element_type=jnp.float32) * ATT_SCALE
        s_snk = jnp.broadcast_to(sink32[None, :, :, None, None], s_ctx.shape[:-1] + (1,))
        p = jax.nn.softmax(jnp.concatenate([s_snk, s_ctx, s_loc], axis=-1), axis=-1).astype(vj.dtype)
        return (jnp.einsum('bhgqc,bchd->bqhgd', p[..., 1:1 + L], vc)
                + jnp.einsum('bhgqk,bkhd->bqhgd', p[..., 1 + L:], vj))

    out = lax.map(block, (jnp.arange(nb), jnp.moveaxis(qb, 1, 0), jnp.moveaxis(kw, 1, 0), jnp.moveaxis(vw, 1, 0)))
    return jnp.moveaxis(out, 0, 1).reshape(B_, T, Q_W)


def context_attention(qc, kc, vc, sink):
    B_, L = qc.shape[:2]
    s = jnp.einsum('bqhgd,bkhd->bhgqk', qc, kc, preferred_element_type=jnp.float32) * ATT_SCALE
    s_snk = jnp.broadcast_to(sink.astype(jnp.float32)[None, :, :, None, None], s.shape[:-1] + (1,))
    p = jax.nn.softmax(jnp.concatenate([s_snk, s], axis=-1), axis=-1)[..., 1:].astype(vc.dtype)
    return jnp.einsum('bhgqk,bkhd->bqhgd', p, vc).reshape(B_, L, Q_W)


def token_shift_lerp(z, mu):
    zp = jnp.pad(z, ((0, 0), (1, 1), (0, 0)))
    return z + mu * (0.5 * (zp[:, :-2] + zp[:, 2:]) - z)


def rwkv_inputs(zr, lp, v_first):
    B_, T, _ = zr.shape
    f32 = jnp.float32

    def heads(t):
        return t.reshape(t.shape[:-1] + (RWKV_HEADS, RWKV_HEAD))

    r = zr[..., :RWKV_W]
    k = zr[..., RWKV_W:2 * RWKV_W]
    v = zr[..., 2 * RWKV_W:3 * RWKV_W]
    o = 3 * RWKV_W
    zw = zr[..., o:o + 2 * DECAY_RANK].reshape(B_, T, 2, DECAY_RANK)
    o += 2 * DECAY_RANK
    za = zr[..., o:o + 2 * ICLR_RANK].reshape(B_, T, 2, ICLR_RANK)
    o += 2 * ICLR_RANK
    zg = zr[..., o:o + GATE_RANK]
    w_log = -jax.nn.softplus(-(lp['w0'] + jnp.einsum('btdr,drc->btdc', jnp.tanh(zw), lp['w2']))) - 0.5
    decay = jnp.exp(-jnp.exp(w_log.astype(f32)))
    a = jax.nn.sigmoid(lp['a0'] + jnp.einsum('btdr,drc->btdc', za, lp['a2']))
    v_in = v
    if v_first is not None:
        v = v + (v_first - v) * jax.nn.sigmoid(lp['v0'] + (v @ lp['v1']) @ lp['v2'])
    kk = heads(k * lp['k_k']).astype(f32)
    kk = kk / jnp.maximum(jnp.linalg.norm(kk, axis=-1, keepdims=True), 1e-12)
    k_dir = k[:, :, None, :] * (1.0 + (a - 1.0) * lp['k_a'])
    return {'r': heads(r).astype(f32), 'k': heads(k_dir).astype(f32), 'v': heads(v).astype(f32),
            'decay': heads(decay), 'a': heads(a).astype(f32), 'kk': kk, 'zg': zg, 'v_in': v_in}


def wkv_scan(S0, inp, d, reverse, emit):
    seq = [inp['decay'][:, :, d], inp['k'][:, :, d], inp['v'], inp['kk'], inp['a'][:, :, d]]
    if emit:
        seq.append(inp['r'])
    xs = tuple(jnp.swapaxes(t, 0, 1) for t in seq)

    def step(S, xt):
        w_t, k_t, v_t, kk_t, a_t = xt[:5]
        sa = jnp.einsum('bhvk,bhk->bhv', S, -kk_t)
        S = S * w_t[:, :, None, :] + sa[..., None] * (kk_t * a_t)[:, :, None, :] + v_t[..., None] * k_t[:, :, None, :]
        y = jnp.einsum('bhvk,bhk->bhv', S, xt[5]) if emit else None
        return S, y

    S, ys = lax.scan(step, S0, xs, reverse=reverse)
    return S, (jnp.swapaxes(ys, 0, 1) if emit else None)


def bidirectional_wkv(in_l, in_c, need_ctx):
    B_ = in_l['v'].shape[0]
    S0 = jnp.zeros((B_, RWKV_HEADS, RWKV_HEAD, RWKV_HEAD), jnp.float32)
    y_l, y_c = None, None
    for d in range(2):
        rev = d == 1
        s_ctx, yc = wkv_scan(S0, in_c, d, rev, need_ctx)
        _, yl = wkv_scan(s_ctx, in_l, d, rev, True)
        y_l = yl if y_l is None else y_l + yl
        if need_ctx:
            y_c = yc if y_c is None else y_c + yc
    return y_l, y_c


def rwkv_out(y, inp, lp, dtype):
    B_, T = y.shape[:2]
    m = jnp.mean(y, axis=-1, keepdims=True)
    var = jnp.mean(jnp.square(y - m), axis=-1, keepdims=True)
    yn = ((y - m) * lax.rsqrt(var + LNX_EPS)).reshape(B_, T, RWKV_W) * lp['lnx_g'] + lp['lnx_b']
    bonus = jnp.sum(inp['r'][:, :, None] * inp['k'] * lp['r_k'], axis=-1, keepdims=True).sum(axis=2)
    yn = yn + (bonus * inp['v']).reshape(B_, T, RWKV_W)
    g = jax.nn.sigmoid(inp['zg']) @ lp['g2']
    return (yn * g).astype(dtype)


def conv_module(zc, lp):
    u = zc[..., :CONV_W] * jax.nn.sigmoid(zc[..., CONV_W:])
    u = lax.conv_general_dilated(u, lp['dw_w'][:, None, :], window_strides=(1,),
                                 padding=((CONV_K // 2, CONV_K // 2),),
                                 dimension_numbers=('NWC', 'WIO', 'NWC'),
                                 feature_group_count=CONV_W) + lp['dw_b']
    u32 = u.astype(jnp.float32)
    m = jnp.mean(u32, axis=-1, keepdims=True)
    var = jnp.mean(jnp.square(u32 - m), axis=-1, keepdims=True)
    u32 = (u32 - m) * lax.rsqrt(var + CONV_LN_EPS) * lp['cln_g'] + lp['cln_b']
    return jax.nn.silu(u32).astype(zc.dtype)


def merge_branches(zg, att, rw, cv):
    gates = jax.nn.sigmoid(zg.reshape(zg.shape[:-1] + (N_BRANCH, D_MODEL)))
    return gates[..., 0, :] * att + gates[..., 1, :] * rw + gates[..., 2, :] * cv


def token_mixer(hl, hc, lp, rope, v_first, need_ctx):
    B_, T, _ = hl.shape
    L = hc.shape[1]
    zl = hl @ lp['w_in']
    base = 0 if need_ctx else OFF_K
    zc = hc @ (lp['w_in'] if need_ctx else lp['w_in'][:, OFF_K:OFF_CONV])

    def cz(lo, hi):
        return zc[..., lo - base:hi - base]

    sink = lp['sink'].reshape(ATT_KV_HEADS, ATT_GROUP)
    q = apply_axial_rope(zl[..., OFF_Q:OFF_K].reshape(B_, T, ATT_HEADS, HEAD_DIM), rope)
    q = q.reshape(B_, T, ATT_KV_HEADS, ATT_GROUP, HEAD_DIM)
    k = apply_axial_rope(zl[..., OFF_K:OFF_V].reshape(B_, T, ATT_KV_HEADS, HEAD_DIM), rope)
    v = zl[..., OFF_V:OFF_R].reshape(B_, T, ATT_KV_HEADS, HEAD_DIM)
    kc = cz(OFF_K, OFF_V).reshape(B_, L, ATT_KV_HEADS, HEAD_DIM)
    vc = cz(OFF_V, OFF_R).reshape(B_, L, ATT_KV_HEADS, HEAD_DIM)
    att_l = windowed_attention(q, k, v, kc, vc, sink) @ lp['w_attn_o']
    vf_l = None if v_first is None else v_first[0]
    vf_c = None if v_first is None else v_first[1]
    in_l = rwkv_inputs(token_shift_lerp(zl[..., OFF_R:OFF_CONV], lp['mu']), lp, vf_l)
    in_c = rwkv_inputs(token_shift_lerp(cz(OFF_R, OFF_CONV), lp['mu']), lp, vf_c)
    y_l, y_c = bidirectional_wkv(in_l, in_c, need_ctx)
    rw_l = rwkv_out(y_l, in_l, lp, hl.dtype) @ lp['w_rwkv_o']
    cv_l = conv_module(zl[..., OFF_CONV:OFF_GATE], lp) @ lp['w_conv_o']
    out_l = merge_branches(zl[..., OFF_GATE:], att_l, rw_l, cv_l) @ lp['w_out']
    out_c = None
    if need_ctx:
        qc = cz(OFF_Q, OFF_K).reshape(B_, L, ATT_KV_HEADS, ATT_GROUP, HEAD_DIM)
        att_c = context_attention(qc, kc, vc, sink) @ lp['w_attn_o']
        rw_c = rwkv_out(y_c, in_c, lp, hc.dtype) @ lp['w_rwkv_o']
        cv_c = conv_module(cz(OFF_CONV, OFF_GATE), lp) @ lp['w_conv_o']
        out_c = merge_branches(cz(OFF_GATE, IN_COLS), att_c, rw_c, cv_c) @ lp['w_out']
    return out_l, out_c, (in_l['v_in'], in_c['v_in'])


def swiglu(t, wg, wu, wd):
    return (jax.nn.silu(t @ wg) * (t @ wu)) @ wd


def moe_swiglu(t, router, wg, wu, wd):
    logits = jnp.matmul(t, router, preferred_element_type=jnp.float32)
    top_v, top_i = lax.top_k(logits, TOP_K)
    w = jax.nn.softmax(top_v, axis=-1)
    comb = jnp.sum(jax.nn.one_hot(top_i, N_EXPERTS, dtype=jnp.float32) * w[..., None], axis=1).astype(t.dtype)
    out = comb[:, 0:1] * swiglu(t, wg[0], wu[0], wd[0])
    for e in range(1, N_EXPERTS):
        out = out + comb[:, e:e + 1] * swiglu(t, wg[e], wu[e], wd[e])
    return out


def setup_inputs(seed: int = 0) -> dict:
    key = jax.random.key(seed)
    ks = iter(jax.random.split(key, 40))
    f32 = jnp.float32

    def nrm(shape, scale):
        return jax.random.normal(next(ks), shape, f32) * scale

    def uni(shape, lo, hi):
        return jax.random.uniform(next(ks), shape, f32, lo, hi)

    D, C, NV = D_MODEL, RWKV_W, DEPTH - 1
    return {
        'x': nrm((BATCH, SEQ, D), 1.0),
        'c': nrm((BATCH, D), 1.0),
        'ctx': nrm((BATCH, CTX_LEN, D), 1.0),
        'c_ctx': nrm((D,), 1.0),
        'mod_w': nrm((DEPTH, D, 6 * D), D ** -0.5),
        'mod_b': nrm((DEPTH, 6 * D), 0.02),
        'norm_g': 1.0 + nrm((DEPTH, 4, D), 0.05),
        'w_in': nrm((DEPTH, D, IN_COLS), D ** -0.5),
        'attn_sink': nrm((DEPTH, ATT_HEADS), 0.5),
        'rwkv_mu': uni((DEPTH, RWKV_COLS), 0.0, 1.0),
        'rwkv_w0': uni((DEPTH, 2, C), -6.0, -1.0),
        'rwkv_w2': nrm((DEPTH, 2, DECAY_RANK, C), 0.1),
        'rwkv_a0': nrm((DEPTH, 2, C), 0.1),
        'rwkv_a2': nrm((DEPTH, 2, ICLR_RANK, C), 0.1),
        'rwkv_g2': nrm((DEPTH, GATE_RANK, C), GATE_RANK ** -0.5),
        'rwkv_k_k': 0.85 + nrm((DEPTH, C), 0.05),
        'rwkv_k_a': 1.0 + nrm((DEPTH, C), 0.05),
        'rwkv_r_k': nrm((DEPTH, RWKV_HEADS, RWKV_HEAD), 0.1),
        'rwkv_lnx_g': 1.0 + nrm((DEPTH, C), 0.05),
        'rwkv_lnx_b': nrm((DEPTH, C), 0.02),
        'vres_v0': 1.0 + nrm((NV, C), 0.1),
        'vres_w1': nrm((NV, C, VRES_RANK), C ** -0.5),
        'vres_w2': nrm((NV, VRES_RANK, C), 0.1),
        'conv_dw_w': nrm((DEPTH, CONV_K, CONV_W), CONV_K ** -0.5),
        'conv_dw_b': nrm((DEPTH, CONV_W), 0.02),
        'conv_ln_g': 1.0 + nrm((DEPTH, CONV_W), 0.05),
        'conv_ln_b': nrm((DEPTH, CONV_W), 0.02),
        'w_attn_o': nrm((DEPTH, Q_W, D), Q_W ** -0.5),
        'w_rwkv_o': nrm((DEPTH, C, D), C ** -0.5),
        'w_conv_o': nrm((DEPTH, CONV_W, D), CONV_W ** -0.5),
        'w_out': nrm((DEPTH, D, D), D ** -0.5),
        'ffn_wg': nrm((N_DENSE, D, D_FF), D ** -0.5),
        'ffn_wu': nrm((N_DENSE, D, D_FF), D ** -0.5),
        'ffn_wd': nrm((N_DENSE, D_FF, D), D_FF ** -0.5),
        'moe_router': nrm((N_MOE, D, N_EXPERTS), D ** -0.5),
        'moe_wg': nrm((N_MOE, N_EXPERTS, D, D_FF_EXPERT), D ** -0.5),
        'moe_wu': nrm((N_MOE, N_EXPERTS, D, D_FF_EXPERT), D ** -0.5),
        'moe_wd': nrm((N_MOE, N_EXPERTS, D_FF_EXPERT, D), D_FF_EXPERT ** -0.5),
    }


def reference(x, c, ctx, c_ctx, mod_w, mod_b, norm_g, w_in, attn_sink,
              rwkv_mu, rwkv_w0, rwkv_w2, rwkv_a0, rwkv_a2, rwkv_g2, rwkv_k_k, rwkv_k_a, rwkv_r_k,
              rwkv_lnx_g, rwkv_lnx_b, vres_v0, vres_w1, vres_w2,
              conv_dw_w, conv_dw_b, conv_ln_g, conv_ln_b,
              w_attn_o, w_rwkv_o, w_conv_o, w_out,
              ffn_wg, ffn_wu, ffn_wd, moe_router, moe_wg, moe_wu, moe_wd):
    B_, T, D = x.shape
    ROWS = T // GRID_W
    rope = axial_rope_tables(ROWS)
    xl, xc = x, ctx
    v_first = None
    for i in range(DEPTH):
        need_ctx = i < DEPTH - 1
        mod_l = (jax.nn.silu(c) @ mod_w[i] + mod_b[i])[:, None, :]
        mod_c = jax.nn.silu(c_ctx) @ mod_w[i] + mod_b[i]
        sh1_l, sc1_l, gt1_l, sh2_l, sc2_l, gt2_l = jnp.split(mod_l, 6, axis=-1)
        sh1_c, sc1_c, gt1_c, sh2_c, sc2_c, gt2_c = jnp.split(mod_c, 6, axis=-1)
        lp = {'w_in': w_in[i], 'sink': attn_sink[i], 'mu': rwkv_mu[i],
              'w0': rwkv_w0[i], 'w2': rwkv_w2[i], 'a0': rwkv_a0[i], 'a2': rwkv_a2[i], 'g2': rwkv_g2[i],
              'k_k': rwkv_k_k[i], 'k_a': rwkv_k_a[i], 'r_k': rwkv_r_k[i],
              'lnx_g': rwkv_lnx_g[i], 'lnx_b': rwkv_lnx_b[i],
              'dw_w': conv_dw_w[i], 'dw_b': conv_dw_b[i], 'cln_g': conv_ln_g[i], 'cln_b': conv_ln_b[i],
              'w_attn_o': w_attn_o[i], 'w_rwkv_o': w_rwkv_o[i], 'w_conv_o': w_conv_o[i], 'w_out': w_out[i]}
        if i > 0:
            lp['v0'] = vres_v0[i - 1]
            lp['v1'] = vres_w1[i - 1]
            lp['v2'] = vres_w2[i - 1]
        hl = rms_norm(xl, norm_g[i, 0]) * (1.0 + sc1_l) + sh1_l
        hc = rms_norm(xc, norm_g[i, 0]) * (1.0 + sc1_c) + sh1_c
        ml, mc, vf = token_mixer(hl, hc, lp, rope, v_first, need_ctx)
        if i == 0:
            v_first = vf
        xl = xl + gt1_l * rms_norm(ml, norm_g[i, 1])
        if need_ctx:
            xc = xc + gt1_c * rms_norm(mc, norm_g[i, 1])
        hl = rms_norm(xl, norm_g[i, 2]) * (1.0 + sc2_l) + sh2_l
        tokens = hl.reshape(-1, D)
        if need_ctx:
            hc = rms_norm(xc, norm_g[i, 2]) * (1.0 + sc2_c) + sh2_c
            tokens = jnp.concatenate([tokens, hc.reshape(-1, D)], axis=0)
        if i % 2 == 0:
            j = i // 2
            f = swiglu(tokens, ffn_wg[j], ffn_wu[j], ffn_wd[j])
        else:
            j = i // 2
            f = moe_swiglu(tokens, moe_router[j], moe_wg[j], moe_wu[j], moe_wd[j])
        n_lat = B_ * T
        xl = xl + gt2_l * rms_norm(f[:n_lat].reshape(xl.shape), norm_g[i, 3])
        if need_ctx:
            xc = xc + gt2_c * rms_norm(f[n_lat:].reshape(xc.shape), norm_g[i, 3])
    return xl
```

```python
import functools
import math

import jax
import jax.numpy as jnp
from jax import lax
from jax.experimental import pallas as pl
from jax.experimental.pallas import tpu as pltpu

F32 = jnp.float32
BF16 = jnp.bfloat16

D_MODEL = 1024
GRID_W = 64
ATT_HEADS = 8
ATT_KV_HEADS = 2
ATT_GROUP = ATT_HEADS // ATT_KV_HEADS
HEAD_DIM = 64
WINDOW = 128
ATT_BLOCK = 128
ATT_SCALE = 1.0 / math.sqrt(HEAD_DIM)
ROPE_BASE = 10000.0
AXIS_DIM = HEAD_DIM // 2
RWKV_HEADS = 8
RWKV_HEAD = 64
RWKV_W = RWKV_HEADS * RWKV_HEAD
DECAY_RANK = 32
ICLR_RANK = 32
GATE_RANK = 96
LNX_EPS = 64e-5
CONV_W = 512
CONV_K = 31
CONV_LN_EPS = 1e-5
N_BRANCH = 3
N_EXPERTS = 8
TOP_K = 2
NORM_EPS = 1e-6
Q_W = ATT_HEADS * HEAD_DIM
KV_W = ATT_KV_HEADS * HEAD_DIM
OFF_Q = 0
OFF_K = OFF_Q + Q_W
OFF_V = OFF_K + KV_W
OFF_R = OFF_V + KV_W
OFF_CONV = OFF_R + 3 * RWKV_W + 2 * DECAY_RANK + 2 * ICLR_RANK + GATE_RANK
OFF_GATE = OFF_CONV + 2 * CONV_W
IN_COLS = OFF_GATE + N_BRANCH * D_MODEL

ZQ = 0
ZQP = 512
ZK = 1024
ZKP = 1152
ZV = 1280
ZWA = 1408
ZCA = 1536
ZCB = 2048
ZR = 2560
ZRK = 3072
ZRV = 3584
ZG = 4096
ZZG = 7168
NZ = 7680

VMEM_LIMIT = 56 * 1024 * 1024


def _cparams(sem):
    return pltpu.CompilerParams(dimension_semantics=sem, vmem_limit_bytes=VMEM_LIMIT)


def _pick_tile(n, candidates):
    for c in candidates:
        if n % c == 0:
            return c
    return n


def _matmul_bias_kernel(a_ref, w_ref, b_ref, o_ref):
    o_ref[...] = jnp.dot(a_ref[...], w_ref[...], preferred_element_type=F32,
                         precision=lax.Precision.HIGHEST) + b_ref[...]


def matmul_bias_f32(a, w, b):
    m, k = a.shape
    n = w.shape[1]
    tn = _pick_tile(n, (1536, 1024, 512, 256, 128))
    return pl.pallas_call(
        _matmul_bias_kernel,
        grid=(n // tn,),
        in_specs=[pl.BlockSpec((m, k), lambda j: (0, 0)),
                  pl.BlockSpec((k, tn), lambda j: (0, j)),
                  pl.BlockSpec((1, tn), lambda j: (0, j))],
        out_specs=pl.BlockSpec((m, tn), lambda j: (0, j)),
        out_shape=jax.ShapeDtypeStruct((m, n), F32),
        compiler_params=_cparams(("parallel",)),
        name="mod_matmul",
    )(a, w, b.reshape(1, n))


def _norm_mod(x, g, sc, sh):
    ms = jnp.mean(x * x, axis=-1, keepdims=True)
    return (x * lax.rsqrt(ms + NORM_EPS) * g) * (1.0 + sc) + sh


def _nm_matmul_kernel(x_ref, g_ref, sc_ref, sh_ref, w_ref, o_ref, h_ref):
    @pl.when(pl.program_id(1) == 0)
    def _():
        h_ref[...] = _norm_mod(x_ref[...], g_ref[...], sc_ref[0], sh_ref[0]).astype(BF16)

    o_ref[...] = jnp.dot(h_ref[...], w_ref[...], preferred_element_type=F32).astype(o_ref.dtype)


def nm_matmul(x, g, sc, sh, w, rows_per_group):
    m, d = x.shape
    n = w.shape[1]
    tm = _pick_tile(rows_per_group, (512, 256, 128, 64, 32, 16, 8))
    tn = _pick_tile(n, (1280, 1024, 512, 256, 128))
    grp = lambda i, j: ((i * tm) // rows_per_group, 0, 0)
    return pl.pallas_call(
        _nm_matmul_kernel,
        grid=(m // tm, n // tn),
        in_specs=[pl.BlockSpec((tm, d), lambda i, j: (i, 0)),
                  pl.BlockSpec((1, d), lambda i, j: (0, 0)),
                  pl.BlockSpec((1, 1, d), grp),
                  pl.BlockSpec((1, 1, d), grp),
                  pl.BlockSpec((d, tn), lambda i, j: (0, j))],
        out_specs=pl.BlockSpec((tm, tn), lambda i, j: (i, j)),
        out_shape=jax.ShapeDtypeStruct((m, n), F32),
        scratch_shapes=[pltpu.VMEM((tm, d), BF16)],
        compiler_params=_cparams(("parallel", "arbitrary")),
        name="in_proj",
    )(x, g.reshape(1, d), sc, sh, w)


def _prenorm_kernel(x_ref, g_ref, sc_ref, sh_ref, h_ref):
    h_ref[...] = _norm_mod(x_ref[...], g_ref[...], sc_ref[0], sh_ref[0]).astype(h_ref.dtype)


def _prenorm_router_kernel(x_ref, g_ref, sc_ref, sh_ref, r_ref, h_ref, l_ref):
    h = _norm_mod(x_ref[...], g_ref[...], sc_ref[0], sh_ref[0])
    h_ref[...] = h.astype(h_ref.dtype)
    l_ref[...] = jnp.dot(h, r_ref[...], preferred_element_type=F32, precision=lax.Precision.HIGHEST)


def prenorm(x, g, sc, sh, rows_per_group, router=None):
    m, d = x.shape
    tm = _pick_tile(rows_per_group, (512, 256, 128, 64, 32, 16, 8))
    grp = lambda i: ((i * tm) // rows_per_group, 0, 0)
    in_specs = [pl.BlockSpec((tm, d), lambda i: (i, 0)),
                pl.BlockSpec((1, d), lambda i: (0, 0)),
                pl.BlockSpec((1, 1, d), grp),
                pl.BlockSpec((1, 1, d), grp)]
    if router is None:
        return pl.pallas_call(
            _prenorm_kernel, grid=(m // tm,), in_specs=in_specs,
            out_specs=pl.BlockSpec((tm, d), lambda i: (i, 0)),
            out_shape=jax.ShapeDtypeStruct((m, d), BF16),
            compiler_params=_cparams(("parallel",)), name="prenorm",
        )(x, g.reshape(1, d), sc, sh)
    ne = router.shape[1]
    rpad = jnp.zeros((d, 128), F32).at[:, :ne].set(router)
    h, logits = pl.pallas_call(
        _prenorm_router_kernel, grid=(m // tm,),
        in_specs=in_specs + [pl.BlockSpec((d, 128), lambda i: (0, 0))],
        out_specs=[pl.BlockSpec((tm, d), lambda i: (i, 0)), pl.BlockSpec((tm, 128), lambda i: (i, 0))],
        out_shape=[jax.ShapeDtypeStruct((m, d), BF16), jax.ShapeDtypeStruct((m, 128), F32)],
        compiler_params=_cparams(("parallel",)), name="prenorm_router",
    )(x, g.reshape(1, d), sc, sh, rpad)
    return h, logits[:, :ne]


def _resnorm_kernel(x_ref, f_ref, g_ref, gt_ref, o_ref):
    f = f_ref[...]
    ms = jnp.mean(f * f, axis=-1, keepdims=True)
    o_ref[...] = x_ref[...] + gt_ref[0] * (f * lax.rsqrt(ms + NORM_EPS) * g_ref[...])


def resnorm(x, f, g, gt, rows_per_group):
    m, d = x.shape
    tm = _pick_tile(rows_per_group, (512, 256, 128, 64, 32, 16, 8))
    grp = lambda i: ((i * tm) // rows_per_group, 0, 0)
    return pl.pallas_call(
        _resnorm_kernel, grid=(m // tm,),
        in_specs=[pl.BlockSpec((tm, d), lambda i: (i, 0)),
                  pl.BlockSpec((tm, d), lambda i: (i, 0)),
                  pl.BlockSpec((1, d), lambda i: (0, 0)),
                  pl.BlockSpec((1, 1, d), grp)],
        out_specs=pl.BlockSpec((tm, d), lambda i: (i, 0)),
        out_shape=jax.ShapeDtypeStruct((m, d), F32),
        compiler_params=_cparams(("parallel",)), name="resnorm",
    )(x, f, g.reshape(1, d), gt)


def _softmax_pv(q4, sink_col, parts):
    scores = []
    m = sink_col
    for k, _, bias in parts:
        s = lax.dot_general(q4, k, (((1,), (1,)), ((), ())), preferred_element_type=F32) * ATT_SCALE
        if bias is not None:
            s = s + bias
        scores.append(s)
        m = jnp.maximum(m, jnp.max(s, axis=-1, keepdims=True))
    den = jnp.exp(sink_col - m)
    acc = None
    for s, (_, v, _) in zip(scores, parts):
        p = jnp.exp(s - m)
        den = den + jnp.sum(p, axis=-1, keepdims=True)
        pv = jnp.dot(p.astype(BF16), v, preferred_element_type=F32)
        acc = pv if acc is None else acc + pv
    return acc / den


def _attn_heads(q, parts_fn, sink_ref, o_ref):
    bq = q.shape[0]
    outs = []
    for h in range(ATT_KV_HEADS):
        q4 = jnp.concatenate(
            [q[:, (h * ATT_GROUP + g) * HEAD_DIM:(h * ATT_GROUP + g + 1) * HEAD_DIM] for g in range(ATT_GROUP)],
            axis=0).astype(BF16)
        sink_col = jnp.concatenate(
            [jnp.full((bq, 1), sink_ref[h * ATT_GROUP + g], F32) for g in range(ATT_GROUP)], axis=0)
        o4 = _softmax_pv(q4, sink_col, parts_fn(h))
        outs.extend(o4[g * bq:(g + 1) * bq] for g in range(ATT_GROUP))
    o_ref[0] = jnp.concatenate(outs, axis=-1)


def _hslice(t, h):
    return t[:, h * HEAD_DIM:(h + 1) * HEAD_DIM]


def _win_attn_kernel(sink_ref, q_ref, qp_ref, cq_ref, sq_ref,
                     k0_ref, k1_ref, k2_ref, kp0_ref, kp1_ref, kp2_ref,
                     v0_ref, v1_ref, v2_ref, c0_ref, c1_ref, c2_ref, s0_ref, s1_ref, s2_ref,
                     kc_ref, vc_ref, o_ref, *, nb):
    j = pl.program_id(1)
    q = q_ref[0] * cq_ref[...] + qp_ref[0] * sq_ref[...]
    kb = jnp.concatenate([k0_ref[0] * c0_ref[...] + kp0_ref[0] * s0_ref[...],
                          k1_ref[0] * c1_ref[...] + kp1_ref[0] * s1_ref[...],
                          k2_ref[0] * c2_ref[...] + kp2_ref[0] * s2_ref[...]], axis=0).astype(BF16)
    vb = jnp.concatenate([v0_ref[0], v1_ref[0], v2_ref[0]], axis=0).astype(BF16)
    kc = kc_ref[0].astype(BF16)
    vc = vc_ref[0].astype(BF16)
    rows = ATT_GROUP * ATT_BLOCK
    qi = lax.broadcasted_iota(jnp.int32, (rows, 3 * ATT_BLOCK), 0) % ATT_BLOCK + ATT_BLOCK
    ki = lax.broadcasted_iota(jnp.int32, (rows, 3 * ATT_BLOCK), 1)
    lo = jnp.where(j == 0, ATT_BLOCK, 0)
    hi = jnp.where(j == nb - 1, 2 * ATT_BLOCK, 3 * ATT_BLOCK)
    far = jnp.where(ki < lo, 4 * ATT_BLOCK, 0) + jnp.where(ki >= hi, 4 * ATT_BLOCK, 0)
    bias = jnp.where(jnp.abs(qi - ki) + far <= WINDOW, 0.0, -jnp.inf).astype(F32)

    def parts(h):
        return [(_hslice(kc, h), _hslice(vc, h), None), (_hslice(kb, h), _hslice(vb, h), bias)]

    _attn_heads(q, parts, sink_ref, o_ref)


def _ctx_attn_kernel(sink_ref, q_ref, kc_ref, vc_ref, o_ref):
    kc = kc_ref[0].astype(BF16)
    vc = vc_ref[0].astype(BF16)
    _attn_heads(q_ref[0], lambda h: [(_hslice(kc, h), _hslice(vc, h), None)], sink_ref, o_ref)


def windowed_attention(zl3, zc3, cos_t, sin_t, sink):
    b, t, _ = zl3.shape
    l = zc3.shape[1]
    nb = t // ATT_BLOCK
    kcol, kpcol, vcol = ZK // KV_W, ZKP // KV_W, ZV // KV_W
    prev = lambda j: jnp.maximum(j - 1, 0)
    nxt = lambda j: jnp.minimum(j + 1, nb - 1)

    def zspec(width, rowf, col):
        return pl.BlockSpec((1, ATT_BLOCK, width), lambda bi, j: (bi, rowf(j), col))

    def tspec(width, rowf):
        return pl.BlockSpec((ATT_BLOCK, width), lambda bi, j: (rowf(j), 0))

    same = lambda j: j
    in_specs = [pl.BlockSpec(memory_space=pltpu.SMEM),
                zspec(Q_W, same, ZQ // Q_W), zspec(Q_W, same, ZQP // Q_W), tspec(Q_W, same), tspec(Q_W, same)]
    args = [sink, zl3, zl3, cos_t, sin_t]
    for col in (kcol, kpcol, vcol):
        in_specs += [zspec(KV_W, prev, col), zspec(KV_W, same, col), zspec(KV_W, nxt, col)]
        args += [zl3, zl3, zl3]
    for tab in (cos_t, sin_t):
        in_specs += [tspec(KV_W, prev), tspec(KV_W, same), tspec(KV_W, nxt)]
        args += [tab, tab, tab]
    in_specs += [pl.BlockSpec((1, l, KV_W), lambda bi, j: (bi, 0, kcol)),
                 pl.BlockSpec((1, l, KV_W), lambda bi, j: (bi, 0, vcol))]
    args += [zc3, zc3]
    return pl.pallas_call(
        functools.partial(_win_attn_kernel, nb=nb),
        grid=(b, nb), in_specs=in_specs,
        out_specs=pl.BlockSpec((1, ATT_BLOCK, Q_W), lambda bi, j: (bi, j, 0)),
        out_shape=jax.ShapeDtypeStruct((b, t, Q_W), F32),
        compiler_params=_cparams(("parallel", "parallel")), name="win_attn",
    )(*args)


def context_attention(zc3, sink):
    b, l, _ = zc3.shape
    return pl.pallas_call(
        _ctx_attn_kernel, grid=(b,),
        in_specs=[pl.BlockSpec(memory_space=pltpu.SMEM),
                  pl.BlockSpec((1, l, Q_W), lambda bi: (bi, 0, ZQ // Q_W)),
                  pl.BlockSpec((1, l, KV_W), lambda bi: (bi, 0, ZK // KV_W)),
                  pl.BlockSpec((1, l, KV_W), lambda bi: (bi, 0, ZV // KV_W))],
        out_specs=pl.BlockSpec((1, l, Q_W), lambda bi: (bi, 0, 0)),
        out_shape=jax.ShapeDtypeStruct((b, l, Q_W), F32),
        compiler_params=_cparams(("parallel",)), name="ctx_attn",
    )(sink, zc3, zc3, zc3)


SUBLANES = 8


def _wkv_scan_kernel(w_ref, kd_ref, nkk_ref, bb_ref, v_ref, r_ref, y_ref, s_ref, *, tt):
    @pl.when(pl.program_id(0) == 0)
    def _():
        s_ref[...] = jnp.zeros_like(s_ref)

    nv = s_ref.shape[0]

    def step(t, carry):
        w = w_ref[t]
        kd = kd_ref[t]
        nkk = nkk_ref[t]
        bb = bb_ref[t]
        r = r_ref[t]

        def vgroup(vg, c2):
            base = pl.multiple_of(vg * SUBLANES, SUBLANES)
            vrows = v_ref[t, pl.ds(base, SUBLANES), :]
            ys = []
            for i in range(SUBLANES):
                s_old = s_ref[base + i]
                sa = jnp.sum(s_old * nkk, axis=0, keepdims=True)
                s_new = s_old * w + sa * bb + vrows[i:i + 1, :] * kd
                s_ref[base + i] = s_new
                ys.append(jnp.sum(s_new * r, axis=0, keepdims=True))
            y_ref[t, pl.ds(base, SUBLANES), :] = jnp.concatenate(ys, axis=0)
            return c2

        lax.fori_loop(0, nv // SUBLANES, vgroup, 0)
        return carry

    lax.fori_loop(0, tt, step, 0)


def wkv_scan(w, kd, nkk, bb, v, r):
    ttot, n, ch = w.shape
    tt = _pick_tile(ttot, (32, 16, 8, 4, 2, 1))
    spec = pl.BlockSpec((tt, n, ch), lambda i: (i, 0, 0))
    return pl.pallas_call(
        functools.partial(_wkv_scan_kernel, tt=tt),
        grid=(ttot // tt,), in_specs=[spec] * 6, out_specs=spec,
        out_shape=jax.ShapeDtypeStruct((ttot, n, ch), F32),
        scratch_shapes=[pltpu.VMEM((n, n, ch), F32)],
        compiler_params=_cparams(("arbitrary",)), name="wkv_scan",
    )(w, kd, nkk, bb, v, r)


def _merge_kernel(x_ref, att_ref, rw_ref, cv_ref, g0_ref, g1_ref, g2_ref,
                  wa_ref, wr_ref, wc_ref, wo_ref, ng_ref, gt_ref, o_ref):
    def proj(a_ref, w_ref):
        return jnp.dot(a_ref[...].astype(BF16), w_ref[...], preferred_element_type=F32)

    m = (jax.nn.sigmoid(g0_ref[...]) * proj(att_ref, wa_ref)
         + jax.nn.sigmoid(g1_ref[...]) * proj(rw_ref, wr_ref)
         + jax.nn.sigmoid(g2_ref[...]) * proj(cv_ref, wc_ref))
    ml = jnp.dot(m.astype(BF16), wo_ref[...], preferred_element_type=F32)
    ms = jnp.mean(ml * ml, axis=-1, keepdims=True)
    o_ref[...] = x_ref[...] + gt_ref[0] * (ml * lax.rsqrt(ms + NORM_EPS) * ng_ref[...])


def merge_out(x, att, rw, cv, z, wa, wr, wc, wo, ng, gt, rows_per_group):
    m, d = x.shape
    tm = _pick_tile(rows_per_group, (512, 256, 128, 64, 32, 16, 8))
    row = lambda i: (i, 0)
    const = lambda i: (0, 0)
    gcol = ZG // d
    return pl.pallas_call(
        _merge_kernel, grid=(m // tm,),
        in_specs=[pl.BlockSpec((tm, d), row),
                  pl.BlockSpec((tm, Q_W), row), pl.BlockSpec((tm, RWKV_W), row), pl.BlockSpec((tm, CONV_W), row),
                  pl.BlockSpec((tm, d), lambda i: (i, gcol)),
                  pl.BlockSpec((tm, d), lambda i: (i, gcol + 1)),
                  pl.BlockSpec((tm, d), lambda i: (i, gcol + 2)),
                  pl.BlockSpec((Q_W, d), const), pl.BlockSpec((RWKV_W, d), const), pl.BlockSpec((CONV_W, d), const),
                  pl.BlockSpec((d, d), const), pl.BlockSpec((1, d), const),
                  pl.BlockSpec((1, 1, d), lambda i: ((i * tm) // rows_per_group, 0, 0))],
        out_specs=pl.BlockSpec((tm, d), row),
        out_shape=jax.ShapeDtypeStruct((m, d), F32),
        compiler_params=_cparams(("parallel",)), name="merge_out",
    )(x, att, rw, cv, z, z, z, wa, wr, wc, wo, ng.reshape(1, d), gt)


def _swiglu_kernel(te_ref, h_ref, wg_ref, wu_ref, wd_ref, o_ref):
    f = pl.program_id(1)
    h = h_ref[...]
    hg = jnp.dot(h, wg_ref[0], preferred_element_type=F32)
    hu = jnp.dot(h, wu_ref[0], preferred_element_type=F32)
    act = (hg * jax.nn.sigmoid(hg) * hu).astype(BF16)
    part = jnp.dot(act, wd_ref[0], preferred_element_type=F32)

    @pl.when(f == 0)
    def _():
        o_ref[...] = part

    @pl.when(f > 0)
    def _():
        o_ref[...] += part


def grouped_swiglu(h, tile_expert, wg, wu, wd, tm):
    p, d = h.shape
    ff = wg.shape[2]
    tf = _pick_tile(ff, (512, 1408, 256, 128))
    grid_spec = pltpu.PrefetchScalarGridSpec(
        num_scalar_prefetch=1, grid=(p // tm, ff // tf),
        in_specs=[pl.BlockSpec((tm, d), lambda i, f, te: (i, 0)),
                  pl.BlockSpec((1, d, tf), lambda i, f, te: (te[i], 0, f)),
                  pl.BlockSpec((1, d, tf), lambda i, f, te: (te[i], 0, f)),
                  pl.BlockSpec((1, tf, d), lambda i, f, te: (te[i], f, 0))],
        out_specs=pl.BlockSpec((tm, d), lambda i, f, te: (i, 0)))
    return pl.pallas_call(
        _swiglu_kernel, grid_spec=grid_spec,
        out_shape=jax.ShapeDtypeStruct((p, d), F32),
        compiler_params=_cparams(("parallel", "arbitrary")), name="swiglu",
    )(tile_expert, h, wg, wu, wd)


def _rope_tables(t):
    rows = t // GRID_W
    row = jnp.broadcast_to(jnp.arange(rows)[:, None], (rows, GRID_W)).reshape(-1).astype(F32)
    col = jnp.broadcast_to(jnp.arange(GRID_W)[None, :], (rows, GRID_W)).reshape(-1).astype(F32)
    inv = ROPE_BASE ** (-jnp.arange(0, AXIS_DIM, 2, dtype=F32) / AXIS_DIM)
    ar = row[:, None] * inv[None, :]
    ac = col[:, None] * inv[None, :]
    cos64 = jnp.concatenate([jnp.cos(ar), jnp.cos(ar), jnp.cos(ac), jnp.cos(ac)], axis=-1)
    sin64 = jnp.concatenate([jnp.sin(ar), jnp.sin(ar), jnp.sin(ac), jnp.sin(ac)], axis=-1)
    return jnp.tile(cos64, (1, ATT_HEADS)), jnp.tile(sin64, (1, ATT_HEADS))


def _rope_partner(w):
    d = w.shape[0]
    q = AXIS_DIM // 2
    wh = w.reshape(d, -1, 4, q)
    return jnp.stack([-wh[:, :, 1], wh[:, :, 0], -wh[:, :, 3], wh[:, :, 2]], axis=2).reshape(w.shape)


def _build_w_ext(w_in):
    d = w_in.shape[0]
    wq = w_in[:, OFF_Q:OFF_K]
    wk = w_in[:, OFF_K:OFF_V]
    o = OFF_R + 3 * RWKV_W
    parts = [(ZQ, wq), (ZQP, _rope_partner(wq)), (ZK, wk), (ZKP, _rope_partner(wk)),
             (ZV, w_in[:, OFF_V:OFF_R]),
             (ZWA, w_in[:, o:o + 2 * DECAY_RANK + 2 * ICLR_RANK]),
             (ZCA, w_in[:, OFF_CONV:OFF_CONV + CONV_W]), (ZCB, w_in[:, OFF_CONV + CONV_W:OFF_GATE]),
             (ZR, w_in[:, OFF_R:OFF_R + RWKV_W]), (ZRK, w_in[:, OFF_R + RWKV_W:OFF_R + 2 * RWKV_W]),
             (ZRV, w_in[:, OFF_R + 2 * RWKV_W:OFF_R + 3 * RWKV_W]),
             (ZG, w_in[:, OFF_GATE:IN_COLS]),
             (ZZG, w_in[:, OFF_CONV - GATE_RANK:OFF_CONV])]
    parts.sort(key=lambda p: p[0])
    cols, pos = [], 0
    for off, wpart in parts:
        if off > pos:
            cols.append(jnp.zeros((d, off - pos), w_in.dtype))
        cols.append(wpart)
        pos = off + wpart.shape[1]
    if pos < NZ:
        cols.append(jnp.zeros((d, NZ - pos), w_in.dtype))
    return jnp.concatenate(cols, axis=1).astype(BF16)


def _shift_lerp(z, mu):
    zp = jnp.pad(z, ((0, 0), (1, 1), (0, 0)))
    return z + mu * (0.5 * (zp[:, :-2] + zp[:, 2:]) - z)


def _rwkv_inputs(z3, lp, v_first):
    mu = lp['mu']
    r = _shift_lerp(z3[..., ZR:ZR + RWKV_W], mu[:RWKV_W])
    k = _shift_lerp(z3[..., ZRK:ZRK + RWKV_W], mu[RWKV_W:2 * RWKV_W])
    v = _shift_lerp(z3[..., ZRV:ZRV + RWKV_W], mu[2 * RWKV_W:3 * RWKV_W])
    o = 3 * RWKV_W
    nwa = 2 * DECAY_RANK + 2 * ICLR_RANK
    zwa = _shift_lerp(z3[..., ZWA:ZWA + nwa], mu[o:o + nwa])
    zg = _shift_lerp(z3[..., ZZG:ZZG + GATE_RANK], mu[o + nwa:])
    bsz, tx = r.shape[:2]
    zw = zwa[..., :2 * DECAY_RANK].reshape(bsz, tx, 2, DECAY_RANK)
    za = zwa[..., 2 * DECAY_RANK:].reshape(bsz, tx, 2, ICLR_RANK)
    hi = lax.Precision.HIGHEST
    w_log = -jax.nn.softplus(-(lp['w0'] + jnp.einsum('btdr,drc->btdc', jnp.tanh(zw), lp['w2'], precision=hi))) - 0.5
    decay = jnp.exp(-jnp.exp(w_log))
    a = jax.nn.sigmoid(lp['a0'] + jnp.einsum('btdr,drc->btdc', za, lp['a2'], precision=hi))
    v_in = v
    if v_first is not None:
        mix = jnp.einsum('btr,rc->btc', jnp.einsum('btc,cr->btr', v, lp['v1'], precision=hi), lp['v2'], precision=hi)
        v = v + (v_first - v) * jax.nn.sigmoid(lp['v0'] + mix)
    kk = (k * lp['k_k']).reshape(bsz, tx, RWKV_HEADS, RWKV_HEAD)
    kk = kk / jnp.maximum(jnp.sqrt(jnp.sum(kk * kk, axis=-1, keepdims=True)), 1e-12)
    kk = kk.reshape(bsz, tx, RWKV_W)
    k_dir = k[:, :, None, :] * (1.0 + (a - 1.0) * lp['k_a'])
    return {'r': r, 'k_dir': k_dir, 'v': v, 'decay': decay, 'a': a, 'kk': kk, 'zg': zg, 'v_in': v_in}


def _to_chains(ctx_fwd, lat_fwd, ctx_bwd, lat_bwd):
    fwd = jnp.concatenate([ctx_fwd, lat_fwd], axis=1)
    bwd = jnp.concatenate([jnp.flip(ctx_bwd, axis=1), jnp.flip(lat_bwd, axis=1)], axis=1)
    both = jnp.stack([fwd, bwd], axis=0)
    bsz, ttot = fwd.shape[:2]
    both = both.reshape(2, bsz, ttot, RWKV_HEADS, RWKV_HEAD)
    return jnp.transpose(both, (2, 4, 0, 1, 3)).reshape(ttot, RWKV_HEAD, 2 * bsz * RWKV_HEADS)


def _bidirectional_wkv(in_l, in_c):
    bsz, t = in_l['r'].shape[:2]
    l = in_c['r'].shape[1]

    def both_dirs(name, fn=lambda x: x):
        return _to_chains(fn(in_c[name][:, :, 0]), fn(in_l[name][:, :, 0]),
                          fn(in_c[name][:, :, 1]), fn(in_l[name][:, :, 1]))

    def shared(name, fn=lambda x: x):
        return _to_chains(fn(in_c[name]), fn(in_l[name]), fn(in_c[name]), fn(in_l[name]))

    w = both_dirs('decay')
    kd = both_dirs('k_dir')
    bb = _to_chains(in_c['kk'] * in_c['a'][:, :, 0], in_l['kk'] * in_l['a'][:, :, 0],
                    in_c['kk'] * in_c['a'][:, :, 1], in_l['kk'] * in_l['a'][:, :, 1])
    nkk = shared('kk', lambda x: -x)
    v = shared('v')
    r = shared('r')
    y = wkv_scan(w, kd, nkk, bb, v, r)
    y = y.reshape(l + t, RWKV_HEAD, 2, bsz, RWKV_HEADS)
    y = jnp.transpose(y, (2, 3, 0, 4, 1)).reshape(2, bsz, l + t, RWKV_W)
    y_c = y[0, :, :l] + jnp.flip(y[1, :, :l], axis=1)
    y_l = y[0, :, l:] + jnp.flip(y[1, :, l:], axis=1)
    return y_l, y_c


def _rwkv_out(y, inp, lp):
    bsz, tx = y.shape[:2]
    yh = y.reshape(bsz, tx, RWKV_HEADS, RWKV_HEAD)
    m = jnp.mean(yh, axis=-1, keepdims=True)
    var = jnp.mean(jnp.square(yh - m), axis=-1, keepdims=True)
    yn = ((yh - m) * lax.rsqrt(var + LNX_EPS)).reshape(bsz, tx, RWKV_W) * lp['lnx_g'] + lp['lnx_b']
    rk = (inp['r'][:, :, None, :] * inp['k_dir']).reshape(bsz, tx, 2, RWKV_HEADS, RWKV_HEAD) * lp['r_k']
    bonus = jnp.sum(rk, axis=-1, keepdims=True).sum(axis=2)
    yn = yn + (bonus * inp['v'].reshape(bsz, tx, RWKV_HEADS, RWKV_HEAD)).reshape(bsz, tx, RWKV_W)
    g = jnp.einsum('btr,rc->btc', jax.nn.sigmoid(inp['zg']), lp['g2'], precision=lax.Precision.HIGHEST)
    return yn * g


def _conv_module(z3, lp):
    u = z3[..., ZCA:ZCA + CONV_W] * jax.nn.sigmoid(z3[..., ZCB:ZCB + CONV_W])
    tx = u.shape[1]
    half = CONV_K // 2
    up = jnp.pad(u, ((0, 0), (half, half), (0, 0)))
    acc = lp['dw_b'] + up[:, 0:tx] * lp['dw_w'][0]
    for j in range(1, CONV_K):
        acc = acc + up[:, j:j + tx] * lp['dw_w'][j]
    m = jnp.mean(acc, axis=-1, keepdims=True)
    var = jnp.mean(jnp.square(acc - m), axis=-1, keepdims=True)
    y = (acc - m) * lax.rsqrt(var + CONV_LN_EPS) * lp['cln_g'] + lp['cln_b']
    return jax.nn.silu(y)


def _moe_dispatch(logits, tm):
    m = logits.shape[0]
    top_v, top_i = lax.top_k(logits, TOP_K)
    wts = jax.nn.softmax(top_v, axis=-1)
    e_flat = top_i.reshape(-1)
    n_assign = e_flat.shape[0]
    onehot = (e_flat[:, None] == jnp.arange(N_EXPERTS)[None, :]).astype(jnp.int32)
    csum = jnp.cumsum(onehot, axis=0)
    rank = jnp.sum(onehot * csum, axis=1) - 1
    cnt = csum[-1]
    pcnt = ((cnt + tm - 1) // tm) * tm
    pend = jnp.cumsum(pcnt)
    pstart = pend - pcnt
    dest = pstart[e_flat] + rank
    p_max = n_assign + N_EXPERTS * tm
    row_token = jnp.zeros((p_max,), jnp.int32).at[dest].set(jnp.arange(n_assign, dtype=jnp.int32) // TOP_K)
    tile_start = jnp.arange(p_max // tm, dtype=jnp.int32) * tm
    tile_expert = jnp.minimum(jnp.searchsorted(pend, tile_start, side='right'), N_EXPERTS - 1).astype(jnp.int32)
    return row_token, tile_expert, dest.reshape(m, TOP_K), wts


def kernel(x, c, ctx, c_ctx, mod_w, mod_b, norm_g, w_in, attn_sink, rwkv_mu, rwkv_w0, rwkv_w2, rwkv_a0, rwkv_a2, rwkv_g2, rwkv_k_k, rwkv_k_a, rwkv_r_k, rwkv_lnx_g, rwkv_lnx_b, vres_v0, vres_w1, vres_w2, conv_dw_w, conv_dw_b, conv_ln_g, conv_ln_b, w_attn_o, w_rwkv_o, w_conv_o, w_out, ffn_wg, ffn_wu, ffn_wd, moe_router, moe_wg, moe_wu, moe_wd):
    bsz, t, d = x.shape
    l = ctx.shape[1]
    depth = mod_w.shape[0]
    n_lat, n_ctx = bsz * t, bsz * l
    cos_t, sin_t = _rope_tables(t)
    xl = x.reshape(n_lat, d)
    xc = ctx.reshape(n_ctx, d)
    n_rows = -(-(bsz + 1) // SUBLANES) * SUBLANES
    cond = jnp.zeros((n_rows, d), F32).at[:bsz].set(jax.nn.silu(c)).at[bsz].set(jax.nn.silu(c_ctx))
    v_first = None
    for i in range(depth):
        need_ctx = i < depth - 1
        mod = matmul_bias_f32(cond, mod_w[i], mod_b[i])
        mod_l = [mod[:bsz, k * d:(k + 1) * d].reshape(bsz, 1, d) for k in range(6)]
        mod_c = [mod[bsz:bsz + 1, k * d:(k + 1) * d].reshape(1, 1, d) for k in range(6)]
        lp = {'mu': rwkv_mu[i], 'w0': rwkv_w0[i], 'w2': rwkv_w2[i], 'a0': rwkv_a0[i], 'a2': rwkv_a2[i],
              'g2': rwkv_g2[i], 'k_k': rwkv_k_k[i], 'k_a': rwkv_k_a[i], 'r_k': rwkv_r_k[i],
              'lnx_g': rwkv_lnx_g[i], 'lnx_b': rwkv_lnx_b[i],
              'dw_w': conv_dw_w[i], 'dw_b': conv_dw_b[i], 'cln_g': conv_ln_g[i], 'cln_b': conv_ln_b[i]}
        if i > 0:
            lp['v0'], lp['v1'], lp['v2'] = vres_v0[i - 1], vres_w1[i - 1], vres_w2[i - 1]
        w_ext = _build_w_ext(w_in[i])
        wa, wr, wc, wo = (w.astype(BF16) for w in (w_attn_o[i], w_rwkv_o[i], w_conv_o[i], w_out[i]))

        zl = nm_matmul(xl, norm_g[i, 0], mod_l[1], mod_l[0], w_ext, t)
        zc = nm_matmul(xc, norm_g[i, 0], mod_c[1], mod_c[0], w_ext, n_ctx)
        zl3 = zl.reshape(bsz, t, NZ)
        zc3 = zc.reshape(bsz, l, NZ)
        att_l = windowed_attention(zl3, zc3, cos_t, sin_t, attn_sink[i])
        in_l = _rwkv_inputs(zl3, lp, None if v_first is None else v_first[0])
        in_c = _rwkv_inputs(zc3, lp, None if v_first is None else v_first[1])
        if i == 0:
            v_first = (in_l['v_in'], in_c['v_in'])
        y_l, y_c = _bidirectional_wkv(in_l, in_c)
        rw_l = _rwkv_out(y_l, in_l, lp)
        cv_l = _conv_module(zl3, lp)
        xl = merge_out(xl, att_l.reshape(n_lat, Q_W), rw_l.reshape(n_lat, RWKV_W), cv_l.reshape(n_lat, CONV_W),
                       zl, wa, wr, wc, wo, norm_g[i, 1], mod_l[2], t)
        if need_ctx:
            att_c = context_attention(zc3, attn_sink[i])
            rw_c = _rwkv_out(y_c, in_c, lp)
            cv_c = _conv_module(zc3, lp)
            xc = merge_out(xc, att_c.reshape(n_ctx, Q_W), rw_c.reshape(n_ctx, RWKV_W), cv_c.reshape(n_ctx, CONV_W),
                           zc, wa, wr, wc, wo, norm_g[i, 1], mod_c[2], n_ctx)

        j = i // 2
        if i % 2 == 0:
            wg, wu, wd = (w[j][None].astype(BF16) for w in (ffn_wg, ffn_wu, ffn_wd))
            streams = [(xl, mod_l, t)] + ([(xc, mod_c, n_ctx)] if need_ctx else [])
            outs = []
            for xs, ms, rpg in streams:
                tm = _pick_tile(rpg, (512, 256, 128, 64, 32, 16, 8))
                h = prenorm(xs, norm_g[i, 2], ms[4], ms[3], rpg)
                f = grouped_swiglu(h, jnp.zeros((xs.shape[0] // tm,), jnp.int32), wg, wu, wd, tm)
                outs.append(resnorm(xs, f, norm_g[i, 3], ms[5], rpg))
            xl = outs[0]
            if need_ctx:
                xc = outs[1]
        else:
            wg, wu, wd = (w[j].astype(BF16) for w in (moe_wg, moe_wu, moe_wd))
            streams = [(xl, mod_l, t)] + ([(xc, mod_c, n_ctx)] if need_ctx else [])
            outs = []
            for xs, ms, rpg in streams:
                tm = _pick_tile(rpg, (512, 256, 128, 64, 32, 16, 8))
                h, logits = prenorm(xs, norm_g[i, 2], ms[4], ms[3], rpg, router=moe_router[j])
                row_token, tile_expert, dest, wts = _moe_dispatch(logits, tm)
                yg = grouped_swiglu(jnp.take(h, row_token, axis=0), tile_expert, wg, wu, wd, tm)
                f = wts[:, 0:1] * jnp.take(yg, dest[:, 0], axis=0) + wts[:, 1:2] * jnp.take(yg, dest[:, 1], axis=0)
                outs.append(resnorm(xs, f, norm_g[i, 3], ms[5], rpg))
            xl = outs[0]
            if need_ctx:
                xc = outs[1]
    return xl.reshape(bsz, t, d)
```

```python
import functools
import math

import jax
import jax.numpy as jnp
from jax import lax
from jax.experimental import pallas as pl
from jax.experimental.pallas import tpu as pltpu

F32 = jnp.float32
BF16 = jnp.bfloat16

D_MODEL = 1024
GRID_W = 64
ATT_HEADS = 8
ATT_KV_HEADS = 2
ATT_GROUP = ATT_HEADS // ATT_KV_HEADS
HEAD_DIM = 64
WINDOW = 128
ATT_BLOCK = 128
ATT_SCALE = 1.0 / math.sqrt(HEAD_DIM)
ROPE_BASE = 10000.0
AXIS_DIM = HEAD_DIM // 2
RWKV_HEADS = 8
RWKV_HEAD = 64
RWKV_W = RWKV_HEADS * RWKV_HEAD
DECAY_RANK = 32
ICLR_RANK = 32
GATE_RANK = 96
LNX_EPS = 64e-5
CONV_W = 512
CONV_K = 31
CONV_LN_EPS = 1e-5
N_BRANCH = 3
N_EXPERTS = 8
TOP_K = 2
NORM_EPS = 1e-6
Q_W = ATT_HEADS * HEAD_DIM
KV_W = ATT_KV_HEADS * HEAD_DIM
OFF_Q = 0
OFF_K = OFF_Q + Q_W
OFF_V = OFF_K + KV_W
OFF_R = OFF_V + KV_W
OFF_CONV = OFF_R + 3 * RWKV_W + 2 * DECAY_RANK + 2 * ICLR_RANK + GATE_RANK
OFF_GATE = OFF_CONV + 2 * CONV_W
IN_COLS = OFF_GATE + N_BRANCH * D_MODEL

ZQ = 0
ZQP = 512
ZK = 1024
ZKP = 1152
ZV = 1280
ZWA = 1408
ZCA = 1536
ZCB = 2048
ZR = 2560
ZRK = 3072
ZRV = 3584
ZG = 4096
ZZG = 7168
NZ = 7680

VMEM_LIMIT = 56 * 1024 * 1024


def _cparams(sem):
    return pltpu.CompilerParams(dimension_semantics=sem, vmem_limit_bytes=VMEM_LIMIT)


def _pick_tile(n, candidates):
    for c in candidates:
        if n % c == 0:
            return c
    return n


def _matmul_bias_kernel(a_ref, w_ref, b_ref, o_ref):
    o_ref[...] = jnp.dot(a_ref[...], w_ref[...], preferred_element_type=F32,
                         precision=lax.Precision.HIGHEST) + b_ref[...]


def matmul_bias_f32(a, w, b):
    m, k = a.shape
    n = w.shape[1]
    tn = _pick_tile(n, (1536, 1024, 512, 256, 128))
    return pl.pallas_call(
        _matmul_bias_kernel,
        grid=(n // tn,),
        in_specs=[pl.BlockSpec((m, k), lambda j: (0, 0)),
                  pl.BlockSpec((k, tn), lambda j: (0, j)),
                  pl.BlockSpec((1, tn), lambda j: (0, j))],
        out_specs=pl.BlockSpec((m, tn), lambda j: (0, j)),
        out_shape=jax.ShapeDtypeStruct((m, n), F32),
        compiler_params=_cparams(("parallel",)),
        name="mod_matmul",
    )(a, w, b.reshape(1, n))


def _norm_mod(x, g, sc, sh):
    ms = jnp.mean(x * x, axis=-1, keepdims=True)
    return (x * lax.rsqrt(ms + NORM_EPS) * g) * (1.0 + sc) + sh


def _nm_matmul_kernel(x_ref, g_ref, sc_ref, sh_ref, w_ref, o_ref, h_ref):
    @pl.when(pl.program_id(1) == 0)
    def _():
        h_ref[...] = _norm_mod(x_ref[...], g_ref[...], sc_ref[0], sh_ref[0]).astype(BF16)

    o_ref[...] = jnp.dot(h_ref[...], w_ref[...], preferred_element_type=F32).astype(o_ref.dtype)


def nm_matmul(x, g, sc, sh, w, rows_per_group):
    m, d = x.shape
    n = w.shape[1]
    tm = _pick_tile(rows_per_group, (512, 256, 128, 64, 32, 16, 8))
    tn = _pick_tile(n, (1280, 1024, 512, 256, 128))
    grp = lambda i, j: ((i * tm) // rows_per_group, 0, 0)
    return pl.pallas_call(
        _nm_matmul_kernel,
        grid=(m // tm, n // tn),
        in_specs=[pl.BlockSpec((tm, d), lambda i, j: (i, 0)),
                  pl.BlockSpec((1, d), lambda i, j: (0, 0)),
                  pl.BlockSpec((1, 1, d), grp),
                  pl.BlockSpec((1, 1, d), grp),
                  pl.BlockSpec((d, tn), lambda i, j: (0, j))],
        out_specs=pl.BlockSpec((tm, tn), lambda i, j: (i, j)),
        out_shape=jax.ShapeDtypeStruct((m, n), F32),
        scratch_shapes=[pltpu.VMEM((tm, d), BF16)],
        compiler_params=_cparams(("parallel", "arbitrary")),
        name="in_proj",
    )(x, g.reshape(1, d), sc, sh, w)


def _prenorm_kernel(x_ref, g_ref, sc_ref, sh_ref, h_ref):
    h_ref[...] = _norm_mod(x_ref[...], g_ref[...], sc_ref[0], sh_ref[0]).astype(h_ref.dtype)


def _prenorm_router_kernel(x_ref, g_ref, sc_ref, sh_ref, r_ref, h_ref, l_ref):
    h = _norm_mod(x_ref[...], g_ref[...], sc_ref[0], sh_ref[0])
    h_ref[...] = h.astype(h_ref.dtype)
    l_ref[...] = jnp.dot(h, r_ref[...], preferred_element_type=F32, precision=lax.Precision.HIGHEST)


def prenorm(x, g, sc, sh, rows_per_group, router=None):
    m, d = x.shape
    tm = _pick_tile(rows_per_group, (512, 256, 128, 64, 32, 16, 8))
    grp = lambda i: ((i * tm) // rows_per_group, 0, 0)
    in_specs = [pl.BlockSpec((tm, d), lambda i: (i, 0)),
                pl.BlockSpec((1, d), lambda i: (0, 0)),
                pl.BlockSpec((1, 1, d), grp),
                pl.BlockSpec((1, 1, d), grp)]
    if router is None:
        return pl.pallas_call(
            _prenorm_kernel, grid=(m // tm,), in_specs=in_specs,
            out_specs=pl.BlockSpec((tm, d), lambda i: (i, 0)),
            out_shape=jax.ShapeDtypeStruct((m, d), BF16),
            compiler_params=_cparams(("parallel",)), name="prenorm",
        )(x, g.reshape(1, d), sc, sh)
    ne = router.shape[1]
    rpad = jnp.zeros((d, 128), F32).at[:, :ne].set(router)
    h, logits = pl.pallas_call(
        _prenorm_router_kernel, grid=(m // tm,),
        in_specs=in_specs + [pl.BlockSpec((d, 128), lambda i: (0, 0))],
        out_specs=[pl.BlockSpec((tm, d), lambda i: (i, 0)), pl.BlockSpec((tm, 128), lambda i: (i, 0))],
        out_shape=[jax.ShapeDtypeStruct((m, d), BF16), jax.ShapeDtypeStruct((m, 128), F32)],
        compiler_params=_cparams(("parallel",)), name="prenorm_router",
    )(x, g.reshape(1, d), sc, sh, rpad)
    return h, logits[:, :ne]


def _resnorm_kernel(x_ref, f_ref, g_ref, gt_ref, o_ref):
    f = f_ref[...]
    ms = jnp.mean(f * f, axis=-1, keepdims=True)
    o_ref[...] = x_ref[...] + gt_ref[0] * (f * lax.rsqrt(ms + NORM_EPS) * g_ref[...])


def resnorm(x, f, g, gt, rows_per_group):
    m, d = x.shape
    tm = _pick_tile(rows_per_group, (512, 256, 128, 64, 32, 16, 8))
    grp = lambda i: ((i * tm) // rows_per_group, 0, 0)
    return pl.pallas_call(
        _resnorm_kernel, grid=(m // tm,),
        in_specs=[pl.BlockSpec((tm, d), lambda i: (i, 0)),
                  pl.BlockSpec((tm, d), lambda i: (i, 0)),
                  pl.BlockSpec((1, d), lambda i: (0, 0)),
                  pl.BlockSpec((1, 1, d), grp)],
        out_specs=pl.BlockSpec((tm, d), lambda i: (i, 0)),
        out_shape=jax.ShapeDtypeStruct((m, d), F32),
        compiler_params=_cparams(("parallel",)), name="resnorm",
    )(x, f, g.reshape(1, d), gt)


def _softmax_pv(q4, sink_col, parts):
    scores = []
    m = sink_col
    for k, _, bias in parts:
        s = lax.dot_general(q4, k, (((1,), (1,)), ((), ())), preferred_element_type=F32) * ATT_SCALE
        if bias is not None:
            s = s + bias
        scores.append(s)
        m = jnp.maximum(m, jnp.max(s, axis=-1, keepdims=True))
    den = jnp.exp(sink_col - m)
    acc = None
    for s, (_, v, _) in zip(scores, parts):
        p = jnp.exp(s - m)
        den = den + jnp.sum(p, axis=-1, keepdims=True)
        pv = jnp.dot(p.astype(BF16), v, preferred_element_type=F32)
        acc = pv if acc is None else acc + pv
    return acc / den


def _attn_heads(q, parts_fn, sink_ref, o_ref):
    bq = q.shape[0]
    outs = []
    for h in range(ATT_KV_HEADS):
        q4 = jnp.concatenate(
            [q[:, (h * ATT_GROUP + g) * HEAD_DIM:(h * ATT_GROUP + g + 1) * HEAD_DIM] for g in range(ATT_GROUP)],
            axis=0).astype(BF16)
        sink_col = jnp.concatenate(
            [jnp.full((bq, 1), sink_ref[h * ATT_GROUP + g], F32) for g in range(ATT_GROUP)], axis=0)
        o4 = _softmax_pv(q4, sink_col, parts_fn(h))
        outs.extend(o4[g * bq:(g + 1) * bq] for g in range(ATT_GROUP))
    o_ref[0] = jnp.concatenate(outs, axis=-1)


def _hslice(t, h):
    return t[:, h * HEAD_DIM:(h + 1) * HEAD_DIM]


def _win_attn_kernel(sink_ref, q_ref, qp_ref, cq_ref, sq_ref,
                     k0_ref, k1_ref, k2_ref, kp0_ref, kp1_ref, kp2_ref,
                     v0_ref, v1_ref, v2_ref, c0_ref, c1_ref, c2_ref, s0_ref, s1_ref, s2_ref,
                     kc_ref, vc_ref, o_ref, *, nb):
    j = pl.program_id(1)
    q = q_ref[0] * cq_ref[...] + qp_ref[0] * sq_ref[...]
    kb = jnp.concatenate([k0_ref[0] * c0_ref[...] + kp0_ref[0] * s0_ref[...],
                          k1_ref[0] * c1_ref[...] + kp1_ref[0] * s1_ref[...],
                          k2_ref[0] * c2_ref[...] + kp2_ref[0] * s2_ref[...]], axis=0).astype(BF16)
    vb = jnp.concatenate([v0_ref[0], v1_ref[0], v2_ref[0]], axis=0).astype(BF16)
    kc = kc_ref[0].astype(BF16)
    vc = vc_ref[0].astype(BF16)
    rows = ATT_GROUP * ATT_BLOCK
    qi = lax.broadcasted_iota(jnp.int32, (rows, 3 * ATT_BLOCK), 0) % ATT_BLOCK + ATT_BLOCK
    ki = lax.broadcasted_iota(jnp.int32, (rows, 3 * ATT_BLOCK), 1)
    lo = jnp.where(j == 0, ATT_BLOCK, 0)
    hi = jnp.where(j == nb - 1, 2 * ATT_BLOCK, 3 * ATT_BLOCK)
    far = jnp.where(ki < lo, 4 * ATT_BLOCK, 0) + jnp.where(ki >= hi, 4 * ATT_BLOCK, 0)
    bias = jnp.where(jnp.abs(qi - ki) + far <= WINDOW, 0.0, -jnp.inf).astype(F32)

    def parts(h):
        return [(_hslice(kc, h), _hslice(vc, h), None), (_hslice(kb, h), _hslice(vb, h), bias)]

    _attn_heads(q, parts, sink_ref, o_ref)


def _ctx_attn_kernel(sink_ref, q_ref, kc_ref, vc_ref, o_ref):
    kc = kc_ref[0].astype(BF16)
    vc = vc_ref[0].astype(BF16)
    _attn_heads(q_ref[0], lambda h: [(_hslice(kc, h), _hslice(vc, h), None)], sink_ref, o_ref)


def windowed_attention(zl3, zc3, cos_t, sin_t, sink):
    b, t, _ = zl3.shape
    l = zc3.shape[1]
    nb = t // ATT_BLOCK
    kcol, kpcol, vcol = ZK // KV_W, ZKP // KV_W, ZV // KV_W
    prev = lambda j: jnp.maximum(j - 1, 0)
    nxt = lambda j: jnp.minimum(j + 1, nb - 1)

    def zspec(width, rowf, col):
        return pl.BlockSpec((1, ATT_BLOCK, width), lambda bi, j: (bi, rowf(j), col))

    def tspec(width, rowf):
        return pl.BlockSpec((ATT_BLOCK, width), lambda bi, j: (rowf(j), 0))

    same = lambda j: j
    in_specs = [pl.BlockSpec(memory_space=pltpu.SMEM),
                zspec(Q_W, same, ZQ // Q_W), zspec(Q_W, same, ZQP // Q_W), tspec(Q_W, same), tspec(Q_W, same)]
    args = [sink, zl3, zl3, cos_t, sin_t]
    for col in (kcol, kpcol, vcol):
        in_specs += [zspec(KV_W, prev, col), zspec(KV_W, same, col), zspec(KV_W, nxt, col)]
        args += [zl3, zl3, zl3]
    for tab in (cos_t, sin_t):
        in_specs += [tspec(KV_W, prev), tspec(KV_W, same), tspec(KV_W, nxt)]
        args += [tab, tab, tab]
    in_specs += [pl.BlockSpec((1, l, KV_W), lambda bi, j: (bi, 0, kcol)),
                 pl.BlockSpec((1, l, KV_W), lambda bi, j: (bi, 0, vcol))]
    args += [zc3, zc3]
    return pl.pallas_call(
        functools.partial(_win_attn_kernel, nb=nb),
        grid=(b, nb), in_specs=in_specs,
        out_specs=pl.BlockSpec((1, ATT_BLOCK, Q_W), lambda bi, j: (bi, j, 0)),
        out_shape=jax.ShapeDtypeStruct((b, t, Q_W), F32),
        compiler_params=_cparams(("parallel", "parallel")), name="win_attn",
    )(*args)


def context_attention(zc3, sink):
    b, l, _ = zc3.shape
    return pl.pallas_call(
        _ctx_attn_kernel, grid=(b,),
        in_specs=[pl.BlockSpec(memory_space=pltpu.SMEM),
                  pl.BlockSpec((1, l, Q_W), lambda bi: (bi, 0, ZQ // Q_W)),
                  pl.BlockSpec((1, l, KV_W), lambda bi: (bi, 0, ZK // KV_W)),
                  pl.BlockSpec((1, l, KV_W), lambda bi: (bi, 0, ZV // KV_W))],
        out_specs=pl.BlockSpec((1, l, Q_W), lambda bi: (bi, 0, 0)),
        out_shape=jax.ShapeDtypeStruct((b, l, Q_W), F32),
        compiler_params=_cparams(("parallel",)), name="ctx_attn",
    )(sink, zc3, zc3, zc3)


HALO = 8


def _dot_hi(a, b):
    return jnp.dot(a, b, preferred_element_type=F32, precision=lax.Precision.HIGHEST)


def _head_sums(x, e_ref):
    hi = x.astype(BF16)
    lo = (x - hi.astype(F32)).astype(BF16)
    return (jnp.dot(hi, e_ref[...], preferred_element_type=F32)
            + jnp.dot(lo, e_ref[...], preferred_element_type=F32))


def _shift_lerp_tile(cur_ref, prev_ref, next_ref, mu_ref, buf_ref, first, last):
    tt = cur_ref.shape[1]
    z = cur_ref[0]
    buf_ref[0:HALO] = jnp.where(first, 0.0, prev_ref[0])
    buf_ref[HALO:HALO + tt] = z
    buf_ref[HALO + tt:2 * HALO + tt] = jnp.where(last, 0.0, next_ref[0])
    nb = buf_ref[HALO - 1:HALO - 1 + tt] + buf_ref[HALO + 1:HALO + 1 + tt]
    return z + mu_ref[...] * (0.5 * nb - z)


def _rwkv_prep_kernel(*refs, has_vres):
    (wa_c, wa_p, wa_n, r_c, r_p, r_n, k_c, k_p, k_n, v_c, v_p, v_n, g_c, g_p, g_n,
     mu_wa, mu_r, mu_k, mu_v, mu_g, w0_ref, w2_ref, a0_ref, a2_ref, g2_ref, kk_ref, ka_ref, rk_ref, e_ref) = refs[:29]
    pos = 29
    if has_vres:
        vf_ref, v0_ref, v1_ref, v2_ref = refs[pos:pos + 4]
        pos += 4
    (r_o, k_o, v_o, kk_o, w0_o, w1_o, an0_o, an1_o, bv_o, g_o) = refs[pos:pos + 10]
    pos += 10
    if not has_vres:
        vin_o = refs[pos]
        pos += 1
    buf_wa, buf_r, buf_k, buf_v, buf_g = refs[pos:pos + 5]

    i = pl.program_id(1)
    first = i == 0
    last = i == pl.num_programs(1) - 1
    zwa = _shift_lerp_tile(wa_c, wa_p, wa_n, mu_wa, buf_wa, first, last)
    r = _shift_lerp_tile(r_c, r_p, r_n, mu_r, buf_r, first, last)
    k = _shift_lerp_tile(k_c, k_p, k_n, mu_k, buf_k, first, last)
    v = _shift_lerp_tile(v_c, v_p, v_n, mu_v, buf_v, first, last)
    zg = _shift_lerp_tile(g_c, g_p, g_n, mu_g, buf_g, first, last)

    if has_vres:
        mix = _dot_hi(_dot_hi(v, v1_ref[...]), v2_ref[...])
        v = v + (vf_ref[0] - v) * jax.nn.sigmoid(v0_ref[...] + mix)
    else:
        vin_o[0] = v
    kk = k * kk_ref[...]
    kk = kk / jnp.maximum(jnp.sqrt(_head_sums(kk * kk, e_ref)), 1e-12)
    tz = jnp.tanh(zwa)
    ksum = None
    for d, (w_o, an_o) in enumerate(((w0_o, an0_o), (w1_o, an1_o))):
        w_log = -jax.nn.softplus(-(w0_ref[d:d + 1, :] + _dot_hi(tz, w2_ref[d]))) - 0.5
        w_o[0] = jnp.exp(-jnp.exp(w_log))
        a = jax.nn.sigmoid(a0_ref[d:d + 1, :] + _dot_hi(zwa, a2_ref[d]))
        an_o[0] = -a
        kd = k * (1.0 + (a - 1.0) * ka_ref[...])
        ksum = kd if ksum is None else ksum + kd
    r_o[0] = r
    k_o[0] = k
    v_o[0] = v
    kk_o[0] = kk
    bv_o[0] = _head_sums(r * ksum * rk_ref[...], e_ref) * v
    g_o[0] = _dot_hi(jax.nn.sigmoid(zg), g2_ref[...])


def _head_ones():
    idx = jnp.arange(RWKV_W) // RWKV_HEAD
    return (idx[:, None] == idx[None, :]).astype(BF16)


def rwkv_prep(z3, lp, v_first):
    b, tx, _ = z3.shape
    tt = _pick_tile(tx, (256, 128, 64, 32, 16, 8))
    nh = tt // HALO
    nblk = tx // HALO
    has_vres = v_first is not None
    lanes = 128

    def trio(width, col):
        return [pl.BlockSpec((1, tt, width), lambda bi, i: (bi, i, col)),
                pl.BlockSpec((1, HALO, width), lambda bi, i: (bi, jnp.maximum(i * nh - 1, 0), col)),
                pl.BlockSpec((1, HALO, width), lambda bi, i: (bi, jnp.minimum((i + 1) * nh, nblk - 1), col))]

    def full(shape):
        return pl.BlockSpec(shape, lambda bi, i: (0,) * len(shape))

    mu = lp['mu']
    o = 3 * RWKV_W
    nwa = 2 * DECAY_RANK + 2 * ICLR_RANK
    mu_g = jnp.zeros((lanes,), F32).at[:GATE_RANK].set(mu[o + nwa:])
    w2f = jnp.zeros((2, lanes, RWKV_W), F32)
    a2f = jnp.zeros((2, lanes, RWKV_W), F32)
    for d in range(2):
        w2f = w2f.at[d, d * DECAY_RANK:(d + 1) * DECAY_RANK].set(lp['w2'][d])
        a2f = a2f.at[d, 2 * DECAY_RANK + d * ICLR_RANK:2 * DECAY_RANK + (d + 1) * ICLR_RANK].set(lp['a2'][d])
    g2f = jnp.zeros((lanes, RWKV_W), F32).at[:GATE_RANK].set(lp['g2'])
    row = lambda a: a.reshape(1, -1)
    in_specs = (trio(lanes, ZWA // lanes) + trio(RWKV_W, ZR // RWKV_W) + trio(RWKV_W, ZRK // RWKV_W)
                + trio(RWKV_W, ZRV // RWKV_W) + trio(lanes, ZZG // lanes)
                + [full((1, lanes)), full((1, RWKV_W)), full((1, RWKV_W)), full((1, RWKV_W)), full((1, lanes)),
                   full((2, RWKV_W)), full((2, lanes, RWKV_W)), full((2, RWKV_W)), full((2, lanes, RWKV_W)),
                   full((lanes, RWKV_W)), full((1, RWKV_W)), full((1, RWKV_W)), full((1, RWKV_W)),
                   full((RWKV_W, RWKV_W))])
    args = [z3] * 15 + [row(mu[o:o + nwa]), row(mu[:RWKV_W]), row(mu[RWKV_W:2 * RWKV_W]), row(mu[2 * RWKV_W:o]),
                        row(mu_g), lp['w0'], w2f, lp['a0'], a2f, g2f, row(lp['k_k']), row(lp['k_a']),
                        row(lp['r_k']), _head_ones()]
    tok = pl.BlockSpec((1, tt, RWKV_W), lambda bi, i: (bi, i, 0))
    if has_vres:
        rank = lp['v1'].shape[1]
        v1f = jnp.zeros((RWKV_W, lanes), F32).at[:, :rank].set(lp['v1'])
        v2f = jnp.zeros((lanes, RWKV_W), F32).at[:rank].set(lp['v2'])
        in_specs += [tok, full((1, RWKV_W)), full((RWKV_W, lanes)), full((lanes, RWKV_W))]
        args += [v_first, row(lp['v0']), v1f, v2f]
    names = ['r', 'k', 'v', 'kk', 'w0', 'w1', 'an0', 'an1', 'bv', 'g'] + ([] if has_vres else ['v_in'])
    outs = pl.pallas_call(
        functools.partial(_rwkv_prep_kernel, has_vres=has_vres),
        grid=(b, tx // tt), in_specs=in_specs,
        out_specs=[tok] * len(names),
        out_shape=[jax.ShapeDtypeStruct((b, tx, RWKV_W), F32)] * len(names),
        scratch_shapes=[pltpu.VMEM((tt + 2 * HALO, lanes), F32)] + [pltpu.VMEM((tt + 2 * HALO, RWKV_W), F32)] * 3
                       + [pltpu.VMEM((tt + 2 * HALO, lanes), F32)],
        compiler_params=_cparams(("parallel", "parallel")), name="rwkv_prep",
    )(*args)
    return dict(zip(names, outs))


CONV_HALO = 16
CONV_ROWS = 32


def _conv_kernel(a_c, a_p, a_n, b_c, b_p, b_n, w_ref, bias_ref, lg_ref, lb_ref, o_ref, buf_ref):
    i = pl.program_id(1)
    first = i == 0
    last = i == pl.num_programs(1) - 1
    tt = a_c.shape[1]
    glu = lambda a, b: a * jax.nn.sigmoid(b)
    buf_ref[0:CONV_HALO] = jnp.where(first, 0.0, glu(a_p[0], b_p[0]))
    buf_ref[CONV_HALO:CONV_HALO + tt] = glu(a_c[0], b_c[0])
    buf_ref[CONV_HALO + tt:2 * CONV_HALO + tt] = jnp.where(last, 0.0, glu(a_n[0], b_n[0]))
    half = CONV_K // 2
    for r0 in range(0, tt, CONV_ROWS):
        acc = bias_ref[...] + buf_ref[CONV_HALO - half + r0:CONV_HALO - half + r0 + CONV_ROWS] * w_ref[0:1, :]
        for j in range(1, CONV_K):
            lo = CONV_HALO - half + r0 + j
            acc = acc + buf_ref[lo:lo + CONV_ROWS] * w_ref[j:j + 1, :]
        m = jnp.mean(acc, axis=-1, keepdims=True)
        cen = acc - m
        var = jnp.mean(cen * cen, axis=-1, keepdims=True)
        y = cen * lax.rsqrt(var + CONV_LN_EPS) * lg_ref[...] + lb_ref[...]
        o_ref[0, r0:r0 + CONV_ROWS, :] = y * jax.nn.sigmoid(y)


def conv_module(z3, lp):
    b, tx, _ = z3.shape
    tt = _pick_tile(tx, (256, 128, 64, 32))
    nh = tt // CONV_HALO
    nblk = tx // CONV_HALO

    def trio(col):
        return [pl.BlockSpec((1, tt, CONV_W), lambda bi, i: (bi, i, col)),
                pl.BlockSpec((1, CONV_HALO, CONV_W), lambda bi, i: (bi, jnp.maximum(i * nh - 1, 0), col)),
                pl.BlockSpec((1, CONV_HALO, CONV_W), lambda bi, i: (bi, jnp.minimum((i + 1) * nh, nblk - 1), col))]

    kpad = -(-CONV_K // SUBLANES) * SUBLANES
    wpad = jnp.zeros((kpad, CONV_W), F32).at[:CONV_K].set(lp['dw_w'])
    vec = pl.BlockSpec((1, CONV_W), lambda bi, i: (0, 0))
    return pl.pallas_call(
        _conv_kernel, grid=(b, tx // tt),
        in_specs=trio(ZCA // CONV_W) + trio(ZCB // CONV_W)
                 + [pl.BlockSpec((kpad, CONV_W), lambda bi, i: (0, 0)), vec, vec, vec],
        out_specs=pl.BlockSpec((1, tt, CONV_W), lambda bi, i: (bi, i, 0)),
        out_shape=jax.ShapeDtypeStruct((b, tx, CONV_W), F32),
        scratch_shapes=[pltpu.VMEM((tt + 2 * CONV_HALO, CONV_W), F32)],
        compiler_params=_cparams(("parallel", "parallel")), name="conv_module",
    )(z3, z3, z3, z3, z3, z3, wpad, lp['dw_b'].reshape(1, -1), lp['cln_g'].reshape(1, -1), lp['cln_b'].reshape(1, -1))


SUBLANES = 8


def _wkv_scan_kernel(rk_f, vk_f, w_f, an_f, rk_b, vk_b, w_b, an_b, ka_ref, yf_ref, yb_ref, s_ref, *, tt):
    @pl.when(pl.program_id(0) == 0)
    def _():
        s_ref[...] = jnp.zeros_like(s_ref)

    nv, nk, lanes = s_ref.shape
    half = lanes // 2
    fwd = lax.broadcasted_iota(jnp.int32, (nk, lanes), 1) < half
    ka = ka_ref[...]

    def step(t, carry):
        tb = tt - 1 - t
        rkf, rkb, vkf, vkb = rk_f[t], rk_b[tb], vk_f[t], vk_b[tb]
        r = jnp.where(fwd, rkf, pltpu.roll(rkb, half, 1))
        k = jnp.where(fwd, pltpu.roll(rkf, half, 1), rkb)
        v = jnp.where(fwd, vkf, pltpu.roll(vkb, half, 1))
        kk = jnp.where(fwd, pltpu.roll(vkf, half, 1), vkb)
        w = jnp.where(fwd, w_f[t], w_b[tb])
        an = jnp.where(fwd, an_f[t], an_b[tb])
        kd = k * (1.0 - (an + 1.0) * ka)
        bbn = kk * an
        for g in range(nv // SUBLANES):
            ys = []
            for i in range(SUBLANES):
                vi = g * SUBLANES + i
                s_old = s_ref[vi]
                sa = jnp.sum(s_old * kk, axis=0, keepdims=True)
                s_new = s_old * w + sa * bbn + v[vi:vi + 1, :] * kd
                s_ref[vi] = s_new
                ys.append(jnp.sum(s_new * r, axis=0, keepdims=True))
            yg = jnp.concatenate(ys, axis=0)
            yf_ref[t, g * SUBLANES:(g + 1) * SUBLANES, :] = yg
            yb_ref[tb, g * SUBLANES:(g + 1) * SUBLANES, :] = yg
        return carry

    lax.fori_loop(0, tt, step, 0)


def wkv_scan(rk, vk, w, an, ka, n_ctx_steps):
    ttot, n, ch = rk.shape
    tt = _pick_tile(math.gcd(n_ctx_steps, ttot - n_ctx_steps), (32, 16, 8, 4, 2, 1))
    nt, nc = ttot // tt, n_ctx_steps // tt
    fspec = pl.BlockSpec((tt, n, ch), lambda i: (i, 0, 0))
    bspec = pl.BlockSpec((tt, n, ch), lambda i: (jnp.where(i < nc, nc - 1 - i, nt - 1 - (i - nc)), 0, 0))
    return pl.pallas_call(
        functools.partial(_wkv_scan_kernel, tt=tt),
        grid=(nt,), in_specs=[fspec] * 4 + [bspec] * 4 + [pl.BlockSpec((n, ch), lambda i: (0, 0))],
        out_specs=[fspec, bspec],
        out_shape=[jax.ShapeDtypeStruct((ttot, n, ch), F32)] * 2,
        scratch_shapes=[pltpu.VMEM((n, n, ch), F32)],
        compiler_params=_cparams(("arbitrary",)), name="wkv_scan",
    )(rk, vk, w, an, rk, vk, w, an, ka)


def _merge_kernel(x_ref, att_ref, y_ref, bv_ref, rg_ref, cv_ref, g0_ref, g1_ref, g2_ref,
                  lg_ref, lb_ref, e_ref, wa_ref, wr_ref, wc_ref, wo_ref, ng_ref, gt_ref, o_ref):
    def proj(a, w_ref):
        return jnp.dot(a.astype(BF16), w_ref[...], preferred_element_type=F32)

    y = y_ref[...]
    cen = y - _head_sums(y, e_ref) * (1.0 / RWKV_HEAD)
    var = _head_sums(cen * cen, e_ref) * (1.0 / RWKV_HEAD)
    rw = (cen * lax.rsqrt(var + LNX_EPS) * lg_ref[...] + lb_ref[...] + bv_ref[...]) * rg_ref[...]
    m = (jax.nn.sigmoid(g0_ref[...]) * proj(att_ref[...], wa_ref)
         + jax.nn.sigmoid(g1_ref[...]) * proj(rw, wr_ref)
         + jax.nn.sigmoid(g2_ref[...]) * proj(cv_ref[...], wc_ref))
    ml = proj(m, wo_ref)
    ms = jnp.mean(ml * ml, axis=-1, keepdims=True)
    o_ref[...] = x_ref[...] + gt_ref[0] * (ml * lax.rsqrt(ms + NORM_EPS) * ng_ref[...])


def merge_out(x, att, y, bv, rg, cv, z, lnx_g, lnx_b, wa, wr, wc, wo, ng, gt, rows_per_group):
    m, d = x.shape
    tm = _pick_tile(rows_per_group, (512, 256, 128, 64, 32, 16, 8))
    row = lambda i: (i, 0)
    const = lambda i: (0, 0)
    gcol = ZG // d
    br = pl.BlockSpec((tm, RWKV_W), row)
    return pl.pallas_call(
        _merge_kernel, grid=(m // tm,),
        in_specs=[pl.BlockSpec((tm, d), row),
                  pl.BlockSpec((tm, Q_W), row), br, br, br, pl.BlockSpec((tm, CONV_W), row),
                  pl.BlockSpec((tm, d), lambda i: (i, gcol)),
                  pl.BlockSpec((tm, d), lambda i: (i, gcol + 1)),
                  pl.BlockSpec((tm, d), lambda i: (i, gcol + 2)),
                  pl.BlockSpec((1, RWKV_W), const), pl.BlockSpec((1, RWKV_W), const),
                  pl.BlockSpec((RWKV_W, RWKV_W), const),
                  pl.BlockSpec((Q_W, d), const), pl.BlockSpec((RWKV_W, d), const), pl.BlockSpec((CONV_W, d), const),
                  pl.BlockSpec((d, d), const), pl.BlockSpec((1, d), const),
                  pl.BlockSpec((1, 1, d), lambda i: ((i * tm) // rows_per_group, 0, 0))],
        out_specs=pl.BlockSpec((tm, d), row),
        out_shape=jax.ShapeDtypeStruct((m, d), F32),
        compiler_params=_cparams(("parallel",)), name="merge_out",
    )(x, att, y, bv, rg, cv, z, z, z, lnx_g.reshape(1, -1), lnx_b.reshape(1, -1), _head_ones(),
      wa, wr, wc, wo, ng.reshape(1, d), gt)


def _swiglu_kernel(te_ref, h_ref, wg_ref, wu_ref, wd_ref, o_ref):
    f = pl.program_id(1)
    h = h_ref[...]
    hg = jnp.dot(h, wg_ref[0], preferred_element_type=F32)
    hu = jnp.dot(h, wu_ref[0], preferred_element_type=F32)
    act = (hg * jax.nn.sigmoid(hg) * hu).astype(BF16)
    part = jnp.dot(act, wd_ref[0], preferred_element_type=F32)

    @pl.when(f == 0)
    def _():
        o_ref[...] = part

    @pl.when(f > 0)
    def _():
        o_ref[...] += part


def grouped_swiglu(h, tile_expert, wg, wu, wd, tm):
    p, d = h.shape
    ff = wg.shape[2]
    tf = _pick_tile(ff, (512, 1408, 256, 128))
    grid_spec = pltpu.PrefetchScalarGridSpec(
        num_scalar_prefetch=1, grid=(p // tm, ff // tf),
        in_specs=[pl.BlockSpec((tm, d), lambda i, f, te: (i, 0)),
                  pl.BlockSpec((1, d, tf), lambda i, f, te: (te[i], 0, f)),
                  pl.BlockSpec((1, d, tf), lambda i, f, te: (te[i], 0, f)),
                  pl.BlockSpec((1, tf, d), lambda i, f, te: (te[i], f, 0))],
        out_specs=pl.BlockSpec((tm, d), lambda i, f, te: (i, 0)))
    return pl.pallas_call(
        _swiglu_kernel, grid_spec=grid_spec,
        out_shape=jax.ShapeDtypeStruct((p, d), F32),
        compiler_params=_cparams(("parallel", "arbitrary")), name="swiglu",
    )(tile_expert, h, wg, wu, wd)


def _rope_tables(t):
    rows = t // GRID_W
    row = jnp.broadcast_to(jnp.arange(rows)[:, None], (rows, GRID_W)).reshape(-1).astype(F32)
    col = jnp.broadcast_to(jnp.arange(GRID_W)[None, :], (rows, GRID_W)).reshape(-1).astype(F32)
    inv = ROPE_BASE ** (-jnp.arange(0, AXIS_DIM, 2, dtype=F32) / AXIS_DIM)
    ar = row[:, None] * inv[None, :]
    ac = col[:, None] * inv[None, :]
    cos64 = jnp.concatenate([jnp.cos(ar), jnp.cos(ar), jnp.cos(ac), jnp.cos(ac)], axis=-1)
    sin64 = jnp.concatenate([jnp.sin(ar), jnp.sin(ar), jnp.sin(ac), jnp.sin(ac)], axis=-1)
    return jnp.tile(cos64, (1, ATT_HEADS)), jnp.tile(sin64, (1, ATT_HEADS))


def _rope_partner(w):
    d = w.shape[0]
    q = AXIS_DIM // 2
    wh = w.reshape(d, -1, 4, q)
    return jnp.stack([-wh[:, :, 1], wh[:, :, 0], -wh[:, :, 3], wh[:, :, 2]], axis=2).reshape(w.shape)


def _build_w_ext(w_in):
    d = w_in.shape[0]
    wq = w_in[:, OFF_Q:OFF_K]
    wk = w_in[:, OFF_K:OFF_V]
    o = OFF_R + 3 * RWKV_W
    parts = [(ZQ, wq), (ZQP, _rope_partner(wq)), (ZK, wk), (ZKP, _rope_partner(wk)),
             (ZV, w_in[:, OFF_V:OFF_R]),
             (ZWA, w_in[:, o:o + 2 * DECAY_RANK + 2 * ICLR_RANK]),
             (ZCA, w_in[:, OFF_CONV:OFF_CONV + CONV_W]), (ZCB, w_in[:, OFF_CONV + CONV_W:OFF_GATE]),
             (ZR, w_in[:, OFF_R:OFF_R + RWKV_W]), (ZRK, w_in[:, OFF_R + RWKV_W:OFF_R + 2 * RWKV_W]),
             (ZRV, w_in[:, OFF_R + 2 * RWKV_W:OFF_R + 3 * RWKV_W]),
             (ZG, w_in[:, OFF_GATE:IN_COLS]),
             (ZZG, w_in[:, OFF_CONV - GATE_RANK:OFF_CONV])]
    parts.sort(key=lambda p: p[0])
    cols, pos = [], 0
    for off, wpart in parts:
        if off > pos:
            cols.append(jnp.zeros((d, off - pos), w_in.dtype))
        cols.append(wpart)
        pos = off + wpart.shape[1]
    if pos < NZ:
        cols.append(jnp.zeros((d, NZ - pos), w_in.dtype))
    return jnp.concatenate(cols, axis=1).astype(BF16)


def _chain_layout(x0, x1):
    bsz, ttot = x0.shape[:2]
    both = jnp.stack([x0, x1], axis=0).reshape(2, bsz, ttot, RWKV_HEADS, RWKV_HEAD)
    return jnp.transpose(both, (2, 4, 0, 1, 3)).reshape(ttot, RWKV_HEAD, 2 * bsz * RWKV_HEADS)


def _bidirectional_wkv(p_l, p_c, k_a):
    bsz, l = p_c['r'].shape[:2]
    cat = lambda name: jnp.concatenate([p_c[name], p_l[name]], axis=1)
    rk = _chain_layout(cat('r'), cat('k'))
    vk = _chain_layout(cat('v'), cat('kk'))
    w = _chain_layout(cat('w0'), cat('w1'))
    an = _chain_layout(cat('an0'), cat('an1'))
    ka = jnp.tile(k_a.reshape(RWKV_HEADS, RWKV_HEAD).T, (1, 2 * bsz))
    yf, yb = wkv_scan(rk, vk, w, an, ka, l)
    half = bsz * RWKV_HEADS
    y = (yf[:, :, :half] + yb[:, :, half:]).reshape(-1, RWKV_HEAD, bsz, RWKV_HEADS)
    y = jnp.transpose(y, (2, 0, 3, 1)).reshape(bsz, -1, RWKV_W)
    return y[:, l:], y[:, :l]


def _moe_dispatch(logits, tm):
    m = logits.shape[0]
    top_v, top_i = lax.top_k(logits, TOP_K)
    wts = jax.nn.softmax(top_v, axis=-1)
    e_flat = top_i.reshape(-1)
    n_assign = e_flat.shape[0]
    onehot = (e_flat[:, None] == jnp.arange(N_EXPERTS)[None, :]).astype(jnp.int32)
    csum = jnp.cumsum(onehot, axis=0)
    rank = jnp.sum(onehot * csum, axis=1) - 1
    cnt = csum[-1]
    pcnt = ((cnt + tm - 1) // tm) * tm
    pend = jnp.cumsum(pcnt)
    pstart = pend - pcnt
    dest = pstart[e_flat] + rank
    p_max = n_assign + N_EXPERTS * tm
    row_token = jnp.zeros((p_max,), jnp.int32).at[dest].set(jnp.arange(n_assign, dtype=jnp.int32) // TOP_K)
    tile_start = jnp.arange(p_max // tm, dtype=jnp.int32) * tm
    tile_expert = jnp.minimum(jnp.searchsorted(pend, tile_start, side='right'), N_EXPERTS - 1).astype(jnp.int32)
    return row_token, tile_expert, dest.reshape(m, TOP_K), wts


def kernel(x, c, ctx, c_ctx, mod_w, mod_b, norm_g, w_in, attn_sink, rwkv_mu, rwkv_w0, rwkv_w2, rwkv_a0, rwkv_a2, rwkv_g2, rwkv_k_k, rwkv_k_a, rwkv_r_k, rwkv_lnx_g, rwkv_lnx_b, vres_v0, vres_w1, vres_w2, conv_dw_w, conv_dw_b, conv_ln_g, conv_ln_b, w_attn_o, w_rwkv_o, w_conv_o, w_out, ffn_wg, ffn_wu, ffn_wd, moe_router, moe_wg, moe_wu, moe_wd):
    bsz, t, d = x.shape
    l = ctx.shape[1]
    depth = mod_w.shape[0]
    n_lat, n_ctx = bsz * t, bsz * l
    cos_t, sin_t = _rope_tables(t)
    xl = x.reshape(n_lat, d)
    xc = ctx.reshape(n_ctx, d)
    n_rows = -(-(bsz + 1) // SUBLANES) * SUBLANES
    cond = jnp.zeros((n_rows, d), F32).at[:bsz].set(jax.nn.silu(c)).at[bsz].set(jax.nn.silu(c_ctx))
    v_first = None
    for i in range(depth):
        need_ctx = i < depth - 1
        mod = matmul_bias_f32(cond, mod_w[i], mod_b[i])
        mod_l = [mod[:bsz, k * d:(k + 1) * d].reshape(bsz, 1, d) for k in range(6)]
        mod_c = [mod[bsz:bsz + 1, k * d:(k + 1) * d].reshape(1, 1, d) for k in range(6)]
        lp = {'mu': rwkv_mu[i], 'w0': rwkv_w0[i], 'w2': rwkv_w2[i], 'a0': rwkv_a0[i], 'a2': rwkv_a2[i],
              'g2': rwkv_g2[i], 'k_k': rwkv_k_k[i], 'k_a': rwkv_k_a[i], 'r_k': rwkv_r_k[i],
              'lnx_g': rwkv_lnx_g[i], 'lnx_b': rwkv_lnx_b[i],
              'dw_w': conv_dw_w[i], 'dw_b': conv_dw_b[i], 'cln_g': conv_ln_g[i], 'cln_b': conv_ln_b[i]}
        if i > 0:
            lp['v0'], lp['v1'], lp['v2'] = vres_v0[i - 1], vres_w1[i - 1], vres_w2[i - 1]
        w_ext = _build_w_ext(w_in[i])
        wa, wr, wc, wo = (w.astype(BF16) for w in (w_attn_o[i], w_rwkv_o[i], w_conv_o[i], w_out[i]))

        zl = nm_matmul(xl, norm_g[i, 0], mod_l[1], mod_l[0], w_ext, t)
        zc = nm_matmul(xc, norm_g[i, 0], mod_c[1], mod_c[0], w_ext, n_ctx)
        zl3 = zl.reshape(bsz, t, NZ)
        zc3 = zc.reshape(bsz, l, NZ)
        att_l = windowed_attention(zl3, zc3, cos_t, sin_t, attn_sink[i])
        p_l = rwkv_prep(zl3, lp, None if v_first is None else v_first[0])
        p_c = rwkv_prep(zc3, lp, None if v_first is None else v_first[1])
        if i == 0:
            v_first = (p_l['v_in'], p_c['v_in'])
        y_l, y_c = _bidirectional_wkv(p_l, p_c, lp['k_a'])
        cv_l = conv_module(zl3, lp)
        flat = lambda a: a.reshape(-1, a.shape[-1])
        xl = merge_out(xl, flat(att_l), flat(y_l), flat(p_l['bv']), flat(p_l['g']), flat(cv_l), zl,
                       lp['lnx_g'], lp['lnx_b'], wa, wr, wc, wo, norm_g[i, 1], mod_l[2], t)
        if need_ctx:
            att_c = context_attention(zc3, attn_sink[i])
            cv_c = conv_module(zc3, lp)
            xc = merge_out(xc, flat(att_c), flat(y_c), flat(p_c['bv']), flat(p_c['g']), flat(cv_c), zc,
                           lp['lnx_g'], lp['lnx_b'], wa, wr, wc, wo, norm_g[i, 1], mod_c[2], n_ctx)

        j = i // 2
        if i % 2 == 0:
            wg, wu, wd = (w[j][None].astype(BF16) for w in (ffn_wg, ffn_wu, ffn_wd))
            streams = [(xl, mod_l, t)] + ([(xc, mod_c, n_ctx)] if need_ctx else [])
            outs = []
            for xs, ms, rpg in streams:
                tm = _pick_tile(rpg, (512, 256, 128, 64, 32, 16, 8))
                h = prenorm(xs, norm_g[i, 2], ms[4], ms[3], rpg)
                f = grouped_swiglu(h, jnp.zeros((xs.shape[0] // tm,), jnp.int32), wg, wu, wd, tm)
                outs.append(resnorm(xs, f, norm_g[i, 3], ms[5], rpg))
            xl = outs[0]
            if need_ctx:
                xc = outs[1]
        else:
            wg, wu, wd = (w[j].astype(BF16) for w in (moe_wg, moe_wu, moe_wd))
            streams = [(xl, mod_l, t)] + ([(xc, mod_c, n_ctx)] if need_ctx else [])
            outs = []
            for xs, ms, rpg in streams:
                tm = _pick_tile(rpg, (512, 256, 128, 64, 32, 16, 8))
                h, logits = prenorm(xs, norm_g[i, 2], ms[4], ms[3], rpg, router=moe_router[j])
                row_token, tile_expert, dest, wts = _moe_dispatch(logits, tm)
                yg = grouped_swiglu(jnp.take(h, row_token, axis=0), tile_expert, wg, wu, wd, tm)
                f = wts[:, 0:1] * jnp.take(yg, dest[:, 0], axis=0) + wts[:, 1:2] * jnp.take(yg, dest[:, 1], axis=0)
                outs.append(resnorm(xs, f, norm_g[i, 3], ms[5], rpg))
            xl = outs[0]
            if need_ctx:
                xc = outs[1]
    return xl.reshape(bsz, t, d)
```

```python
import functools
import math

import jax
import jax.numpy as jnp
from jax import lax
from jax.experimental import pallas as pl
from jax.experimental.pallas import tpu as pltpu

F32 = jnp.float32
BF16 = jnp.bfloat16

D_MODEL = 1024
GRID_W = 64
ATT_HEADS = 8
ATT_KV_HEADS = 2
ATT_GROUP = ATT_HEADS // ATT_KV_HEADS
HEAD_DIM = 64
WINDOW = 128
ATT_BLOCK = 128
ATT_SCALE = 1.0 / math.sqrt(HEAD_DIM)
ROPE_BASE = 10000.0
AXIS_DIM = HEAD_DIM // 2
RWKV_HEADS = 8
RWKV_HEAD = 64
RWKV_W = RWKV_HEADS * RWKV_HEAD
DECAY_RANK = 32
ICLR_RANK = 32
GATE_RANK = 96
LNX_EPS = 64e-5
CONV_W = 512
CONV_K = 31
CONV_LN_EPS = 1e-5
N_BRANCH = 3
N_EXPERTS = 8
TOP_K = 2
NORM_EPS = 1e-6
Q_W = ATT_HEADS * HEAD_DIM
KV_W = ATT_KV_HEADS * HEAD_DIM
OFF_Q = 0
OFF_K = OFF_Q + Q_W
OFF_V = OFF_K + KV_W
OFF_R = OFF_V + KV_W
OFF_CONV = OFF_R + 3 * RWKV_W + 2 * DECAY_RANK + 2 * ICLR_RANK + GATE_RANK
OFF_GATE = OFF_CONV + 2 * CONV_W
IN_COLS = OFF_GATE + N_BRANCH * D_MODEL

ZQ = 0
ZQP = 512
ZK = 1024
ZKP = 1152
ZV = 1280
ZWA = 1408
ZCA = 1536
ZCB = 2048
ZR = 2560
ZRK = 3072
ZRV = 3584
ZG = 4096
ZZG = 7168
NZ = 7680

VMEM_LIMIT = 56 * 1024 * 1024


def _cparams(sem):
    return pltpu.CompilerParams(dimension_semantics=sem, vmem_limit_bytes=VMEM_LIMIT)


def _pick_tile(n, candidates):
    for c in candidates:
        if n % c == 0:
            return c
    return n


def _matmul_bias_kernel(a_ref, w_ref, b_ref, o_ref):
    o_ref[...] = jnp.dot(a_ref[...], w_ref[...], preferred_element_type=F32,
                         precision=lax.Precision.HIGHEST) + b_ref[...]


def matmul_bias_f32(a, w, b):
    m, k = a.shape
    n = w.shape[1]
    tn = _pick_tile(n, (1536, 1024, 512, 256, 128))
    return pl.pallas_call(
        _matmul_bias_kernel,
        grid=(n // tn,),
        in_specs=[pl.BlockSpec((m, k), lambda j: (0, 0)),
                  pl.BlockSpec((k, tn), lambda j: (0, j)),
                  pl.BlockSpec((1, tn), lambda j: (0, j))],
        out_specs=pl.BlockSpec((m, tn), lambda j: (0, j)),
        out_shape=jax.ShapeDtypeStruct((m, n), F32),
        compiler_params=_cparams(("parallel",)),
        name="mod_matmul",
    )(a, w, b.reshape(1, n))


def _norm_mod(x, g, sc, sh):
    ms = jnp.mean(x * x, axis=-1, keepdims=True)
    return (x * lax.rsqrt(ms + NORM_EPS) * g) * (1.0 + sc) + sh


def _nm_matmul_kernel(x_ref, g_ref, sc_ref, sh_ref, w_ref, o_ref, h_ref):
    @pl.when(pl.program_id(1) == 0)
    def _():
        h_ref[...] = _norm_mod(x_ref[...], g_ref[...], sc_ref[0], sh_ref[0]).astype(BF16)

    o_ref[...] = jnp.dot(h_ref[...], w_ref[...], preferred_element_type=F32).astype(o_ref.dtype)


def nm_matmul(x, g, sc, sh, w, rows_per_group):
    m, d = x.shape
    n = w.shape[1]
    tm = _pick_tile(rows_per_group, (512, 256, 128, 64, 32, 16, 8))
    tn = _pick_tile(n, (1280, 1024, 512, 256, 128))
    grp = lambda i, j: ((i * tm) // rows_per_group, 0, 0)
    return pl.pallas_call(
        _nm_matmul_kernel,
        grid=(m // tm, n // tn),
        in_specs=[pl.BlockSpec((tm, d), lambda i, j: (i, 0)),
                  pl.BlockSpec((1, d), lambda i, j: (0, 0)),
                  pl.BlockSpec((1, 1, d), grp),
                  pl.BlockSpec((1, 1, d), grp),
                  pl.BlockSpec((d, tn), lambda i, j: (0, j))],
        out_specs=pl.BlockSpec((tm, tn), lambda i, j: (i, j)),
        out_shape=jax.ShapeDtypeStruct((m, n), F32),
        scratch_shapes=[pltpu.VMEM((tm, d), BF16)],
        compiler_params=_cparams(("parallel", "arbitrary")),
        name="in_proj",
    )(x, g.reshape(1, d), sc, sh, w)


def _prenorm_kernel(x_ref, g_ref, sc_ref, sh_ref, h_ref):
    h_ref[...] = _norm_mod(x_ref[...], g_ref[...], sc_ref[0], sh_ref[0]).astype(h_ref.dtype)


def _prenorm_router_kernel(x_ref, g_ref, sc_ref, sh_ref, r_ref, h_ref, l_ref):
    h = _norm_mod(x_ref[...], g_ref[...], sc_ref[0], sh_ref[0])
    h_ref[...] = h.astype(h_ref.dtype)
    l_ref[...] = jnp.dot(h, r_ref[...], preferred_element_type=F32, precision=lax.Precision.HIGHEST)


def prenorm(x, g, sc, sh, rows_per_group, router=None):
    m, d = x.shape
    tm = _pick_tile(rows_per_group, (512, 256, 128, 64, 32, 16, 8))
    grp = lambda i: ((i * tm) // rows_per_group, 0, 0)
    in_specs = [pl.BlockSpec((tm, d), lambda i: (i, 0)),
                pl.BlockSpec((1, d), lambda i: (0, 0)),
                pl.BlockSpec((1, 1, d), grp),
                pl.BlockSpec((1, 1, d), grp)]
    if router is None:
        return pl.pallas_call(
            _prenorm_kernel, grid=(m // tm,), in_specs=in_specs,
            out_specs=pl.BlockSpec((tm, d), lambda i: (i, 0)),
            out_shape=jax.ShapeDtypeStruct((m, d), BF16),
            compiler_params=_cparams(("parallel",)), name="prenorm",
        )(x, g.reshape(1, d), sc, sh)
    ne = router.shape[1]
    rpad = jnp.zeros((d, 128), F32).at[:, :ne].set(router)
    h, logits = pl.pallas_call(
        _prenorm_router_kernel, grid=(m // tm,),
        in_specs=in_specs + [pl.BlockSpec((d, 128), lambda i: (0, 0))],
        out_specs=[pl.BlockSpec((tm, d), lambda i: (i, 0)), pl.BlockSpec((tm, 128), lambda i: (i, 0))],
        out_shape=[jax.ShapeDtypeStruct((m, d), BF16), jax.ShapeDtypeStruct((m, 128), F32)],
        compiler_params=_cparams(("parallel",)), name="prenorm_router",
    )(x, g.reshape(1, d), sc, sh, rpad)
    return h, logits[:, :ne]


def _resnorm_kernel(x_ref, f_ref, g_ref, gt_ref, o_ref):
    f = f_ref[...]
    ms = jnp.mean(f * f, axis=-1, keepdims=True)
    o_ref[...] = x_ref[...] + gt_ref[0] * (f * lax.rsqrt(ms + NORM_EPS) * g_ref[...])


def resnorm(x, f, g, gt, rows_per_group):
    m, d = x.shape
    tm = _pick_tile(rows_per_group, (512, 256, 128, 64, 32, 16, 8))
    grp = lambda i: ((i * tm) // rows_per_group, 0, 0)
    return pl.pallas_call(
        _resnorm_kernel, grid=(m // tm,),
        in_specs=[pl.BlockSpec((tm, d), lambda i: (i, 0)),
                  pl.BlockSpec((tm, d), lambda i: (i, 0)),
                  pl.BlockSpec((1, d), lambda i: (0, 0)),
                  pl.BlockSpec((1, 1, d), grp)],
        out_specs=pl.BlockSpec((tm, d), lambda i: (i, 0)),
        out_shape=jax.ShapeDtypeStruct((m, d), F32),
        compiler_params=_cparams(("parallel",)), name="resnorm",
    )(x, f, g.reshape(1, d), gt)


def _softmax_pv(q4, sink_col, parts):
    scores = []
    m = sink_col
    for k, _, bias in parts:
        s = lax.dot_general(q4, k, (((1,), (1,)), ((), ())), preferred_element_type=F32) * ATT_SCALE
        if bias is not None:
            s = s + bias
        scores.append(s)
        m = jnp.maximum(m, jnp.max(s, axis=-1, keepdims=True))
    den = jnp.exp(sink_col - m)
    acc = None
    for s, (_, v, _) in zip(scores, parts):
        p = jnp.exp(s - m)
        den = den + jnp.sum(p, axis=-1, keepdims=True)
        pv = jnp.dot(p.astype(BF16), v, preferred_element_type=F32)
        acc = pv if acc is None else acc + pv
    return acc / den


def _attn_heads(q, parts_fn, sink_ref, o_ref):
    bq = q.shape[0]
    outs = []
    for h in range(ATT_KV_HEADS):
        q4 = jnp.concatenate(
            [q[:, (h * ATT_GROUP + g) * HEAD_DIM:(h * ATT_GROUP + g + 1) * HEAD_DIM] for g in range(ATT_GROUP)],
            axis=0).astype(BF16)
        sink_col = jnp.concatenate(
            [jnp.full((bq, 1), sink_ref[h * ATT_GROUP + g], F32) for g in range(ATT_GROUP)], axis=0)
        o4 = _softmax_pv(q4, sink_col, parts_fn(h))
        outs.extend(o4[g * bq:(g + 1) * bq] for g in range(ATT_GROUP))
    o_ref[0] = jnp.concatenate(outs, axis=-1)


def _hslice(t, h):
    return t[:, h * HEAD_DIM:(h + 1) * HEAD_DIM]


def _win_attn_kernel(sink_ref, q_ref, qp_ref, cq_ref, sq_ref,
                     k0_ref, k1_ref, k2_ref, kp0_ref, kp1_ref, kp2_ref,
                     v0_ref, v1_ref, v2_ref, c0_ref, c1_ref, c2_ref, s0_ref, s1_ref, s2_ref,
                     kc_ref, vc_ref, o_ref, *, nb):
    j = pl.program_id(1)
    q = q_ref[0] * cq_ref[...] + qp_ref[0] * sq_ref[...]
    kb = jnp.concatenate([k0_ref[0] * c0_ref[...] + kp0_ref[0] * s0_ref[...],
                          k1_ref[0] * c1_ref[...] + kp1_ref[0] * s1_ref[...],
                          k2_ref[0] * c2_ref[...] + kp2_ref[0] * s2_ref[...]], axis=0).astype(BF16)
    vb = jnp.concatenate([v0_ref[0], v1_ref[0], v2_ref[0]], axis=0).astype(BF16)
    kc = kc_ref[0].astype(BF16)
    vc = vc_ref[0].astype(BF16)
    rows = ATT_GROUP * ATT_BLOCK
    qi = lax.broadcasted_iota(jnp.int32, (rows, 3 * ATT_BLOCK), 0) % ATT_BLOCK + ATT_BLOCK
    ki = lax.broadcasted_iota(jnp.int32, (rows, 3 * ATT_BLOCK), 1)
    lo = jnp.where(j == 0, ATT_BLOCK, 0)
    hi = jnp.where(j == nb - 1, 2 * ATT_BLOCK, 3 * ATT_BLOCK)
    far = jnp.where(ki < lo, 4 * ATT_BLOCK, 0) + jnp.where(ki >= hi, 4 * ATT_BLOCK, 0)
    bias = jnp.where(jnp.abs(qi - ki) + far <= WINDOW, 0.0, -jnp.inf).astype(F32)

    def parts(h):
        return [(_hslice(kc, h), _hslice(vc, h), None), (_hslice(kb, h), _hslice(vb, h), bias)]

    _attn_heads(q, parts, sink_ref, o_ref)


def _ctx_attn_kernel(sink_ref, q_ref, kc_ref, vc_ref, o_ref):
    kc = kc_ref[0].astype(BF16)
    vc = vc_ref[0].astype(BF16)
    _attn_heads(q_ref[0], lambda h: [(_hslice(kc, h), _hslice(vc, h), None)], sink_ref, o_ref)


def windowed_attention(zl3, zc3, cos_t, sin_t, sink):
    b, t, _ = zl3.shape
    l = zc3.shape[1]
    nb = t // ATT_BLOCK
    kcol, kpcol, vcol = ZK // KV_W, ZKP // KV_W, ZV // KV_W
    prev = lambda j: jnp.maximum(j - 1, 0)
    nxt = lambda j: jnp.minimum(j + 1, nb - 1)

    def zspec(width, rowf, col):
        return pl.BlockSpec((1, ATT_BLOCK, width), lambda bi, j: (bi, rowf(j), col))

    def tspec(width, rowf):
        return pl.BlockSpec((ATT_BLOCK, width), lambda bi, j: (rowf(j), 0))

    same = lambda j: j
    in_specs = [pl.BlockSpec(memory_space=pltpu.SMEM),
                zspec(Q_W, same, ZQ // Q_W), zspec(Q_W, same, ZQP // Q_W), tspec(Q_W, same), tspec(Q_W, same)]
    args = [sink, zl3, zl3, cos_t, sin_t]
    for col in (kcol, kpcol, vcol):
        in_specs += [zspec(KV_W, prev, col), zspec(KV_W, same, col), zspec(KV_W, nxt, col)]
        args += [zl3, zl3, zl3]
    for tab in (cos_t, sin_t):
        in_specs += [tspec(KV_W, prev), tspec(KV_W, same), tspec(KV_W, nxt)]
        args += [tab, tab, tab]
    in_specs += [pl.BlockSpec((1, l, KV_W), lambda bi, j: (bi, 0, kcol)),
                 pl.BlockSpec((1, l, KV_W), lambda bi, j: (bi, 0, vcol))]
    args += [zc3, zc3]
    return pl.pallas_call(
        functools.partial(_win_attn_kernel, nb=nb),
        grid=(b, nb), in_specs=in_specs,
        out_specs=pl.BlockSpec((1, ATT_BLOCK, Q_W), lambda bi, j: (bi, j, 0)),
        out_shape=jax.ShapeDtypeStruct((b, t, Q_W), F32),
        compiler_params=_cparams(("parallel", "parallel")), name="win_attn",
    )(*args)


def context_attention(zc3, sink):
    b, l, _ = zc3.shape
    return pl.pallas_call(
        _ctx_attn_kernel, grid=(b,),
        in_specs=[pl.BlockSpec(memory_space=pltpu.SMEM),
                  pl.BlockSpec((1, l, Q_W), lambda bi: (bi, 0, ZQ // Q_W)),
                  pl.BlockSpec((1, l, KV_W), lambda bi: (bi, 0, ZK // KV_W)),
                  pl.BlockSpec((1, l, KV_W), lambda bi: (bi, 0, ZV // KV_W))],
        out_specs=pl.BlockSpec((1, l, Q_W), lambda bi: (bi, 0, 0)),
        out_shape=jax.ShapeDtypeStruct((b, l, Q_W), F32),
        compiler_params=_cparams(("parallel",)), name="ctx_attn",
    )(sink, zc3, zc3, zc3)


HALO = 8


def _head_sums(x, e_ref):
    hi = x.astype(BF16)
    lo = (x - hi.astype(F32)).astype(BF16)
    return (jnp.dot(hi, e_ref[...], preferred_element_type=F32)
            + jnp.dot(lo, e_ref[...], preferred_element_type=F32))


def _shift_lerp_tile(cur_ref, prev_ref, next_ref, mu_ref, buf_ref, first, last):
    tt = cur_ref.shape[1]
    z = cur_ref[0]
    buf_ref[0:HALO] = jnp.where(first, 0.0, prev_ref[0])
    buf_ref[HALO:HALO + tt] = z
    buf_ref[HALO + tt:2 * HALO + tt] = jnp.where(last, 0.0, next_ref[0])
    nb = buf_ref[HALO - 1:HALO - 1 + tt] + buf_ref[HALO + 1:HALO + 1 + tt]
    return z + mu_ref[...] * (0.5 * nb - z)


def _split_bf16(a):
    hi = a.astype(BF16)
    return hi, (a - hi.astype(F32)).astype(BF16)


def _dot_3pass(a, b):
    a_hi, a_lo = _split_bf16(a)
    b_hi, b_lo = _split_bf16(b)
    return (jnp.dot(a_hi, b_hi, preferred_element_type=F32) + jnp.dot(a_lo, b_hi, preferred_element_type=F32)
            + jnp.dot(a_hi, b_lo, preferred_element_type=F32))


def _rwkv_prep_kernel(*refs, has_vres, n_alias):
    (wa_c, wa_p, wa_n, r_c, r_p, r_n, k_c, k_p, k_n, v_c, v_p, v_n, g_c, g_p, g_n,
     mu_wa, mu_r, mu_k, mu_v, mu_g, w0_ref, w2_ref, a0_ref, a2_ref, g2_ref, kk_ref, ka_ref, rk_ref, e_ref) = refs[:29]
    pos = 29
    if has_vres:
        vf_ref, v0_ref, v1_ref, v2_ref = refs[pos:pos + 4]
        pos += 4
    pos += n_alias
    (rk_o, vk_o, w_o, an_o, bv_o, g_o) = refs[pos:pos + 6]
    pos += 6
    if not has_vres:
        vin_o = refs[pos]
        pos += 1
    buf_wa, buf_r, buf_k, buf_v, buf_g = refs[pos:pos + 5]

    i = pl.program_id(1)
    first = i == 0
    last = i == pl.num_programs(1) - 1
    zwa = _shift_lerp_tile(wa_c, wa_p, wa_n, mu_wa, buf_wa, first, last)
    r = _shift_lerp_tile(r_c, r_p, r_n, mu_r, buf_r, first, last)
    k = _shift_lerp_tile(k_c, k_p, k_n, mu_k, buf_k, first, last)
    v = _shift_lerp_tile(v_c, v_p, v_n, mu_v, buf_v, first, last)
    zg = _shift_lerp_tile(g_c, g_p, g_n, mu_g, buf_g, first, last)

    if has_vres:
        mix = _dot_3pass(_dot_3pass(v, v1_ref[...]), v2_ref[...])
        v = v + (vf_ref[0] - v) * jax.nn.sigmoid(v0_ref[...] + mix)
    else:
        vin_o[0] = v
    kk = k * kk_ref[...]
    kk = kk / jnp.maximum(jnp.sqrt(_head_sums(kk * kk, e_ref)), 1e-12)
    tz = jnp.tanh(zwa)
    ksum = None
    for d in range(2):
        w_log = -jax.nn.softplus(-(w0_ref[d:d + 1, :] + _dot_3pass(tz, w2_ref[d]))) - 0.5
        w_o[d, 0] = jnp.exp(-jnp.exp(w_log))
        a = jax.nn.sigmoid(a0_ref[d:d + 1, :] + _dot_3pass(zwa, a2_ref[d]))
        an_o[d, 0] = -a
        kd = k * (1.0 + (a - 1.0) * ka_ref[...])
        ksum = kd if ksum is None else ksum + kd
    rk_o[0, 0] = r
    rk_o[1, 0] = k
    vk_o[0, 0] = v
    vk_o[1, 0] = kk
    bv_o[0] = _head_sums(r * ksum * rk_ref[...], e_ref) * v
    g_o[0] = _dot_3pass(jax.nn.sigmoid(zg), g2_ref[...])


def _head_ones():
    idx = jnp.arange(RWKV_W) // RWKV_HEAD
    return (idx[:, None] == idx[None, :]).astype(BF16)


def rwkv_prep(z3, lp, v_first, ttot, row_off, pair_bufs):
    b, tx, _ = z3.shape
    tt = _pick_tile(math.gcd(tx, row_off) if row_off else tx, (256, 128, 64, 32, 16, 8))
    nh = tt // HALO
    nblk = tx // HALO
    has_vres = v_first is not None
    lanes = 128

    def trio(width, col):
        return [pl.BlockSpec((1, tt, width), lambda bi, i: (bi, i, col)),
                pl.BlockSpec((1, HALO, width), lambda bi, i: (bi, jnp.maximum(i * nh - 1, 0), col)),
                pl.BlockSpec((1, HALO, width), lambda bi, i: (bi, jnp.minimum((i + 1) * nh, nblk - 1), col))]

    def full(shape):
        return pl.BlockSpec(shape, lambda bi, i: (0,) * len(shape))

    mu = lp['mu']
    o = 3 * RWKV_W
    nwa = 2 * DECAY_RANK + 2 * ICLR_RANK
    mu_g = jnp.zeros((lanes,), F32).at[:GATE_RANK].set(mu[o + nwa:])
    w2f = jnp.zeros((2, lanes, RWKV_W), F32)
    a2f = jnp.zeros((2, lanes, RWKV_W), F32)
    for d in range(2):
        w2f = w2f.at[d, d * DECAY_RANK:(d + 1) * DECAY_RANK].set(lp['w2'][d])
        a2f = a2f.at[d, 2 * DECAY_RANK + d * ICLR_RANK:2 * DECAY_RANK + (d + 1) * ICLR_RANK].set(lp['a2'][d])
    g2f = jnp.zeros((lanes, RWKV_W), F32).at[:GATE_RANK].set(lp['g2'])
    row = lambda a: a.reshape(1, -1)
    in_specs = (trio(lanes, ZWA // lanes) + trio(RWKV_W, ZR // RWKV_W) + trio(RWKV_W, ZRK // RWKV_W)
                + trio(RWKV_W, ZRV // RWKV_W) + trio(lanes, ZZG // lanes)
                + [full((1, lanes)), full((1, RWKV_W)), full((1, RWKV_W)), full((1, RWKV_W)), full((1, lanes)),
                   full((2, RWKV_W)), full((2, lanes, RWKV_W)), full((2, RWKV_W)), full((2, lanes, RWKV_W)),
                   full((lanes, RWKV_W)), full((1, RWKV_W)), full((1, RWKV_W)), full((1, RWKV_W)),
                   full((RWKV_W, RWKV_W))])
    args = [z3] * 15 + [row(mu[o:o + nwa]), row(mu[:RWKV_W]), row(mu[RWKV_W:2 * RWKV_W]), row(mu[2 * RWKV_W:o]),
                        row(mu_g), lp['w0'], w2f, lp['a0'], a2f, g2f, row(lp['k_k']), row(lp['k_a']),
                        row(lp['r_k']), _head_ones()]
    tok = pl.BlockSpec((1, tt, RWKV_W), lambda bi, i: (bi, i, 0))
    if has_vres:
        rank = lp['v1'].shape[1]
        v1f = jnp.zeros((RWKV_W, lanes), F32).at[:, :rank].set(lp['v1'])
        v2f = jnp.zeros((lanes, RWKV_W), F32).at[:rank].set(lp['v2'])
        in_specs += [tok, full((1, RWKV_W)), full((RWKV_W, lanes)), full((lanes, RWKV_W))]
        args += [v_first, row(lp['v0']), v1f, v2f]
    n_alias = 0 if pair_bufs is None else len(pair_bufs)
    aliases = {}
    if pair_bufs is not None:
        aliases = {len(args) + n: n for n in range(n_alias)}
        in_specs += [pl.BlockSpec(memory_space=pl.ANY)] * n_alias
        args += list(pair_bufs)
    blk_off = row_off // tt
    pair = pl.BlockSpec((2, 1, tt, RWKV_W), lambda bi, i: (0, bi, i + blk_off, 0))
    pair_shape = jax.ShapeDtypeStruct((2, b, ttot, RWKV_W), F32)
    tok_shape = jax.ShapeDtypeStruct((b, tx, RWKV_W), F32)
    n_tok = 2 if has_vres else 3
    outs = pl.pallas_call(
        functools.partial(_rwkv_prep_kernel, has_vres=has_vres, n_alias=n_alias),
        grid=(b, tx // tt), in_specs=in_specs,
        out_specs=[pair] * 4 + [tok] * n_tok,
        out_shape=[pair_shape] * 4 + [tok_shape] * n_tok,
        input_output_aliases=aliases,
        scratch_shapes=[pltpu.VMEM((tt + 2 * HALO, lanes), F32)] + [pltpu.VMEM((tt + 2 * HALO, RWKV_W), F32)] * 3
                       + [pltpu.VMEM((tt + 2 * HALO, lanes), F32)],
        compiler_params=_cparams(("parallel", "parallel")), name="rwkv_prep",
    )(*args)
    res = {'pairs': tuple(outs[:4]), 'bv': outs[4], 'g': outs[5]}
    if not has_vres:
        res['v_in'] = outs[6]
    return res


CONV_HALO = 16
CONV_ROWS = 32


def _conv_kernel(a_c, a_p, a_n, b_c, b_p, b_n, w_ref, bias_ref, lg_ref, lb_ref, o_ref, buf_ref, win_ref):
    i = pl.program_id(1)
    first = i == 0
    last = i == pl.num_programs(1) - 1
    tt = a_c.shape[1]
    glu = lambda a, b: a * jax.nn.sigmoid(b)
    buf_ref[0:CONV_HALO] = jnp.where(first, 0.0, glu(a_p[0], b_p[0]))
    buf_ref[CONV_HALO:CONV_HALO + tt] = glu(a_c[0], b_c[0])
    buf_ref[CONV_HALO + tt:2 * CONV_HALO + tt] = jnp.where(last, 0.0, glu(a_n[0], b_n[0]))
    half = CONV_K // 2
    span = CONV_ROWS + ((CONV_K - 1) // SUBLANES) * SUBLANES
    for r0 in range(0, tt, CONV_ROWS):
        base = CONV_HALO - half + r0
        acc = bias_ref[...]
        for ph in range(SUBLANES):
            win_ref[...] = buf_ref[base + ph:base + ph + span]
            for j in range(ph, CONV_K, SUBLANES):
                acc = acc + win_ref[j - ph:j - ph + CONV_ROWS] * w_ref[j:j + 1, :]
        m = jnp.mean(acc, axis=-1, keepdims=True)
        cen = acc - m
        var = jnp.mean(cen * cen, axis=-1, keepdims=True)
        y = cen * lax.rsqrt(var + CONV_LN_EPS) * lg_ref[...] + lb_ref[...]
        o_ref[0, r0:r0 + CONV_ROWS, :] = y * jax.nn.sigmoid(y)


def conv_module(z3, lp):
    b, tx, _ = z3.shape
    tt = _pick_tile(tx, (256, 128, 64, 32))
    nh = tt // CONV_HALO
    nblk = tx // CONV_HALO

    def trio(col):
        return [pl.BlockSpec((1, tt, CONV_W), lambda bi, i: (bi, i, col)),
                pl.BlockSpec((1, CONV_HALO, CONV_W), lambda bi, i: (bi, jnp.maximum(i * nh - 1, 0), col)),
                pl.BlockSpec((1, CONV_HALO, CONV_W), lambda bi, i: (bi, jnp.minimum((i + 1) * nh, nblk - 1), col))]

    kpad = -(-CONV_K // SUBLANES) * SUBLANES
    wpad = jnp.zeros((kpad, CONV_W), F32).at[:CONV_K].set(lp['dw_w'])
    vec = pl.BlockSpec((1, CONV_W), lambda bi, i: (0, 0))
    return pl.pallas_call(
        _conv_kernel, grid=(b, tx // tt),
        in_specs=trio(ZCA // CONV_W) + trio(ZCB // CONV_W)
                 + [pl.BlockSpec((kpad, CONV_W), lambda bi, i: (0, 0)), vec, vec, vec],
        out_specs=pl.BlockSpec((1, tt, CONV_W), lambda bi, i: (bi, i, 0)),
        out_shape=jax.ShapeDtypeStruct((b, tx, CONV_W), F32),
        scratch_shapes=[pltpu.VMEM((tt + 2 * CONV_HALO, CONV_W), F32),
                        pltpu.VMEM((CONV_ROWS + ((CONV_K - 1) // SUBLANES) * SUBLANES, CONV_W), F32)],
        compiler_params=_cparams(("parallel", "parallel")), name="conv_module",
    )(z3, z3, z3, z3, z3, z3, wpad, lp['dw_b'].reshape(1, -1), lp['cln_g'].reshape(1, -1), lp['cln_b'].reshape(1, -1))


SUBLANES = 8


def _wkv_scan_kernel(rk_f, vk_f, w_f, an_f, rk_b, vk_b, w_b, an_b, ka_ref, yf_ref, yb_ref, s_ref, *, tt):
    @pl.when(pl.program_id(0) == 0)
    def _():
        s_ref[...] = jnp.zeros_like(s_ref)

    nv, nk, lanes = s_ref.shape
    half = lanes // 2
    fwd = lax.broadcasted_iota(jnp.int32, (nk, lanes), 1) < half
    ka = ka_ref[...]

    def step(t, carry):
        tb = tt - 1 - t
        rkf, rkb, vkf, vkb = rk_f[t], rk_b[tb], vk_f[t], vk_b[tb]
        r = jnp.where(fwd, rkf, pltpu.roll(rkb, half, 1))
        k = jnp.where(fwd, pltpu.roll(rkf, half, 1), rkb)
        v = jnp.where(fwd, vkf, pltpu.roll(vkb, half, 1))
        kk = jnp.where(fwd, pltpu.roll(vkf, half, 1), vkb)
        w = jnp.where(fwd, w_f[t], w_b[tb])
        an = jnp.where(fwd, an_f[t], an_b[tb])
        kd = k * (1.0 - (an + 1.0) * ka)
        bbn = kk * an
        for g in range(nv // SUBLANES):
            ys = []
            for i in range(SUBLANES):
                vi = g * SUBLANES + i
                s_old = s_ref[vi]
                sa = jnp.sum(s_old * kk, axis=0, keepdims=True)
                s_new = s_old * w + sa * bbn + v[vi:vi + 1, :] * kd
                s_ref[vi] = s_new
                ys.append(jnp.sum(s_new * r, axis=0, keepdims=True))
            yg = jnp.concatenate(ys, axis=0)
            yf_ref[t, g * SUBLANES:(g + 1) * SUBLANES, :] = yg
            yb_ref[tb, g * SUBLANES:(g + 1) * SUBLANES, :] = yg
        return carry

    lax.fori_loop(0, tt, step, 0)


def wkv_scan(rk, vk, w, an, ka, n_ctx_steps):
    ttot, n, ch = rk.shape
    tt = _pick_tile(math.gcd(n_ctx_steps, ttot - n_ctx_steps), (32, 16, 8, 4, 2, 1))
    nt, nc = ttot // tt, n_ctx_steps // tt
    fspec = pl.BlockSpec((tt, n, ch), lambda i: (i, 0, 0))
    bspec = pl.BlockSpec((tt, n, ch), lambda i: (jnp.where(i < nc, nc - 1 - i, nt - 1 - (i - nc)), 0, 0))
    return pl.pallas_call(
        functools.partial(_wkv_scan_kernel, tt=tt),
        grid=(nt,), in_specs=[fspec] * 4 + [bspec] * 4 + [pl.BlockSpec((n, ch), lambda i: (0, 0))],
        out_specs=[fspec, bspec],
        out_shape=[jax.ShapeDtypeStruct((ttot, n, ch), F32)] * 2,
        scratch_shapes=[pltpu.VMEM((n, n, ch), F32)],
        compiler_params=_cparams(("arbitrary",)), name="wkv_scan",
    )(rk, vk, w, an, rk, vk, w, an, ka)


def _merge_kernel(x_ref, att_ref, y_ref, bv_ref, rg_ref, cv_ref, g0_ref, g1_ref, g2_ref,
                  lg_ref, lb_ref, e_ref, wa_ref, wr_ref, wc_ref, wo_ref, ng_ref, gt_ref, o_ref):
    def proj(a, w_ref):
        return jnp.dot(a.astype(BF16), w_ref[...], preferred_element_type=F32)

    y = y_ref[...]
    cen = y - _head_sums(y, e_ref) * (1.0 / RWKV_HEAD)
    var = _head_sums(cen * cen, e_ref) * (1.0 / RWKV_HEAD)
    rw = (cen * lax.rsqrt(var + LNX_EPS) * lg_ref[...] + lb_ref[...] + bv_ref[...]) * rg_ref[...]
    m = (jax.nn.sigmoid(g0_ref[...]) * proj(att_ref[...], wa_ref)
         + jax.nn.sigmoid(g1_ref[...]) * proj(rw, wr_ref)
         + jax.nn.sigmoid(g2_ref[...]) * proj(cv_ref[...], wc_ref))
    ml = proj(m, wo_ref)
    ms = jnp.mean(ml * ml, axis=-1, keepdims=True)
    o_ref[...] = x_ref[...] + gt_ref[0] * (ml * lax.rsqrt(ms + NORM_EPS) * ng_ref[...])


def merge_out(x, att, y, bv, rg, cv, z, lnx_g, lnx_b, wa, wr, wc, wo, ng, gt, rows_per_group):
    m, d = x.shape
    tm = _pick_tile(rows_per_group, (512, 256, 128, 64, 32, 16, 8))
    row = lambda i: (i, 0)
    const = lambda i: (0, 0)
    gcol = ZG // d
    br = pl.BlockSpec((tm, RWKV_W), row)
    return pl.pallas_call(
        _merge_kernel, grid=(m // tm,),
        in_specs=[pl.BlockSpec((tm, d), row),
                  pl.BlockSpec((tm, Q_W), row), br, br, br, pl.BlockSpec((tm, CONV_W), row),
                  pl.BlockSpec((tm, d), lambda i: (i, gcol)),
                  pl.BlockSpec((tm, d), lambda i: (i, gcol + 1)),
                  pl.BlockSpec((tm, d), lambda i: (i, gcol + 2)),
                  pl.BlockSpec((1, RWKV_W), const), pl.BlockSpec((1, RWKV_W), const),
                  pl.BlockSpec((RWKV_W, RWKV_W), const),
                  pl.BlockSpec((Q_W, d), const), pl.BlockSpec((RWKV_W, d), const), pl.BlockSpec((CONV_W, d), const),
                  pl.BlockSpec((d, d), const), pl.BlockSpec((1, d), const),
                  pl.BlockSpec((1, 1, d), lambda i: ((i * tm) // rows_per_group, 0, 0))],
        out_specs=pl.BlockSpec((tm, d), row),
        out_shape=jax.ShapeDtypeStruct((m, d), F32),
        compiler_params=_cparams(("parallel",)), name="merge_out",
    )(x, att, y, bv, rg, cv, z, z, z, lnx_g.reshape(1, -1), lnx_b.reshape(1, -1), _head_ones(),
      wa, wr, wc, wo, ng.reshape(1, d), gt)


def _swiglu_kernel(te_ref, nu_ref, h_ref, wg_ref, wu_ref, wd_ref, o_ref):
    f = pl.program_id(1)

    @pl.when(pl.program_id(0) < nu_ref[0])
    def _():
        h = h_ref[...]
        hg = jnp.dot(h, wg_ref[0].astype(BF16), preferred_element_type=F32)
        hu = jnp.dot(h, wu_ref[0].astype(BF16), preferred_element_type=F32)
        act = (hg * jax.nn.sigmoid(hg) * hu).astype(BF16)
        part = jnp.dot(act, wd_ref[0].astype(BF16), preferred_element_type=F32)

        @pl.when(f == 0)
        def _():
            o_ref[...] = part

        @pl.when(f > 0)
        def _():
            o_ref[...] += part


def grouped_swiglu(h, tile_expert, n_used, wg, wu, wd, tm):
    p, d = h.shape
    ff = wg.shape[2]
    tf = _pick_tile(ff, (512, 1408, 256, 128))
    grid_spec = pltpu.PrefetchScalarGridSpec(
        num_scalar_prefetch=2, grid=(p // tm, ff // tf),
        in_specs=[pl.BlockSpec((tm, d), lambda i, f, te, nu: (i, 0)),
                  pl.BlockSpec((1, d, tf), lambda i, f, te, nu: (te[i], 0, f)),
                  pl.BlockSpec((1, d, tf), lambda i, f, te, nu: (te[i], 0, f)),
                  pl.BlockSpec((1, tf, d), lambda i, f, te, nu: (te[i], f, 0))],
        out_specs=pl.BlockSpec((tm, d), lambda i, f, te, nu: (i, 0)))
    return pl.pallas_call(
        _swiglu_kernel, grid_spec=grid_spec,
        out_shape=jax.ShapeDtypeStruct((p, d), F32),
        compiler_params=_cparams(("parallel", "arbitrary")), name="swiglu",
    )(tile_expert, n_used, h, wg, wu, wd)


def _rope_tables(t):
    rows = t // GRID_W
    row = jnp.broadcast_to(jnp.arange(rows)[:, None], (rows, GRID_W)).reshape(-1).astype(F32)
    col = jnp.broadcast_to(jnp.arange(GRID_W)[None, :], (rows, GRID_W)).reshape(-1).astype(F32)
    inv = ROPE_BASE ** (-jnp.arange(0, AXIS_DIM, 2, dtype=F32) / AXIS_DIM)
    ar = row[:, None] * inv[None, :]
    ac = col[:, None] * inv[None, :]
    cos64 = jnp.concatenate([jnp.cos(ar), jnp.cos(ar), jnp.cos(ac), jnp.cos(ac)], axis=-1)
    sin64 = jnp.concatenate([jnp.sin(ar), jnp.sin(ar), jnp.sin(ac), jnp.sin(ac)], axis=-1)
    return jnp.tile(cos64, (1, ATT_HEADS)), jnp.tile(sin64, (1, ATT_HEADS))


def _rope_partner(w):
    d = w.shape[0]
    q = AXIS_DIM // 2
    wh = w.reshape(d, -1, 4, q)
    return jnp.stack([-wh[:, :, 1], wh[:, :, 0], -wh[:, :, 3], wh[:, :, 2]], axis=2).reshape(w.shape)


def _build_w_ext(w_in):
    d = w_in.shape[0]
    wq = w_in[:, OFF_Q:OFF_K]
    wk = w_in[:, OFF_K:OFF_V]
    o = OFF_R + 3 * RWKV_W
    parts = [(ZQ, wq), (ZQP, _rope_partner(wq)), (ZK, wk), (ZKP, _rope_partner(wk)),
             (ZV, w_in[:, OFF_V:OFF_R]),
             (ZWA, w_in[:, o:o + 2 * DECAY_RANK + 2 * ICLR_RANK]),
             (ZCA, w_in[:, OFF_CONV:OFF_CONV + CONV_W]), (ZCB, w_in[:, OFF_CONV + CONV_W:OFF_GATE]),
             (ZR, w_in[:, OFF_R:OFF_R + RWKV_W]), (ZRK, w_in[:, OFF_R + RWKV_W:OFF_R + 2 * RWKV_W]),
             (ZRV, w_in[:, OFF_R + 2 * RWKV_W:OFF_R + 3 * RWKV_W]),
             (ZG, w_in[:, OFF_GATE:IN_COLS]),
             (ZZG, w_in[:, OFF_CONV - GATE_RANK:OFF_CONV])]
    parts.sort(key=lambda p: p[0])
    cols, pos = [], 0
    for off, wpart in parts:
        if off > pos:
            cols.append(jnp.zeros((d, off - pos), w_in.dtype))
        cols.append(wpart)
        pos = off + wpart.shape[1]
    if pos < NZ:
        cols.append(jnp.zeros((d, NZ - pos), w_in.dtype))
    return jnp.concatenate(cols, axis=1).astype(BF16)


def _chain_layout(pair):
    _, bsz, ttot, _ = pair.shape
    both = pair.reshape(2, bsz, ttot, RWKV_HEADS, RWKV_HEAD)
    return jnp.transpose(both, (2, 4, 0, 1, 3)).reshape(ttot, RWKV_HEAD, 2 * bsz * RWKV_HEADS)


def _bidirectional_wkv(pairs, k_a, l):
    bsz = pairs[0].shape[1]
    rk, vk, w, an = (_chain_layout(p) for p in pairs)
    ka = jnp.tile(k_a.reshape(RWKV_HEADS, RWKV_HEAD).T, (1, 2 * bsz))
    yf, yb = wkv_scan(rk, vk, w, an, ka, l)
    half = bsz * RWKV_HEADS
    y = (yf[:, :, :half] + yb[:, :, half:]).reshape(-1, RWKV_HEAD, bsz, RWKV_HEADS)
    y = jnp.transpose(y, (2, 0, 3, 1)).reshape(bsz, -1, RWKV_W)
    return y[:, l:], y[:, :l]


def _moe_dispatch(logits, tm):
    m = logits.shape[0]
    top_v, top_i = lax.top_k(logits, TOP_K)
    wts = jax.nn.softmax(top_v, axis=-1)
    e_flat = top_i.reshape(-1)
    n_assign = e_flat.shape[0]
    onehot = (e_flat[:, None] == jnp.arange(N_EXPERTS)[None, :]).astype(jnp.int32)
    csum = jnp.cumsum(onehot, axis=0)
    rank = jnp.sum(onehot * csum, axis=1) - 1
    cnt = csum[-1]
    pcnt = ((cnt + tm - 1) // tm) * tm
    pend = jnp.cumsum(pcnt)
    pstart = pend - pcnt
    dest = pstart[e_flat] + rank
    p_max = n_assign + N_EXPERTS * tm
    row_token = jnp.zeros((p_max,), jnp.int32).at[dest].set(jnp.arange(n_assign, dtype=jnp.int32) // TOP_K)
    tile_start = jnp.arange(p_max // tm, dtype=jnp.int32) * tm
    tile_expert = jnp.minimum(jnp.searchsorted(pend, tile_start, side='right'), N_EXPERTS - 1).astype(jnp.int32)
    n_used = (pend[-1:] // tm).astype(jnp.int32)
    return row_token, tile_expert, n_used, dest.reshape(m, TOP_K), wts


def kernel(x, c, ctx, c_ctx, mod_w, mod_b, norm_g, w_in, attn_sink, rwkv_mu, rwkv_w0, rwkv_w2, rwkv_a0, rwkv_a2, rwkv_g2, rwkv_k_k, rwkv_k_a, rwkv_r_k, rwkv_lnx_g, rwkv_lnx_b, vres_v0, vres_w1, vres_w2, conv_dw_w, conv_dw_b, conv_ln_g, conv_ln_b, w_attn_o, w_rwkv_o, w_conv_o, w_out, ffn_wg, ffn_wu, ffn_wd, moe_router, moe_wg, moe_wu, moe_wd):
    bsz, t, d = x.shape
    l = ctx.shape[1]
    depth = mod_w.shape[0]
    n_lat, n_ctx = bsz * t, bsz * l
    cos_t, sin_t = _rope_tables(t)
    xl = x.reshape(n_lat, d)
    xc = ctx.reshape(n_ctx, d)
    n_rows = -(-(bsz + 1) // SUBLANES) * SUBLANES
    cond = jnp.zeros((n_rows, d), F32).at[:bsz].set(jax.nn.silu(c)).at[bsz].set(jax.nn.silu(c_ctx))
    v_first = None
    for i in range(depth):
        need_ctx = i < depth - 1
        mod = matmul_bias_f32(cond, mod_w[i], mod_b[i])
        mod_l = [mod[:bsz, k * d:(k + 1) * d].reshape(bsz, 1, d) for k in range(6)]
        mod_c = [mod[bsz:bsz + 1, k * d:(k + 1) * d].reshape(1, 1, d) for k in range(6)]
        lp = {'mu': rwkv_mu[i], 'w0': rwkv_w0[i], 'w2': rwkv_w2[i], 'a0': rwkv_a0[i], 'a2': rwkv_a2[i],
              'g2': rwkv_g2[i], 'k_k': rwkv_k_k[i], 'k_a': rwkv_k_a[i], 'r_k': rwkv_r_k[i],
              'lnx_g': rwkv_lnx_g[i], 'lnx_b': rwkv_lnx_b[i],
              'dw_w': conv_dw_w[i], 'dw_b': conv_dw_b[i], 'cln_g': conv_ln_g[i], 'cln_b': conv_ln_b[i]}
        if i > 0:
            lp['v0'], lp['v1'], lp['v2'] = vres_v0[i - 1], vres_w1[i - 1], vres_w2[i - 1]
        w_ext = _build_w_ext(w_in[i])
        wa, wr, wc, wo = (w.astype(BF16) for w in (w_attn_o[i], w_rwkv_o[i], w_conv_o[i], w_out[i]))

        zl = nm_matmul(xl, norm_g[i, 0], mod_l[1], mod_l[0], w_ext, t)
        zc = nm_matmul(xc, norm_g[i, 0], mod_c[1], mod_c[0], w_ext, n_ctx)
        zl3 = zl.reshape(bsz, t, NZ)
        zc3 = zc.reshape(bsz, l, NZ)
        att_l = windowed_attention(zl3, zc3, cos_t, sin_t, attn_sink[i])
        p_c = rwkv_prep(zc3, lp, None if v_first is None else v_first[1], l + t, 0, None)
        p_l = rwkv_prep(zl3, lp, None if v_first is None else v_first[0], l + t, l, p_c['pairs'])
        if i == 0:
            v_first = (p_l['v_in'], p_c['v_in'])
        y_l, y_c = _bidirectional_wkv(p_l['pairs'], lp['k_a'], l)
        cv_l = conv_module(zl3, lp)
        flat = lambda a: a.reshape(-1, a.shape[-1])
        xl = merge_out(xl, flat(att_l), flat(y_l), flat(p_l['bv']), flat(p_l['g']), flat(cv_l), zl,
                       lp['lnx_g'], lp['lnx_b'], wa, wr, wc, wo, norm_g[i, 1], mod_l[2], t)
        if need_ctx:
            att_c = context_attention(zc3, attn_sink[i])
            cv_c = conv_module(zc3, lp)
            xc = merge_out(xc, flat(att_c), flat(y_c), flat(p_c['bv']), flat(p_c['g']), flat(cv_c), zc,
                           lp['lnx_g'], lp['lnx_b'], wa, wr, wc, wo, norm_g[i, 1], mod_c[2], n_ctx)

        j = i // 2
        if i % 2 == 0:
            wg, wu, wd = (w[j][None].astype(BF16) for w in (ffn_wg, ffn_wu, ffn_wd))
            streams = [(xl, mod_l, t)] + ([(xc, mod_c, n_ctx)] if need_ctx else [])
            outs = []
            for xs, ms, rpg in streams:
                tm = _pick_tile(rpg, (512, 256, 128, 64, 32, 16, 8))
                h = prenorm(xs, norm_g[i, 2], ms[4], ms[3], rpg)
                n_tiles = xs.shape[0] // tm
                f = grouped_swiglu(h, jnp.zeros((n_tiles,), jnp.int32), jnp.full((1,), n_tiles, jnp.int32),
                                   wg, wu, wd, tm)
                outs.append(resnorm(xs, f, norm_g[i, 3], ms[5], rpg))
            xl = outs[0]
            if need_ctx:
                xc = outs[1]
        else:
            wg, wu, wd = moe_wg[j], moe_wu[j], moe_wd[j]
            streams = [(xl, mod_l, t)] + ([(xc, mod_c, n_ctx)] if need_ctx else [])
            outs = []
            for xs, ms, rpg in streams:
                tm = _pick_tile(xs.shape[0], (1024, 512, 256, 128, 64, 32, 16, 8))
                h, logits = prenorm(xs, norm_g[i, 2], ms[4], ms[3], rpg, router=moe_router[j])
                row_token, tile_expert, n_used, dest, wts = _moe_dispatch(logits, tm)
                yg = grouped_swiglu(jnp.take(h, row_token, axis=0), tile_expert, n_used, wg, wu, wd, tm)
                f = wts[:, 0:1] * jnp.take(yg, dest[:, 0], axis=0) + wts[:, 1:2] * jnp.take(yg, dest[:, 1], axis=0)
                outs.append(resnorm(xs, f, norm_g[i, 3], ms[5], rpg))
            xl = outs[0]
            if need_ctx:
                xc = outs[1]
    return xl.reshape(bsz, t, d)
```

```python
import functools
import math

import jax
import jax.numpy as jnp
from jax import lax
from jax.experimental import pallas as pl
from jax.experimental.pallas import tpu as pltpu

F32 = jnp.float32
BF16 = jnp.bfloat16

D_MODEL = 1024
GRID_W = 64
ATT_HEADS = 8
ATT_KV_HEADS = 2
ATT_GROUP = ATT_HEADS // ATT_KV_HEADS
HEAD_DIM = 64
WINDOW = 128
ATT_BLOCK = 128
ATT_SCALE = 1.0 / math.sqrt(HEAD_DIM)
ROPE_BASE = 10000.0
AXIS_DIM = HEAD_DIM // 2
RWKV_HEADS = 8
RWKV_HEAD = 64
RWKV_W = RWKV_HEADS * RWKV_HEAD
DECAY_RANK = 32
ICLR_RANK = 32
GATE_RANK = 96
LNX_EPS = 64e-5
CONV_W = 512
CONV_K = 31
CONV_LN_EPS = 1e-5
N_BRANCH = 3
N_EXPERTS = 8
TOP_K = 2
NORM_EPS = 1e-6
Q_W = ATT_HEADS * HEAD_DIM
KV_W = ATT_KV_HEADS * HEAD_DIM
OFF_Q = 0
OFF_K = OFF_Q + Q_W
OFF_V = OFF_K + KV_W
OFF_R = OFF_V + KV_W
OFF_CONV = OFF_R + 3 * RWKV_W + 2 * DECAY_RANK + 2 * ICLR_RANK + GATE_RANK
OFF_GATE = OFF_CONV + 2 * CONV_W
IN_COLS = OFF_GATE + N_BRANCH * D_MODEL

ZQ = 0
ZQP = 512
ZK = 1024
ZKP = 1152
ZV = 1280
ZWA = 1408
ZCA = 1536
ZCB = 2048
ZR = 2560
ZRK = 3072
ZRV = 3584
ZG = 4096
ZZG = 7168
NZ = 7680

VMEM_LIMIT = 56 * 1024 * 1024
MOE_GROUPS = 2


def _cparams(sem):
    return pltpu.CompilerParams(dimension_semantics=sem, vmem_limit_bytes=VMEM_LIMIT)


def _pick_tile(n, candidates):
    for c in candidates:
        if n % c == 0:
            return c
    return n


def _matmul_bias_kernel(a_ref, w_ref, b_ref, o_ref):
    o_ref[...] = jnp.dot(a_ref[...], w_ref[...], preferred_element_type=F32,
                         precision=lax.Precision.HIGHEST) + b_ref[...]


def matmul_bias_f32(a, w, b):
    m, k = a.shape
    n = w.shape[1]
    tn = _pick_tile(n, (1536, 1024, 512, 256, 128))
    return pl.pallas_call(
        _matmul_bias_kernel,
        grid=(n // tn,),
        in_specs=[pl.BlockSpec((m, k), lambda j: (0, 0)),
                  pl.BlockSpec((k, tn), lambda j: (0, j)),
                  pl.BlockSpec((1, tn), lambda j: (0, j))],
        out_specs=pl.BlockSpec((m, tn), lambda j: (0, j)),
        out_shape=jax.ShapeDtypeStruct((m, n), F32),
        compiler_params=_cparams(("parallel",)),
        name="mod_matmul",
    )(a, w, b.reshape(1, n))


def _norm_mod(x, g, sc, sh):
    ms = jnp.mean(x * x, axis=-1, keepdims=True)
    return (x * lax.rsqrt(ms + NORM_EPS) * g) * (1.0 + sc) + sh


def _nm_matmul_kernel(x_ref, g_ref, sc_ref, sh_ref, w_ref, o_ref, h_ref):
    @pl.when(pl.program_id(1) == 0)
    def _():
        h_ref[...] = _norm_mod(x_ref[...], g_ref[...], sc_ref[0], sh_ref[0]).astype(BF16)

    o_ref[...] = jnp.dot(h_ref[...], w_ref[...], preferred_element_type=F32).astype(o_ref.dtype)


def nm_matmul(x, g, sc, sh, w, rows_per_group):
    m, d = x.shape
    n = w.shape[1]
    tm = _pick_tile(rows_per_group, (1024, 512, 256, 128, 64, 32, 16, 8))
    tn = _pick_tile(n, (1280, 1024, 512, 256, 128))
    grp = lambda i, j: ((i * tm) // rows_per_group, 0, 0)
    return pl.pallas_call(
        _nm_matmul_kernel,
        grid=(m // tm, n // tn),
        in_specs=[pl.BlockSpec((tm, d), lambda i, j: (i, 0)),
                  pl.BlockSpec((1, d), lambda i, j: (0, 0)),
                  pl.BlockSpec((1, 1, d), grp),
                  pl.BlockSpec((1, 1, d), grp),
                  pl.BlockSpec((d, tn), lambda i, j: (0, j))],
        out_specs=pl.BlockSpec((tm, tn), lambda i, j: (i, j)),
        out_shape=jax.ShapeDtypeStruct((m, n), F32),
        scratch_shapes=[pltpu.VMEM((tm, d), BF16)],
        compiler_params=_cparams(("parallel", "arbitrary")),
        name="in_proj",
    )(x, g.reshape(1, d), sc, sh, w)


def _prenorm_kernel(x_ref, g_ref, sc_ref, sh_ref, h_ref):
    h_ref[...] = _norm_mod(x_ref[...], g_ref[...], sc_ref[0], sh_ref[0]).astype(h_ref.dtype)


def _prenorm_router_kernel(x_ref, g_ref, sc_ref, sh_ref, r_ref, h_ref, l_ref):
    h = _norm_mod(x_ref[...], g_ref[...], sc_ref[0], sh_ref[0])
    h_ref[...] = h.astype(h_ref.dtype)
    l_ref[...] = jnp.dot(h, r_ref[...], preferred_element_type=F32, precision=lax.Precision.HIGHEST)


def prenorm(x, g, sc, sh, rows_per_group, router=None):
    m, d = x.shape
    tm = _pick_tile(rows_per_group, (512, 256, 128, 64, 32, 16, 8))
    grp = lambda i: ((i * tm) // rows_per_group, 0, 0)
    in_specs = [pl.BlockSpec((tm, d), lambda i: (i, 0)),
                pl.BlockSpec((1, d), lambda i: (0, 0)),
                pl.BlockSpec((1, 1, d), grp),
                pl.BlockSpec((1, 1, d), grp)]
    if router is None:
        return pl.pallas_call(
            _prenorm_kernel, grid=(m // tm,), in_specs=in_specs,
            out_specs=pl.BlockSpec((tm, d), lambda i: (i, 0)),
            out_shape=jax.ShapeDtypeStruct((m, d), BF16),
            compiler_params=_cparams(("parallel",)), name="prenorm",
        )(x, g.reshape(1, d), sc, sh)
    ne = router.shape[1]
    rpad = jnp.zeros((d, 128), F32).at[:, :ne].set(router)
    h, logits = pl.pallas_call(
        _prenorm_router_kernel, grid=(m // tm,),
        in_specs=in_specs + [pl.BlockSpec((d, 128), lambda i: (0, 0))],
        out_specs=[pl.BlockSpec((tm, d), lambda i: (i, 0)), pl.BlockSpec((tm, 128), lambda i: (i, 0))],
        out_shape=[jax.ShapeDtypeStruct((m, d), BF16), jax.ShapeDtypeStruct((m, 128), F32)],
        compiler_params=_cparams(("parallel",)), name="prenorm_router",
    )(x, g.reshape(1, d), sc, sh, rpad)
    return h, logits[:, :ne]


def _resnorm_kernel(x_ref, f_ref, g_ref, gt_ref, o_ref):
    f = f_ref[...]
    ms = jnp.mean(f * f, axis=-1, keepdims=True)
    o_ref[...] = x_ref[...] + gt_ref[0] * (f * lax.rsqrt(ms + NORM_EPS) * g_ref[...])


def resnorm(x, f, g, gt, rows_per_group):
    m, d = x.shape
    tm = _pick_tile(rows_per_group, (512, 256, 128, 64, 32, 16, 8))
    grp = lambda i: ((i * tm) // rows_per_group, 0, 0)
    return pl.pallas_call(
        _resnorm_kernel, grid=(m // tm,),
        in_specs=[pl.BlockSpec((tm, d), lambda i: (i, 0)),
                  pl.BlockSpec((tm, d), lambda i: (i, 0)),
                  pl.BlockSpec((1, d), lambda i: (0, 0)),
                  pl.BlockSpec((1, 1, d), grp)],
        out_specs=pl.BlockSpec((tm, d), lambda i: (i, 0)),
        out_shape=jax.ShapeDtypeStruct((m, d), F32),
        compiler_params=_cparams(("parallel",)), name="resnorm",
    )(x, f, g.reshape(1, d), gt)


LOG2E = 1.0 / math.log(2.0)


def _softmax_pv(qh, sink2, parts):
    scores = []
    m = jnp.full((qh.shape[0], 1), sink2, F32)
    for k, _, bias in parts:
        s = lax.dot_general(qh, k, (((1,), (1,)), ((), ())), preferred_element_type=F32)
        if bias is not None:
            s = s + bias
        scores.append(s)
        m = jnp.maximum(m, jnp.max(s, axis=-1, keepdims=True))
    den = jnp.exp2(sink2 - m)
    acc = None
    for s, (_, v, _) in zip(scores, parts):
        p = jnp.exp2(s - m)
        den = den + jnp.sum(p, axis=-1, keepdims=True)
        pv = jnp.dot(p.astype(BF16), v, preferred_element_type=F32)
        acc = pv if acc is None else acc + pv
    return acc / den


def _attn_heads(q, parts_fn, sink_ref, o_ref):
    qs = (q * (ATT_SCALE * LOG2E)).astype(BF16)
    outs = []
    for hd in range(ATT_HEADS):
        outs.append(_softmax_pv(qs[:, hd * HEAD_DIM:(hd + 1) * HEAD_DIM], sink_ref[hd] * LOG2E,
                                parts_fn(hd // ATT_GROUP)))
    o_ref[0] = jnp.concatenate(outs, axis=-1)


def _hslice(t, h):
    return t[:, h * HEAD_DIM:(h + 1) * HEAD_DIM]


def _win_attn_kernel(sink_ref, q_ref, qp_ref, cq_ref, sq_ref,
                     k0_ref, k1_ref, k2_ref, kp0_ref, kp1_ref, kp2_ref,
                     v0_ref, v1_ref, v2_ref, c0_ref, c1_ref, c2_ref, s0_ref, s1_ref, s2_ref,
                     kc_ref, vc_ref, o_ref, *, nb):
    j = pl.program_id(1)
    q = q_ref[0] * cq_ref[...] + qp_ref[0] * sq_ref[...]
    kb = jnp.concatenate([k0_ref[0] * c0_ref[...] + kp0_ref[0] * s0_ref[...],
                          k1_ref[0] * c1_ref[...] + kp1_ref[0] * s1_ref[...],
                          k2_ref[0] * c2_ref[...] + kp2_ref[0] * s2_ref[...]], axis=0).astype(BF16)
    vb = jnp.concatenate([v0_ref[0], v1_ref[0], v2_ref[0]], axis=0).astype(BF16)
    kc = kc_ref[0].astype(BF16)
    vc = vc_ref[0].astype(BF16)
    qi = lax.broadcasted_iota(jnp.int32, (ATT_BLOCK, 3 * ATT_BLOCK), 0) + ATT_BLOCK
    ki = lax.broadcasted_iota(jnp.int32, (ATT_BLOCK, 3 * ATT_BLOCK), 1)
    lo = jnp.where(j == 0, ATT_BLOCK, 0)
    hi = jnp.where(j == nb - 1, 2 * ATT_BLOCK, 3 * ATT_BLOCK)
    far = jnp.where(ki < lo, 4 * ATT_BLOCK, 0) + jnp.where(ki >= hi, 4 * ATT_BLOCK, 0)
    bias = jnp.where(jnp.abs(qi - ki) + far <= WINDOW, 0.0, -jnp.inf).astype(F32)

    def parts(h):
        return [(_hslice(kc, h), _hslice(vc, h), None), (_hslice(kb, h), _hslice(vb, h), bias)]

    _attn_heads(q, parts, sink_ref, o_ref)


def _ctx_attn_kernel(sink_ref, q_ref, kc_ref, vc_ref, o_ref):
    kc = kc_ref[0].astype(BF16)
    vc = vc_ref[0].astype(BF16)
    _attn_heads(q_ref[0], lambda h: [(_hslice(kc, h), _hslice(vc, h), None)], sink_ref, o_ref)


def windowed_attention(zl3, zc3, cos_t, sin_t, sink):
    b, t, _ = zl3.shape
    l = zc3.shape[1]
    nb = t // ATT_BLOCK
    kcol, kpcol, vcol = ZK // KV_W, ZKP // KV_W, ZV // KV_W
    prev = lambda j: jnp.maximum(j - 1, 0)
    nxt = lambda j: jnp.minimum(j + 1, nb - 1)

    def zspec(width, rowf, col):
        return pl.BlockSpec((1, ATT_BLOCK, width), lambda bi, j: (bi, rowf(j), col))

    def tspec(width, rowf):
        return pl.BlockSpec((ATT_BLOCK, width), lambda bi, j: (rowf(j), 0))

    same = lambda j: j
    in_specs = [pl.BlockSpec(memory_space=pltpu.SMEM),
                zspec(Q_W, same, ZQ // Q_W), zspec(Q_W, same, ZQP // Q_W), tspec(Q_W, same), tspec(Q_W, same)]
    args = [sink, zl3, zl3, cos_t, sin_t]
    for col in (kcol, kpcol, vcol):
        in_specs += [zspec(KV_W, prev, col), zspec(KV_W, same, col), zspec(KV_W, nxt, col)]
        args += [zl3, zl3, zl3]
    for tab in (cos_t, sin_t):
        in_specs += [tspec(KV_W, prev), tspec(KV_W, same), tspec(KV_W, nxt)]
        args += [tab, tab, tab]
    in_specs += [pl.BlockSpec((1, l, KV_W), lambda bi, j: (bi, 0, kcol)),
                 pl.BlockSpec((1, l, KV_W), lambda bi, j: (bi, 0, vcol))]
    args += [zc3, zc3]
    return pl.pallas_call(
        functools.partial(_win_attn_kernel, nb=nb),
        grid=(b, nb), in_specs=in_specs,
        out_specs=pl.BlockSpec((1, ATT_BLOCK, Q_W), lambda bi, j: (bi, j, 0)),
        out_shape=jax.ShapeDtypeStruct((b, t, Q_W), F32),
        compiler_params=_cparams(("parallel", "parallel")), name="win_attn",
    )(*args)


def context_attention(zc3, sink):
    b, l, _ = zc3.shape
    return pl.pallas_call(
        _ctx_attn_kernel, grid=(b,),
        in_specs=[pl.BlockSpec(memory_space=pltpu.SMEM),
                  pl.BlockSpec((1, l, Q_W), lambda bi: (bi, 0, ZQ // Q_W)),
                  pl.BlockSpec((1, l, KV_W), lambda bi: (bi, 0, ZK // KV_W)),
                  pl.BlockSpec((1, l, KV_W), lambda bi: (bi, 0, ZV // KV_W))],
        out_specs=pl.BlockSpec((1, l, Q_W), lambda bi: (bi, 0, 0)),
        out_shape=jax.ShapeDtypeStruct((b, l, Q_W), F32),
        compiler_params=_cparams(("parallel",)), name="ctx_attn",
    )(sink, zc3, zc3, zc3)


HALO = 8


def _head_sums(x, e_ref):
    hi = x.astype(BF16)
    lo = (x - hi.astype(F32)).astype(BF16)
    return (jnp.dot(hi, e_ref[...], preferred_element_type=F32)
            + jnp.dot(lo, e_ref[...], preferred_element_type=F32))


def _shift_lerp_tile(cur_ref, prev_ref, next_ref, mu_ref, buf_ref, first, last):
    tt = cur_ref.shape[1]
    z = cur_ref[0]
    buf_ref[0:HALO] = jnp.where(first, 0.0, prev_ref[0])
    buf_ref[HALO:HALO + tt] = z
    buf_ref[HALO + tt:2 * HALO + tt] = jnp.where(last, 0.0, next_ref[0])
    nb = buf_ref[HALO - 1:HALO - 1 + tt] + buf_ref[HALO + 1:HALO + 1 + tt]
    return z + mu_ref[...] * (0.5 * nb - z)


def _split_bf16(a):
    hi = a.astype(BF16)
    return hi, (a - hi.astype(F32)).astype(BF16)


def _dot_3pass(a, b):
    a_hi, a_lo = _split_bf16(a)
    b_hi, b_lo = _split_bf16(b)
    return (jnp.dot(a_hi, b_hi, preferred_element_type=F32) + jnp.dot(a_lo, b_hi, preferred_element_type=F32)
            + jnp.dot(a_hi, b_lo, preferred_element_type=F32))


def _rwkv_prep_kernel(*refs, has_vres, n_alias):
    (wa_c, wa_p, wa_n, r_c, r_p, r_n, k_c, k_p, k_n, v_c, v_p, v_n, g_c, g_p, g_n,
     mu_wa, mu_r, mu_k, mu_v, mu_g, w0_ref, w2_ref, a0_ref, a2_ref, g2_ref, kk_ref, ka_ref, rk_ref, e_ref) = refs[:29]
    pos = 29
    if has_vres:
        vf_ref, v0_ref, v1_ref, v2_ref = refs[pos:pos + 4]
        pos += 4
    pos += n_alias
    (rk_o, vk_o, w_o, an_o, bv_o, g_o) = refs[pos:pos + 6]
    pos += 6
    if not has_vres:
        vin_o = refs[pos]
        pos += 1
    buf_wa, buf_r, buf_k, buf_v, buf_g = refs[pos:pos + 5]

    i = pl.program_id(1)
    first = i == 0
    last = i == pl.num_programs(1) - 1
    zwa = _shift_lerp_tile(wa_c, wa_p, wa_n, mu_wa, buf_wa, first, last)
    r = _shift_lerp_tile(r_c, r_p, r_n, mu_r, buf_r, first, last)
    k = _shift_lerp_tile(k_c, k_p, k_n, mu_k, buf_k, first, last)
    v = _shift_lerp_tile(v_c, v_p, v_n, mu_v, buf_v, first, last)
    zg = _shift_lerp_tile(g_c, g_p, g_n, mu_g, buf_g, first, last)

    if has_vres:
        mix = _dot_3pass(_dot_3pass(v, v1_ref[...]), v2_ref[...])
        v = v + (vf_ref[0] - v) * jax.nn.sigmoid(v0_ref[...] + mix)
    else:
        vin_o[0] = v
    kk = k * kk_ref[...]
    kk = kk / jnp.maximum(jnp.sqrt(_head_sums(kk * kk, e_ref)), 1e-12)
    tz = jnp.tanh(zwa)
    ksum = None
    for d in range(2):
        w_log = -jax.nn.softplus(-(w0_ref[d:d + 1, :] + _dot_3pass(tz, w2_ref[d]))) - 0.5
        w_o[d, 0] = jnp.exp(-jnp.exp(w_log))
        a = jax.nn.sigmoid(a0_ref[d:d + 1, :] + _dot_3pass(zwa, a2_ref[d]))
        an_o[d, 0] = -a
        kd = k * (1.0 + (a - 1.0) * ka_ref[...])
        ksum = kd if ksum is None else ksum + kd
    rk_o[0, 0] = r
    rk_o[1, 0] = k
    vk_o[0, 0] = v
    vk_o[1, 0] = kk
    bv_o[0] = _head_sums(r * ksum * rk_ref[...], e_ref) * v
    g_o[0] = _dot_3pass(jax.nn.sigmoid(zg), g2_ref[...])


def _head_ones():
    idx = jnp.arange(RWKV_W) // RWKV_HEAD
    return (idx[:, None] == idx[None, :]).astype(BF16)


def rwkv_prep(z3, lp, v_first, ttot, row_off, pair_bufs):
    b, tx, _ = z3.shape
    tt = _pick_tile(math.gcd(tx, row_off) if row_off else tx, (256, 128, 64, 32, 16, 8))
    nh = tt // HALO
    nblk = tx // HALO
    has_vres = v_first is not None
    lanes = 128

    def trio(width, col):
        return [pl.BlockSpec((1, tt, width), lambda bi, i: (bi, i, col)),
                pl.BlockSpec((1, HALO, width), lambda bi, i: (bi, jnp.maximum(i * nh - 1, 0), col)),
                pl.BlockSpec((1, HALO, width), lambda bi, i: (bi, jnp.minimum((i + 1) * nh, nblk - 1), col))]

    def full(shape):
        return pl.BlockSpec(shape, lambda bi, i: (0,) * len(shape))

    mu = lp['mu']
    o = 3 * RWKV_W
    nwa = 2 * DECAY_RANK + 2 * ICLR_RANK
    mu_g = jnp.zeros((lanes,), F32).at[:GATE_RANK].set(mu[o + nwa:])
    w2f = jnp.zeros((2, lanes, RWKV_W), F32)
    a2f = jnp.zeros((2, lanes, RWKV_W), F32)
    for d in range(2):
        w2f = w2f.at[d, d * DECAY_RANK:(d + 1) * DECAY_RANK].set(lp['w2'][d])
        a2f = a2f.at[d, 2 * DECAY_RANK + d * ICLR_RANK:2 * DECAY_RANK + (d + 1) * ICLR_RANK].set(lp['a2'][d])
    g2f = jnp.zeros((lanes, RWKV_W), F32).at[:GATE_RANK].set(lp['g2'])
    row = lambda a: a.reshape(1, -1)
    in_specs = (trio(lanes, ZWA // lanes) + trio(RWKV_W, ZR // RWKV_W) + trio(RWKV_W, ZRK // RWKV_W)
                + trio(RWKV_W, ZRV // RWKV_W) + trio(lanes, ZZG // lanes)
                + [full((1, lanes)), full((1, RWKV_W)), full((1, RWKV_W)), full((1, RWKV_W)), full((1, lanes)),
                   full((2, RWKV_W)), full((2, lanes, RWKV_W)), full((2, RWKV_W)), full((2, lanes, RWKV_W)),
                   full((lanes, RWKV_W)), full((1, RWKV_W)), full((1, RWKV_W)), full((1, RWKV_W)),
                   full((RWKV_W, RWKV_W))])
    args = [z3] * 15 + [row(mu[o:o + nwa]), row(mu[:RWKV_W]), row(mu[RWKV_W:2 * RWKV_W]), row(mu[2 * RWKV_W:o]),
                        row(mu_g), lp['w0'], w2f, lp['a0'], a2f, g2f, row(lp['k_k']), row(lp['k_a']),
                        row(lp['r_k']), _head_ones()]
    tok = pl.BlockSpec((1, tt, RWKV_W), lambda bi, i: (bi, i, 0))
    if has_vres:
        rank = lp['v1'].shape[1]
        v1f = jnp.zeros((RWKV_W, lanes), F32).at[:, :rank].set(lp['v1'])
        v2f = jnp.zeros((lanes, RWKV_W), F32).at[:rank].set(lp['v2'])
        in_specs += [tok, full((1, RWKV_W)), full((RWKV_W, lanes)), full((lanes, RWKV_W))]
        args += [v_first, row(lp['v0']), v1f, v2f]
    n_alias = 0 if pair_bufs is None else len(pair_bufs)
    aliases = {}
    if pair_bufs is not None:
        aliases = {len(args) + n: n for n in range(n_alias)}
        in_specs += [pl.BlockSpec(memory_space=pl.ANY)] * n_alias
        args += list(pair_bufs)
    blk_off = row_off // tt
    pair = pl.BlockSpec((2, 1, tt, RWKV_W), lambda bi, i: (0, bi, i + blk_off, 0))
    pair_shape = jax.ShapeDtypeStruct((2, b, ttot, RWKV_W), F32)
    tok_shape = jax.ShapeDtypeStruct((b, tx, RWKV_W), F32)
    n_tok = 2 if has_vres else 3
    outs = pl.pallas_call(
        functools.partial(_rwkv_prep_kernel, has_vres=has_vres, n_alias=n_alias),
        grid=(b, tx // tt), in_specs=in_specs,
        out_specs=[pair] * 4 + [tok] * n_tok,
        out_shape=[pair_shape] * 4 + [tok_shape] * n_tok,
        input_output_aliases=aliases,
        scratch_shapes=[pltpu.VMEM((tt + 2 * HALO, lanes), F32)] + [pltpu.VMEM((tt + 2 * HALO, RWKV_W), F32)] * 3
                       + [pltpu.VMEM((tt + 2 * HALO, lanes), F32)],
        compiler_params=_cparams(("parallel", "parallel")), name="rwkv_prep",
    )(*args)
    res = {'pairs': tuple(outs[:4]), 'bv': outs[4], 'g': outs[5]}
    if not has_vres:
        res['v_in'] = outs[6]
    return res


CONV_HALO = 16
CONV_ROWS = 32


def _conv_kernel(a_c, a_p, a_n, b_c, b_p, b_n, w_ref, bias_ref, lg_ref, lb_ref, o_ref, buf_ref, win_ref):
    i = pl.program_id(1)
    first = i == 0
    last = i == pl.num_programs(1) - 1
    tt = a_c.shape[1]
    glu = lambda a, b: a * jax.nn.sigmoid(b)
    buf_ref[0:CONV_HALO] = jnp.where(first, 0.0, glu(a_p[0], b_p[0]))
    buf_ref[CONV_HALO:CONV_HALO + tt] = glu(a_c[0], b_c[0])
    buf_ref[CONV_HALO + tt:2 * CONV_HALO + tt] = jnp.where(last, 0.0, glu(a_n[0], b_n[0]))
    half = CONV_K // 2
    span = CONV_ROWS + ((CONV_K - 1) // SUBLANES) * SUBLANES
    for r0 in range(0, tt, CONV_ROWS):
        base = CONV_HALO - half + r0
        acc = bias_ref[...]
        for ph in range(SUBLANES):
            win_ref[...] = buf_ref[base + ph:base + ph + span]
            for j in range(ph, CONV_K, SUBLANES):
                acc = acc + win_ref[j - ph:j - ph + CONV_ROWS] * w_ref[j:j + 1, :]
        m = jnp.mean(acc, axis=-1, keepdims=True)
        cen = acc - m
        var = jnp.mean(cen * cen, axis=-1, keepdims=True)
        y = cen * lax.rsqrt(var + CONV_LN_EPS) * lg_ref[...] + lb_ref[...]
        o_ref[0, r0:r0 + CONV_ROWS, :] = y * jax.nn.sigmoid(y)


def conv_module(z3, lp):
    b, tx, _ = z3.shape
    tt = _pick_tile(tx, (256, 128, 64, 32))
    nh = tt // CONV_HALO
    nblk = tx // CONV_HALO

    def trio(col):
        return [pl.BlockSpec((1, tt, CONV_W), lambda bi, i: (bi, i, col)),
                pl.BlockSpec((1, CONV_HALO, CONV_W), lambda bi, i: (bi, jnp.maximum(i * nh - 1, 0), col)),
                pl.BlockSpec((1, CONV_HALO, CONV_W), lambda bi, i: (bi, jnp.minimum((i + 1) * nh, nblk - 1), col))]

    kpad = -(-CONV_K // SUBLANES) * SUBLANES
    wpad = jnp.zeros((kpad, CONV_W), F32).at[:CONV_K].set(lp['dw_w'])
    vec = pl.BlockSpec((1, CONV_W), lambda bi, i: (0, 0))
    return pl.pallas_call(
        _conv_kernel, grid=(b, tx // tt),
        in_specs=trio(ZCA // CONV_W) + trio(ZCB // CONV_W)
                 + [pl.BlockSpec((kpad, CONV_W), lambda bi, i: (0, 0)), vec, vec, vec],
        out_specs=pl.BlockSpec((1, tt, CONV_W), lambda bi, i: (bi, i, 0)),
        out_shape=jax.ShapeDtypeStruct((b, tx, CONV_W), F32),
        scratch_shapes=[pltpu.VMEM((tt + 2 * CONV_HALO, CONV_W), F32),
                        pltpu.VMEM((CONV_ROWS + ((CONV_K - 1) // SUBLANES) * SUBLANES, CONV_W), F32)],
        compiler_params=_cparams(("parallel", "parallel")), name="conv_module",
    )(z3, z3, z3, z3, z3, z3, wpad, lp['dw_b'].reshape(1, -1), lp['cln_g'].reshape(1, -1), lp['cln_b'].reshape(1, -1))


SUBLANES = 8


def _wkv_scan_kernel(rk_f, vk_f, w_f, an_f, rk_b, vk_b, w_b, an_b, ka_ref, yf_ref, yb_ref, s_ref, *, tt):
    @pl.when(pl.program_id(0) == 0)
    def _():
        s_ref[...] = jnp.zeros_like(s_ref)

    nv, nk, lanes = s_ref.shape
    half = lanes // 2
    fwd = lax.broadcasted_iota(jnp.int32, (nk, lanes), 1) < half
    ka = ka_ref[...]

    def step(t, carry):
        tb = tt - 1 - t
        rkf, rkb, vkf, vkb = rk_f[t], rk_b[tb], vk_f[t], vk_b[tb]
        r = jnp.where(fwd, rkf, pltpu.roll(rkb, half, 1))
        k = jnp.where(fwd, pltpu.roll(rkf, half, 1), rkb)
        v = jnp.where(fwd, vkf, pltpu.roll(vkb, half, 1))
        kk = jnp.where(fwd, pltpu.roll(vkf, half, 1), vkb)
        w = jnp.where(fwd, w_f[t], w_b[tb])
        an = jnp.where(fwd, an_f[t], an_b[tb])
        kd = k * (1.0 - (an + 1.0) * ka)
        bbn = kk * an
        for g in range(nv // SUBLANES):
            ys = []
            for i in range(SUBLANES):
                vi = g * SUBLANES + i
                s_old = s_ref[vi]
                sa = jnp.sum(s_old * kk, axis=0, keepdims=True)
                s_new = s_old * w + sa * bbn + v[vi:vi + 1, :] * kd
                s_ref[vi] = s_new
                ys.append(jnp.sum(s_new * r, axis=0, keepdims=True))
            yg = jnp.concatenate(ys, axis=0)
            yf_ref[t, g * SUBLANES:(g + 1) * SUBLANES, :] = yg
            yb_ref[tb, g * SUBLANES:(g + 1) * SUBLANES, :] = yg
        return carry

    lax.fori_loop(0, tt, step, 0)


def wkv_scan(rk, vk, w, an, ka, n_ctx_steps):
    ttot, n, ch = rk.shape
    tt = _pick_tile(math.gcd(n_ctx_steps, ttot - n_ctx_steps), (32, 16, 8, 4, 2, 1))
    nt, nc = ttot // tt, n_ctx_steps // tt
    fspec = pl.BlockSpec((tt, n, ch), lambda i: (i, 0, 0))
    bspec = pl.BlockSpec((tt, n, ch), lambda i: (jnp.where(i < nc, nc - 1 - i, nt - 1 - (i - nc)), 0, 0))
    return pl.pallas_call(
        functools.partial(_wkv_scan_kernel, tt=tt),
        grid=(nt,), in_specs=[fspec] * 4 + [bspec] * 4 + [pl.BlockSpec((n, ch), lambda i: (0, 0))],
        out_specs=[fspec, bspec],
        out_shape=[jax.ShapeDtypeStruct((ttot, n, ch), F32)] * 2,
        scratch_shapes=[pltpu.VMEM((n, n, ch), F32)],
        compiler_params=_cparams(("arbitrary",)), name="wkv_scan",
    )(rk, vk, w, an, rk, vk, w, an, ka)


def _merge_kernel(x_ref, att_ref, y_ref, bv_ref, rg_ref, cv_ref, g0_ref, g1_ref, g2_ref,
                  lg_ref, lb_ref, e_ref, wa_ref, wr_ref, wc_ref, wo_ref, ng_ref, gt_ref, o_ref):
    def proj(a, w_ref):
        return jnp.dot(a.astype(BF16), w_ref[...], preferred_element_type=F32)

    y = y_ref[...]
    cen = y - _head_sums(y, e_ref) * (1.0 / RWKV_HEAD)
    var = _head_sums(cen * cen, e_ref) * (1.0 / RWKV_HEAD)
    rw = (cen * lax.rsqrt(var + LNX_EPS) * lg_ref[...] + lb_ref[...] + bv_ref[...]) * rg_ref[...]
    m = (jax.nn.sigmoid(g0_ref[...]) * proj(att_ref[...], wa_ref)
         + jax.nn.sigmoid(g1_ref[...]) * proj(rw, wr_ref)
         + jax.nn.sigmoid(g2_ref[...]) * proj(cv_ref[...], wc_ref))
    ml = proj(m, wo_ref)
    ms = jnp.mean(ml * ml, axis=-1, keepdims=True)
    o_ref[...] = x_ref[...] + gt_ref[0] * (ml * lax.rsqrt(ms + NORM_EPS) * ng_ref[...])


def merge_out(x, att, y, bv, rg, cv, z, lnx_g, lnx_b, wa, wr, wc, wo, ng, gt, rows_per_group):
    m, d = x.shape
    tm = _pick_tile(rows_per_group, (512, 256, 128, 64, 32, 16, 8))
    row = lambda i: (i, 0)
    const = lambda i: (0, 0)
    gcol = ZG // d
    br = pl.BlockSpec((tm, RWKV_W), row)
    return pl.pallas_call(
        _merge_kernel, grid=(m // tm,),
        in_specs=[pl.BlockSpec((tm, d), row),
                  pl.BlockSpec((tm, Q_W), row), br, br, br, pl.BlockSpec((tm, CONV_W), row),
                  pl.BlockSpec((tm, d), lambda i: (i, gcol)),
                  pl.BlockSpec((tm, d), lambda i: (i, gcol + 1)),
                  pl.BlockSpec((tm, d), lambda i: (i, gcol + 2)),
                  pl.BlockSpec((1, RWKV_W), const), pl.BlockSpec((1, RWKV_W), const),
                  pl.BlockSpec((RWKV_W, RWKV_W), const),
                  pl.BlockSpec((Q_W, d), const), pl.BlockSpec((RWKV_W, d), const), pl.BlockSpec((CONV_W, d), const),
                  pl.BlockSpec((d, d), const), pl.BlockSpec((1, d), const),
                  pl.BlockSpec((1, 1, d), lambda i: ((i * tm) // rows_per_group, 0, 0))],
        out_specs=pl.BlockSpec((tm, d), row),
        out_shape=jax.ShapeDtypeStruct((m, d), F32),
        compiler_params=_cparams(("parallel",)), name="merge_out",
    )(x, att, y, bv, rg, cv, z, z, z, lnx_g.reshape(1, -1), lnx_b.reshape(1, -1), _head_ones(),
      wa, wr, wc, wo, ng.reshape(1, d), gt)


def _swiglu_kernel(te_ref, nu_ref, h_ref, wg_ref, wu_ref, wd_ref, o_ref):
    f = pl.program_id(1)

    @pl.when(pl.program_id(0) < nu_ref[0])
    def _():
        h = h_ref[...]
        hg = jnp.dot(h, wg_ref[0].astype(BF16), preferred_element_type=F32)
        hu = jnp.dot(h, wu_ref[0].astype(BF16), preferred_element_type=F32)
        act = (hg * jax.nn.sigmoid(hg) * hu).astype(BF16)
        part = jnp.dot(act, wd_ref[0].astype(BF16), preferred_element_type=F32)

        @pl.when(f == 0)
        def _():
            o_ref[...] = part

        @pl.when(f > 0)
        def _():
            o_ref[...] += part


def grouped_swiglu(h, tile_expert, n_used, wg, wu, wd, tm):
    p, d = h.shape
    ff = wg.shape[2]
    tf = _pick_tile(ff, (512, 1408, 256, 128))
    nf = ff // tf
    ti = lambda i, nu: jnp.minimum(i, nu[0] - 1)
    fi = lambda i, f, nu: jnp.where(i < nu[0], f, nf - 1)
    grid_spec = pltpu.PrefetchScalarGridSpec(
        num_scalar_prefetch=2, grid=(p // tm, nf),
        in_specs=[pl.BlockSpec((tm, d), lambda i, f, te, nu: (ti(i, nu), 0)),
                  pl.BlockSpec((1, d, tf), lambda i, f, te, nu: (te[ti(i, nu)], 0, fi(i, f, nu))),
                  pl.BlockSpec((1, d, tf), lambda i, f, te, nu: (te[ti(i, nu)], 0, fi(i, f, nu))),
                  pl.BlockSpec((1, tf, d), lambda i, f, te, nu: (te[ti(i, nu)], fi(i, f, nu), 0))],
        out_specs=pl.BlockSpec((tm, d), lambda i, f, te, nu: (ti(i, nu), 0)))
    return pl.pallas_call(
        _swiglu_kernel, grid_spec=grid_spec,
        out_shape=jax.ShapeDtypeStruct((p, d), F32),
        compiler_params=_cparams(("arbitrary", "arbitrary")), name="swiglu",
    )(tile_expert, n_used, h, wg, wu, wd)


def _dense_ffn_kernel(x_ref, g_ref, sc_ref, sh_ref, wg_ref, wu_ref, wd_ref, g2_ref, gt_ref, o_ref, h_ref, acc_ref):
    f = pl.program_id(1)

    @pl.when(f == 0)
    def _():
        h_ref[...] = _norm_mod(x_ref[...], g_ref[...], sc_ref[0], sh_ref[0]).astype(BF16)

    h = h_ref[...]
    hg = jnp.dot(h, wg_ref[...], preferred_element_type=F32)
    hu = jnp.dot(h, wu_ref[...], preferred_element_type=F32)
    act = (hg * jax.nn.sigmoid(hg) * hu).astype(BF16)
    part = jnp.dot(act, wd_ref[...], preferred_element_type=F32)

    @pl.when(f == 0)
    def _():
        acc_ref[...] = part

    @pl.when(f > 0)
    def _():
        acc_ref[...] += part

    @pl.when(f == pl.num_programs(1) - 1)
    def _():
        y = acc_ref[...]
        ms = jnp.mean(y * y, axis=-1, keepdims=True)
        o_ref[...] = x_ref[...] + gt_ref[0] * (y * lax.rsqrt(ms + NORM_EPS) * g2_ref[...])


def dense_ffn(x, g_in, sc, sh, wg, wu, wd, g_out, gt, rows_per_group):
    m, d = x.shape
    ff = wg.shape[1]
    tm = _pick_tile(rows_per_group, (512, 256, 128, 64, 32, 16, 8))
    tf = _pick_tile(ff, (1408, 512, 256, 128))
    grp = lambda i, f: ((i * tm) // rows_per_group, 0, 0)
    vec = pl.BlockSpec((1, d), lambda i, f: (0, 0))
    return pl.pallas_call(
        _dense_ffn_kernel, grid=(m // tm, ff // tf),
        in_specs=[pl.BlockSpec((tm, d), lambda i, f: (i, 0)), vec,
                  pl.BlockSpec((1, 1, d), grp), pl.BlockSpec((1, 1, d), grp),
                  pl.BlockSpec((d, tf), lambda i, f: (0, f)), pl.BlockSpec((d, tf), lambda i, f: (0, f)),
                  pl.BlockSpec((tf, d), lambda i, f: (f, 0)), vec, pl.BlockSpec((1, 1, d), grp)],
        out_specs=pl.BlockSpec((tm, d), lambda i, f: (i, 0)),
        out_shape=jax.ShapeDtypeStruct((m, d), F32),
        scratch_shapes=[pltpu.VMEM((tm, d), BF16), pltpu.VMEM((tm, d), F32)],
        compiler_params=_cparams(("parallel", "arbitrary")), name="dense_ffn",
    )(x, g_in.reshape(1, d), sc, sh, wg, wu, wd, g_out.reshape(1, d), gt)


def _rope_tables(t):
    rows = t // GRID_W
    row = jnp.broadcast_to(jnp.arange(rows)[:, None], (rows, GRID_W)).reshape(-1).astype(F32)
    col = jnp.broadcast_to(jnp.arange(GRID_W)[None, :], (rows, GRID_W)).reshape(-1).astype(F32)
    inv = ROPE_BASE ** (-jnp.arange(0, AXIS_DIM, 2, dtype=F32) / AXIS_DIM)
    ar = row[:, None] * inv[None, :]
    ac = col[:, None] * inv[None, :]
    cos64 = jnp.concatenate([jnp.cos(ar), jnp.cos(ar), jnp.cos(ac), jnp.cos(ac)], axis=-1)
    sin64 = jnp.concatenate([jnp.sin(ar), jnp.sin(ar), jnp.sin(ac), jnp.sin(ac)], axis=-1)
    return jnp.tile(cos64, (1, ATT_HEADS)), jnp.tile(sin64, (1, ATT_HEADS))


def _rope_partner(w):
    d = w.shape[0]
    q = AXIS_DIM // 2
    wh = w.reshape(d, -1, 4, q)
    return jnp.stack([-wh[:, :, 1], wh[:, :, 0], -wh[:, :, 3], wh[:, :, 2]], axis=2).reshape(w.shape)


def _build_w_ext(w_in):
    d = w_in.shape[0]
    wq = w_in[:, OFF_Q:OFF_K]
    wk = w_in[:, OFF_K:OFF_V]
    o = OFF_R + 3 * RWKV_W
    parts = [(ZQ, wq), (ZQP, _rope_partner(wq)), (ZK, wk), (ZKP, _rope_partner(wk)),
             (ZV, w_in[:, OFF_V:OFF_R]),
             (ZWA, w_in[:, o:o + 2 * DECAY_RANK + 2 * ICLR_RANK]),
             (ZCA, w_in[:, OFF_CONV:OFF_CONV + CONV_W]), (ZCB, w_in[:, OFF_CONV + CONV_W:OFF_GATE]),
             (ZR, w_in[:, OFF_R:OFF_R + RWKV_W]), (ZRK, w_in[:, OFF_R + RWKV_W:OFF_R + 2 * RWKV_W]),
             (ZRV, w_in[:, OFF_R + 2 * RWKV_W:OFF_R + 3 * RWKV_W]),
             (ZG, w_in[:, OFF_GATE:IN_COLS]),
             (ZZG, w_in[:, OFF_CONV - GATE_RANK:OFF_CONV])]
    parts.sort(key=lambda p: p[0])
    cols, pos = [], 0
    for off, wpart in parts:
        if off > pos:
            cols.append(jnp.zeros((d, off - pos), w_in.dtype))
        cols.append(wpart)
        pos = off + wpart.shape[1]
    if pos < NZ:
        cols.append(jnp.zeros((d, NZ - pos), w_in.dtype))
    return jnp.concatenate(cols, axis=1).astype(BF16)


def _chain_layout(pair):
    _, bsz, ttot, _ = pair.shape
    both = pair.reshape(2, bsz, ttot, RWKV_HEADS, RWKV_HEAD)
    return jnp.transpose(both, (2, 4, 0, 1, 3)).reshape(ttot, RWKV_HEAD, 2 * bsz * RWKV_HEADS)


def _bidirectional_wkv(pairs, k_a, l):
    bsz = pairs[0].shape[1]
    rk, vk, w, an = (_chain_layout(p) for p in pairs)
    ka = jnp.tile(k_a.reshape(RWKV_HEADS, RWKV_HEAD).T, (1, 2 * bsz))
    yf, yb = wkv_scan(rk, vk, w, an, ka, l)
    half = bsz * RWKV_HEADS
    y = (yf[:, :, :half] + yb[:, :, half:]).reshape(-1, RWKV_HEAD, bsz, RWKV_HEADS)
    y = jnp.transpose(y, (2, 0, 3, 1)).reshape(bsz, -1, RWKV_W)
    return y[:, l:], y[:, :l]


def _moe_dispatch(logits, tm):
    m = logits.shape[0]
    top_v, top_i = lax.top_k(logits, TOP_K)
    wts = jax.nn.softmax(top_v, axis=-1)
    e_flat = top_i.reshape(-1)
    n_assign = e_flat.shape[0]
    onehot = (e_flat[:, None] == jnp.arange(N_EXPERTS)[None, :]).astype(jnp.int32)
    csum = jnp.cumsum(onehot, axis=0)
    rank = jnp.sum(onehot * csum, axis=1) - 1
    cnt = csum[-1]
    pcnt = ((cnt + tm - 1) // tm) * tm
    pend = jnp.cumsum(pcnt)
    pstart = pend - pcnt
    dest = pstart[e_flat] + rank
    p_max = n_assign + N_EXPERTS * tm
    row_token = jnp.zeros((p_max,), jnp.int32).at[dest].set(jnp.arange(n_assign, dtype=jnp.int32) // TOP_K)
    tile_start = jnp.arange(p_max // tm, dtype=jnp.int32) * tm
    tile_expert = jnp.minimum(jnp.searchsorted(pend, tile_start, side='right'), N_EXPERTS - 1).astype(jnp.int32)
    n_used = (pend[-1:] // tm).astype(jnp.int32)
    return row_token, tile_expert, n_used, dest.reshape(m, TOP_K), wts


def kernel(x, c, ctx, c_ctx, mod_w, mod_b, norm_g, w_in, attn_sink, rwkv_mu, rwkv_w0, rwkv_w2, rwkv_a0, rwkv_a2, rwkv_g2, rwkv_k_k, rwkv_k_a, rwkv_r_k, rwkv_lnx_g, rwkv_lnx_b, vres_v0, vres_w1, vres_w2, conv_dw_w, conv_dw_b, conv_ln_g, conv_ln_b, w_attn_o, w_rwkv_o, w_conv_o, w_out, ffn_wg, ffn_wu, ffn_wd, moe_router, moe_wg, moe_wu, moe_wd):
    bsz, t, d = x.shape
    l = ctx.shape[1]
    depth = mod_w.shape[0]
    n_lat, n_ctx = bsz * t, bsz * l
    cos_t, sin_t = _rope_tables(t)
    xl = x.reshape(n_lat, d)
    xc = ctx.reshape(n_ctx, d)
    n_rows = -(-(bsz + 1) // SUBLANES) * SUBLANES
    cond = jnp.zeros((n_rows, d), F32).at[:bsz].set(jax.nn.silu(c)).at[bsz].set(jax.nn.silu(c_ctx))
    v_first = None
    for i in range(depth):
        need_ctx = i < depth - 1
        mod = matmul_bias_f32(cond, mod_w[i], mod_b[i])
        mod_l = [mod[:bsz, k * d:(k + 1) * d].reshape(bsz, 1, d) for k in range(6)]
        mod_c = [mod[bsz:bsz + 1, k * d:(k + 1) * d].reshape(1, 1, d) for k in range(6)]
        lp = {'mu': rwkv_mu[i], 'w0': rwkv_w0[i], 'w2': rwkv_w2[i], 'a0': rwkv_a0[i], 'a2': rwkv_a2[i],
              'g2': rwkv_g2[i], 'k_k': rwkv_k_k[i], 'k_a': rwkv_k_a[i], 'r_k': rwkv_r_k[i],
              'lnx_g': rwkv_lnx_g[i], 'lnx_b': rwkv_lnx_b[i],
              'dw_w': conv_dw_w[i], 'dw_b': conv_dw_b[i], 'cln_g': conv_ln_g[i], 'cln_b': conv_ln_b[i]}
        if i > 0:
            lp['v0'], lp['v1'], lp['v2'] = vres_v0[i - 1], vres_w1[i - 1], vres_w2[i - 1]
        w_ext = _build_w_ext(w_in[i])
        wa, wr, wc, wo = (w.astype(BF16) for w in (w_attn_o[i], w_rwkv_o[i], w_conv_o[i], w_out[i]))

        zl = nm_matmul(xl, norm_g[i, 0], mod_l[1], mod_l[0], w_ext, t)
        zc = nm_matmul(xc, norm_g[i, 0], mod_c[1], mod_c[0], w_ext, n_ctx)
        zl3 = zl.reshape(bsz, t, NZ)
        zc3 = zc.reshape(bsz, l, NZ)
        att_l = windowed_attention(zl3, zc3, cos_t, sin_t, attn_sink[i])
        p_c = rwkv_prep(zc3, lp, None if v_first is None else v_first[1], l + t, 0, None)
        p_l = rwkv_prep(zl3, lp, None if v_first is None else v_first[0], l + t, l, p_c['pairs'])
        if i == 0:
            v_first = (p_l['v_in'], p_c['v_in'])
        y_l, y_c = _bidirectional_wkv(p_l['pairs'], lp['k_a'], l)
        cv_l = conv_module(zl3, lp)
        flat = lambda a: a.reshape(-1, a.shape[-1])
        xl = merge_out(xl, flat(att_l), flat(y_l), flat(p_l['bv']), flat(p_l['g']), flat(cv_l), zl,
                       lp['lnx_g'], lp['lnx_b'], wa, wr, wc, wo, norm_g[i, 1], mod_l[2], t)
        if need_ctx:
            att_c = context_attention(zc3, attn_sink[i])
            cv_c = conv_module(zc3, lp)
            xc = merge_out(xc, flat(att_c), flat(y_c), flat(p_c['bv']), flat(p_c['g']), flat(cv_c), zc,
                           lp['lnx_g'], lp['lnx_b'], wa, wr, wc, wo, norm_g[i, 1], mod_c[2], n_ctx)

        j = i // 2
        if i % 2 == 0:
            wg, wu, wd = (w[j].astype(BF16) for w in (ffn_wg, ffn_wu, ffn_wd))
            xl = dense_ffn(xl, norm_g[i, 2], mod_l[4], mod_l[3], wg, wu, wd, norm_g[i, 3], mod_l[5], t)
            if need_ctx:
                xc = dense_ffn(xc, norm_g[i, 2], mod_c[4], mod_c[3], wg, wu, wd, norm_g[i, 3], mod_c[5], n_ctx)
        else:
            wg, wu, wd = moe_wg[j], moe_wu[j], moe_wd[j]
            streams = [(xl, mod_l, t)] + ([(xc, mod_c, n_ctx)] if need_ctx else [])
            outs = []
            for xs, ms, rpg in streams:
                n_tok = xs.shape[0]
                h, logits = prenorm(xs, norm_g[i, 2], ms[4], ms[3], rpg, router=moe_router[j])
                n_grp = MOE_GROUPS if n_tok % (MOE_GROUPS * SUBLANES) == 0 else 1
                gsz = n_tok // n_grp
                tm = _pick_tile(gsz, (1024, 512, 256, 128, 64, 32, 16, 8))
                fs = []
                for gi in range(n_grp):
                    row_token, tile_expert, n_used, dest, wts = _moe_dispatch(logits[gi * gsz:(gi + 1) * gsz], tm)
                    yg = grouped_swiglu(jnp.take(h, row_token + gi * gsz, axis=0), tile_expert, n_used,
                                        wg, wu, wd, tm)
                    fs.append(wts[:, 0:1] * jnp.take(yg, dest[:, 0], axis=0)
                              + wts[:, 1:2] * jnp.take(yg, dest[:, 1], axis=0))
                f = fs[0] if n_grp == 1 else jnp.concatenate(fs, axis=0)
                outs.append(resnorm(xs, f, norm_g[i, 3], ms[5], rpg))
            xl = outs[0]
            if need_ctx:
                xc = outs[1]
    return xl.reshape(bsz, t, d)
```

```python
import functools
import math

import jax
import jax.numpy as jnp
from jax import lax
from jax.experimental import pallas as pl
from jax.experimental.pallas import tpu as pltpu

F32 = jnp.float32
BF16 = jnp.bfloat16

D_MODEL = 1024
GRID_W = 64
ATT_HEADS = 8
ATT_KV_HEADS = 2
ATT_GROUP = ATT_HEADS // ATT_KV_HEADS
HEAD_DIM = 64
WINDOW = 128
ATT_BLOCK = 128
ATT_SCALE = 1.0 / math.sqrt(HEAD_DIM)
ROPE_BASE = 10000.0
AXIS_DIM = HEAD_DIM // 2
RWKV_HEADS = 8
RWKV_HEAD = 64
RWKV_W = RWKV_HEADS * RWKV_HEAD
DECAY_RANK = 32
ICLR_RANK = 32
GATE_RANK = 96
LNX_EPS = 64e-5
CONV_W = 512
CONV_K = 31
CONV_LN_EPS = 1e-5
N_BRANCH = 3
N_EXPERTS = 8
TOP_K = 2
NORM_EPS = 1e-6
Q_W = ATT_HEADS * HEAD_DIM
KV_W = ATT_KV_HEADS * HEAD_DIM
OFF_Q = 0
OFF_K = OFF_Q + Q_W
OFF_V = OFF_K + KV_W
OFF_R = OFF_V + KV_W
OFF_CONV = OFF_R + 3 * RWKV_W + 2 * DECAY_RANK + 2 * ICLR_RANK + GATE_RANK
OFF_GATE = OFF_CONV + 2 * CONV_W
IN_COLS = OFF_GATE + N_BRANCH * D_MODEL

ZQ = 0
ZQP = 512
ZK = 1024
ZKP = 1152
ZV = 1280
ZWA = 1408
ZCA = 1536
ZCB = 2048
ZR = 2560
ZRK = 3072
ZRV = 3584
ZG = 4096
ZZG = 7168
NZ = 7680

VMEM_LIMIT = 56 * 1024 * 1024
MOE_GROUPS = 2


def _cparams(sem):
    return pltpu.CompilerParams(dimension_semantics=sem, vmem_limit_bytes=VMEM_LIMIT)


def _pick_tile(n, candidates):
    for c in candidates:
        if n % c == 0:
            return c
    return n


def _matmul_bias_kernel(a_ref, w_ref, b_ref, o_ref):
    o_ref[...] = jnp.dot(a_ref[...], w_ref[...], preferred_element_type=F32,
                         precision=lax.Precision.HIGHEST) + b_ref[...]


def matmul_bias_f32(a, w, b):
    m, k = a.shape
    n = w.shape[1]
    tn = _pick_tile(n, (1536, 1024, 512, 256, 128))
    return pl.pallas_call(
        _matmul_bias_kernel,
        grid=(n // tn,),
        in_specs=[pl.BlockSpec((m, k), lambda j: (0, 0)),
                  pl.BlockSpec((k, tn), lambda j: (0, j)),
                  pl.BlockSpec((1, tn), lambda j: (0, j))],
        out_specs=pl.BlockSpec((m, tn), lambda j: (0, j)),
        out_shape=jax.ShapeDtypeStruct((m, n), F32),
        compiler_params=_cparams(("parallel",)),
        name="mod_matmul",
    )(a, w, b.reshape(1, n))


def _norm_mod(x, g, sc, sh):
    ms = jnp.mean(x * x, axis=-1, keepdims=True)
    return (x * lax.rsqrt(ms + NORM_EPS) * g) * (1.0 + sc) + sh


def _nm_matmul_kernel(x_ref, g_ref, sc_ref, sh_ref, w_ref, o_ref, h_ref):
    @pl.when(pl.program_id(1) == 0)
    def _():
        h_ref[...] = _norm_mod(x_ref[...], g_ref[...], sc_ref[0], sh_ref[0]).astype(BF16)

    o_ref[...] = jnp.dot(h_ref[...], w_ref[...], preferred_element_type=F32).astype(o_ref.dtype)


def nm_matmul(x, g, sc, sh, w, rows_per_group):
    m, d = x.shape
    n = w.shape[1]
    tm = _pick_tile(rows_per_group, (1024, 512, 256, 128, 64, 32, 16, 8))
    tn = _pick_tile(n, (1280, 1024, 512, 256, 128))
    grp = lambda i, j: ((i * tm) // rows_per_group, 0, 0)
    return pl.pallas_call(
        _nm_matmul_kernel,
        grid=(m // tm, n // tn),
        in_specs=[pl.BlockSpec((tm, d), lambda i, j: (i, 0)),
                  pl.BlockSpec((1, d), lambda i, j: (0, 0)),
                  pl.BlockSpec((1, 1, d), grp),
                  pl.BlockSpec((1, 1, d), grp),
                  pl.BlockSpec((d, tn), lambda i, j: (0, j))],
        out_specs=pl.BlockSpec((tm, tn), lambda i, j: (i, j)),
        out_shape=jax.ShapeDtypeStruct((m, n), F32),
        scratch_shapes=[pltpu.VMEM((tm, d), BF16)],
        compiler_params=_cparams(("parallel", "arbitrary")),
        name="in_proj",
    )(x, g.reshape(1, d), sc, sh, w)


def _prenorm_kernel(x_ref, g_ref, sc_ref, sh_ref, h_ref):
    h_ref[...] = _norm_mod(x_ref[...], g_ref[...], sc_ref[0], sh_ref[0]).astype(h_ref.dtype)


def _prenorm_router_kernel(x_ref, g_ref, sc_ref, sh_ref, r_ref, h_ref, l_ref):
    h = _norm_mod(x_ref[...], g_ref[...], sc_ref[0], sh_ref[0])
    h_ref[...] = h.astype(h_ref.dtype)
    l_ref[...] = jnp.dot(h, r_ref[...], preferred_element_type=F32, precision=lax.Precision.HIGHEST)


def prenorm(x, g, sc, sh, rows_per_group, router=None):
    m, d = x.shape
    tm = _pick_tile(rows_per_group, (512, 256, 128, 64, 32, 16, 8))
    grp = lambda i: ((i * tm) // rows_per_group, 0, 0)
    in_specs = [pl.BlockSpec((tm, d), lambda i: (i, 0)),
                pl.BlockSpec((1, d), lambda i: (0, 0)),
                pl.BlockSpec((1, 1, d), grp),
                pl.BlockSpec((1, 1, d), grp)]
    if router is None:
        return pl.pallas_call(
            _prenorm_kernel, grid=(m // tm,), in_specs=in_specs,
            out_specs=pl.BlockSpec((tm, d), lambda i: (i, 0)),
            out_shape=jax.ShapeDtypeStruct((m, d), BF16),
            compiler_params=_cparams(("parallel",)), name="prenorm",
        )(x, g.reshape(1, d), sc, sh)
    ne = router.shape[1]
    rpad = jnp.zeros((d, 128), F32).at[:, :ne].set(router)
    h, logits = pl.pallas_call(
        _prenorm_router_kernel, grid=(m // tm,),
        in_specs=in_specs + [pl.BlockSpec((d, 128), lambda i: (0, 0))],
        out_specs=[pl.BlockSpec((tm, d), lambda i: (i, 0)), pl.BlockSpec((tm, 128), lambda i: (i, 0))],
        out_shape=[jax.ShapeDtypeStruct((m, d), BF16), jax.ShapeDtypeStruct((m, 128), F32)],
        compiler_params=_cparams(("parallel",)), name="prenorm_router",
    )(x, g.reshape(1, d), sc, sh, rpad)
    return h, logits[:, :ne]


def _resnorm_kernel(x_ref, f_ref, g_ref, gt_ref, o_ref):
    f = f_ref[...]
    ms = jnp.mean(f * f, axis=-1, keepdims=True)
    o_ref[...] = x_ref[...] + gt_ref[0] * (f * lax.rsqrt(ms + NORM_EPS) * g_ref[...])


def resnorm(x, f, g, gt, rows_per_group):
    m, d = x.shape
    tm = _pick_tile(rows_per_group, (512, 256, 128, 64, 32, 16, 8))
    grp = lambda i: ((i * tm) // rows_per_group, 0, 0)
    return pl.pallas_call(
        _resnorm_kernel, grid=(m // tm,),
        in_specs=[pl.BlockSpec((tm, d), lambda i: (i, 0)),
                  pl.BlockSpec((tm, d), lambda i: (i, 0)),
                  pl.BlockSpec((1, d), lambda i: (0, 0)),
                  pl.BlockSpec((1, 1, d), grp)],
        out_specs=pl.BlockSpec((tm, d), lambda i: (i, 0)),
        out_shape=jax.ShapeDtypeStruct((m, d), F32),
        compiler_params=_cparams(("parallel",)), name="resnorm",
    )(x, f, g.reshape(1, d), gt)


LOG2E = 1.0 / math.log(2.0)


def _attn_heads(q, k, v, bias, sink_ref, o_ref):
    qs = (q * (ATT_SCALE * LOG2E)).astype(BF16)
    ones = jnp.ones((v.shape[0], HEAD_DIM), BF16)
    heads = range(ATT_HEADS)
    kh = [_hslice(k, h) for h in range(ATT_KV_HEADS)]
    v1 = [jnp.concatenate([_hslice(v, h), ones], axis=1) for h in range(ATT_KV_HEADS)]
    sink2 = [sink_ref[hd] * LOG2E for hd in heads]
    s = [lax.dot_general(_hslice(qs, hd), kh[hd // ATT_GROUP], (((1,), (1,)), ((), ())),
                         preferred_element_type=F32) for hd in heads]
    if bias is not None:
        s = [sh + bias for sh in s]
    m = [jnp.maximum(jnp.max(s[hd], axis=-1, keepdims=True), sink2[hd]) for hd in heads]
    p = [jnp.exp2(s[hd] - m[hd]).astype(BF16) for hd in heads]
    pv = [jnp.dot(p[hd], v1[hd // ATT_GROUP], preferred_element_type=F32) for hd in heads]
    outs = [pv[hd][:, :HEAD_DIM] / (pv[hd][:, HEAD_DIM:] + jnp.exp2(sink2[hd] - m[hd])) for hd in heads]
    o_ref[0] = jnp.concatenate(outs, axis=-1)


def _hslice(t, h):
    return t[:, h * HEAD_DIM:(h + 1) * HEAD_DIM]


def _win_attn_kernel(sink_ref, q_ref, qp_ref, cq_ref, sq_ref,
                     k0_ref, k1_ref, k2_ref, kp0_ref, kp1_ref, kp2_ref,
                     v0_ref, v1_ref, v2_ref, c0_ref, c1_ref, c2_ref, s0_ref, s1_ref, s2_ref,
                     kc_ref, vc_ref, o_ref, *, nb):
    j = pl.program_id(1)
    q = q_ref[0] * cq_ref[...] + qp_ref[0] * sq_ref[...]
    kb = jnp.concatenate([k0_ref[0] * c0_ref[...] + kp0_ref[0] * s0_ref[...],
                          k1_ref[0] * c1_ref[...] + kp1_ref[0] * s1_ref[...],
                          k2_ref[0] * c2_ref[...] + kp2_ref[0] * s2_ref[...]], axis=0).astype(BF16)
    vb = jnp.concatenate([v0_ref[0], v1_ref[0], v2_ref[0]], axis=0).astype(BF16)
    n_ctx = kc_ref.shape[1]
    k = jnp.concatenate([kc_ref[0].astype(BF16), kb], axis=0)
    v = jnp.concatenate([vc_ref[0].astype(BF16), vb], axis=0)
    n = n_ctx + 3 * ATT_BLOCK
    qi = lax.broadcasted_iota(jnp.int32, (ATT_BLOCK, n), 0) + ATT_BLOCK
    ki = lax.broadcasted_iota(jnp.int32, (ATT_BLOCK, n), 1) - n_ctx
    lo = jnp.where(j == 0, ATT_BLOCK, 0)
    hi = jnp.where(j == nb - 1, 2 * ATT_BLOCK, 3 * ATT_BLOCK)
    far = jnp.where(ki < lo, 4 * ATT_BLOCK, 0) + jnp.where(ki >= hi, 4 * ATT_BLOCK, 0)
    seen = jnp.where(ki < 0, 0, jnp.abs(qi - ki) + far) <= WINDOW
    bias = jnp.where(seen, 0.0, -jnp.inf).astype(F32)
    _attn_heads(q, k, v, bias, sink_ref, o_ref)


def _ctx_attn_kernel(sink_ref, q_ref, kc_ref, vc_ref, o_ref):
    _attn_heads(q_ref[0], kc_ref[0].astype(BF16), vc_ref[0].astype(BF16), None, sink_ref, o_ref)


def windowed_attention(zl3, zc3, cos_t, sin_t, sink):
    b, t, _ = zl3.shape
    l = zc3.shape[1]
    nb = t // ATT_BLOCK
    kcol, kpcol, vcol = ZK // KV_W, ZKP // KV_W, ZV // KV_W
    prev = lambda j: jnp.maximum(j - 1, 0)
    nxt = lambda j: jnp.minimum(j + 1, nb - 1)

    def zspec(width, rowf, col):
        return pl.BlockSpec((1, ATT_BLOCK, width), lambda bi, j: (bi, rowf(j), col))

    def tspec(width, rowf):
        return pl.BlockSpec((ATT_BLOCK, width), lambda bi, j: (rowf(j), 0))

    same = lambda j: j
    in_specs = [pl.BlockSpec(memory_space=pltpu.SMEM),
                zspec(Q_W, same, ZQ // Q_W), zspec(Q_W, same, ZQP // Q_W), tspec(Q_W, same), tspec(Q_W, same)]
    args = [sink, zl3, zl3, cos_t, sin_t]
    for col in (kcol, kpcol, vcol):
        in_specs += [zspec(KV_W, prev, col), zspec(KV_W, same, col), zspec(KV_W, nxt, col)]
        args += [zl3, zl3, zl3]
    for tab in (cos_t, sin_t):
        in_specs += [tspec(KV_W, prev), tspec(KV_W, same), tspec(KV_W, nxt)]
        args += [tab, tab, tab]
    in_specs += [pl.BlockSpec((1, l, KV_W), lambda bi, j: (bi, 0, kcol)),
                 pl.BlockSpec((1, l, KV_W), lambda bi, j: (bi, 0, vcol))]
    args += [zc3, zc3]
    return pl.pallas_call(
        functools.partial(_win_attn_kernel, nb=nb),
        grid=(b, nb), in_specs=in_specs,
        out_specs=pl.BlockSpec((1, ATT_BLOCK, Q_W), lambda bi, j: (bi, j, 0)),
        out_shape=jax.ShapeDtypeStruct((b, t, Q_W), F32),
        compiler_params=_cparams(("parallel", "parallel")), name="win_attn",
    )(*args)


def context_attention(zc3, sink):
    b, l, _ = zc3.shape
    return pl.pallas_call(
        _ctx_attn_kernel, grid=(b,),
        in_specs=[pl.BlockSpec(memory_space=pltpu.SMEM),
                  pl.BlockSpec((1, l, Q_W), lambda bi: (bi, 0, ZQ // Q_W)),
                  pl.BlockSpec((1, l, KV_W), lambda bi: (bi, 0, ZK // KV_W)),
                  pl.BlockSpec((1, l, KV_W), lambda bi: (bi, 0, ZV // KV_W))],
        out_specs=pl.BlockSpec((1, l, Q_W), lambda bi: (bi, 0, 0)),
        out_shape=jax.ShapeDtypeStruct((b, l, Q_W), F32),
        compiler_params=_cparams(("parallel",)), name="ctx_attn",
    )(sink, zc3, zc3, zc3)


HALO = 8


def _head_sums(x, e_ref):
    hi = x.astype(BF16)
    lo = (x - hi.astype(F32)).astype(BF16)
    return (jnp.dot(hi, e_ref[...], preferred_element_type=F32)
            + jnp.dot(lo, e_ref[...], preferred_element_type=F32))


def _shift_lerp_tile(cur_ref, prev_ref, next_ref, mu_ref, buf_ref, first, last):
    tt = cur_ref.shape[1]
    z = cur_ref[0]
    buf_ref[0:HALO] = jnp.where(first, 0.0, prev_ref[0])
    buf_ref[HALO:HALO + tt] = z
    buf_ref[HALO + tt:2 * HALO + tt] = jnp.where(last, 0.0, next_ref[0])
    nb = buf_ref[HALO - 1:HALO - 1 + tt] + buf_ref[HALO + 1:HALO + 1 + tt]
    return z + mu_ref[...] * (0.5 * nb - z)


def _split_bf16(a):
    hi = a.astype(BF16)
    return hi, (a - hi.astype(F32)).astype(BF16)


def _dot_3pass(a, b):
    a_hi, a_lo = _split_bf16(a)
    b_hi, b_lo = _split_bf16(b)
    return (jnp.dot(a_hi, b_hi, preferred_element_type=F32) + jnp.dot(a_lo, b_hi, preferred_element_type=F32)
            + jnp.dot(a_hi, b_lo, preferred_element_type=F32))


def _rwkv_prep_kernel(*refs, has_vres, n_alias):
    (wa_c, wa_p, wa_n, r_c, r_p, r_n, k_c, k_p, k_n, v_c, v_p, v_n, g_c, g_p, g_n,
     mu_wa, mu_r, mu_k, mu_v, mu_g, w0_ref, w2_ref, a0_ref, a2_ref, g2_ref, kk_ref, ka_ref, rk_ref, e_ref) = refs[:29]
    pos = 29
    if has_vres:
        vf_ref, v0_ref, v1_ref, v2_ref = refs[pos:pos + 4]
        pos += 4
    pos += n_alias
    (rk_o, vk_o, w_o, an_o, bv_o, g_o) = refs[pos:pos + 6]
    pos += 6
    if not has_vres:
        vin_o = refs[pos]
        pos += 1
    buf_wa, buf_r, buf_k, buf_v, buf_g = refs[pos:pos + 5]

    i = pl.program_id(1)
    first = i == 0
    last = i == pl.num_programs(1) - 1
    zwa = _shift_lerp_tile(wa_c, wa_p, wa_n, mu_wa, buf_wa, first, last)
    r = _shift_lerp_tile(r_c, r_p, r_n, mu_r, buf_r, first, last)
    k = _shift_lerp_tile(k_c, k_p, k_n, mu_k, buf_k, first, last)
    v = _shift_lerp_tile(v_c, v_p, v_n, mu_v, buf_v, first, last)
    zg = _shift_lerp_tile(g_c, g_p, g_n, mu_g, buf_g, first, last)

    if has_vres:
        mix = _dot_3pass(_dot_3pass(v, v1_ref[...]), v2_ref[...])
        v = v + (vf_ref[0] - v) * jax.nn.sigmoid(v0_ref[...] + mix)
    else:
        vin_o[0] = v
    kk = k * kk_ref[...]
    kk = kk / jnp.maximum(jnp.sqrt(_head_sums(kk * kk, e_ref)), 1e-12)
    tz = jnp.tanh(zwa)
    ksum = None
    for d in range(2):
        w_log = -jax.nn.softplus(-(w0_ref[d:d + 1, :] + _dot_3pass(tz, w2_ref[d]))) - 0.5
        w_o[d, 0] = jnp.exp(-jnp.exp(w_log))
        a = jax.nn.sigmoid(a0_ref[d:d + 1, :] + _dot_3pass(zwa, a2_ref[d]))
        an_o[d, 0] = -a
        kd = k * (1.0 + (a - 1.0) * ka_ref[...])
        ksum = kd if ksum is None else ksum + kd
    rk_o[0, 0] = r
    rk_o[1, 0] = k
    vk_o[0, 0] = v
    vk_o[1, 0] = kk
    bv_o[0] = _head_sums(r * ksum * rk_ref[...], e_ref) * v
    g_o[0] = _dot_3pass(jax.nn.sigmoid(zg), g2_ref[...])


def _head_ones():
    idx = jnp.arange(RWKV_W) // RWKV_HEAD
    return (idx[:, None] == idx[None, :]).astype(BF16)


def rwkv_prep(z3, lp, v_first, ttot, row_off, pair_bufs):
    b, tx, _ = z3.shape
    tt = _pick_tile(math.gcd(tx, row_off) if row_off else tx, (256, 128, 64, 32, 16, 8))
    nh = tt // HALO
    nblk = tx // HALO
    has_vres = v_first is not None
    lanes = 128

    def trio(width, col):
        return [pl.BlockSpec((1, tt, width), lambda bi, i: (bi, i, col)),
                pl.BlockSpec((1, HALO, width), lambda bi, i: (bi, jnp.maximum(i * nh - 1, 0), col)),
                pl.BlockSpec((1, HALO, width), lambda bi, i: (bi, jnp.minimum((i + 1) * nh, nblk - 1), col))]

    def full(shape):
        return pl.BlockSpec(shape, lambda bi, i: (0,) * len(shape))

    mu = lp['mu']
    o = 3 * RWKV_W
    nwa = 2 * DECAY_RANK + 2 * ICLR_RANK
    mu_g = jnp.zeros((lanes,), F32).at[:GATE_RANK].set(mu[o + nwa:])
    w2f = jnp.zeros((2, lanes, RWKV_W), F32)
    a2f = jnp.zeros((2, lanes, RWKV_W), F32)
    for d in range(2):
        w2f = w2f.at[d, d * DECAY_RANK:(d + 1) * DECAY_RANK].set(lp['w2'][d])
        a2f = a2f.at[d, 2 * DECAY_RANK + d * ICLR_RANK:2 * DECAY_RANK + (d + 1) * ICLR_RANK].set(lp['a2'][d])
    g2f = jnp.zeros((lanes, RWKV_W), F32).at[:GATE_RANK].set(lp['g2'])
    row = lambda a: a.reshape(1, -1)
    in_specs = (trio(lanes, ZWA // lanes) + trio(RWKV_W, ZR // RWKV_W) + trio(RWKV_W, ZRK // RWKV_W)
                + trio(RWKV_W, ZRV // RWKV_W) + trio(lanes, ZZG // lanes)
                + [full((1, lanes)), full((1, RWKV_W)), full((1, RWKV_W)), full((1, RWKV_W)), full((1, lanes)),
                   full((2, RWKV_W)), full((2, lanes, RWKV_W)), full((2, RWKV_W)), full((2, lanes, RWKV_W)),
                   full((lanes, RWKV_W)), full((1, RWKV_W)), full((1, RWKV_W)), full((1, RWKV_W)),
                   full((RWKV_W, RWKV_W))])
    args = [z3] * 15 + [row(mu[o:o + nwa]), row(mu[:RWKV_W]), row(mu[RWKV_W:2 * RWKV_W]), row(mu[2 * RWKV_W:o]),
                        row(mu_g), lp['w0'], w2f, lp['a0'], a2f, g2f, row(lp['k_k']), row(lp['k_a']),
                        row(lp['r_k']), _head_ones()]
    tok = pl.BlockSpec((1, tt, RWKV_W), lambda bi, i: (bi, i, 0))
    if has_vres:
        rank = lp['v1'].shape[1]
        v1f = jnp.zeros((RWKV_W, lanes), F32).at[:, :rank].set(lp['v1'])
        v2f = jnp.zeros((lanes, RWKV_W), F32).at[:rank].set(lp['v2'])
        in_specs += [tok, full((1, RWKV_W)), full((RWKV_W, lanes)), full((lanes, RWKV_W))]
        args += [v_first, row(lp['v0']), v1f, v2f]
    n_alias = 0 if pair_bufs is None else len(pair_bufs)
    aliases = {}
    if pair_bufs is not None:
        aliases = {len(args) + n: n for n in range(n_alias)}
        in_specs += [pl.BlockSpec(memory_space=pl.ANY)] * n_alias
        args += list(pair_bufs)
    blk_off = row_off // tt
    pair = pl.BlockSpec((2, 1, tt, RWKV_W), lambda bi, i: (0, bi, i + blk_off, 0))
    pair_shape = jax.ShapeDtypeStruct((2, b, ttot, RWKV_W), F32)
    tok_shape = jax.ShapeDtypeStruct((b, tx, RWKV_W), F32)
    n_tok = 2 if has_vres else 3
    outs = pl.pallas_call(
        functools.partial(_rwkv_prep_kernel, has_vres=has_vres, n_alias=n_alias),
        grid=(b, tx // tt), in_specs=in_specs,
        out_specs=[pair] * 4 + [tok] * n_tok,
        out_shape=[pair_shape] * 4 + [tok_shape] * n_tok,
        input_output_aliases=aliases,
        scratch_shapes=[pltpu.VMEM((tt + 2 * HALO, lanes), F32)] + [pltpu.VMEM((tt + 2 * HALO, RWKV_W), F32)] * 3
                       + [pltpu.VMEM((tt + 2 * HALO, lanes), F32)],
        compiler_params=_cparams(("parallel", "parallel")), name="rwkv_prep",
    )(*args)
    res = {'pairs': tuple(outs[:4]), 'bv': outs[4], 'g': outs[5]}
    if not has_vres:
        res['v_in'] = outs[6]
    return res


CONV_HALO = 16
CONV_ROWS = 32


def _conv_kernel(a_c, a_p, a_n, b_c, b_p, b_n, w_ref, bias_ref, lg_ref, lb_ref, o_ref, buf_ref, win_ref):
    i = pl.program_id(1)
    first = i == 0
    last = i == pl.num_programs(1) - 1
    tt = a_c.shape[1]
    glu = lambda a, b: a * jax.nn.sigmoid(b)
    buf_ref[0:CONV_HALO] = jnp.where(first, 0.0, glu(a_p[0], b_p[0]))
    buf_ref[CONV_HALO:CONV_HALO + tt] = glu(a_c[0], b_c[0])
    buf_ref[CONV_HALO + tt:2 * CONV_HALO + tt] = jnp.where(last, 0.0, glu(a_n[0], b_n[0]))
    half = CONV_K // 2
    span = CONV_ROWS + ((CONV_K - 1) // SUBLANES) * SUBLANES
    for r0 in range(0, tt, CONV_ROWS):
        base = CONV_HALO - half + r0
        acc = bias_ref[...]
        for ph in range(SUBLANES):
            win_ref[...] = buf_ref[base + ph:base + ph + span]
            for j in range(ph, CONV_K, SUBLANES):
                acc = acc + win_ref[j - ph:j - ph + CONV_ROWS] * w_ref[j:j + 1, :]
        m = jnp.mean(acc, axis=-1, keepdims=True)
        cen = acc - m
        var = jnp.mean(cen * cen, axis=-1, keepdims=True)
        y = cen * lax.rsqrt(var + CONV_LN_EPS) * lg_ref[...] + lb_ref[...]
        o_ref[0, r0:r0 + CONV_ROWS, :] = y * jax.nn.sigmoid(y)


def conv_module(z3, lp):
    b, tx, _ = z3.shape
    tt = _pick_tile(tx, (256, 128, 64, 32))
    nh = tt // CONV_HALO
    nblk = tx // CONV_HALO

    def trio(col):
        return [pl.BlockSpec((1, tt, CONV_W), lambda bi, i: (bi, i, col)),
                pl.BlockSpec((1, CONV_HALO, CONV_W), lambda bi, i: (bi, jnp.maximum(i * nh - 1, 0), col)),
                pl.BlockSpec((1, CONV_HALO, CONV_W), lambda bi, i: (bi, jnp.minimum((i + 1) * nh, nblk - 1), col))]

    kpad = -(-CONV_K // SUBLANES) * SUBLANES
    wpad = jnp.zeros((kpad, CONV_W), F32).at[:CONV_K].set(lp['dw_w'])
    vec = pl.BlockSpec((1, CONV_W), lambda bi, i: (0, 0))
    return pl.pallas_call(
        _conv_kernel, grid=(b, tx // tt),
        in_specs=trio(ZCA // CONV_W) + trio(ZCB // CONV_W)
                 + [pl.BlockSpec((kpad, CONV_W), lambda bi, i: (0, 0)), vec, vec, vec],
        out_specs=pl.BlockSpec((1, tt, CONV_W), lambda bi, i: (bi, i, 0)),
        out_shape=jax.ShapeDtypeStruct((b, tx, CONV_W), F32),
        scratch_shapes=[pltpu.VMEM((tt + 2 * CONV_HALO, CONV_W), F32),
                        pltpu.VMEM((CONV_ROWS + ((CONV_K - 1) // SUBLANES) * SUBLANES, CONV_W), F32)],
        compiler_params=_cparams(("parallel", "parallel")), name="conv_module",
    )(z3, z3, z3, z3, z3, z3, wpad, lp['dw_b'].reshape(1, -1), lp['cln_g'].reshape(1, -1), lp['cln_b'].reshape(1, -1))


SUBLANES = 8
DECAY_FOLD = 16


def _wkv_scan_kernel(rk_f, vk_f, w_f, an_f, rk_b, vk_b, w_b, an_b, ka_ref, yf_ref, yb_ref, s_ref, *, tt):
    @pl.when(pl.program_id(0) == 0)
    def _():
        s_ref[...] = jnp.zeros_like(s_ref)

    nv, nk, lanes = s_ref.shape
    half = lanes // 2
    fwd = lax.broadcasted_iota(jnp.int32, (nk, lanes), 1) < half
    ka = ka_ref[...]
    fold = math.gcd(tt, DECAY_FOLD)

    def group(gi, carry):
        def step(ti, gam):
            t = gi * fold + ti
            tb = tt - 1 - t
            rkf, rkb, vkf, vkb = rk_f[t], rk_b[tb], vk_f[t], vk_b[tb]
            r = jnp.where(fwd, rkf, pltpu.roll(rkb, half, 1))
            k = jnp.where(fwd, pltpu.roll(rkf, half, 1), rkb)
            v = jnp.where(fwd, vkf, pltpu.roll(vkb, half, 1))
            kk = jnp.where(fwd, pltpu.roll(vkf, half, 1), vkb)
            w = jnp.where(fwd, w_f[t], w_b[tb])
            an = jnp.where(fwd, an_f[t], an_b[tb])
            gam_new = gam * w
            inv = 1.0 / gam_new
            kks = kk * gam
            kds = k * (1.0 - (an + 1.0) * ka) * inv
            bbs = kk * an * inv
            rs = r * gam_new
            for g in range(nv // SUBLANES):
                ys = []
                for i in range(SUBLANES):
                    vi = g * SUBLANES + i
                    s_old = s_ref[vi]
                    sa = jnp.sum(s_old * kks, axis=0, keepdims=True)
                    s_new = s_old + sa * bbs + v[vi:vi + 1, :] * kds
                    s_ref[vi] = s_new
                    ys.append(jnp.sum(s_new * rs, axis=0, keepdims=True))
                yg = jnp.concatenate(ys, axis=0)
                yf_ref[t, g * SUBLANES:(g + 1) * SUBLANES, :] = yg
                yb_ref[tb, g * SUBLANES:(g + 1) * SUBLANES, :] = yg
            return gam_new

        gam = lax.fori_loop(0, fold, step, jnp.ones((nk, lanes), F32))
        for vi in range(nv):
            s_ref[vi] = s_ref[vi] * gam
        return carry

    lax.fori_loop(0, tt // fold, group, 0)


def wkv_scan(rk, vk, w, an, ka, n_ctx_steps):
    ttot, n, ch = rk.shape
    tt = _pick_tile(math.gcd(n_ctx_steps, ttot - n_ctx_steps), (32, 16, 8, 4, 2, 1))
    nt, nc = ttot // tt, n_ctx_steps // tt
    fspec = pl.BlockSpec((tt, n, ch), lambda i: (i, 0, 0))
    bspec = pl.BlockSpec((tt, n, ch), lambda i: (jnp.where(i < nc, nc - 1 - i, nt - 1 - (i - nc)), 0, 0))
    return pl.pallas_call(
        functools.partial(_wkv_scan_kernel, tt=tt),
        grid=(nt,), in_specs=[fspec] * 4 + [bspec] * 4 + [pl.BlockSpec((n, ch), lambda i: (0, 0))],
        out_specs=[fspec, bspec],
        out_shape=[jax.ShapeDtypeStruct((ttot, n, ch), F32)] * 2,
        scratch_shapes=[pltpu.VMEM((n, n, ch), F32)],
        compiler_params=_cparams(("arbitrary",)), name="wkv_scan",
    )(rk, vk, w, an, rk, vk, w, an, ka)


def _merge_kernel(x_ref, att_ref, y_ref, bv_ref, rg_ref, cv_ref, g0_ref, g1_ref, g2_ref,
                  lg_ref, lb_ref, e_ref, wa_ref, wr_ref, wc_ref, wo_ref, ng_ref, gt_ref, o_ref):
    def proj(a, w_ref):
        return jnp.dot(a.astype(BF16), w_ref[...], preferred_element_type=F32)

    y = y_ref[...]
    cen = y - _head_sums(y, e_ref) * (1.0 / RWKV_HEAD)
    var = _head_sums(cen * cen, e_ref) * (1.0 / RWKV_HEAD)
    rw = (cen * lax.rsqrt(var + LNX_EPS) * lg_ref[...] + lb_ref[...] + bv_ref[...]) * rg_ref[...]
    m = (jax.nn.sigmoid(g0_ref[...]) * proj(att_ref[...], wa_ref)
         + jax.nn.sigmoid(g1_ref[...]) * proj(rw, wr_ref)
         + jax.nn.sigmoid(g2_ref[...]) * proj(cv_ref[...], wc_ref))
    ml = proj(m, wo_ref)
    ms = jnp.mean(ml * ml, axis=-1, keepdims=True)
    o_ref[...] = x_ref[...] + gt_ref[0] * (ml * lax.rsqrt(ms + NORM_EPS) * ng_ref[...])


def merge_out(x, att, y, bv, rg, cv, z, lnx_g, lnx_b, wa, wr, wc, wo, ng, gt, rows_per_group):
    m, d = x.shape
    tm = _pick_tile(rows_per_group, (512, 256, 128, 64, 32, 16, 8))
    row = lambda i: (i, 0)
    const = lambda i: (0, 0)
    gcol = ZG // d
    br = pl.BlockSpec((tm, RWKV_W), row)
    return pl.pallas_call(
        _merge_kernel, grid=(m // tm,),
        in_specs=[pl.BlockSpec((tm, d), row),
                  pl.BlockSpec((tm, Q_W), row), br, br, br, pl.BlockSpec((tm, CONV_W), row),
                  pl.BlockSpec((tm, d), lambda i: (i, gcol)),
                  pl.BlockSpec((tm, d), lambda i: (i, gcol + 1)),
                  pl.BlockSpec((tm, d), lambda i: (i, gcol + 2)),
                  pl.BlockSpec((1, RWKV_W), const), pl.BlockSpec((1, RWKV_W), const),
                  pl.BlockSpec((RWKV_W, RWKV_W), const),
                  pl.BlockSpec((Q_W, d), const), pl.BlockSpec((RWKV_W, d), const), pl.BlockSpec((CONV_W, d), const),
                  pl.BlockSpec((d, d), const), pl.BlockSpec((1, d), const),
                  pl.BlockSpec((1, 1, d), lambda i: ((i * tm) // rows_per_group, 0, 0))],
        out_specs=pl.BlockSpec((tm, d), row),
        out_shape=jax.ShapeDtypeStruct((m, d), F32),
        compiler_params=_cparams(("parallel",)), name="merge_out",
    )(x, att, y, bv, rg, cv, z, z, z, lnx_g.reshape(1, -1), lnx_b.reshape(1, -1), _head_ones(),
      wa, wr, wc, wo, ng.reshape(1, d), gt)


def _swiglu_kernel(te_ref, nu_ref, h_ref, wg_ref, wu_ref, wd_ref, o_ref):
    f = pl.program_id(1)

    @pl.when(pl.program_id(0) < nu_ref[0])
    def _():
        h = h_ref[...]
        hg = jnp.dot(h, wg_ref[0].astype(BF16), preferred_element_type=F32)
        hu = jnp.dot(h, wu_ref[0].astype(BF16), preferred_element_type=F32)
        act = (hg * jax.nn.sigmoid(hg) * hu).astype(BF16)
        part = jnp.dot(act, wd_ref[0].astype(BF16), preferred_element_type=F32)

        @pl.when(f == 0)
        def _():
            o_ref[...] = part

        @pl.when(f > 0)
        def _():
            o_ref[...] += part


def grouped_swiglu(h, tile_expert, n_used, wg, wu, wd, tm):
    p, d = h.shape
    ff = wg.shape[2]
    tf = _pick_tile(ff, (512, 1408, 256, 128))
    nf = ff // tf
    ti = lambda i, nu: jnp.minimum(i, nu[0] - 1)
    fi = lambda i, f, nu: jnp.where(i < nu[0], f, nf - 1)
    grid_spec = pltpu.PrefetchScalarGridSpec(
        num_scalar_prefetch=2, grid=(p // tm, nf),
        in_specs=[pl.BlockSpec((tm, d), lambda i, f, te, nu: (ti(i, nu), 0)),
                  pl.BlockSpec((1, d, tf), lambda i, f, te, nu: (te[ti(i, nu)], 0, fi(i, f, nu))),
                  pl.BlockSpec((1, d, tf), lambda i, f, te, nu: (te[ti(i, nu)], 0, fi(i, f, nu))),
                  pl.BlockSpec((1, tf, d), lambda i, f, te, nu: (te[ti(i, nu)], fi(i, f, nu), 0))],
        out_specs=pl.BlockSpec((tm, d), lambda i, f, te, nu: (ti(i, nu), 0)))
    return pl.pallas_call(
        _swiglu_kernel, grid_spec=grid_spec,
        out_shape=jax.ShapeDtypeStruct((p, d), F32),
        compiler_params=_cparams(("arbitrary", "arbitrary")), name="swiglu",
    )(tile_expert, n_used, h, wg, wu, wd)


def _dense_ffn_kernel(x_ref, g_ref, sc_ref, sh_ref, wg_ref, wu_ref, wd_ref, g2_ref, gt_ref, o_ref, h_ref, acc_ref):
    f = pl.program_id(1)

    @pl.when(f == 0)
    def _():
        h_ref[...] = _norm_mod(x_ref[...], g_ref[...], sc_ref[0], sh_ref[0]).astype(BF16)

    h = h_ref[...]
    hg = jnp.dot(h, wg_ref[...], preferred_element_type=F32)
    hu = jnp.dot(h, wu_ref[...], preferred_element_type=F32)
    act = (hg * jax.nn.sigmoid(hg) * hu).astype(BF16)
    part = jnp.dot(act, wd_ref[...], preferred_element_type=F32)

    @pl.when(f == 0)
    def _():
        acc_ref[...] = part

    @pl.when(f > 0)
    def _():
        acc_ref[...] += part

    @pl.when(f == pl.num_programs(1) - 1)
    def _():
        y = acc_ref[...]
        ms = jnp.mean(y * y, axis=-1, keepdims=True)
        o_ref[...] = x_ref[...] + gt_ref[0] * (y * lax.rsqrt(ms + NORM_EPS) * g2_ref[...])


def dense_ffn(x, g_in, sc, sh, wg, wu, wd, g_out, gt, rows_per_group):
    m, d = x.shape
    ff = wg.shape[1]
    tm = _pick_tile(rows_per_group, (512, 256, 128, 64, 32, 16, 8))
    tf = _pick_tile(ff, (1408, 512, 256, 128))
    grp = lambda i, f: ((i * tm) // rows_per_group, 0, 0)
    vec = pl.BlockSpec((1, d), lambda i, f: (0, 0))
    return pl.pallas_call(
        _dense_ffn_kernel, grid=(m // tm, ff // tf),
        in_specs=[pl.BlockSpec((tm, d), lambda i, f: (i, 0)), vec,
                  pl.BlockSpec((1, 1, d), grp), pl.BlockSpec((1, 1, d), grp),
                  pl.BlockSpec((d, tf), lambda i, f: (0, f)), pl.BlockSpec((d, tf), lambda i, f: (0, f)),
                  pl.BlockSpec((tf, d), lambda i, f: (f, 0)), vec, pl.BlockSpec((1, 1, d), grp)],
        out_specs=pl.BlockSpec((tm, d), lambda i, f: (i, 0)),
        out_shape=jax.ShapeDtypeStruct((m, d), F32),
        scratch_shapes=[pltpu.VMEM((tm, d), BF16), pltpu.VMEM((tm, d), F32)],
        compiler_params=_cparams(("parallel", "arbitrary")), name="dense_ffn",
    )(x, g_in.reshape(1, d), sc, sh, wg, wu, wd, g_out.reshape(1, d), gt)


def _rope_tables(t):
    rows = t // GRID_W
    row = jnp.broadcast_to(jnp.arange(rows)[:, None], (rows, GRID_W)).reshape(-1).astype(F32)
    col = jnp.broadcast_to(jnp.arange(GRID_W)[None, :], (rows, GRID_W)).reshape(-1).astype(F32)
    inv = ROPE_BASE ** (-jnp.arange(0, AXIS_DIM, 2, dtype=F32) / AXIS_DIM)
    ar = row[:, None] * inv[None, :]
    ac = col[:, None] * inv[None, :]
    cos64 = jnp.concatenate([jnp.cos(ar), jnp.cos(ar), jnp.cos(ac), jnp.cos(ac)], axis=-1)
    sin64 = jnp.concatenate([jnp.sin(ar), jnp.sin(ar), jnp.sin(ac), jnp.sin(ac)], axis=-1)
    return jnp.tile(cos64, (1, ATT_HEADS)), jnp.tile(sin64, (1, ATT_HEADS))


def _rope_partner(w):
    d = w.shape[0]
    q = AXIS_DIM // 2
    wh = w.reshape(d, -1, 4, q)
    return jnp.stack([-wh[:, :, 1], wh[:, :, 0], -wh[:, :, 3], wh[:, :, 2]], axis=2).reshape(w.shape)


def _build_w_ext(w_in):
    d = w_in.shape[0]
    wq = w_in[:, OFF_Q:OFF_K]
    wk = w_in[:, OFF_K:OFF_V]
    o = OFF_R + 3 * RWKV_W
    parts = [(ZQ, wq), (ZQP, _rope_partner(wq)), (ZK, wk), (ZKP, _rope_partner(wk)),
             (ZV, w_in[:, OFF_V:OFF_R]),
             (ZWA, w_in[:, o:o + 2 * DECAY_RANK + 2 * ICLR_RANK]),
             (ZCA, w_in[:, OFF_CONV:OFF_CONV + CONV_W]), (ZCB, w_in[:, OFF_CONV + CONV_W:OFF_GATE]),
             (ZR, w_in[:, OFF_R:OFF_R + RWKV_W]), (ZRK, w_in[:, OFF_R + RWKV_W:OFF_R + 2 * RWKV_W]),
             (ZRV, w_in[:, OFF_R + 2 * RWKV_W:OFF_R + 3 * RWKV_W]),
             (ZG, w_in[:, OFF_GATE:IN_COLS]),
             (ZZG, w_in[:, OFF_CONV - GATE_RANK:OFF_CONV])]
    parts.sort(key=lambda p: p[0])
    cols, pos = [], 0
    for off, wpart in parts:
        if off > pos:
            cols.append(jnp.zeros((d, off - pos), w_in.dtype))
        cols.append(wpart)
        pos = off + wpart.shape[1]
    if pos < NZ:
        cols.append(jnp.zeros((d, NZ - pos), w_in.dtype))
    return jnp.concatenate(cols, axis=1).astype(BF16)


def _chain_layout(pair):
    _, bsz, ttot, _ = pair.shape
    both = pair.reshape(2, bsz, ttot, RWKV_HEADS, RWKV_HEAD)
    return jnp.transpose(both, (2, 4, 0, 1, 3)).reshape(ttot, RWKV_HEAD, 2 * bsz * RWKV_HEADS)


def _bidirectional_wkv(pairs, k_a, l):
    bsz = pairs[0].shape[1]
    rk, vk, w, an = (_chain_layout(p) for p in pairs)
    ka = jnp.tile(k_a.reshape(RWKV_HEADS, RWKV_HEAD).T, (1, 2 * bsz))
    yf, yb = wkv_scan(rk, vk, w, an, ka, l)
    half = bsz * RWKV_HEADS
    y = (yf[:, :, :half] + yb[:, :, half:]).reshape(-1, RWKV_HEAD, bsz, RWKV_HEADS)
    y = jnp.transpose(y, (2, 0, 3, 1)).reshape(bsz, -1, RWKV_W)
    return y[:, l:], y[:, :l]


def _moe_dispatch(logits, tm):
    m = logits.shape[0]
    top_v, top_i = lax.top_k(logits, TOP_K)
    wts = jax.nn.softmax(top_v, axis=-1)
    e_flat = top_i.reshape(-1)
    n_assign = e_flat.shape[0]
    onehot = (e_flat[:, None] == jnp.arange(N_EXPERTS)[None, :]).astype(jnp.int32)
    csum = jnp.cumsum(onehot, axis=0)
    rank = jnp.sum(onehot * csum, axis=1) - 1
    cnt = csum[-1]
    pcnt = ((cnt + tm - 1) // tm) * tm
    pend = jnp.cumsum(pcnt)
    pstart = pend - pcnt
    dest = pstart[e_flat] + rank
    p_max = n_assign + N_EXPERTS * tm
    row_token = jnp.zeros((p_max,), jnp.int32).at[dest].set(jnp.arange(n_assign, dtype=jnp.int32) // TOP_K)
    tile_start = jnp.arange(p_max // tm, dtype=jnp.int32) * tm
    tile_expert = jnp.minimum(jnp.searchsorted(pend, tile_start, side='right'), N_EXPERTS - 1).astype(jnp.int32)
    n_used = (pend[-1:] // tm).astype(jnp.int32)
    return row_token, tile_expert, n_used, dest.reshape(m, TOP_K), wts


def kernel(x, c, ctx, c_ctx, mod_w, mod_b, norm_g, w_in, attn_sink, rwkv_mu, rwkv_w0, rwkv_w2, rwkv_a0, rwkv_a2, rwkv_g2, rwkv_k_k, rwkv_k_a, rwkv_r_k, rwkv_lnx_g, rwkv_lnx_b, vres_v0, vres_w1, vres_w2, conv_dw_w, conv_dw_b, conv_ln_g, conv_ln_b, w_attn_o, w_rwkv_o, w_conv_o, w_out, ffn_wg, ffn_wu, ffn_wd, moe_router, moe_wg, moe_wu, moe_wd):
    bsz, t, d = x.shape
    l = ctx.shape[1]
    depth = mod_w.shape[0]
    n_lat, n_ctx = bsz * t, bsz * l
    cos_t, sin_t = _rope_tables(t)
    xl = x.reshape(n_lat, d)
    xc = ctx.reshape(n_ctx, d)
    n_rows = -(-(bsz + 1) // SUBLANES) * SUBLANES
    cond = jnp.zeros((n_rows, d), F32).at[:bsz].set(jax.nn.silu(c)).at[bsz].set(jax.nn.silu(c_ctx))
    v_first = None
    for i in range(depth):
        need_ctx = i < depth - 1
        mod = matmul_bias_f32(cond, mod_w[i], mod_b[i])
        mod_l = [mod[:bsz, k * d:(k + 1) * d].reshape(bsz, 1, d) for k in range(6)]
        mod_c = [mod[bsz:bsz + 1, k * d:(k + 1) * d].reshape(1, 1, d) for k in range(6)]
        lp = {'mu': rwkv_mu[i], 'w0': rwkv_w0[i], 'w2': rwkv_w2[i], 'a0': rwkv_a0[i], 'a2': rwkv_a2[i],
              'g2': rwkv_g2[i], 'k_k': rwkv_k_k[i], 'k_a': rwkv_k_a[i], 'r_k': rwkv_r_k[i],
              'lnx_g': rwkv_lnx_g[i], 'lnx_b': rwkv_lnx_b[i],
              'dw_w': conv_dw_w[i], 'dw_b': conv_dw_b[i], 'cln_g': conv_ln_g[i], 'cln_b': conv_ln_b[i]}
        if i > 0:
            lp['v0'], lp['v1'], lp['v2'] = vres_v0[i - 1], vres_w1[i - 1], vres_w2[i - 1]
        w_ext = _build_w_ext(w_in[i])
        wa, wr, wc, wo = (w.astype(BF16) for w in (w_attn_o[i], w_rwkv_o[i], w_conv_o[i], w_out[i]))

        zl = nm_matmul(xl, norm_g[i, 0], mod_l[1], mod_l[0], w_ext, t)
        zc = nm_matmul(xc, norm_g[i, 0], mod_c[1], mod_c[0], w_ext, n_ctx)
        zl3 = zl.reshape(bsz, t, NZ)
        zc3 = zc.reshape(bsz, l, NZ)
        att_l = windowed_attention(zl3, zc3, cos_t, sin_t, attn_sink[i])
        p_c = rwkv_prep(zc3, lp, None if v_first is None else v_first[1], l + t, 0, None)
        p_l = rwkv_prep(zl3, lp, None if v_first is None else v_first[0], l + t, l, p_c['pairs'])
        if i == 0:
            v_first = (p_l['v_in'], p_c['v_in'])
        y_l, y_c = _bidirectional_wkv(p_l['pairs'], lp['k_a'], l)
        cv_l = conv_module(zl3, lp)
        flat = lambda a: a.reshape(-1, a.shape[-1])
        xl = merge_out(xl, flat(att_l), flat(y_l), flat(p_l['bv']), flat(p_l['g']), flat(cv_l), zl,
                       lp['lnx_g'], lp['lnx_b'], wa, wr, wc, wo, norm_g[i, 1], mod_l[2], t)
        if need_ctx:
            att_c = context_attention(zc3, attn_sink[i])
            cv_c = conv_module(zc3, lp)
            xc = merge_out(xc, flat(att_c), flat(y_c), flat(p_c['bv']), flat(p_c['g']), flat(cv_c), zc,
                           lp['lnx_g'], lp['lnx_b'], wa, wr, wc, wo, norm_g[i, 1], mod_c[2], n_ctx)

        j = i // 2
        if i % 2 == 0:
            wg, wu, wd = (w[j].astype(BF16) for w in (ffn_wg, ffn_wu, ffn_wd))
            xl = dense_ffn(xl, norm_g[i, 2], mod_l[4], mod_l[3], wg, wu, wd, norm_g[i, 3], mod_l[5], t)
            if need_ctx:
                xc = dense_ffn(xc, norm_g[i, 2], mod_c[4], mod_c[3], wg, wu, wd, norm_g[i, 3], mod_c[5], n_ctx)
        else:
            wg, wu, wd = moe_wg[j], moe_wu[j], moe_wd[j]
            streams = [(xl, mod_l, t)] + ([(xc, mod_c, n_ctx)] if need_ctx else [])
            outs = []
            for xs, ms, rpg in streams:
                n_tok = xs.shape[0]
                h, logits = prenorm(xs, norm_g[i, 2], ms[4], ms[3], rpg, router=moe_router[j])
                n_grp = MOE_GROUPS if n_tok % (MOE_GROUPS * SUBLANES) == 0 else 1
                gsz = n_tok // n_grp
                tm = _pick_tile(gsz, (1024, 512, 256, 128, 64, 32, 16, 8))
                fs = []
                for gi in range(n_grp):
                    row_token, tile_expert, n_used, dest, wts = _moe_dispatch(logits[gi * gsz:(gi + 1) * gsz], tm)
                    yg = grouped_swiglu(jnp.take(h, row_token + gi * gsz, axis=0), tile_expert, n_used,
                                        wg, wu, wd, tm)
                    fs.append(wts[:, 0:1] * jnp.take(yg, dest[:, 0], axis=0)
                              + wts[:, 1:2] * jnp.take(yg, dest[:, 1], axis=0))
                f = fs[0] if n_grp == 1 else jnp.concatenate(fs, axis=0)
                outs.append(resnorm(xs, f, norm_g[i, 3], ms[5], rpg))
            xl = outs[0]
            if need_ctx:
                xc = outs[1]
    return xl.reshape(bsz, t, d)
```

```python
import functools
import math

import jax
import jax.numpy as jnp
from jax import lax
from jax.experimental import pallas as pl
from jax.experimental.pallas import tpu as pltpu

F32 = jnp.float32
BF16 = jnp.bfloat16

D_MODEL = 1024
GRID_W = 64
ATT_HEADS = 8
ATT_KV_HEADS = 2
ATT_GROUP = ATT_HEADS // ATT_KV_HEADS
HEAD_DIM = 64
WINDOW = 128
ATT_BLOCK = 128
ATT_SCALE = 1.0 / math.sqrt(HEAD_DIM)
ROPE_BASE = 10000.0
AXIS_DIM = HEAD_DIM // 2
RWKV_HEADS = 8
RWKV_HEAD = 64
RWKV_W = RWKV_HEADS * RWKV_HEAD
DECAY_RANK = 32
ICLR_RANK = 32
GATE_RANK = 96
LNX_EPS = 64e-5
CONV_W = 512
CONV_K = 31
CONV_LN_EPS = 1e-5
N_BRANCH = 3
N_EXPERTS = 8
TOP_K = 2
NORM_EPS = 1e-6
Q_W = ATT_HEADS * HEAD_DIM
KV_W = ATT_KV_HEADS * HEAD_DIM
OFF_Q = 0
OFF_K = OFF_Q + Q_W
OFF_V = OFF_K + KV_W
OFF_R = OFF_V + KV_W
OFF_CONV = OFF_R + 3 * RWKV_W + 2 * DECAY_RANK + 2 * ICLR_RANK + GATE_RANK
OFF_GATE = OFF_CONV + 2 * CONV_W
IN_COLS = OFF_GATE + N_BRANCH * D_MODEL

ZQ = 0
ZQP = 512
ZK = 1024
ZKP = 1152
ZV = 1280
ZWA = 1408
ZCA = 1536
ZCB = 2048
ZR = 2560
ZRK = 3072
ZRV = 3584
ZG = 4096
ZZG = 7168
NZ = 7680

VMEM_LIMIT = 56 * 1024 * 1024
MOE_GROUPS = 2


def _cparams(sem):
    return pltpu.CompilerParams(dimension_semantics=sem, vmem_limit_bytes=VMEM_LIMIT)


def _pick_tile(n, candidates):
    for c in candidates:
        if n % c == 0:
            return c
    return n


def _matmul_bias_kernel(a_ref, w_ref, b_ref, o_ref):
    o_ref[...] = jnp.dot(a_ref[...], w_ref[...], preferred_element_type=F32,
                         precision=lax.Precision.HIGHEST) + b_ref[...]


def matmul_bias_f32(a, w, b):
    m, k = a.shape
    n = w.shape[1]
    tn = _pick_tile(n, (1536, 1024, 512, 256, 128))
    return pl.pallas_call(
        _matmul_bias_kernel,
        grid=(n // tn,),
        in_specs=[pl.BlockSpec((m, k), lambda j: (0, 0)),
                  pl.BlockSpec((k, tn), lambda j: (0, j)),
                  pl.BlockSpec((1, tn), lambda j: (0, j))],
        out_specs=pl.BlockSpec((m, tn), lambda j: (0, j)),
        out_shape=jax.ShapeDtypeStruct((m, n), F32),
        compiler_params=_cparams(("parallel",)),
        name="mod_matmul",
    )(a, w, b.reshape(1, n))


def _norm_mod(x, g, sc, sh):
    ms = jnp.mean(x * x, axis=-1, keepdims=True)
    return (x * lax.rsqrt(ms + NORM_EPS) * g) * (1.0 + sc) + sh


def _nm_matmul_kernel(x_ref, g_ref, sc_ref, sh_ref, w_ref, o_ref, h_ref):
    @pl.when(pl.program_id(1) == 0)
    def _():
        h_ref[...] = _norm_mod(x_ref[...], g_ref[...], sc_ref[0], sh_ref[0]).astype(BF16)

    o_ref[...] = jnp.dot(h_ref[...], w_ref[...], preferred_element_type=F32).astype(o_ref.dtype)


def nm_matmul(x, g, sc, sh, w, rows_per_group):
    m, d = x.shape
    n = w.shape[1]
    tm = _pick_tile(rows_per_group, (1024, 512, 256, 128, 64, 32, 16, 8))
    tn = _pick_tile(n, (1280, 1024, 512, 256, 128))
    grp = lambda i, j: ((i * tm) // rows_per_group, 0, 0)
    return pl.pallas_call(
        _nm_matmul_kernel,
        grid=(m // tm, n // tn),
        in_specs=[pl.BlockSpec((tm, d), lambda i, j: (i, 0)),
                  pl.BlockSpec((1, d), lambda i, j: (0, 0)),
                  pl.BlockSpec((1, 1, d), grp),
                  pl.BlockSpec((1, 1, d), grp),
                  pl.BlockSpec((d, tn), lambda i, j: (0, j))],
        out_specs=pl.BlockSpec((tm, tn), lambda i, j: (i, j)),
        out_shape=jax.ShapeDtypeStruct((m, n), F32),
        scratch_shapes=[pltpu.VMEM((tm, d), BF16)],
        compiler_params=_cparams(("parallel", "arbitrary")),
        name="in_proj",
    )(x, g.reshape(1, d), sc, sh, w)


def _prenorm_kernel(x_ref, g_ref, sc_ref, sh_ref, h_ref):
    h_ref[...] = _norm_mod(x_ref[...], g_ref[...], sc_ref[0], sh_ref[0]).astype(h_ref.dtype)


def _prenorm_router_kernel(x_ref, g_ref, sc_ref, sh_ref, r_ref, h_ref, l_ref):
    h = _norm_mod(x_ref[...], g_ref[...], sc_ref[0], sh_ref[0])
    h_ref[...] = h.astype(h_ref.dtype)
    l_ref[...] = jnp.dot(h, r_ref[...], preferred_element_type=F32, precision=lax.Precision.HIGHEST)


def prenorm(x, g, sc, sh, rows_per_group, router=None):
    m, d = x.shape
    tm = _pick_tile(rows_per_group, (512, 256, 128, 64, 32, 16, 8))
    grp = lambda i: ((i * tm) // rows_per_group, 0, 0)
    in_specs = [pl.BlockSpec((tm, d), lambda i: (i, 0)),
                pl.BlockSpec((1, d), lambda i: (0, 0)),
                pl.BlockSpec((1, 1, d), grp),
                pl.BlockSpec((1, 1, d), grp)]
    if router is None:
        return pl.pallas_call(
            _prenorm_kernel, grid=(m // tm,), in_specs=in_specs,
            out_specs=pl.BlockSpec((tm, d), lambda i: (i, 0)),
            out_shape=jax.ShapeDtypeStruct((m, d), BF16),
            compiler_params=_cparams(("parallel",)), name="prenorm",
        )(x, g.reshape(1, d), sc, sh)
    ne = router.shape[1]
    rpad = jnp.zeros((d, 128), F32).at[:, :ne].set(router)
    h, logits = pl.pallas_call(
        _prenorm_router_kernel, grid=(m // tm,),
        in_specs=in_specs + [pl.BlockSpec((d, 128), lambda i: (0, 0))],
        out_specs=[pl.BlockSpec((tm, d), lambda i: (i, 0)), pl.BlockSpec((tm, 128), lambda i: (i, 0))],
        out_shape=[jax.ShapeDtypeStruct((m, d), BF16), jax.ShapeDtypeStruct((m, 128), F32)],
        compiler_params=_cparams(("parallel",)), name="prenorm_router",
    )(x, g.reshape(1, d), sc, sh, rpad)
    return h, logits[:, :ne]


def _resnorm_kernel(x_ref, f_ref, g_ref, gt_ref, o_ref):
    f = f_ref[...]
    ms = jnp.mean(f * f, axis=-1, keepdims=True)
    o_ref[...] = x_ref[...] + gt_ref[0] * (f * lax.rsqrt(ms + NORM_EPS) * g_ref[...])


def resnorm(x, f, g, gt, rows_per_group):
    m, d = x.shape
    tm = _pick_tile(rows_per_group, (512, 256, 128, 64, 32, 16, 8))
    grp = lambda i: ((i * tm) // rows_per_group, 0, 0)
    return pl.pallas_call(
        _resnorm_kernel, grid=(m // tm,),
        in_specs=[pl.BlockSpec((tm, d), lambda i: (i, 0)),
                  pl.BlockSpec((tm, d), lambda i: (i, 0)),
                  pl.BlockSpec((1, d), lambda i: (0, 0)),
                  pl.BlockSpec((1, 1, d), grp)],
        out_specs=pl.BlockSpec((tm, d), lambda i: (i, 0)),
        out_shape=jax.ShapeDtypeStruct((m, d), F32),
        compiler_params=_cparams(("parallel",)), name="resnorm",
    )(x, f, g.reshape(1, d), gt)


LOG2E = 1.0 / math.log(2.0)


def _attn_heads(q, k, v, bias, sink_ref, o_ref):
    qs = (q * (ATT_SCALE * LOG2E)).astype(BF16)
    ones = jnp.ones((v.shape[0], HEAD_DIM), BF16)
    heads = range(ATT_HEADS)
    kh = [_hslice(k, h) for h in range(ATT_KV_HEADS)]
    v1 = [jnp.concatenate([_hslice(v, h), ones], axis=1) for h in range(ATT_KV_HEADS)]
    sink2 = [sink_ref[hd] * LOG2E for hd in heads]
    s = [lax.dot_general(_hslice(qs, hd), kh[hd // ATT_GROUP], (((1,), (1,)), ((), ())),
                         preferred_element_type=F32) for hd in heads]
    if bias is not None:
        s = [sh + bias for sh in s]
    m = [jnp.maximum(jnp.max(s[hd], axis=-1, keepdims=True), sink2[hd]) for hd in heads]
    p = [jnp.exp2(s[hd] - m[hd]).astype(BF16) for hd in heads]
    pv = [jnp.dot(p[hd], v1[hd // ATT_GROUP], preferred_element_type=F32) for hd in heads]
    outs = [pv[hd][:, :HEAD_DIM] / (pv[hd][:, HEAD_DIM:] + jnp.exp2(sink2[hd] - m[hd])) for hd in heads]
    o_ref[0] = jnp.concatenate(outs, axis=-1)


def _hslice(t, h):
    return t[:, h * HEAD_DIM:(h + 1) * HEAD_DIM]


def _win_attn_kernel(sink_ref, q_ref, qp_ref, cq_ref, sq_ref,
                     k0_ref, k1_ref, k2_ref, kp0_ref, kp1_ref, kp2_ref,
                     v0_ref, v1_ref, v2_ref, c0_ref, c1_ref, c2_ref, s0_ref, s1_ref, s2_ref,
                     kc_ref, vc_ref, o_ref, *, nb):
    j = pl.program_id(1)
    q = q_ref[0] * cq_ref[...] + qp_ref[0] * sq_ref[...]
    kb = jnp.concatenate([k0_ref[0] * c0_ref[...] + kp0_ref[0] * s0_ref[...],
                          k1_ref[0] * c1_ref[...] + kp1_ref[0] * s1_ref[...],
                          k2_ref[0] * c2_ref[...] + kp2_ref[0] * s2_ref[...]], axis=0).astype(BF16)
    vb = jnp.concatenate([v0_ref[0], v1_ref[0], v2_ref[0]], axis=0).astype(BF16)
    n_ctx = kc_ref.shape[1]
    k = jnp.concatenate([kc_ref[0].astype(BF16), kb], axis=0)
    v = jnp.concatenate([vc_ref[0].astype(BF16), vb], axis=0)
    n = n_ctx + 3 * ATT_BLOCK
    qi = lax.broadcasted_iota(jnp.int32, (ATT_BLOCK, n), 0) + ATT_BLOCK
    ki = lax.broadcasted_iota(jnp.int32, (ATT_BLOCK, n), 1) - n_ctx
    lo = jnp.where(j == 0, ATT_BLOCK, 0)
    hi = jnp.where(j == nb - 1, 2 * ATT_BLOCK, 3 * ATT_BLOCK)
    far = jnp.where(ki < lo, 4 * ATT_BLOCK, 0) + jnp.where(ki >= hi, 4 * ATT_BLOCK, 0)
    seen = jnp.where(ki < 0, 0, jnp.abs(qi - ki) + far) <= WINDOW
    bias = jnp.where(seen, 0.0, -jnp.inf).astype(F32)
    _attn_heads(q, k, v, bias, sink_ref, o_ref)


def _ctx_attn_kernel(sink_ref, q_ref, kc_ref, vc_ref, o_ref):
    _attn_heads(q_ref[0], kc_ref[0].astype(BF16), vc_ref[0].astype(BF16), None, sink_ref, o_ref)


def windowed_attention(zl3, zc3, cos_t, sin_t, sink):
    b, t, _ = zl3.shape
    l = zc3.shape[1]
    nb = t // ATT_BLOCK
    kcol, kpcol, vcol = ZK // KV_W, ZKP // KV_W, ZV // KV_W
    prev = lambda j: jnp.maximum(j - 1, 0)
    nxt = lambda j: jnp.minimum(j + 1, nb - 1)

    def zspec(width, rowf, col):
        return pl.BlockSpec((1, ATT_BLOCK, width), lambda bi, j: (bi, rowf(j), col))

    def tspec(width, rowf):
        return pl.BlockSpec((ATT_BLOCK, width), lambda bi, j: (rowf(j), 0))

    same = lambda j: j
    in_specs = [pl.BlockSpec(memory_space=pltpu.SMEM),
                zspec(Q_W, same, ZQ // Q_W), zspec(Q_W, same, ZQP // Q_W), tspec(Q_W, same), tspec(Q_W, same)]
    args = [sink, zl3, zl3, cos_t, sin_t]
    for col in (kcol, kpcol, vcol):
        in_specs += [zspec(KV_W, prev, col), zspec(KV_W, same, col), zspec(KV_W, nxt, col)]
        args += [zl3, zl3, zl3]
    for tab in (cos_t, sin_t):
        in_specs += [tspec(KV_W, prev), tspec(KV_W, same), tspec(KV_W, nxt)]
        args += [tab, tab, tab]
    in_specs += [pl.BlockSpec((1, l, KV_W), lambda bi, j: (bi, 0, kcol)),
                 pl.BlockSpec((1, l, KV_W), lambda bi, j: (bi, 0, vcol))]
    args += [zc3, zc3]
    return pl.pallas_call(
        functools.partial(_win_attn_kernel, nb=nb),
        grid=(b, nb), in_specs=in_specs,
        out_specs=pl.BlockSpec((1, ATT_BLOCK, Q_W), lambda bi, j: (bi, j, 0)),
        out_shape=jax.ShapeDtypeStruct((b, t, Q_W), F32),
        compiler_params=_cparams(("parallel", "parallel")), name="win_attn",
    )(*args)


def context_attention(zc3, sink):
    b, l, _ = zc3.shape
    return pl.pallas_call(
        _ctx_attn_kernel, grid=(b,),
        in_specs=[pl.BlockSpec(memory_space=pltpu.SMEM),
                  pl.BlockSpec((1, l, Q_W), lambda bi: (bi, 0, ZQ // Q_W)),
                  pl.BlockSpec((1, l, KV_W), lambda bi: (bi, 0, ZK // KV_W)),
                  pl.BlockSpec((1, l, KV_W), lambda bi: (bi, 0, ZV // KV_W))],
        out_specs=pl.BlockSpec((1, l, Q_W), lambda bi: (bi, 0, 0)),
        out_shape=jax.ShapeDtypeStruct((b, l, Q_W), F32),
        compiler_params=_cparams(("parallel",)), name="ctx_attn",
    )(sink, zc3, zc3, zc3)


HALO = 8


def _head_sums(x, e_ref):
    hi = x.astype(BF16)
    lo = (x - hi.astype(F32)).astype(BF16)
    return (jnp.dot(hi, e_ref[...], preferred_element_type=F32)
            + jnp.dot(lo, e_ref[...], preferred_element_type=F32))


def _shift_lerp_tile(cur_ref, prev_ref, next_ref, mu_ref, buf_ref, first, last):
    tt = cur_ref.shape[1]
    z = cur_ref[0]
    buf_ref[0:HALO] = jnp.where(first, 0.0, prev_ref[0])
    buf_ref[HALO:HALO + tt] = z
    buf_ref[HALO + tt:2 * HALO + tt] = jnp.where(last, 0.0, next_ref[0])
    nb = buf_ref[HALO - 1:HALO - 1 + tt] + buf_ref[HALO + 1:HALO + 1 + tt]
    return z + mu_ref[...] * (0.5 * nb - z)


def _split_bf16(a):
    hi = a.astype(BF16)
    return hi, (a - hi.astype(F32)).astype(BF16)


def _dot_3pass(a, b):
    a_hi, a_lo = _split_bf16(a)
    b_hi, b_lo = _split_bf16(b)
    return (jnp.dot(a_hi, b_hi, preferred_element_type=F32) + jnp.dot(a_lo, b_hi, preferred_element_type=F32)
            + jnp.dot(a_hi, b_lo, preferred_element_type=F32))


def _rwkv_prep_kernel(*refs, has_vres, n_alias):
    (wa_c, wa_p, wa_n, r_c, r_p, r_n, k_c, k_p, k_n, v_c, v_p, v_n, g_c, g_p, g_n,
     mu_wa, mu_r, mu_k, mu_v, mu_g, w0_ref, w2_ref, a0_ref, a2_ref, g2_ref, kk_ref, ka_ref, rk_ref, e_ref) = refs[:29]
    pos = 29
    if has_vres:
        vf_ref, v0_ref, v1_ref, v2_ref = refs[pos:pos + 4]
        pos += 4
    pos += n_alias
    (rk_o, vk_o, w_o, an_o, bv_o, g_o) = refs[pos:pos + 6]
    pos += 6
    if not has_vres:
        vin_o = refs[pos]
        pos += 1
    buf_wa, buf_r, buf_k, buf_v, buf_g = refs[pos:pos + 5]

    i = pl.program_id(1)
    first = i == 0
    last = i == pl.num_programs(1) - 1
    zwa = _shift_lerp_tile(wa_c, wa_p, wa_n, mu_wa, buf_wa, first, last)
    r = _shift_lerp_tile(r_c, r_p, r_n, mu_r, buf_r, first, last)
    k = _shift_lerp_tile(k_c, k_p, k_n, mu_k, buf_k, first, last)
    v = _shift_lerp_tile(v_c, v_p, v_n, mu_v, buf_v, first, last)
    zg = _shift_lerp_tile(g_c, g_p, g_n, mu_g, buf_g, first, last)

    if has_vres:
        mix = _dot_3pass(_dot_3pass(v, v1_ref[...]), v2_ref[...])
        v = v + (vf_ref[0] - v) * jax.nn.sigmoid(v0_ref[...] + mix)
    else:
        vin_o[0] = v
    kk = k * kk_ref[...]
    kk = kk / jnp.maximum(jnp.sqrt(_head_sums(kk * kk, e_ref)), 1e-12)
    tz = jnp.tanh(zwa)
    ksum = None
    for d in range(2):
        w_log = -jax.nn.softplus(-(w0_ref[d:d + 1, :] + _dot_3pass(tz, w2_ref[d]))) - 0.5
        w_o[d, 0] = jnp.exp(-jnp.exp(w_log))
        a = jax.nn.sigmoid(a0_ref[d:d + 1, :] + _dot_3pass(zwa, a2_ref[d]))
        an_o[d, 0] = -a
        kd = k * (1.0 + (a - 1.0) * ka_ref[...])
        ksum = kd if ksum is None else ksum + kd
    rk_o[0, 0] = r
    rk_o[1, 0] = k
    vk_o[0, 0] = v
    vk_o[1, 0] = kk
    bv_o[0] = _head_sums(r * ksum * rk_ref[...], e_ref) * v
    g_o[0] = _dot_3pass(jax.nn.sigmoid(zg), g2_ref[...])


def _head_ones():
    idx = jnp.arange(RWKV_W) // RWKV_HEAD
    return (idx[:, None] == idx[None, :]).astype(BF16)


def rwkv_prep(z3, lp, v_first, ttot, row_off, pair_bufs):
    b, tx, _ = z3.shape
    tt = _pick_tile(math.gcd(tx, row_off) if row_off else tx, (256, 128, 64, 32, 16, 8))
    nh = tt // HALO
    nblk = tx // HALO
    has_vres = v_first is not None
    lanes = 128

    def trio(width, col):
        return [pl.BlockSpec((1, tt, width), lambda bi, i: (bi, i, col)),
                pl.BlockSpec((1, HALO, width), lambda bi, i: (bi, jnp.maximum(i * nh - 1, 0), col)),
                pl.BlockSpec((1, HALO, width), lambda bi, i: (bi, jnp.minimum((i + 1) * nh, nblk - 1), col))]

    def full(shape):
        return pl.BlockSpec(shape, lambda bi, i: (0,) * len(shape))

    mu = lp['mu']
    o = 3 * RWKV_W
    nwa = 2 * DECAY_RANK + 2 * ICLR_RANK
    mu_g = jnp.zeros((lanes,), F32).at[:GATE_RANK].set(mu[o + nwa:])
    w2f = jnp.zeros((2, lanes, RWKV_W), F32)
    a2f = jnp.zeros((2, lanes, RWKV_W), F32)
    for d in range(2):
        w2f = w2f.at[d, d * DECAY_RANK:(d + 1) * DECAY_RANK].set(lp['w2'][d])
        a2f = a2f.at[d, 2 * DECAY_RANK + d * ICLR_RANK:2 * DECAY_RANK + (d + 1) * ICLR_RANK].set(lp['a2'][d])
    g2f = jnp.zeros((lanes, RWKV_W), F32).at[:GATE_RANK].set(lp['g2'])
    row = lambda a: a.reshape(1, -1)
    in_specs = (trio(lanes, ZWA // lanes) + trio(RWKV_W, ZR // RWKV_W) + trio(RWKV_W, ZRK // RWKV_W)
                + trio(RWKV_W, ZRV // RWKV_W) + trio(lanes, ZZG // lanes)
                + [full((1, lanes)), full((1, RWKV_W)), full((1, RWKV_W)), full((1, RWKV_W)), full((1, lanes)),
                   full((2, RWKV_W)), full((2, lanes, RWKV_W)), full((2, RWKV_W)), full((2, lanes, RWKV_W)),
                   full((lanes, RWKV_W)), full((1, RWKV_W)), full((1, RWKV_W)), full((1, RWKV_W)),
                   full((RWKV_W, RWKV_W))])
    args = [z3] * 15 + [row(mu[o:o + nwa]), row(mu[:RWKV_W]), row(mu[RWKV_W:2 * RWKV_W]), row(mu[2 * RWKV_W:o]),
                        row(mu_g), lp['w0'], w2f, lp['a0'], a2f, g2f, row(lp['k_k']), row(lp['k_a']),
                        row(lp['r_k']), _head_ones()]
    tok = pl.BlockSpec((1, tt, RWKV_W), lambda bi, i: (bi, i, 0))
    if has_vres:
        rank = lp['v1'].shape[1]
        v1f = jnp.zeros((RWKV_W, lanes), F32).at[:, :rank].set(lp['v1'])
        v2f = jnp.zeros((lanes, RWKV_W), F32).at[:rank].set(lp['v2'])
        in_specs += [tok, full((1, RWKV_W)), full((RWKV_W, lanes)), full((lanes, RWKV_W))]
        args += [v_first, row(lp['v0']), v1f, v2f]
    n_alias = 0 if pair_bufs is None else len(pair_bufs)
    aliases = {}
    if pair_bufs is not None:
        aliases = {len(args) + n: n for n in range(n_alias)}
        in_specs += [pl.BlockSpec(memory_space=pl.ANY)] * n_alias
        args += list(pair_bufs)
    blk_off = row_off // tt
    pair = pl.BlockSpec((2, 1, tt, RWKV_W), lambda bi, i: (0, bi, i + blk_off, 0))
    pair_shape = jax.ShapeDtypeStruct((2, b, ttot, RWKV_W), F32)
    tok_shape = jax.ShapeDtypeStruct((b, tx, RWKV_W), F32)
    n_tok = 2 if has_vres else 3
    outs = pl.pallas_call(
        functools.partial(_rwkv_prep_kernel, has_vres=has_vres, n_alias=n_alias),
        grid=(b, tx // tt), in_specs=in_specs,
        out_specs=[pair] * 4 + [tok] * n_tok,
        out_shape=[pair_shape] * 4 + [tok_shape] * n_tok,
        input_output_aliases=aliases,
        scratch_shapes=[pltpu.VMEM((tt + 2 * HALO, lanes), F32)] + [pltpu.VMEM((tt + 2 * HALO, RWKV_W), F32)] * 3
                       + [pltpu.VMEM((tt + 2 * HALO, lanes), F32)],
        compiler_params=_cparams(("parallel", "parallel")), name="rwkv_prep",
    )(*args)
    res = {'pairs': tuple(outs[:4]), 'bv': outs[4], 'g': outs[5]}
    if not has_vres:
        res['v_in'] = outs[6]
    return res


CONV_HALO = 16
CONV_ROWS = 32


def _conv_kernel(a_c, a_p, a_n, b_c, b_p, b_n, w_ref, bias_ref, lg_ref, lb_ref, o_ref, buf_ref, win_ref):
    i = pl.program_id(1)
    first = i == 0
    last = i == pl.num_programs(1) - 1
    tt = a_c.shape[1]
    glu = lambda a, b: a * jax.nn.sigmoid(b)
    buf_ref[0:CONV_HALO] = jnp.where(first, 0.0, glu(a_p[0], b_p[0]))
    buf_ref[CONV_HALO:CONV_HALO + tt] = glu(a_c[0], b_c[0])
    buf_ref[CONV_HALO + tt:2 * CONV_HALO + tt] = jnp.where(last, 0.0, glu(a_n[0], b_n[0]))
    half = CONV_K // 2
    span = CONV_ROWS + ((CONV_K - 1) // SUBLANES) * SUBLANES
    for r0 in range(0, tt, CONV_ROWS):
        base = CONV_HALO - half + r0
        acc = bias_ref[...]
        for ph in range(SUBLANES):
            win_ref[...] = buf_ref[base + ph:base + ph + span]
            for j in range(ph, CONV_K, SUBLANES):
                acc = acc + win_ref[j - ph:j - ph + CONV_ROWS] * w_ref[j:j + 1, :]
        m = jnp.mean(acc, axis=-1, keepdims=True)
        cen = acc - m
        var = jnp.mean(cen * cen, axis=-1, keepdims=True)
        y = cen * lax.rsqrt(var + CONV_LN_EPS) * lg_ref[...] + lb_ref[...]
        o_ref[0, r0:r0 + CONV_ROWS, :] = y * jax.nn.sigmoid(y)


def conv_module(z3, lp):
    b, tx, _ = z3.shape
    tt = _pick_tile(tx, (256, 128, 64, 32))
    nh = tt // CONV_HALO
    nblk = tx // CONV_HALO

    def trio(col):
        return [pl.BlockSpec((1, tt, CONV_W), lambda bi, i: (bi, i, col)),
                pl.BlockSpec((1, CONV_HALO, CONV_W), lambda bi, i: (bi, jnp.maximum(i * nh - 1, 0), col)),
                pl.BlockSpec((1, CONV_HALO, CONV_W), lambda bi, i: (bi, jnp.minimum((i + 1) * nh, nblk - 1), col))]

    kpad = -(-CONV_K // SUBLANES) * SUBLANES
    wpad = jnp.zeros((kpad, CONV_W), F32).at[:CONV_K].set(lp['dw_w'])
    vec = pl.BlockSpec((1, CONV_W), lambda bi, i: (0, 0))
    return pl.pallas_call(
        _conv_kernel, grid=(b, tx // tt),
        in_specs=trio(ZCA // CONV_W) + trio(ZCB // CONV_W)
                 + [pl.BlockSpec((kpad, CONV_W), lambda bi, i: (0, 0)), vec, vec, vec],
        out_specs=pl.BlockSpec((1, tt, CONV_W), lambda bi, i: (bi, i, 0)),
        out_shape=jax.ShapeDtypeStruct((b, tx, CONV_W), F32),
        scratch_shapes=[pltpu.VMEM((tt + 2 * CONV_HALO, CONV_W), F32),
                        pltpu.VMEM((CONV_ROWS + ((CONV_K - 1) // SUBLANES) * SUBLANES, CONV_W), F32)],
        compiler_params=_cparams(("parallel", "parallel")), name="conv_module",
    )(z3, z3, z3, z3, z3, z3, wpad, lp['dw_b'].reshape(1, -1), lp['cln_g'].reshape(1, -1), lp['cln_b'].reshape(1, -1))


SUBLANES = 8
DECAY_FOLD = 16


def _wkv_scan_kernel(rk_f, vk_f, w_f, an_f, rk_b, vk_b, w_b, an_b, ka_ref, yf_ref, yb_ref, s_ref, *, tt):
    @pl.when(pl.program_id(0) == 0)
    def _():
        s_ref[...] = jnp.zeros_like(s_ref)

    nk, _, lanes = s_ref.shape
    half = lanes // 2
    fwd = lax.broadcasted_iota(jnp.int32, (nk, lanes), 1) < half
    ka = ka_ref[...]
    fold = math.gcd(tt, DECAY_FOLD)

    def group(gi, carry):
        def step(ti, gam):
            t = gi * fold + ti
            tb = tt - 1 - t
            rkf, rkb, vkf, vkb = rk_f[t], rk_b[tb], vk_f[t], vk_b[tb]
            r = jnp.where(fwd, rkf, pltpu.roll(rkb, half, 1))
            k = jnp.where(fwd, pltpu.roll(rkf, half, 1), rkb)
            v = jnp.where(fwd, vkf, pltpu.roll(vkb, half, 1))
            kk = jnp.where(fwd, pltpu.roll(vkf, half, 1), vkb)
            w = jnp.where(fwd, w_f[t], w_b[tb])
            an = jnp.where(fwd, an_f[t], an_b[tb])
            gam_new = gam * w
            inv = 1.0 / gam_new
            kks = kk * gam
            kds = k * (1.0 - (an + 1.0) * ka) * inv
            bbs = kk * an * inv
            rs = r * gam_new
            row = lambda a, ki: a[ki:ki + 1, :]
            sa = [s_ref[0] * row(kks, 0), s_ref[1] * row(kks, 1)]
            for ki in range(2, nk):
                sa[ki % 2] = sa[ki % 2] + s_ref[ki] * row(kks, ki)
            sa = sa[0] + sa[1]
            ys = [None, None]
            for ki in range(nk):
                s_new = s_ref[ki] + sa * row(bbs, ki) + v * row(kds, ki)
                s_ref[ki] = s_new
                contrib = s_new * row(rs, ki)
                ys[ki % 2] = contrib if ys[ki % 2] is None else ys[ki % 2] + contrib
            y = ys[0] + ys[1]
            yf_ref[t] = y
            yb_ref[tb] = y
            return gam_new

        gam = lax.fori_loop(0, fold, step, jnp.ones((nk, lanes), F32))
        for ki in range(nk):
            s_ref[ki] = s_ref[ki] * gam[ki:ki + 1, :]
        return carry

    lax.fori_loop(0, tt // fold, group, 0)


def wkv_scan(rk, vk, w, an, ka, n_ctx_steps):
    ttot, n, ch = rk.shape
    tt = _pick_tile(math.gcd(n_ctx_steps, ttot - n_ctx_steps), (32, 16, 8, 4, 2, 1))
    nt, nc = ttot // tt, n_ctx_steps // tt
    fspec = pl.BlockSpec((tt, n, ch), lambda i: (i, 0, 0))
    bspec = pl.BlockSpec((tt, n, ch), lambda i: (jnp.where(i < nc, nc - 1 - i, nt - 1 - (i - nc)), 0, 0))
    return pl.pallas_call(
        functools.partial(_wkv_scan_kernel, tt=tt),
        grid=(nt,), in_specs=[fspec] * 4 + [bspec] * 4 + [pl.BlockSpec((n, ch), lambda i: (0, 0))],
        out_specs=[fspec, bspec],
        out_shape=[jax.ShapeDtypeStruct((ttot, n, ch), F32)] * 2,
        scratch_shapes=[pltpu.VMEM((n, n, ch), F32)],
        compiler_params=_cparams(("arbitrary",)), name="wkv_scan",
    )(rk, vk, w, an, rk, vk, w, an, ka)


def _merge_kernel(x_ref, att_ref, y_ref, bv_ref, rg_ref, cv_ref, g0_ref, g1_ref, g2_ref,
                  lg_ref, lb_ref, e_ref, wa_ref, wr_ref, wc_ref, wo_ref, ng_ref, gt_ref, o_ref):
    def proj(a, w_ref):
        return jnp.dot(a.astype(BF16), w_ref[...], preferred_element_type=F32)

    y = y_ref[...]
    cen = y - _head_sums(y, e_ref) * (1.0 / RWKV_HEAD)
    var = _head_sums(cen * cen, e_ref) * (1.0 / RWKV_HEAD)
    rw = (cen * lax.rsqrt(var + LNX_EPS) * lg_ref[...] + lb_ref[...] + bv_ref[...]) * rg_ref[...]
    m = (jax.nn.sigmoid(g0_ref[...]) * proj(att_ref[...], wa_ref)
         + jax.nn.sigmoid(g1_ref[...]) * proj(rw, wr_ref)
         + jax.nn.sigmoid(g2_ref[...]) * proj(cv_ref[...], wc_ref))
    ml = proj(m, wo_ref)
    ms = jnp.mean(ml * ml, axis=-1, keepdims=True)
    o_ref[...] = x_ref[...] + gt_ref[0] * (ml * lax.rsqrt(ms + NORM_EPS) * ng_ref[...])


def merge_out(x, att, y, bv, rg, cv, z, lnx_g, lnx_b, wa, wr, wc, wo, ng, gt, rows_per_group):
    m, d = x.shape
    tm = _pick_tile(rows_per_group, (512, 256, 128, 64, 32, 16, 8))
    row = lambda i: (i, 0)
    const = lambda i: (0, 0)
    gcol = ZG // d
    br = pl.BlockSpec((tm, RWKV_W), row)
    return pl.pallas_call(
        _merge_kernel, grid=(m // tm,),
        in_specs=[pl.BlockSpec((tm, d), row),
                  pl.BlockSpec((tm, Q_W), row), br, br, br, pl.BlockSpec((tm, CONV_W), row),
                  pl.BlockSpec((tm, d), lambda i: (i, gcol)),
                  pl.BlockSpec((tm, d), lambda i: (i, gcol + 1)),
                  pl.BlockSpec((tm, d), lambda i: (i, gcol + 2)),
                  pl.BlockSpec((1, RWKV_W), const), pl.BlockSpec((1, RWKV_W), const),
                  pl.BlockSpec((RWKV_W, RWKV_W), const),
                  pl.BlockSpec((Q_W, d), const), pl.BlockSpec((RWKV_W, d), const), pl.BlockSpec((CONV_W, d), const),
                  pl.BlockSpec((d, d), const), pl.BlockSpec((1, d), const),
                  pl.BlockSpec((1, 1, d), lambda i: ((i * tm) // rows_per_group, 0, 0))],
        out_specs=pl.BlockSpec((tm, d), row),
        out_shape=jax.ShapeDtypeStruct((m, d), F32),
        compiler_params=_cparams(("parallel",)), name="merge_out",
    )(x, att, y, bv, rg, cv, z, z, z, lnx_g.reshape(1, -1), lnx_b.reshape(1, -1), _head_ones(),
      wa, wr, wc, wo, ng.reshape(1, d), gt)


def _swiglu_kernel(te_ref, nu_ref, h_ref, wg_ref, wu_ref, wd_ref, o_ref):
    f = pl.program_id(1)

    @pl.when(pl.program_id(0) < nu_ref[0])
    def _():
        h = h_ref[...]
        hg = jnp.dot(h, wg_ref[0].astype(BF16), preferred_element_type=F32)
        hu = jnp.dot(h, wu_ref[0].astype(BF16), preferred_element_type=F32)
        act = (hg * jax.nn.sigmoid(hg) * hu).astype(BF16)
        part = jnp.dot(act, wd_ref[0].astype(BF16), preferred_element_type=F32)

        @pl.when(f == 0)
        def _():
            o_ref[...] = part

        @pl.when(f > 0)
        def _():
            o_ref[...] += part


def grouped_swiglu(h, tile_expert, n_used, wg, wu, wd, tm):
    p, d = h.shape
    ff = wg.shape[2]
    tf = _pick_tile(ff, (512, 1408, 256, 128))
    nf = ff // tf
    ti = lambda i, nu: jnp.minimum(i, nu[0] - 1)
    fi = lambda i, f, nu: jnp.where(i < nu[0], f, nf - 1)
    grid_spec = pltpu.PrefetchScalarGridSpec(
        num_scalar_prefetch=2, grid=(p // tm, nf),
        in_specs=[pl.BlockSpec((tm, d), lambda i, f, te, nu: (ti(i, nu), 0)),
                  pl.BlockSpec((1, d, tf), lambda i, f, te, nu: (te[ti(i, nu)], 0, fi(i, f, nu))),
                  pl.BlockSpec((1, d, tf), lambda i, f, te, nu: (te[ti(i, nu)], 0, fi(i, f, nu))),
                  pl.BlockSpec((1, tf, d), lambda i, f, te, nu: (te[ti(i, nu)], fi(i, f, nu), 0))],
        out_specs=pl.BlockSpec((tm, d), lambda i, f, te, nu: (ti(i, nu), 0)))
    return pl.pallas_call(
        _swiglu_kernel, grid_spec=grid_spec,
        out_shape=jax.ShapeDtypeStruct((p, d), F32),
        compiler_params=_cparams(("arbitrary", "arbitrary")), name="swiglu",
    )(tile_expert, n_used, h, wg, wu, wd)


def _dense_ffn_kernel(x_ref, g_ref, sc_ref, sh_ref, wg_ref, wu_ref, wd_ref, g2_ref, gt_ref, o_ref, h_ref, acc_ref):
    f = pl.program_id(1)

    @pl.when(f == 0)
    def _():
        h_ref[...] = _norm_mod(x_ref[...], g_ref[...], sc_ref[0], sh_ref[0]).astype(BF16)

    h = h_ref[...]
    hg = jnp.dot(h, wg_ref[...], preferred_element_type=F32)
    hu = jnp.dot(h, wu_ref[...], preferred_element_type=F32)
    act = (hg * jax.nn.sigmoid(hg) * hu).astype(BF16)
    part = jnp.dot(act, wd_ref[...], preferred_element_type=F32)

    @pl.when(f == 0)
    def _():
        acc_ref[...] = part

    @pl.when(f > 0)
    def _():
        acc_ref[...] += part

    @pl.when(f == pl.num_programs(1) - 1)
    def _():
        y = acc_ref[...]
        ms = jnp.mean(y * y, axis=-1, keepdims=True)
        o_ref[...] = x_ref[...] + gt_ref[0] * (y * lax.rsqrt(ms + NORM_EPS) * g2_ref[...])


def dense_ffn(x, g_in, sc, sh, wg, wu, wd, g_out, gt, rows_per_group):
    m, d = x.shape
    ff = wg.shape[1]
    tm = _pick_tile(rows_per_group, (512, 256, 128, 64, 32, 16, 8))
    tf = _pick_tile(ff, (1408, 512, 256, 128))
    grp = lambda i, f: ((i * tm) // rows_per_group, 0, 0)
    vec = pl.BlockSpec((1, d), lambda i, f: (0, 0))
    return pl.pallas_call(
        _dense_ffn_kernel, grid=(m // tm, ff // tf),
        in_specs=[pl.BlockSpec((tm, d), lambda i, f: (i, 0)), vec,
                  pl.BlockSpec((1, 1, d), grp), pl.BlockSpec((1, 1, d), grp),
                  pl.BlockSpec((d, tf), lambda i, f: (0, f)), pl.BlockSpec((d, tf), lambda i, f: (0, f)),
                  pl.BlockSpec((tf, d), lambda i, f: (f, 0)), vec, pl.BlockSpec((1, 1, d), grp)],
        out_specs=pl.BlockSpec((tm, d), lambda i, f: (i, 0)),
        out_shape=jax.ShapeDtypeStruct((m, d), F32),
        scratch_shapes=[pltpu.VMEM((tm, d), BF16), pltpu.VMEM((tm, d), F32)],
        compiler_params=_cparams(("parallel", "arbitrary")), name="dense_ffn",
    )(x, g_in.reshape(1, d), sc, sh, wg, wu, wd, g_out.reshape(1, d), gt)


def _rope_tables(t):
    rows = t // GRID_W
    row = jnp.broadcast_to(jnp.arange(rows)[:, None], (rows, GRID_W)).reshape(-1).astype(F32)
    col = jnp.broadcast_to(jnp.arange(GRID_W)[None, :], (rows, GRID_W)).reshape(-1).astype(F32)
    inv = ROPE_BASE ** (-jnp.arange(0, AXIS_DIM, 2, dtype=F32) / AXIS_DIM)
    ar = row[:, None] * inv[None, :]
    ac = col[:, None] * inv[None, :]
    cos64 = jnp.concatenate([jnp.cos(ar), jnp.cos(ar), jnp.cos(ac), jnp.cos(ac)], axis=-1)
    sin64 = jnp.concatenate([jnp.sin(ar), jnp.sin(ar), jnp.sin(ac), jnp.sin(ac)], axis=-1)
    return jnp.tile(cos64, (1, ATT_HEADS)), jnp.tile(sin64, (1, ATT_HEADS))


def _rope_partner(w):
    d = w.shape[0]
    q = AXIS_DIM // 2
    wh = w.reshape(d, -1, 4, q)
    return jnp.stack([-wh[:, :, 1], wh[:, :, 0], -wh[:, :, 3], wh[:, :, 2]], axis=2).reshape(w.shape)


def _build_w_ext(w_in):
    d = w_in.shape[0]
    wq = w_in[:, OFF_Q:OFF_K]
    wk = w_in[:, OFF_K:OFF_V]
    o = OFF_R + 3 * RWKV_W
    parts = [(ZQ, wq), (ZQP, _rope_partner(wq)), (ZK, wk), (ZKP, _rope_partner(wk)),
             (ZV, w_in[:, OFF_V:OFF_R]),
             (ZWA, w_in[:, o:o + 2 * DECAY_RANK + 2 * ICLR_RANK]),
             (ZCA, w_in[:, OFF_CONV:OFF_CONV + CONV_W]), (ZCB, w_in[:, OFF_CONV + CONV_W:OFF_GATE]),
             (ZR, w_in[:, OFF_R:OFF_R + RWKV_W]), (ZRK, w_in[:, OFF_R + RWKV_W:OFF_R + 2 * RWKV_W]),
             (ZRV, w_in[:, OFF_R + 2 * RWKV_W:OFF_R + 3 * RWKV_W]),
             (ZG, w_in[:, OFF_GATE:IN_COLS]),
             (ZZG, w_in[:, OFF_CONV - GATE_RANK:OFF_CONV])]
    parts.sort(key=lambda p: p[0])
    cols, pos = [], 0
    for off, wpart in parts:
        if off > pos:
            cols.append(jnp.zeros((d, off - pos), w_in.dtype))
        cols.append(wpart)
        pos = off + wpart.shape[1]
    if pos < NZ:
        cols.append(jnp.zeros((d, NZ - pos), w_in.dtype))
    return jnp.concatenate(cols, axis=1).astype(BF16)


def _chain_layout(pair):
    _, bsz, ttot, _ = pair.shape
    both = pair.reshape(2, bsz, ttot, RWKV_HEADS, RWKV_HEAD)
    return jnp.transpose(both, (2, 4, 0, 1, 3)).reshape(ttot, RWKV_HEAD, 2 * bsz * RWKV_HEADS)


def _bidirectional_wkv(pairs, k_a, l):
    bsz = pairs[0].shape[1]
    rk, vk, w, an = (_chain_layout(p) for p in pairs)
    ka = jnp.tile(k_a.reshape(RWKV_HEADS, RWKV_HEAD).T, (1, 2 * bsz))
    yf, yb = wkv_scan(rk, vk, w, an, ka, l)
    half = bsz * RWKV_HEADS
    y = (yf[:, :, :half] + yb[:, :, half:]).reshape(-1, RWKV_HEAD, bsz, RWKV_HEADS)
    y = jnp.transpose(y, (2, 0, 3, 1)).reshape(bsz, -1, RWKV_W)
    return y[:, l:], y[:, :l]


def _moe_dispatch(logits, tm):
    m = logits.shape[0]
    top_v, top_i = lax.top_k(logits, TOP_K)
    wts = jax.nn.softmax(top_v, axis=-1)
    e_flat = top_i.reshape(-1)
    n_assign = e_flat.shape[0]
    onehot = (e_flat[:, None] == jnp.arange(N_EXPERTS)[None, :]).astype(jnp.int32)
    csum = jnp.cumsum(onehot, axis=0)
    rank = jnp.sum(onehot * csum, axis=1) - 1
    cnt = csum[-1]
    pcnt = ((cnt + tm - 1) // tm) * tm
    pend = jnp.cumsum(pcnt)
    pstart = pend - pcnt
    dest = pstart[e_flat] + rank
    p_max = n_assign + N_EXPERTS * tm
    row_token = jnp.zeros((p_max,), jnp.int32).at[dest].set(jnp.arange(n_assign, dtype=jnp.int32) // TOP_K)
    tile_start = jnp.arange(p_max // tm, dtype=jnp.int32) * tm
    tile_expert = jnp.minimum(jnp.searchsorted(pend, tile_start, side='right'), N_EXPERTS - 1).astype(jnp.int32)
    n_used = (pend[-1:] // tm).astype(jnp.int32)
    return row_token, tile_expert, n_used, dest.reshape(m, TOP_K), wts


def kernel(x, c, ctx, c_ctx, mod_w, mod_b, norm_g, w_in, attn_sink, rwkv_mu, rwkv_w0, rwkv_w2, rwkv_a0, rwkv_a2, rwkv_g2, rwkv_k_k, rwkv_k_a, rwkv_r_k, rwkv_lnx_g, rwkv_lnx_b, vres_v0, vres_w1, vres_w2, conv_dw_w, conv_dw_b, conv_ln_g, conv_ln_b, w_attn_o, w_rwkv_o, w_conv_o, w_out, ffn_wg, ffn_wu, ffn_wd, moe_router, moe_wg, moe_wu, moe_wd):
    bsz, t, d = x.shape
    l = ctx.shape[1]
    depth = mod_w.shape[0]
    n_lat, n_ctx = bsz * t, bsz * l
    cos_t, sin_t = _rope_tables(t)
    xl = x.reshape(n_lat, d)
    xc = ctx.reshape(n_ctx, d)
    n_rows = -(-(bsz + 1) // SUBLANES) * SUBLANES
    cond = jnp.zeros((n_rows, d), F32).at[:bsz].set(jax.nn.silu(c)).at[bsz].set(jax.nn.silu(c_ctx))
    v_first = None
    for i in range(depth):
        need_ctx = i < depth - 1
        mod = matmul_bias_f32(cond, mod_w[i], mod_b[i])
        mod_l = [mod[:bsz, k * d:(k + 1) * d].reshape(bsz, 1, d) for k in range(6)]
        mod_c = [mod[bsz:bsz + 1, k * d:(k + 1) * d].reshape(1, 1, d) for k in range(6)]
        lp = {'mu': rwkv_mu[i], 'w0': rwkv_w0[i], 'w2': rwkv_w2[i], 'a0': rwkv_a0[i], 'a2': rwkv_a2[i],
              'g2': rwkv_g2[i], 'k_k': rwkv_k_k[i], 'k_a': rwkv_k_a[i], 'r_k': rwkv_r_k[i],
              'lnx_g': rwkv_lnx_g[i], 'lnx_b': rwkv_lnx_b[i],
              'dw_w': conv_dw_w[i], 'dw_b': conv_dw_b[i], 'cln_g': conv_ln_g[i], 'cln_b': conv_ln_b[i]}
        if i > 0:
            lp['v0'], lp['v1'], lp['v2'] = vres_v0[i - 1], vres_w1[i - 1], vres_w2[i - 1]
        w_ext = _build_w_ext(w_in[i])
        wa, wr, wc, wo = (w.astype(BF16) for w in (w_attn_o[i], w_rwkv_o[i], w_conv_o[i], w_out[i]))

        zl = nm_matmul(xl, norm_g[i, 0], mod_l[1], mod_l[0], w_ext, t)
        zc = nm_matmul(xc, norm_g[i, 0], mod_c[1], mod_c[0], w_ext, n_ctx)
        zl3 = zl.reshape(bsz, t, NZ)
        zc3 = zc.reshape(bsz, l, NZ)
        att_l = windowed_attention(zl3, zc3, cos_t, sin_t, attn_sink[i])
        p_c = rwkv_prep(zc3, lp, None if v_first is None else v_first[1], l + t, 0, None)
        p_l = rwkv_prep(zl3, lp, None if v_first is None else v_first[0], l + t, l, p_c['pairs'])
        if i == 0:
            v_first = (p_l['v_in'], p_c['v_in'])
        y_l, y_c = _bidirectional_wkv(p_l['pairs'], lp['k_a'], l)
        cv_l = conv_module(zl3, lp)
        flat = lambda a: a.reshape(-1, a.shape[-1])
        xl = merge_out(xl, flat(att_l), flat(y_l), flat(p_l['bv']), flat(p_l['g']), flat(cv_l), zl,
                       lp['lnx_g'], lp['lnx_b'], wa, wr, wc, wo, norm_g[i, 1], mod_l[2], t)
        if need_ctx:
            att_c = context_attention(zc3, attn_sink[i])
            cv_c = conv_module(zc3, lp)
            xc = merge_out(xc, flat(att_c), flat(y_c), flat(p_c['bv']), flat(p_c['g']), flat(cv_c), zc,
                           lp['lnx_g'], lp['lnx_b'], wa, wr, wc, wo, norm_g[i, 1], mod_c[2], n_ctx)

        j = i // 2
        if i % 2 == 0:
            wg, wu, wd = (w[j].astype(BF16) for w in (ffn_wg, ffn_wu, ffn_wd))
            xl = dense_ffn(xl, norm_g[i, 2], mod_l[4], mod_l[3], wg, wu, wd, norm_g[i, 3], mod_l[5], t)
            if need_ctx:
                xc = dense_ffn(xc, norm_g[i, 2], mod_c[4], mod_c[3], wg, wu, wd, norm_g[i, 3], mod_c[5], n_ctx)
        else:
            wg, wu, wd = moe_wg[j], moe_wu[j], moe_wd[j]
            streams = [(xl, mod_l, t)] + ([(xc, mod_c, n_ctx)] if need_ctx else [])
            outs = []
            for xs, ms, rpg in streams:
                n_tok = xs.shape[0]
                h, logits = prenorm(xs, norm_g[i, 2], ms[4], ms[3], rpg, router=moe_router[j])
                n_grp = MOE_GROUPS if n_tok % (MOE_GROUPS * SUBLANES) == 0 else 1
                gsz = n_tok // n_grp
                tm = _pick_tile(gsz, (1024, 512, 256, 128, 64, 32, 16, 8))
                fs = []
                for gi in range(n_grp):
                    row_token, tile_expert, n_used, dest, wts = _moe_dispatch(logits[gi * gsz:(gi + 1) * gsz], tm)
                    yg = grouped_swiglu(jnp.take(h, row_token + gi * gsz, axis=0), tile_expert, n_used,
                                        wg, wu, wd, tm)
                    fs.append(wts[:, 0:1] * jnp.take(yg, dest[:, 0], axis=0)
                              + wts[:, 1:2] * jnp.take(yg, dest[:, 1], axis=0))
                f = fs[0] if n_grp == 1 else jnp.concatenate(fs, axis=0)
                outs.append(resnorm(xs, f, norm_g[i, 3], ms[5], rpg))
            xl = outs[0]
            if need_ctx:
                xc = outs[1]
    return xl.reshape(bsz, t, d)
```

```python
import functools
import math

import jax
import jax.numpy as jnp
from jax import lax
from jax.experimental import pallas as pl
from jax.experimental.pallas import tpu as pltpu

F32 = jnp.float32
BF16 = jnp.bfloat16

D_MODEL = 1024
GRID_W = 64
ATT_HEADS = 8
ATT_KV_HEADS = 2
ATT_GROUP = ATT_HEADS // ATT_KV_HEADS
HEAD_DIM = 64
WINDOW = 128
ATT_BLOCK = 128
ATT_SCALE = 1.0 / math.sqrt(HEAD_DIM)
ROPE_BASE = 10000.0
AXIS_DIM = HEAD_DIM // 2
RWKV_HEADS = 8
RWKV_HEAD = 64
RWKV_W = RWKV_HEADS * RWKV_HEAD
DECAY_RANK = 32
ICLR_RANK = 32
GATE_RANK = 96
LNX_EPS = 64e-5
CONV_W = 512
CONV_K = 31
CONV_LN_EPS = 1e-5
N_BRANCH = 3
N_EXPERTS = 8
TOP_K = 2
NORM_EPS = 1e-6
Q_W = ATT_HEADS * HEAD_DIM
KV_W = ATT_KV_HEADS * HEAD_DIM
OFF_Q = 0
OFF_K = OFF_Q + Q_W
OFF_V = OFF_K + KV_W
OFF_R = OFF_V + KV_W
OFF_CONV = OFF_R + 3 * RWKV_W + 2 * DECAY_RANK + 2 * ICLR_RANK + GATE_RANK
OFF_GATE = OFF_CONV + 2 * CONV_W
IN_COLS = OFF_GATE + N_BRANCH * D_MODEL

ZQ = 0
ZQP = 512
ZK = 1024
ZKP = 1152
ZV = 1280
ZWA = 1408
ZCA = 1536
ZCB = 2048
ZR = 2560
ZRK = 3072
ZRV = 3584
ZG = 4096
ZZG = 7168
NZ = 7680

VMEM_LIMIT = 56 * 1024 * 1024
MOE_GROUPS = 2


def _cparams(sem):
    return pltpu.CompilerParams(dimension_semantics=sem, vmem_limit_bytes=VMEM_LIMIT)


def _pick_tile(n, candidates):
    for c in candidates:
        if n % c == 0:
            return c
    return n


def _matmul_bias_kernel(a_ref, w_ref, b_ref, o_ref):
    o_ref[...] = jnp.dot(a_ref[...], w_ref[...], preferred_element_type=F32,
                         precision=lax.Precision.HIGHEST) + b_ref[...]


def matmul_bias_f32(a, w, b):
    m, k = a.shape
    n = w.shape[1]
    tn = _pick_tile(n, (1536, 1024, 512, 256, 128))
    return pl.pallas_call(
        _matmul_bias_kernel,
        grid=(n // tn,),
        in_specs=[pl.BlockSpec((m, k), lambda j: (0, 0)),
                  pl.BlockSpec((k, tn), lambda j: (0, j)),
                  pl.BlockSpec((1, tn), lambda j: (0, j))],
        out_specs=pl.BlockSpec((m, tn), lambda j: (0, j)),
        out_shape=jax.ShapeDtypeStruct((m, n), F32),
        compiler_params=_cparams(("parallel",)),
        name="mod_matmul",
    )(a, w, b.reshape(1, n))


def _norm_mod(x, g, sc, sh):
    ms = jnp.mean(x * x, axis=-1, keepdims=True)
    return (x * lax.rsqrt(ms + NORM_EPS) * g) * (1.0 + sc) + sh


def _nm_matmul_kernel(x_ref, g_ref, sc_ref, sh_ref, w_ref, o_ref, h_ref):
    @pl.when(pl.program_id(1) == 0)
    def _():
        h_ref[...] = _norm_mod(x_ref[...], g_ref[...], sc_ref[0], sh_ref[0]).astype(BF16)

    o_ref[...] = jnp.dot(h_ref[...], w_ref[...], preferred_element_type=F32).astype(o_ref.dtype)


def nm_matmul(x, g, sc, sh, w, rows_per_group):
    m, d = x.shape
    n = w.shape[1]
    tm = _pick_tile(rows_per_group, (1024, 512, 256, 128, 64, 32, 16, 8))
    tn = _pick_tile(n, (2560, 1280, 1024, 512, 256, 128))
    grp = lambda i, j: ((i * tm) // rows_per_group, 0, 0)
    return pl.pallas_call(
        _nm_matmul_kernel,
        grid=(m // tm, n // tn),
        in_specs=[pl.BlockSpec((tm, d), lambda i, j: (i, 0)),
                  pl.BlockSpec((1, d), lambda i, j: (0, 0)),
                  pl.BlockSpec((1, 1, d), grp),
                  pl.BlockSpec((1, 1, d), grp),
                  pl.BlockSpec((d, tn), lambda i, j: (0, j))],
        out_specs=pl.BlockSpec((tm, tn), lambda i, j: (i, j)),
        out_shape=jax.ShapeDtypeStruct((m, n), F32),
        scratch_shapes=[pltpu.VMEM((tm, d), BF16)],
        compiler_params=_cparams(("parallel", "arbitrary")),
        name="in_proj",
    )(x, g.reshape(1, d), sc, sh, w)


def _prenorm_kernel(x_ref, g_ref, sc_ref, sh_ref, h_ref):
    h_ref[...] = _norm_mod(x_ref[...], g_ref[...], sc_ref[0], sh_ref[0]).astype(h_ref.dtype)


def _prenorm_router_kernel(x_ref, g_ref, sc_ref, sh_ref, r_ref, h_ref, l_ref):
    h = _norm_mod(x_ref[...], g_ref[...], sc_ref[0], sh_ref[0])
    h_ref[...] = h.astype(h_ref.dtype)
    l_ref[...] = jnp.dot(h, r_ref[...], preferred_element_type=F32, precision=lax.Precision.HIGHEST)


def prenorm(x, g, sc, sh, rows_per_group, router=None):
    m, d = x.shape
    tm = _pick_tile(rows_per_group, (512, 256, 128, 64, 32, 16, 8))
    grp = lambda i: ((i * tm) // rows_per_group, 0, 0)
    in_specs = [pl.BlockSpec((tm, d), lambda i: (i, 0)),
                pl.BlockSpec((1, d), lambda i: (0, 0)),
                pl.BlockSpec((1, 1, d), grp),
                pl.BlockSpec((1, 1, d), grp)]
    if router is None:
        return pl.pallas_call(
            _prenorm_kernel, grid=(m // tm,), in_specs=in_specs,
            out_specs=pl.BlockSpec((tm, d), lambda i: (i, 0)),
            out_shape=jax.ShapeDtypeStruct((m, d), BF16),
            compiler_params=_cparams(("parallel",)), name="prenorm",
        )(x, g.reshape(1, d), sc, sh)
    ne = router.shape[1]
    rpad = jnp.zeros((d, 128), F32).at[:, :ne].set(router)
    h, logits = pl.pallas_call(
        _prenorm_router_kernel, grid=(m // tm,),
        in_specs=in_specs + [pl.BlockSpec((d, 128), lambda i: (0, 0))],
        out_specs=[pl.BlockSpec((tm, d), lambda i: (i, 0)), pl.BlockSpec((tm, 128), lambda i: (i, 0))],
        out_shape=[jax.ShapeDtypeStruct((m, d), BF16), jax.ShapeDtypeStruct((m, 128), F32)],
        compiler_params=_cparams(("parallel",)), name="prenorm_router",
    )(x, g.reshape(1, d), sc, sh, rpad)
    return h, logits[:, :ne]


def _resnorm_kernel(x_ref, f_ref, g_ref, gt_ref, o_ref):
    f = f_ref[...]
    ms = jnp.mean(f * f, axis=-1, keepdims=True)
    o_ref[...] = x_ref[...] + gt_ref[0] * (f * lax.rsqrt(ms + NORM_EPS) * g_ref[...])


def resnorm(x, f, g, gt, rows_per_group):
    m, d = x.shape
    tm = _pick_tile(rows_per_group, (512, 256, 128, 64, 32, 16, 8))
    grp = lambda i: ((i * tm) // rows_per_group, 0, 0)
    return pl.pallas_call(
        _resnorm_kernel, grid=(m // tm,),
        in_specs=[pl.BlockSpec((tm, d), lambda i: (i, 0)),
                  pl.BlockSpec((tm, d), lambda i: (i, 0)),
                  pl.BlockSpec((1, d), lambda i: (0, 0)),
                  pl.BlockSpec((1, 1, d), grp)],
        out_specs=pl.BlockSpec((tm, d), lambda i: (i, 0)),
        out_shape=jax.ShapeDtypeStruct((m, d), F32),
        compiler_params=_cparams(("parallel",)), name="resnorm",
    )(x, f, g.reshape(1, d), gt)


LOG2E = 1.0 / math.log(2.0)


def _attn_heads(q, k, v, bias, sink_ref, o_ref):
    qs = (q * (ATT_SCALE * LOG2E)).astype(BF16)
    ones = jnp.ones((v.shape[0], HEAD_DIM), BF16)
    heads = range(ATT_HEADS)
    kh = [_hslice(k, h) for h in range(ATT_KV_HEADS)]
    v1 = [jnp.concatenate([_hslice(v, h), ones], axis=1) for h in range(ATT_KV_HEADS)]
    sink2 = [sink_ref[hd] * LOG2E for hd in heads]
    s = [lax.dot_general(_hslice(qs, hd), kh[hd // ATT_GROUP], (((1,), (1,)), ((), ())),
                         preferred_element_type=F32) for hd in heads]
    if bias is not None:
        s = [sh + bias for sh in s]
    m = [jnp.maximum(jnp.max(s[hd], axis=-1, keepdims=True), sink2[hd]) for hd in heads]
    p = [jnp.exp2(s[hd] - m[hd]).astype(BF16) for hd in heads]
    pv = [jnp.dot(p[hd], v1[hd // ATT_GROUP], preferred_element_type=F32) for hd in heads]
    outs = [pv[hd][:, :HEAD_DIM] / (pv[hd][:, HEAD_DIM:] + jnp.exp2(sink2[hd] - m[hd])) for hd in heads]
    o_ref[0] = jnp.concatenate(outs, axis=-1)


def _hslice(t, h):
    return t[:, h * HEAD_DIM:(h + 1) * HEAD_DIM]


def _win_attn_kernel(sink_ref, q_ref, qp_ref, cq_ref, sq_ref,
                     k0_ref, k1_ref, k2_ref, kp0_ref, kp1_ref, kp2_ref,
                     v0_ref, v1_ref, v2_ref, c0_ref, c1_ref, c2_ref, s0_ref, s1_ref, s2_ref,
                     kc_ref, vc_ref, o_ref, *, nb):
    j = pl.program_id(1)
    q = q_ref[0] * cq_ref[...] + qp_ref[0] * sq_ref[...]
    kb = jnp.concatenate([k0_ref[0] * c0_ref[...] + kp0_ref[0] * s0_ref[...],
                          k1_ref[0] * c1_ref[...] + kp1_ref[0] * s1_ref[...],
                          k2_ref[0] * c2_ref[...] + kp2_ref[0] * s2_ref[...]], axis=0).astype(BF16)
    vb = jnp.concatenate([v0_ref[0], v1_ref[0], v2_ref[0]], axis=0).astype(BF16)
    n_ctx = kc_ref.shape[1]
    k = jnp.concatenate([kc_ref[0].astype(BF16), kb], axis=0)
    v = jnp.concatenate([vc_ref[0].astype(BF16), vb], axis=0)
    n = n_ctx + 3 * ATT_BLOCK
    qi = lax.broadcasted_iota(jnp.int32, (ATT_BLOCK, n), 0) + ATT_BLOCK
    ki = lax.broadcasted_iota(jnp.int32, (ATT_BLOCK, n), 1) - n_ctx
    lo = jnp.where(j == 0, ATT_BLOCK, 0)
    hi = jnp.where(j == nb - 1, 2 * ATT_BLOCK, 3 * ATT_BLOCK)
    far = jnp.where(ki < lo, 4 * ATT_BLOCK, 0) + jnp.where(ki >= hi, 4 * ATT_BLOCK, 0)
    seen = jnp.where(ki < 0, 0, jnp.abs(qi - ki) + far) <= WINDOW
    bias = jnp.where(seen, 0.0, -jnp.inf).astype(F32)
    _attn_heads(q, k, v, bias, sink_ref, o_ref)


def _ctx_attn_kernel(sink_ref, q_ref, kc_ref, vc_ref, o_ref):
    _attn_heads(q_ref[0], kc_ref[0].astype(BF16), vc_ref[0].astype(BF16), None, sink_ref, o_ref)


def windowed_attention(zl3, zc3, cos_t, sin_t, sink):
    b, t, _ = zl3.shape
    l = zc3.shape[1]
    nb = t // ATT_BLOCK
    kcol, kpcol, vcol = ZK // KV_W, ZKP // KV_W, ZV // KV_W
    prev = lambda j: jnp.maximum(j - 1, 0)
    nxt = lambda j: jnp.minimum(j + 1, nb - 1)

    def zspec(width, rowf, col):
        return pl.BlockSpec((1, ATT_BLOCK, width), lambda bi, j: (bi, rowf(j), col))

    def tspec(width, rowf):
        return pl.BlockSpec((ATT_BLOCK, width), lambda bi, j: (rowf(j), 0))

    same = lambda j: j
    in_specs = [pl.BlockSpec(memory_space=pltpu.SMEM),
                zspec(Q_W, same, ZQ // Q_W), zspec(Q_W, same, ZQP // Q_W), tspec(Q_W, same), tspec(Q_W, same)]
    args = [sink, zl3, zl3, cos_t, sin_t]
    for col in (kcol, kpcol, vcol):
        in_specs += [zspec(KV_W, prev, col), zspec(KV_W, same, col), zspec(KV_W, nxt, col)]
        args += [zl3, zl3, zl3]
    for tab in (cos_t, sin_t):
        in_specs += [tspec(KV_W, prev), tspec(KV_W, same), tspec(KV_W, nxt)]
        args += [tab, tab, tab]
    in_specs += [pl.BlockSpec((1, l, KV_W), lambda bi, j: (bi, 0, kcol)),
                 pl.BlockSpec((1, l, KV_W), lambda bi, j: (bi, 0, vcol))]
    args += [zc3, zc3]
    return pl.pallas_call(
        functools.partial(_win_attn_kernel, nb=nb),
        grid=(b, nb), in_specs=in_specs,
        out_specs=pl.BlockSpec((1, ATT_BLOCK, Q_W), lambda bi, j: (bi, j, 0)),
        out_shape=jax.ShapeDtypeStruct((b, t, Q_W), F32),
        compiler_params=_cparams(("parallel", "parallel")), name="win_attn",
    )(*args)


def context_attention(zc3, sink):
    b, l, _ = zc3.shape
    return pl.pallas_call(
        _ctx_attn_kernel, grid=(b,),
        in_specs=[pl.BlockSpec(memory_space=pltpu.SMEM),
                  pl.BlockSpec((1, l, Q_W), lambda bi: (bi, 0, ZQ // Q_W)),
                  pl.BlockSpec((1, l, KV_W), lambda bi: (bi, 0, ZK // KV_W)),
                  pl.BlockSpec((1, l, KV_W), lambda bi: (bi, 0, ZV // KV_W))],
        out_specs=pl.BlockSpec((1, l, Q_W), lambda bi: (bi, 0, 0)),
        out_shape=jax.ShapeDtypeStruct((b, l, Q_W), F32),
        compiler_params=_cparams(("parallel",)), name="ctx_attn",
    )(sink, zc3, zc3, zc3)


HALO = 8


def _head_sums(x, e_ref):
    hi = x.astype(BF16)
    lo = (x - hi.astype(F32)).astype(BF16)
    return (jnp.dot(hi, e_ref[...], preferred_element_type=F32)
            + jnp.dot(lo, e_ref[...], preferred_element_type=F32))


def _shift_lerp_tile(cur_ref, prev_ref, next_ref, mu_ref, buf_ref, first, last):
    tt = cur_ref.shape[1]
    z = cur_ref[0]
    buf_ref[0:HALO] = jnp.where(first, 0.0, prev_ref[0])
    buf_ref[HALO:HALO + tt] = z
    buf_ref[HALO + tt:2 * HALO + tt] = jnp.where(last, 0.0, next_ref[0])
    nb = buf_ref[HALO - 1:HALO - 1 + tt] + buf_ref[HALO + 1:HALO + 1 + tt]
    return z + mu_ref[...] * (0.5 * nb - z)


def _split_bf16(a):
    hi = a.astype(BF16)
    return hi, (a - hi.astype(F32)).astype(BF16)


def _dot_3pass(a, b):
    a_hi, a_lo = _split_bf16(a)
    b_hi, b_lo = _split_bf16(b)
    return (jnp.dot(a_hi, b_hi, preferred_element_type=F32) + jnp.dot(a_lo, b_hi, preferred_element_type=F32)
            + jnp.dot(a_hi, b_lo, preferred_element_type=F32))


def _rwkv_prep_kernel(*refs, has_vres, n_alias):
    (wa_c, wa_p, wa_n, r_c, r_p, r_n, k_c, k_p, k_n, v_c, v_p, v_n, g_c, g_p, g_n,
     mu_wa, mu_r, mu_k, mu_v, mu_g, w0_ref, w2_ref, a0_ref, a2_ref, g2_ref, kk_ref, ka_ref, rk_ref, e_ref) = refs[:29]
    pos = 29
    if has_vres:
        vf_ref, v0_ref, v1_ref, v2_ref = refs[pos:pos + 4]
        pos += 4
    pos += n_alias
    (rk_o, vk_o, w_o, an_o, bv_o, g_o) = refs[pos:pos + 6]
    pos += 6
    if not has_vres:
        vin_o = refs[pos]
        pos += 1
    buf_wa, buf_r, buf_k, buf_v, buf_g = refs[pos:pos + 5]

    i = pl.program_id(1)
    first = i == 0
    last = i == pl.num_programs(1) - 1
    zwa = _shift_lerp_tile(wa_c, wa_p, wa_n, mu_wa, buf_wa, first, last)
    r = _shift_lerp_tile(r_c, r_p, r_n, mu_r, buf_r, first, last)
    k = _shift_lerp_tile(k_c, k_p, k_n, mu_k, buf_k, first, last)
    v = _shift_lerp_tile(v_c, v_p, v_n, mu_v, buf_v, first, last)
    zg = _shift_lerp_tile(g_c, g_p, g_n, mu_g, buf_g, first, last)

    if has_vres:
        mix = _dot_3pass(_dot_3pass(v, v1_ref[...]), v2_ref[...])
        v = v + (vf_ref[0] - v) * jax.nn.sigmoid(v0_ref[...] + mix)
    else:
        vin_o[0] = v
    kk = k * kk_ref[...]
    kk = kk / jnp.maximum(jnp.sqrt(_head_sums(kk * kk, e_ref)), 1e-12)
    tz = jnp.tanh(zwa)
    ksum = None
    for d in range(2):
        w_log = -jax.nn.softplus(-(w0_ref[d:d + 1, :] + _dot_3pass(tz, w2_ref[d]))) - 0.5
        w_o[d, 0] = jnp.exp(-jnp.exp(w_log))
        a = jax.nn.sigmoid(a0_ref[d:d + 1, :] + _dot_3pass(zwa, a2_ref[d]))
        an_o[d, 0] = -a
        kd = k * (1.0 + (a - 1.0) * ka_ref[...])
        ksum = kd if ksum is None else ksum + kd
    rk_o[0, 0] = r
    rk_o[1, 0] = k
    vk_o[0, 0] = v
    vk_o[1, 0] = kk
    bv_o[0] = _head_sums(r * ksum * rk_ref[...], e_ref) * v
    g_o[0] = _dot_3pass(jax.nn.sigmoid(zg), g2_ref[...])


def _head_ones():
    idx = jnp.arange(RWKV_W) // RWKV_HEAD
    return (idx[:, None] == idx[None, :]).astype(BF16)


def rwkv_prep(z3, lp, v_first, ttot, row_off, pair_bufs):
    b, tx, _ = z3.shape
    tt = _pick_tile(math.gcd(tx, row_off) if row_off else tx, (256, 128, 64, 32, 16, 8))
    nh = tt // HALO
    nblk = tx // HALO
    has_vres = v_first is not None
    lanes = 128

    def trio(width, col):
        return [pl.BlockSpec((1, tt, width), lambda bi, i: (bi, i, col)),
                pl.BlockSpec((1, HALO, width), lambda bi, i: (bi, jnp.maximum(i * nh - 1, 0), col)),
                pl.BlockSpec((1, HALO, width), lambda bi, i: (bi, jnp.minimum((i + 1) * nh, nblk - 1), col))]

    def full(shape):
        return pl.BlockSpec(shape, lambda bi, i: (0,) * len(shape))

    mu = lp['mu']
    o = 3 * RWKV_W
    nwa = 2 * DECAY_RANK + 2 * ICLR_RANK
    mu_g = jnp.zeros((lanes,), F32).at[:GATE_RANK].set(mu[o + nwa:])
    w2f = jnp.zeros((2, lanes, RWKV_W), F32)
    a2f = jnp.zeros((2, lanes, RWKV_W), F32)
    for d in range(2):
        w2f = w2f.at[d, d * DECAY_RANK:(d + 1) * DECAY_RANK].set(lp['w2'][d])
        a2f = a2f.at[d, 2 * DECAY_RANK + d * ICLR_RANK:2 * DECAY_RANK + (d + 1) * ICLR_RANK].set(lp['a2'][d])
    g2f = jnp.zeros((lanes, RWKV_W), F32).at[:GATE_RANK].set(lp['g2'])
    row = lambda a: a.reshape(1, -1)
    in_specs = (trio(lanes, ZWA // lanes) + trio(RWKV_W, ZR // RWKV_W) + trio(RWKV_W, ZRK // RWKV_W)
                + trio(RWKV_W, ZRV // RWKV_W) + trio(lanes, ZZG // lanes)
                + [full((1, lanes)), full((1, RWKV_W)), full((1, RWKV_W)), full((1, RWKV_W)), full((1, lanes)),
                   full((2, RWKV_W)), full((2, lanes, RWKV_W)), full((2, RWKV_W)), full((2, lanes, RWKV_W)),
                   full((lanes, RWKV_W)), full((1, RWKV_W)), full((1, RWKV_W)), full((1, RWKV_W)),
                   full((RWKV_W, RWKV_W))])
    args = [z3] * 15 + [row(mu[o:o + nwa]), row(mu[:RWKV_W]), row(mu[RWKV_W:2 * RWKV_W]), row(mu[2 * RWKV_W:o]),
                        row(mu_g), lp['w0'], w2f, lp['a0'], a2f, g2f, row(lp['k_k']), row(lp['k_a']),
                        row(lp['r_k']), _head_ones()]
    tok = pl.BlockSpec((1, tt, RWKV_W), lambda bi, i: (bi, i, 0))
    if has_vres:
        rank = lp['v1'].shape[1]
        v1f = jnp.zeros((RWKV_W, lanes), F32).at[:, :rank].set(lp['v1'])
        v2f = jnp.zeros((lanes, RWKV_W), F32).at[:rank].set(lp['v2'])
        in_specs += [tok, full((1, RWKV_W)), full((RWKV_W, lanes)), full((lanes, RWKV_W))]
        args += [v_first, row(lp['v0']), v1f, v2f]
    n_alias = 0 if pair_bufs is None else len(pair_bufs)
    aliases = {}
    if pair_bufs is not None:
        aliases = {len(args) + n: n for n in range(n_alias)}
        in_specs += [pl.BlockSpec(memory_space=pl.ANY)] * n_alias
        args += list(pair_bufs)
    blk_off = row_off // tt
    pair = pl.BlockSpec((2, 1, tt, RWKV_W), lambda bi, i: (0, bi, i + blk_off, 0))
    pair_shape = jax.ShapeDtypeStruct((2, b, ttot, RWKV_W), F32)
    tok_shape = jax.ShapeDtypeStruct((b, tx, RWKV_W), F32)
    n_tok = 2 if has_vres else 3
    outs = pl.pallas_call(
        functools.partial(_rwkv_prep_kernel, has_vres=has_vres, n_alias=n_alias),
        grid=(b, tx // tt), in_specs=in_specs,
        out_specs=[pair] * 4 + [tok] * n_tok,
        out_shape=[pair_shape] * 4 + [tok_shape] * n_tok,
        input_output_aliases=aliases,
        scratch_shapes=[pltpu.VMEM((tt + 2 * HALO, lanes), F32)] + [pltpu.VMEM((tt + 2 * HALO, RWKV_W), F32)] * 3
                       + [pltpu.VMEM((tt + 2 * HALO, lanes), F32)],
        compiler_params=_cparams(("parallel", "parallel")), name="rwkv_prep",
    )(*args)
    res = {'pairs': tuple(outs[:4]), 'bv': outs[4], 'g': outs[5]}
    if not has_vres:
        res['v_in'] = outs[6]
    return res


CONV_HALO = 16
CONV_ROWS = 32


def _conv_kernel(a_c, a_p, a_n, b_c, b_p, b_n, w_ref, bias_ref, lg_ref, lb_ref, o_ref, buf_ref, win_ref):
    i = pl.program_id(1)
    first = i == 0
    last = i == pl.num_programs(1) - 1
    tt = a_c.shape[1]
    glu = lambda a, b: a * jax.nn.sigmoid(b)
    buf_ref[0:CONV_HALO] = jnp.where(first, 0.0, glu(a_p[0], b_p[0]))
    buf_ref[CONV_HALO:CONV_HALO + tt] = glu(a_c[0], b_c[0])
    buf_ref[CONV_HALO + tt:2 * CONV_HALO + tt] = jnp.where(last, 0.0, glu(a_n[0], b_n[0]))
    half = CONV_K // 2
    span = CONV_ROWS + ((CONV_K - 1) // SUBLANES) * SUBLANES
    for r0 in range(0, tt, CONV_ROWS):
        base = CONV_HALO - half + r0
        acc = bias_ref[...]
        for ph in range(SUBLANES):
            win_ref[...] = buf_ref[base + ph:base + ph + span]
            for j in range(ph, CONV_K, SUBLANES):
                acc = acc + win_ref[j - ph:j - ph + CONV_ROWS] * w_ref[j:j + 1, :]
        m = jnp.mean(acc, axis=-1, keepdims=True)
        cen = acc - m
        var = jnp.mean(cen * cen, axis=-1, keepdims=True)
        y = cen * lax.rsqrt(var + CONV_LN_EPS) * lg_ref[...] + lb_ref[...]
        o_ref[0, r0:r0 + CONV_ROWS, :] = y * jax.nn.sigmoid(y)


def conv_module(z3, lp):
    b, tx, _ = z3.shape
    tt = _pick_tile(tx, (256, 128, 64, 32))
    nh = tt // CONV_HALO
    nblk = tx // CONV_HALO

    def trio(col):
        return [pl.BlockSpec((1, tt, CONV_W), lambda bi, i: (bi, i, col)),
                pl.BlockSpec((1, CONV_HALO, CONV_W), lambda bi, i: (bi, jnp.maximum(i * nh - 1, 0), col)),
                pl.BlockSpec((1, CONV_HALO, CONV_W), lambda bi, i: (bi, jnp.minimum((i + 1) * nh, nblk - 1), col))]

    kpad = -(-CONV_K // SUBLANES) * SUBLANES
    wpad = jnp.zeros((kpad, CONV_W), F32).at[:CONV_K].set(lp['dw_w'])
    vec = pl.BlockSpec((1, CONV_W), lambda bi, i: (0, 0))
    return pl.pallas_call(
        _conv_kernel, grid=(b, tx // tt),
        in_specs=trio(ZCA // CONV_W) + trio(ZCB // CONV_W)
                 + [pl.BlockSpec((kpad, CONV_W), lambda bi, i: (0, 0)), vec, vec, vec],
        out_specs=pl.BlockSpec((1, tt, CONV_W), lambda bi, i: (bi, i, 0)),
        out_shape=jax.ShapeDtypeStruct((b, tx, CONV_W), F32),
        scratch_shapes=[pltpu.VMEM((tt + 2 * CONV_HALO, CONV_W), F32),
                        pltpu.VMEM((CONV_ROWS + ((CONV_K - 1) // SUBLANES) * SUBLANES, CONV_W), F32)],
        compiler_params=_cparams(("parallel", "parallel")), name="conv_module",
    )(z3, z3, z3, z3, z3, z3, wpad, lp['dw_b'].reshape(1, -1), lp['cln_g'].reshape(1, -1), lp['cln_b'].reshape(1, -1))


SUBLANES = 8
DECAY_FOLD = 16


def _wkv_scan_kernel(rk_f, vk_f, w_f, an_f, rk_b, vk_b, w_b, an_b, ka_ref, yf_ref, yb_ref, s_ref, *, tt):
    @pl.when(pl.program_id(0) == 0)
    def _():
        s_ref[...] = jnp.zeros_like(s_ref)

    nk, _, lanes = s_ref.shape
    half = lanes // 2
    fwd = lax.broadcasted_iota(jnp.int32, (nk, lanes), 1) < half
    ka = ka_ref[...]
    fold = math.gcd(tt, DECAY_FOLD)

    def group(gi, carry):
        def step(ti, gam):
            t = gi * fold + ti
            tb = tt - 1 - t
            rkf, rkb, vkf, vkb = rk_f[t], rk_b[tb], vk_f[t], vk_b[tb]
            r = jnp.where(fwd, rkf, pltpu.roll(rkb, half, 1))
            k = jnp.where(fwd, pltpu.roll(rkf, half, 1), rkb)
            v = jnp.where(fwd, vkf, pltpu.roll(vkb, half, 1))
            kk = jnp.where(fwd, pltpu.roll(vkf, half, 1), vkb)
            w = jnp.where(fwd, w_f[t], w_b[tb])
            an = jnp.where(fwd, an_f[t], an_b[tb])
            gam_new = gam * w
            inv = 1.0 / gam_new
            kks = kk * gam
            kds = k * (1.0 - (an + 1.0) * ka) * inv
            bbs = kk * an * inv
            rs = r * gam_new
            row = lambda a, ki: a[ki:ki + 1, :]
            sa = [s_ref[0] * row(kks, 0), s_ref[1] * row(kks, 1)]
            for ki in range(2, nk):
                sa[ki % 2] = sa[ki % 2] + s_ref[ki] * row(kks, ki)
            sa = sa[0] + sa[1]
            ys = [None, None]
            for ki in range(nk):
                s_new = s_ref[ki] + sa * row(bbs, ki) + v * row(kds, ki)
                s_ref[ki] = s_new
                contrib = s_new * row(rs, ki)
                ys[ki % 2] = contrib if ys[ki % 2] is None else ys[ki % 2] + contrib
            y = ys[0] + ys[1]
            yf_ref[t] = y
            yb_ref[tb] = y
            return gam_new

        gam = lax.fori_loop(0, fold, step, jnp.ones((nk, lanes), F32), unroll=4)
        for ki in range(nk):
            s_ref[ki] = s_ref[ki] * gam[ki:ki + 1, :]
        return carry

    lax.fori_loop(0, tt // fold, group, 0)


def wkv_scan(rk, vk, w, an, ka, n_ctx_steps):
    ttot, n, ch = rk.shape
    tt = _pick_tile(math.gcd(n_ctx_steps, ttot - n_ctx_steps), (32, 16, 8, 4, 2, 1))
    nt, nc = ttot // tt, n_ctx_steps // tt
    fspec = pl.BlockSpec((tt, n, ch), lambda i: (i, 0, 0))
    bspec = pl.BlockSpec((tt, n, ch), lambda i: (jnp.where(i < nc, nc - 1 - i, nt - 1 - (i - nc)), 0, 0))
    return pl.pallas_call(
        functools.partial(_wkv_scan_kernel, tt=tt),
        grid=(nt,), in_specs=[fspec] * 4 + [bspec] * 4 + [pl.BlockSpec((n, ch), lambda i: (0, 0))],
        out_specs=[fspec, bspec],
        out_shape=[jax.ShapeDtypeStruct((ttot, n, ch), F32)] * 2,
        scratch_shapes=[pltpu.VMEM((n, n, ch), F32)],
        compiler_params=_cparams(("arbitrary",)), name="wkv_scan",
    )(rk, vk, w, an, rk, vk, w, an, ka)


def _merge_kernel(x_ref, att_ref, y_ref, bv_ref, rg_ref, cv_ref, g0_ref, g1_ref, g2_ref,
                  lg_ref, lb_ref, e_ref, wa_ref, wr_ref, wc_ref, wo_ref, ng_ref, gt_ref, o_ref):
    def proj(a, w_ref):
        return jnp.dot(a.astype(BF16), w_ref[...], preferred_element_type=F32)

    y = y_ref[...]
    cen = y - _head_sums(y, e_ref) * (1.0 / RWKV_HEAD)
    var = _head_sums(cen * cen, e_ref) * (1.0 / RWKV_HEAD)
    rw = (cen * lax.rsqrt(var + LNX_EPS) * lg_ref[...] + lb_ref[...] + bv_ref[...]) * rg_ref[...]
    m = (jax.nn.sigmoid(g0_ref[...]) * proj(att_ref[...], wa_ref)
         + jax.nn.sigmoid(g1_ref[...]) * proj(rw, wr_ref)
         + jax.nn.sigmoid(g2_ref[...]) * proj(cv_ref[...], wc_ref))
    ml = proj(m, wo_ref)
    ms = jnp.mean(ml * ml, axis=-1, keepdims=True)
    o_ref[...] = x_ref[...] + gt_ref[0] * (ml * lax.rsqrt(ms + NORM_EPS) * ng_ref[...])


def merge_out(x, att, y, bv, rg, cv, z, lnx_g, lnx_b, wa, wr, wc, wo, ng, gt, rows_per_group):
    m, d = x.shape
    tm = _pick_tile(rows_per_group, (512, 256, 128, 64, 32, 16, 8))
    row = lambda i: (i, 0)
    const = lambda i: (0, 0)
    gcol = ZG // d
    br = pl.BlockSpec((tm, RWKV_W), row)
    return pl.pallas_call(
        _merge_kernel, grid=(m // tm,),
        in_specs=[pl.BlockSpec((tm, d), row),
                  pl.BlockSpec((tm, Q_W), row), br, br, br, pl.BlockSpec((tm, CONV_W), row),
                  pl.BlockSpec((tm, d), lambda i: (i, gcol)),
                  pl.BlockSpec((tm, d), lambda i: (i, gcol + 1)),
                  pl.BlockSpec((tm, d), lambda i: (i, gcol + 2)),
                  pl.BlockSpec((1, RWKV_W), const), pl.BlockSpec((1, RWKV_W), const),
                  pl.BlockSpec((RWKV_W, RWKV_W), const),
                  pl.BlockSpec((Q_W, d), const), pl.BlockSpec((RWKV_W, d), const), pl.BlockSpec((CONV_W, d), const),
                  pl.BlockSpec((d, d), const), pl.BlockSpec((1, d), const),
                  pl.BlockSpec((1, 1, d), lambda i: ((i * tm) // rows_per_group, 0, 0))],
        out_specs=pl.BlockSpec((tm, d), row),
        out_shape=jax.ShapeDtypeStruct((m, d), F32),
        compiler_params=_cparams(("parallel",)), name="merge_out",
    )(x, att, y, bv, rg, cv, z, z, z, lnx_g.reshape(1, -1), lnx_b.reshape(1, -1), _head_ones(),
      wa, wr, wc, wo, ng.reshape(1, d), gt)


def _swiglu_kernel(te_ref, nu_ref, h_ref, wg_ref, wu_ref, wd_ref, o_ref):
    f = pl.program_id(1)

    @pl.when(pl.program_id(0) < nu_ref[0])
    def _():
        h = h_ref[...]
        hg = jnp.dot(h, wg_ref[0].astype(BF16), preferred_element_type=F32)
        hu = jnp.dot(h, wu_ref[0].astype(BF16), preferred_element_type=F32)
        act = (hg * jax.nn.sigmoid(hg) * hu).astype(BF16)
        part = jnp.dot(act, wd_ref[0].astype(BF16), preferred_element_type=F32)

        @pl.when(f == 0)
        def _():
            o_ref[...] = part

        @pl.when(f > 0)
        def _():
            o_ref[...] += part


def grouped_swiglu(h, tile_expert, n_used, wg, wu, wd, tm):
    p, d = h.shape
    ff = wg.shape[2]
    tf = _pick_tile(ff, (512, 1408, 256, 128))
    nf = ff // tf
    ti = lambda i, nu: jnp.minimum(i, nu[0] - 1)
    fi = lambda i, f, nu: jnp.where(i < nu[0], f, nf - 1)
    grid_spec = pltpu.PrefetchScalarGridSpec(
        num_scalar_prefetch=2, grid=(p // tm, nf),
        in_specs=[pl.BlockSpec((tm, d), lambda i, f, te, nu: (ti(i, nu), 0)),
                  pl.BlockSpec((1, d, tf), lambda i, f, te, nu: (te[ti(i, nu)], 0, fi(i, f, nu))),
                  pl.BlockSpec((1, d, tf), lambda i, f, te, nu: (te[ti(i, nu)], 0, fi(i, f, nu))),
                  pl.BlockSpec((1, tf, d), lambda i, f, te, nu: (te[ti(i, nu)], fi(i, f, nu), 0))],
        out_specs=pl.BlockSpec((tm, d), lambda i, f, te, nu: (ti(i, nu), 0)))
    return pl.pallas_call(
        _swiglu_kernel, grid_spec=grid_spec,
        out_shape=jax.ShapeDtypeStruct((p, d), F32),
        compiler_params=_cparams(("arbitrary", "arbitrary")), name="swiglu",
    )(tile_expert, n_used, h, wg, wu, wd)


def _dense_ffn_kernel(x_ref, g_ref, sc_ref, sh_ref, wg_ref, wu_ref, wd_ref, g2_ref, gt_ref, o_ref, h_ref, acc_ref):
    f = pl.program_id(1)

    @pl.when(f == 0)
    def _():
        h_ref[...] = _norm_mod(x_ref[...], g_ref[...], sc_ref[0], sh_ref[0]).astype(BF16)

    h = h_ref[...]
    hg = jnp.dot(h, wg_ref[...], preferred_element_type=F32)
    hu = jnp.dot(h, wu_ref[...], preferred_element_type=F32)
    act = (hg * jax.nn.sigmoid(hg) * hu).astype(BF16)
    part = jnp.dot(act, wd_ref[...], preferred_element_type=F32)

    @pl.when(f == 0)
    def _():
        acc_ref[...] = part

    @pl.when(f > 0)
    def _():
        acc_ref[...] += part

    @pl.when(f == pl.num_programs(1) - 1)
    def _():
        y = acc_ref[...]
        ms = jnp.mean(y * y, axis=-1, keepdims=True)
        o_ref[...] = x_ref[...] + gt_ref[0] * (y * lax.rsqrt(ms + NORM_EPS) * g2_ref[...])


def dense_ffn(x, g_in, sc, sh, wg, wu, wd, g_out, gt, rows_per_group):
    m, d = x.shape
    ff = wg.shape[1]
    tm = _pick_tile(rows_per_group, (512, 256, 128, 64, 32, 16, 8))
    tf = _pick_tile(ff, (1408, 512, 256, 128))
    grp = lambda i, f: ((i * tm) // rows_per_group, 0, 0)
    vec = pl.BlockSpec((1, d), lambda i, f: (0, 0))
    return pl.pallas_call(
        _dense_ffn_kernel, grid=(m // tm, ff // tf),
        in_specs=[pl.BlockSpec((tm, d), lambda i, f: (i, 0)), vec,
                  pl.BlockSpec((1, 1, d), grp), pl.BlockSpec((1, 1, d), grp),
                  pl.BlockSpec((d, tf), lambda i, f: (0, f)), pl.BlockSpec((d, tf), lambda i, f: (0, f)),
                  pl.BlockSpec((tf, d), lambda i, f: (f, 0)), vec, pl.BlockSpec((1, 1, d), grp)],
        out_specs=pl.BlockSpec((tm, d), lambda i, f: (i, 0)),
        out_shape=jax.ShapeDtypeStruct((m, d), F32),
        scratch_shapes=[pltpu.VMEM((tm, d), BF16), pltpu.VMEM((tm, d), F32)],
        compiler_params=_cparams(("parallel", "arbitrary")), name="dense_ffn",
    )(x, g_in.reshape(1, d), sc, sh, wg, wu, wd, g_out.reshape(1, d), gt)


def _rope_tables(t):
    rows = t // GRID_W
    row = jnp.broadcast_to(jnp.arange(rows)[:, None], (rows, GRID_W)).reshape(-1).astype(F32)
    col = jnp.broadcast_to(jnp.arange(GRID_W)[None, :], (rows, GRID_W)).reshape(-1).astype(F32)
    inv = ROPE_BASE ** (-jnp.arange(0, AXIS_DIM, 2, dtype=F32) / AXIS_DIM)
    ar = row[:, None] * inv[None, :]
    ac = col[:, None] * inv[None, :]
    cos64 = jnp.concatenate([jnp.cos(ar), jnp.cos(ar), jnp.cos(ac), jnp.cos(ac)], axis=-1)
    sin64 = jnp.concatenate([jnp.sin(ar), jnp.sin(ar), jnp.sin(ac), jnp.sin(ac)], axis=-1)
    return jnp.tile(cos64, (1, ATT_HEADS)), jnp.tile(sin64, (1, ATT_HEADS))


def _rope_partner(w):
    d = w.shape[0]
    q = AXIS_DIM // 2
    wh = w.reshape(d, -1, 4, q)
    return jnp.stack([-wh[:, :, 1], wh[:, :, 0], -wh[:, :, 3], wh[:, :, 2]], axis=2).reshape(w.shape)


def _build_w_ext(w_in):
    d = w_in.shape[0]
    wq = w_in[:, OFF_Q:OFF_K]
    wk = w_in[:, OFF_K:OFF_V]
    o = OFF_R + 3 * RWKV_W
    parts = [(ZQ, wq), (ZQP, _rope_partner(wq)), (ZK, wk), (ZKP, _rope_partner(wk)),
             (ZV, w_in[:, OFF_V:OFF_R]),
             (ZWA, w_in[:, o:o + 2 * DECAY_RANK + 2 * ICLR_RANK]),
             (ZCA, w_in[:, OFF_CONV:OFF_CONV + CONV_W]), (ZCB, w_in[:, OFF_CONV + CONV_W:OFF_GATE]),
             (ZR, w_in[:, OFF_R:OFF_R + RWKV_W]), (ZRK, w_in[:, OFF_R + RWKV_W:OFF_R + 2 * RWKV_W]),
             (ZRV, w_in[:, OFF_R + 2 * RWKV_W:OFF_R + 3 * RWKV_W]),
             (ZG, w_in[:, OFF_GATE:IN_COLS]),
             (ZZG, w_in[:, OFF_CONV - GATE_RANK:OFF_CONV])]
    parts.sort(key=lambda p: p[0])
    cols, pos = [], 0
    for off, wpart in parts:
        if off > pos:
            cols.append(jnp.zeros((d, off - pos), w_in.dtype))
        cols.append(wpart)
        pos = off + wpart.shape[1]
    if pos < NZ:
        cols.append(jnp.zeros((d, NZ - pos), w_in.dtype))
    return jnp.concatenate(cols, axis=1).astype(BF16)


def _chain_layout(pair):
    _, bsz, ttot, _ = pair.shape
    both = pair.reshape(2, bsz, ttot, RWKV_HEADS, RWKV_HEAD)
    return jnp.transpose(both, (2, 4, 0, 1, 3)).reshape(ttot, RWKV_HEAD, 2 * bsz * RWKV_HEADS)


def _bidirectional_wkv(pairs, k_a, l):
    bsz = pairs[0].shape[1]
    rk, vk, w, an = (_chain_layout(p) for p in pairs)
    ka = jnp.tile(k_a.reshape(RWKV_HEADS, RWKV_HEAD).T, (1, 2 * bsz))
    yf, yb = wkv_scan(rk, vk, w, an, ka, l)
    half = bsz * RWKV_HEADS
    y = (yf[:, :, :half] + yb[:, :, half:]).reshape(-1, RWKV_HEAD, bsz, RWKV_HEADS)
    y = jnp.transpose(y, (2, 0, 3, 1)).reshape(bsz, -1, RWKV_W)
    return y[:, l:], y[:, :l]


def _moe_dispatch(logits, tm):
    m = logits.shape[0]
    top_v, top_i = lax.top_k(logits, TOP_K)
    wts = jax.nn.softmax(top_v, axis=-1)
    e_flat = top_i.reshape(-1)
    n_assign = e_flat.shape[0]
    onehot = (e_flat[:, None] == jnp.arange(N_EXPERTS)[None, :]).astype(jnp.int32)
    csum = jnp.cumsum(onehot, axis=0)
    rank = jnp.sum(onehot * csum, axis=1) - 1
    cnt = csum[-1]
    pcnt = ((cnt + tm - 1) // tm) * tm
    pend = jnp.cumsum(pcnt)
    pstart = pend - pcnt
    dest = pstart[e_flat] + rank
    p_max = n_assign + N_EXPERTS * tm
    row_token = jnp.zeros((p_max,), jnp.int32).at[dest].set(jnp.arange(n_assign, dtype=jnp.int32) // TOP_K)
    tile_start = jnp.arange(p_max // tm, dtype=jnp.int32) * tm
    tile_expert = jnp.minimum(jnp.searchsorted(pend, tile_start, side='right'), N_EXPERTS - 1).astype(jnp.int32)
    n_used = (pend[-1:] // tm).astype(jnp.int32)
    return row_token, tile_expert, n_used, dest.reshape(m, TOP_K), wts


def kernel(x, c, ctx, c_ctx, mod_w, mod_b, norm_g, w_in, attn_sink, rwkv_mu, rwkv_w0, rwkv_w2, rwkv_a0, rwkv_a2, rwkv_g2, rwkv_k_k, rwkv_k_a, rwkv_r_k, rwkv_lnx_g, rwkv_lnx_b, vres_v0, vres_w1, vres_w2, conv_dw_w, conv_dw_b, conv_ln_g, conv_ln_b, w_attn_o, w_rwkv_o, w_conv_o, w_out, ffn_wg, ffn_wu, ffn_wd, moe_router, moe_wg, moe_wu, moe_wd):
    bsz, t, d = x.shape
    l = ctx.shape[1]
    depth = mod_w.shape[0]
    n_lat, n_ctx = bsz * t, bsz * l
    cos_t, sin_t = _rope_tables(t)
    xl = x.reshape(n_lat, d)
    xc = ctx.reshape(n_ctx, d)
    n_rows = -(-(bsz + 1) // SUBLANES) * SUBLANES
    cond = jnp.zeros((n_rows, d), F32).at[:bsz].set(jax.nn.silu(c)).at[bsz].set(jax.nn.silu(c_ctx))
    v_first = None
    for i in range(depth):
        need_ctx = i < depth - 1
        mod = matmul_bias_f32(cond, mod_w[i], mod_b[i])
        mod_l = [mod[:bsz, k * d:(k + 1) * d].reshape(bsz, 1, d) for k in range(6)]
        mod_c = [mod[bsz:bsz + 1, k * d:(k + 1) * d].reshape(1, 1, d) for k in range(6)]
        lp = {'mu': rwkv_mu[i], 'w0': rwkv_w0[i], 'w2': rwkv_w2[i], 'a0': rwkv_a0[i], 'a2': rwkv_a2[i],
              'g2': rwkv_g2[i], 'k_k': rwkv_k_k[i], 'k_a': rwkv_k_a[i], 'r_k': rwkv_r_k[i],
              'lnx_g': rwkv_lnx_g[i], 'lnx_b': rwkv_lnx_b[i],
              'dw_w': conv_dw_w[i], 'dw_b': conv_dw_b[i], 'cln_g': conv_ln_g[i], 'cln_b': conv_ln_b[i]}
        if i > 0:
            lp['v0'], lp['v1'], lp['v2'] = vres_v0[i - 1], vres_w1[i - 1], vres_w2[i - 1]
        w_ext = _build_w_ext(w_in[i])
        wa, wr, wc, wo = (w.astype(BF16) for w in (w_attn_o[i], w_rwkv_o[i], w_conv_o[i], w_out[i]))

        zl = nm_matmul(xl, norm_g[i, 0], mod_l[1], mod_l[0], w_ext, t)
        zc = nm_matmul(xc, norm_g[i, 0], mod_c[1], mod_c[0], w_ext, n_ctx)
        zl3 = zl.reshape(bsz, t, NZ)
        zc3 = zc.reshape(bsz, l, NZ)
        att_l = windowed_attention(zl3, zc3, cos_t, sin_t, attn_sink[i])
        p_c = rwkv_prep(zc3, lp, None if v_first is None else v_first[1], l + t, 0, None)
        p_l = rwkv_prep(zl3, lp, None if v_first is None else v_first[0], l + t, l, p_c['pairs'])
        if i == 0:
            v_first = (p_l['v_in'], p_c['v_in'])
        y_l, y_c = _bidirectional_wkv(p_l['pairs'], lp['k_a'], l)
        cv_l = conv_module(zl3, lp)
        flat = lambda a: a.reshape(-1, a.shape[-1])
        xl = merge_out(xl, flat(att_l), flat(y_l), flat(p_l['bv']), flat(p_l['g']), flat(cv_l), zl,
                       lp['lnx_g'], lp['lnx_b'], wa, wr, wc, wo, norm_g[i, 1], mod_l[2], t)
        if need_ctx:
            att_c = context_attention(zc3, attn_sink[i])
            cv_c = conv_module(zc3, lp)
            xc = merge_out(xc, flat(att_c), flat(y_c), flat(p_c['bv']), flat(p_c['g']), flat(cv_c), zc,
                           lp['lnx_g'], lp['lnx_b'], wa, wr, wc, wo, norm_g[i, 1], mod_c[2], n_ctx)

        j = i // 2
        if i % 2 == 0:
            wg, wu, wd = (w[j].astype(BF16) for w in (ffn_wg, ffn_wu, ffn_wd))
            xl = dense_ffn(xl, norm_g[i, 2], mod_l[4], mod_l[3], wg, wu, wd, norm_g[i, 3], mod_l[5], t)
            if need_ctx:
                xc = dense_ffn(xc, norm_g[i, 2], mod_c[4], mod_c[3], wg, wu, wd, norm_g[i, 3], mod_c[5], n_ctx)
        else:
            wg, wu, wd = moe_wg[j], moe_wu[j], moe_wd[j]
            streams = [(xl, mod_l, t)] + ([(xc, mod_c, n_ctx)] if need_ctx else [])
            outs = []
            for xs, ms, rpg in streams:
                n_tok = xs.shape[0]
                h, logits = prenorm(xs, norm_g[i, 2], ms[4], ms[3], rpg, router=moe_router[j])
                n_grp = MOE_GROUPS if n_tok % (MOE_GROUPS * SUBLANES) == 0 else 1
                gsz = n_tok // n_grp
                tm = _pick_tile(gsz, (1024, 512, 256, 128, 64, 32, 16, 8))
                fs = []
                for gi in range(n_grp):
                    row_token, tile_expert, n_used, dest, wts = _moe_dispatch(logits[gi * gsz:(gi + 1) * gsz], tm)
                    yg = grouped_swiglu(jnp.take(h, row_token + gi * gsz, axis=0), tile_expert, n_used,
                                        wg, wu, wd, tm)
                    fs.append(wts[:, 0:1] * jnp.take(yg, dest[:, 0], axis=0)
                              + wts[:, 1:2] * jnp.take(yg, dest[:, 1], axis=0))
                f = fs[0] if n_grp == 1 else jnp.concatenate(fs, axis=0)
                outs.append(resnorm(xs, f, norm_g[i, 3], ms[5], rpg))
            xl = outs[0]
            if need_ctx:
                xc = outs[1]
    return xl.reshape(bsz, t, d)
```

```python
import functools
import math

import jax
import jax.numpy as jnp
from jax import lax
from jax.experimental import pallas as pl
from jax.experimental.pallas import tpu as pltpu

F32 = jnp.float32
BF16 = jnp.bfloat16

D_MODEL = 1024
GRID_W = 64
ATT_HEADS = 8
ATT_KV_HEADS = 2
ATT_GROUP = ATT_HEADS // ATT_KV_HEADS
HEAD_DIM = 64
WINDOW = 128
ATT_BLOCK = 128
ATT_SCALE = 1.0 / math.sqrt(HEAD_DIM)
ROPE_BASE = 10000.0
AXIS_DIM = HEAD_DIM // 2
RWKV_HEADS = 8
RWKV_HEAD = 64
RWKV_W = RWKV_HEADS * RWKV_HEAD
DECAY_RANK = 32
ICLR_RANK = 32
GATE_RANK = 96
LNX_EPS = 64e-5
CONV_W = 512
CONV_K = 31
CONV_LN_EPS = 1e-5
N_BRANCH = 3
N_EXPERTS = 8
TOP_K = 2
NORM_EPS = 1e-6
Q_W = ATT_HEADS * HEAD_DIM
KV_W = ATT_KV_HEADS * HEAD_DIM
OFF_Q = 0
OFF_K = OFF_Q + Q_W
OFF_V = OFF_K + KV_W
OFF_R = OFF_V + KV_W
OFF_CONV = OFF_R + 3 * RWKV_W + 2 * DECAY_RANK + 2 * ICLR_RANK + GATE_RANK
OFF_GATE = OFF_CONV + 2 * CONV_W
IN_COLS = OFF_GATE + N_BRANCH * D_MODEL

ZQ = 0
ZQP = 512
ZK = 1024
ZKP = 1152
ZV = 1280
ZWA = 1408
ZCA = 1536
ZCB = 2048
ZR = 2560
ZRK = 3072
ZRV = 3584
ZG = 4096
ZZG = 7168
NZ = 7680

VMEM_LIMIT = 56 * 1024 * 1024
MOE_GROUPS = 2


def _cparams(sem):
    return pltpu.CompilerParams(dimension_semantics=sem, vmem_limit_bytes=VMEM_LIMIT)


def _pick_tile(n, candidates):
    for c in candidates:
        if n % c == 0:
            return c
    return n


def _matmul_bias_kernel(a_ref, w_ref, b_ref, o_ref):
    o_ref[...] = jnp.dot(a_ref[...], w_ref[...], preferred_element_type=F32,
                         precision=lax.Precision.HIGHEST) + b_ref[...]


def matmul_bias_f32(a, w, b):
    m, k = a.shape
    n = w.shape[1]
    tn = _pick_tile(n, (1536, 1024, 512, 256, 128))
    return pl.pallas_call(
        _matmul_bias_kernel,
        grid=(n // tn,),
        in_specs=[pl.BlockSpec((m, k), lambda j: (0, 0)),
                  pl.BlockSpec((k, tn), lambda j: (0, j)),
                  pl.BlockSpec((1, tn), lambda j: (0, j))],
        out_specs=pl.BlockSpec((m, tn), lambda j: (0, j)),
        out_shape=jax.ShapeDtypeStruct((m, n), F32),
        compiler_params=_cparams(("parallel",)),
        name="mod_matmul",
    )(a, w, b.reshape(1, n))


def _norm_mod(x, g, sc, sh):
    ms = jnp.mean(x * x, axis=-1, keepdims=True)
    return (x * lax.rsqrt(ms + NORM_EPS) * g) * (1.0 + sc) + sh


def _nm_matmul_kernel(x_ref, g_ref, sc_ref, sh_ref, w_ref, o_ref, h_ref):
    @pl.when(pl.program_id(1) == 0)
    def _():
        h_ref[...] = _norm_mod(x_ref[...], g_ref[...], sc_ref[0], sh_ref[0]).astype(BF16)

    o_ref[...] = jnp.dot(h_ref[...], w_ref[...], preferred_element_type=F32).astype(o_ref.dtype)


def nm_matmul(x, g, sc, sh, w, rows_per_group):
    m, d = x.shape
    n = w.shape[1]
    tm = _pick_tile(rows_per_group, (1024, 512, 256, 128, 64, 32, 16, 8))
    tn = _pick_tile(n, (2560, 1280, 1024, 512, 256, 128))
    grp = lambda i, j: ((i * tm) // rows_per_group, 0, 0)
    return pl.pallas_call(
        _nm_matmul_kernel,
        grid=(m // tm, n // tn),
        in_specs=[pl.BlockSpec((tm, d), lambda i, j: (i, 0)),
                  pl.BlockSpec((1, d), lambda i, j: (0, 0)),
                  pl.BlockSpec((1, 1, d), grp),
                  pl.BlockSpec((1, 1, d), grp),
                  pl.BlockSpec((d, tn), lambda i, j: (0, j))],
        out_specs=pl.BlockSpec((tm, tn), lambda i, j: (i, j)),
        out_shape=jax.ShapeDtypeStruct((m, n), F32),
        scratch_shapes=[pltpu.VMEM((tm, d), BF16)],
        compiler_params=_cparams(("parallel", "arbitrary")),
        name="in_proj",
    )(x, g.reshape(1, d), sc, sh, w)


def _prenorm_kernel(x_ref, g_ref, sc_ref, sh_ref, h_ref):
    h_ref[...] = _norm_mod(x_ref[...], g_ref[...], sc_ref[0], sh_ref[0]).astype(h_ref.dtype)


def _prenorm_router_kernel(x_ref, g_ref, sc_ref, sh_ref, r_ref, h_ref, l_ref):
    h = _norm_mod(x_ref[...], g_ref[...], sc_ref[0], sh_ref[0])
    h_ref[...] = h.astype(h_ref.dtype)
    l_ref[...] = jnp.dot(h, r_ref[...], preferred_element_type=F32, precision=lax.Precision.HIGHEST)


def prenorm(x, g, sc, sh, rows_per_group, router=None):
    m, d = x.shape
    tm = _pick_tile(rows_per_group, (512, 256, 128, 64, 32, 16, 8))
    grp = lambda i: ((i * tm) // rows_per_group, 0, 0)
    in_specs = [pl.BlockSpec((tm, d), lambda i: (i, 0)),
                pl.BlockSpec((1, d), lambda i: (0, 0)),
                pl.BlockSpec((1, 1, d), grp),
                pl.BlockSpec((1, 1, d), grp)]
    if router is None:
        return pl.pallas_call(
            _prenorm_kernel, grid=(m // tm,), in_specs=in_specs,
            out_specs=pl.BlockSpec((tm, d), lambda i: (i, 0)),
            out_shape=jax.ShapeDtypeStruct((m, d), BF16),
            compiler_params=_cparams(("parallel",)), name="prenorm",
        )(x, g.reshape(1, d), sc, sh)
    ne = router.shape[1]
    rpad = jnp.zeros((d, 128), F32).at[:, :ne].set(router)
    h, logits = pl.pallas_call(
        _prenorm_router_kernel, grid=(m // tm,),
        in_specs=in_specs + [pl.BlockSpec((d, 128), lambda i: (0, 0))],
        out_specs=[pl.BlockSpec((tm, d), lambda i: (i, 0)), pl.BlockSpec((tm, 128), lambda i: (i, 0))],
        out_shape=[jax.ShapeDtypeStruct((m, d), BF16), jax.ShapeDtypeStruct((m, 128), F32)],
        compiler_params=_cparams(("parallel",)), name="prenorm_router",
    )(x, g.reshape(1, d), sc, sh, rpad)
    return h, logits[:, :ne]


def _resnorm_kernel(x_ref, f_ref, g_ref, gt_ref, o_ref):
    f = f_ref[...]
    ms = jnp.mean(f * f, axis=-1, keepdims=True)
    o_ref[...] = x_ref[...] + gt_ref[0] * (f * lax.rsqrt(ms + NORM_EPS) * g_ref[...])


def resnorm(x, f, g, gt, rows_per_group):
    m, d = x.shape
    tm = _pick_tile(rows_per_group, (512, 256, 128, 64, 32, 16, 8))
    grp = lambda i: ((i * tm) // rows_per_group, 0, 0)
    return pl.pallas_call(
        _resnorm_kernel, grid=(m // tm,),
        in_specs=[pl.BlockSpec((tm, d), lambda i: (i, 0)),
                  pl.BlockSpec((tm, d), lambda i: (i, 0)),
                  pl.BlockSpec((1, d), lambda i: (0, 0)),
                  pl.BlockSpec((1, 1, d), grp)],
        out_specs=pl.BlockSpec((tm, d), lambda i: (i, 0)),
        out_shape=jax.ShapeDtypeStruct((m, d), F32),
        compiler_params=_cparams(("parallel",)), name="resnorm",
    )(x, f, g.reshape(1, d), gt)


LOG2E = 1.0 / math.log(2.0)


def _attn_heads(q, k, v, bias, sink_ref, o_ref):
    qs = (q * (ATT_SCALE * LOG2E)).astype(BF16)
    ones = jnp.ones((v.shape[0], HEAD_DIM), BF16)
    heads = range(ATT_HEADS)
    kh = [_hslice(k, h) for h in range(ATT_KV_HEADS)]
    v1 = [jnp.concatenate([_hslice(v, h), ones], axis=1) for h in range(ATT_KV_HEADS)]
    sink2 = [sink_ref[hd] * LOG2E for hd in heads]
    s = [lax.dot_general(_hslice(qs, hd), kh[hd // ATT_GROUP], (((1,), (1,)), ((), ())),
                         preferred_element_type=F32) for hd in heads]
    if bias is not None:
        s = [sh + bias for sh in s]
    m = [jnp.maximum(jnp.max(s[hd], axis=-1, keepdims=True), sink2[hd]) for hd in heads]
    p = [jnp.exp2(s[hd] - m[hd]).astype(BF16) for hd in heads]
    pv = [jnp.dot(p[hd], v1[hd // ATT_GROUP], preferred_element_type=F32) for hd in heads]
    outs = [pv[hd][:, :HEAD_DIM] / (pv[hd][:, HEAD_DIM:] + jnp.exp2(sink2[hd] - m[hd])) for hd in heads]
    o_ref[0] = jnp.concatenate(outs, axis=-1)


def _hslice(t, h):
    return t[:, h * HEAD_DIM:(h + 1) * HEAD_DIM]


def _win_attn_kernel(sink_ref, q_ref, qp_ref, cq_ref, sq_ref,
                     k0_ref, k1_ref, k2_ref, kp0_ref, kp1_ref, kp2_ref,
                     v0_ref, v1_ref, v2_ref, c0_ref, c1_ref, c2_ref, s0_ref, s1_ref, s2_ref,
                     kc_ref, vc_ref, o_ref, *, nb):
    j = pl.program_id(1)
    q = q_ref[0] * cq_ref[...] + qp_ref[0] * sq_ref[...]
    kb = jnp.concatenate([k0_ref[0] * c0_ref[...] + kp0_ref[0] * s0_ref[...],
                          k1_ref[0] * c1_ref[...] + kp1_ref[0] * s1_ref[...],
                          k2_ref[0] * c2_ref[...] + kp2_ref[0] * s2_ref[...]], axis=0).astype(BF16)
    vb = jnp.concatenate([v0_ref[0], v1_ref[0], v2_ref[0]], axis=0).astype(BF16)
    n_ctx = kc_ref.shape[1]
    k = jnp.concatenate([kc_ref[0].astype(BF16), kb], axis=0)
    v = jnp.concatenate([vc_ref[0].astype(BF16), vb], axis=0)
    n = n_ctx + 3 * ATT_BLOCK
    qi = lax.broadcasted_iota(jnp.int32, (ATT_BLOCK, n), 0) + ATT_BLOCK
    ki = lax.broadcasted_iota(jnp.int32, (ATT_BLOCK, n), 1) - n_ctx
    lo = jnp.where(j == 0, ATT_BLOCK, 0)
    hi = jnp.where(j == nb - 1, 2 * ATT_BLOCK, 3 * ATT_BLOCK)
    far = jnp.where(ki < lo, 4 * ATT_BLOCK, 0) + jnp.where(ki >= hi, 4 * ATT_BLOCK, 0)
    seen = jnp.where(ki < 0, 0, jnp.abs(qi - ki) + far) <= WINDOW
    bias = jnp.where(seen, 0.0, -jnp.inf).astype(F32)
    _attn_heads(q, k, v, bias, sink_ref, o_ref)


def _ctx_attn_kernel(sink_ref, q_ref, kc_ref, vc_ref, o_ref):
    _attn_heads(q_ref[0], kc_ref[0].astype(BF16), vc_ref[0].astype(BF16), None, sink_ref, o_ref)


def windowed_attention(zl3, zc3, cos_t, sin_t, sink):
    b, t, _ = zl3.shape
    l = zc3.shape[1]
    nb = t // ATT_BLOCK
    kcol, kpcol, vcol = ZK // KV_W, ZKP // KV_W, ZV // KV_W
    prev = lambda j: jnp.maximum(j - 1, 0)
    nxt = lambda j: jnp.minimum(j + 1, nb - 1)

    def zspec(width, rowf, col):
        return pl.BlockSpec((1, ATT_BLOCK, width), lambda bi, j: (bi, rowf(j), col))

    def tspec(width, rowf):
        return pl.BlockSpec((ATT_BLOCK, width), lambda bi, j: (rowf(j), 0))

    same = lambda j: j
    in_specs = [pl.BlockSpec(memory_space=pltpu.SMEM),
                zspec(Q_W, same, ZQ // Q_W), zspec(Q_W, same, ZQP // Q_W), tspec(Q_W, same), tspec(Q_W, same)]
    args = [sink, zl3, zl3, cos_t, sin_t]
    for col in (kcol, kpcol, vcol):
        in_specs += [zspec(KV_W, prev, col), zspec(KV_W, same, col), zspec(KV_W, nxt, col)]
        args += [zl3, zl3, zl3]
    for tab in (cos_t, sin_t):
        in_specs += [tspec(KV_W, prev), tspec(KV_W, same), tspec(KV_W, nxt)]
        args += [tab, tab, tab]
    in_specs += [pl.BlockSpec((1, l, KV_W), lambda bi, j: (bi, 0, kcol)),
                 pl.BlockSpec((1, l, KV_W), lambda bi, j: (bi, 0, vcol))]
    args += [zc3, zc3]
    return pl.pallas_call(
        functools.partial(_win_attn_kernel, nb=nb),
        grid=(b, nb), in_specs=in_specs,
        out_specs=pl.BlockSpec((1, ATT_BLOCK, Q_W), lambda bi, j: (bi, j, 0)),
        out_shape=jax.ShapeDtypeStruct((b, t, Q_W), F32),
        compiler_params=_cparams(("parallel", "parallel")), name="win_attn",
    )(*args)


def context_attention(zc3, sink):
    b, l, _ = zc3.shape
    return pl.pallas_call(
        _ctx_attn_kernel, grid=(b,),
        in_specs=[pl.BlockSpec(memory_space=pltpu.SMEM),
                  pl.BlockSpec((1, l, Q_W), lambda bi: (bi, 0, ZQ // Q_W)),
                  pl.BlockSpec((1, l, KV_W), lambda bi: (bi, 0, ZK // KV_W)),
                  pl.BlockSpec((1, l, KV_W), lambda bi: (bi, 0, ZV // KV_W))],
        out_specs=pl.BlockSpec((1, l, Q_W), lambda bi: (bi, 0, 0)),
        out_shape=jax.ShapeDtypeStruct((b, l, Q_W), F32),
        compiler_params=_cparams(("parallel",)), name="ctx_attn",
    )(sink, zc3, zc3, zc3)


HALO = 8


def _head_sums(x, e_ref):
    hi = x.astype(BF16)
    lo = (x - hi.astype(F32)).astype(BF16)
    return (jnp.dot(hi, e_ref[...], preferred_element_type=F32)
            + jnp.dot(lo, e_ref[...], preferred_element_type=F32))


def _shift_lerp_tile(cur_ref, prev_ref, next_ref, mu_ref, buf_ref, first, last):
    tt = cur_ref.shape[1]
    z = cur_ref[0]
    buf_ref[0:HALO] = jnp.where(first, 0.0, prev_ref[0])
    buf_ref[HALO:HALO + tt] = z
    buf_ref[HALO + tt:2 * HALO + tt] = jnp.where(last, 0.0, next_ref[0])
    nb = buf_ref[HALO - 1:HALO - 1 + tt] + buf_ref[HALO + 1:HALO + 1 + tt]
    return z + mu_ref[...] * (0.5 * nb - z)


def _split_bf16(a):
    hi = a.astype(BF16)
    return hi, (a - hi.astype(F32)).astype(BF16)


def _dot_3pass(a, b):
    a_hi, a_lo = _split_bf16(a)
    b_hi, b_lo = _split_bf16(b)
    return (jnp.dot(a_hi, b_hi, preferred_element_type=F32) + jnp.dot(a_lo, b_hi, preferred_element_type=F32)
            + jnp.dot(a_hi, b_lo, preferred_element_type=F32))


def _rwkv_prep_kernel(*refs, has_vres, n_alias):
    (wa_c, wa_p, wa_n, r_c, r_p, r_n, k_c, k_p, k_n, v_c, v_p, v_n, g_c, g_p, g_n,
     mu_wa, mu_r, mu_k, mu_v, mu_g, w0_ref, w2_ref, a0_ref, a2_ref, g2_ref, kk_ref, ka_ref, rk_ref, e_ref) = refs[:29]
    pos = 29
    if has_vres:
        vf_ref, v0_ref, v1_ref, v2_ref = refs[pos:pos + 4]
        pos += 4
    pos += n_alias
    (rk_o, vk_o, w_o, an_o, bv_o, g_o) = refs[pos:pos + 6]
    pos += 6
    if not has_vres:
        vin_o = refs[pos]
        pos += 1
    buf_wa, buf_r, buf_k, buf_v, buf_g = refs[pos:pos + 5]

    i = pl.program_id(1)
    first = i == 0
    last = i == pl.num_programs(1) - 1
    zwa = _shift_lerp_tile(wa_c, wa_p, wa_n, mu_wa, buf_wa, first, last)
    r = _shift_lerp_tile(r_c, r_p, r_n, mu_r, buf_r, first, last)
    k = _shift_lerp_tile(k_c, k_p, k_n, mu_k, buf_k, first, last)
    v = _shift_lerp_tile(v_c, v_p, v_n, mu_v, buf_v, first, last)
    zg = _shift_lerp_tile(g_c, g_p, g_n, mu_g, buf_g, first, last)

    if has_vres:
        mix = _dot_3pass(_dot_3pass(v, v1_ref[...]), v2_ref[...])
        v = v + (vf_ref[0] - v) * jax.nn.sigmoid(v0_ref[...] + mix)
    else:
        vin_o[0] = v
    kk = k * kk_ref[...]
    kk = kk / jnp.maximum(jnp.sqrt(_head_sums(kk * kk, e_ref)), 1e-12)
    tz = jnp.tanh(zwa)
    ksum = None
    for d in range(2):
        w_log = -jax.nn.softplus(-(w0_ref[d:d + 1, :] + _dot_3pass(tz, w2_ref[d]))) - 0.5
        w_o[d, 0] = jnp.exp(-jnp.exp(w_log))
        a = jax.nn.sigmoid(a0_ref[d:d + 1, :] + _dot_3pass(zwa, a2_ref[d]))
        an_o[d, 0] = -a
        kd = k * (1.0 + (a - 1.0) * ka_ref[...])
        ksum = kd if ksum is None else ksum + kd
    rk_o[0, 0] = r
    rk_o[1, 0] = k
    vk_o[0, 0] = v
    vk_o[1, 0] = kk
    bv_o[0] = _head_sums(r * ksum * rk_ref[...], e_ref) * v
    g_o[0] = _dot_3pass(jax.nn.sigmoid(zg), g2_ref[...])


def _head_ones():
    idx = jnp.arange(RWKV_W) // RWKV_HEAD
    return (idx[:, None] == idx[None, :]).astype(BF16)


def rwkv_prep(z3, lp, v_first, ttot, row_off, pair_bufs):
    b, tx, _ = z3.shape
    tt = _pick_tile(math.gcd(tx, row_off) if row_off else tx, (256, 128, 64, 32, 16, 8))
    nh = tt // HALO
    nblk = tx // HALO
    has_vres = v_first is not None
    lanes = 128

    def trio(width, col):
        return [pl.BlockSpec((1, tt, width), lambda bi, i: (bi, i, col)),
                pl.BlockSpec((1, HALO, width), lambda bi, i: (bi, jnp.maximum(i * nh - 1, 0), col)),
                pl.BlockSpec((1, HALO, width), lambda bi, i: (bi, jnp.minimum((i + 1) * nh, nblk - 1), col))]

    def full(shape):
        return pl.BlockSpec(shape, lambda bi, i: (0,) * len(shape))

    mu = lp['mu']
    o = 3 * RWKV_W
    nwa = 2 * DECAY_RANK + 2 * ICLR_RANK
    mu_g = jnp.zeros((lanes,), F32).at[:GATE_RANK].set(mu[o + nwa:])
    w2f = jnp.zeros((2, lanes, RWKV_W), F32)
    a2f = jnp.zeros((2, lanes, RWKV_W), F32)
    for d in range(2):
        w2f = w2f.at[d, d * DECAY_RANK:(d + 1) * DECAY_RANK].set(lp['w2'][d])
        a2f = a2f.at[d, 2 * DECAY_RANK + d * ICLR_RANK:2 * DECAY_RANK + (d + 1) * ICLR_RANK].set(lp['a2'][d])
    g2f = jnp.zeros((lanes, RWKV_W), F32).at[:GATE_RANK].set(lp['g2'])
    row = lambda a: a.reshape(1, -1)
    in_specs = (trio(lanes, ZWA // lanes) + trio(RWKV_W, ZR // RWKV_W) + trio(RWKV_W, ZRK // RWKV_W)
                + trio(RWKV_W, ZRV // RWKV_W) + trio(lanes, ZZG // lanes)
                + [full((1, lanes)), full((1, RWKV_W)), full((1, RWKV_W)), full((1, RWKV_W)), full((1, lanes)),
                   full((2, RWKV_W)), full((2, lanes, RWKV_W)), full((2, RWKV_W)), full((2, lanes, RWKV_W)),
                   full((lanes, RWKV_W)), full((1, RWKV_W)), full((1, RWKV_W)), full((1, RWKV_W)),
                   full((RWKV_W, RWKV_W))])
    args = [z3] * 15 + [row(mu[o:o + nwa]), row(mu[:RWKV_W]), row(mu[RWKV_W:2 * RWKV_W]), row(mu[2 * RWKV_W:o]),
                        row(mu_g), lp['w0'], w2f, lp['a0'], a2f, g2f, row(lp['k_k']), row(lp['k_a']),
                        row(lp['r_k']), _head_ones()]
    tok = pl.BlockSpec((1, tt, RWKV_W), lambda bi, i: (bi, i, 0))
    if has_vres:
        rank = lp['v1'].shape[1]
        v1f = jnp.zeros((RWKV_W, lanes), F32).at[:, :rank].set(lp['v1'])
        v2f = jnp.zeros((lanes, RWKV_W), F32).at[:rank].set(lp['v2'])
        in_specs += [tok, full((1, RWKV_W)), full((RWKV_W, lanes)), full((lanes, RWKV_W))]
        args += [v_first, row(lp['v0']), v1f, v2f]
    n_alias = 0 if pair_bufs is None else len(pair_bufs)
    aliases = {}
    if pair_bufs is not None:
        aliases = {len(args) + n: n for n in range(n_alias)}
        in_specs += [pl.BlockSpec(memory_space=pl.ANY)] * n_alias
        args += list(pair_bufs)
    blk_off = row_off // tt
    pair = pl.BlockSpec((2, 1, tt, RWKV_W), lambda bi, i: (0, bi, i + blk_off, 0))
    pair_shape = jax.ShapeDtypeStruct((2, b, ttot, RWKV_W), F32)
    tok_shape = jax.ShapeDtypeStruct((b, tx, RWKV_W), F32)
    n_tok = 2 if has_vres else 3
    outs = pl.pallas_call(
        functools.partial(_rwkv_prep_kernel, has_vres=has_vres, n_alias=n_alias),
        grid=(b, tx // tt), in_specs=in_specs,
        out_specs=[pair] * 4 + [tok] * n_tok,
        out_shape=[pair_shape] * 4 + [tok_shape] * n_tok,
        input_output_aliases=aliases,
        scratch_shapes=[pltpu.VMEM((tt + 2 * HALO, lanes), F32)] + [pltpu.VMEM((tt + 2 * HALO, RWKV_W), F32)] * 3
                       + [pltpu.VMEM((tt + 2 * HALO, lanes), F32)],
        compiler_params=_cparams(("parallel", "parallel")), name="rwkv_prep",
    )(*args)
    res = {'pairs': tuple(outs[:4]), 'bv': outs[4], 'g': outs[5]}
    if not has_vres:
        res['v_in'] = outs[6]
    return res


CONV_HALO = 16
CONV_ROWS = 32


def _conv_kernel(a_c, a_p, a_n, b_c, b_p, b_n, w_ref, bias_ref, lg_ref, lb_ref, o_ref, buf_ref, win_ref):
    i = pl.program_id(1)
    first = i == 0
    last = i == pl.num_programs(1) - 1
    tt = a_c.shape[1]
    glu = lambda a, b: a * jax.nn.sigmoid(b)
    buf_ref[0:CONV_HALO] = jnp.where(first, 0.0, glu(a_p[0], b_p[0]))
    buf_ref[CONV_HALO:CONV_HALO + tt] = glu(a_c[0], b_c[0])
    buf_ref[CONV_HALO + tt:2 * CONV_HALO + tt] = jnp.where(last, 0.0, glu(a_n[0], b_n[0]))
    half = CONV_K // 2
    span = CONV_ROWS + ((CONV_K - 1) // SUBLANES) * SUBLANES
    for r0 in range(0, tt, CONV_ROWS):
        base = CONV_HALO - half + r0
        acc = bias_ref[...]
        for ph in range(SUBLANES):
            win_ref[...] = buf_ref[base + ph:base + ph + span]
            for j in range(ph, CONV_K, SUBLANES):
                acc = acc + win_ref[j - ph:j - ph + CONV_ROWS] * w_ref[j:j + 1, :]
        m = jnp.mean(acc, axis=-1, keepdims=True)
        cen = acc - m
        var = jnp.mean(cen * cen, axis=-1, keepdims=True)
        y = cen * lax.rsqrt(var + CONV_LN_EPS) * lg_ref[...] + lb_ref[...]
        o_ref[0, r0:r0 + CONV_ROWS, :] = y * jax.nn.sigmoid(y)


def conv_module(z3, lp):
    b, tx, _ = z3.shape
    tt = _pick_tile(tx, (256, 128, 64, 32))
    nh = tt // CONV_HALO
    nblk = tx // CONV_HALO

    def trio(col):
        return [pl.BlockSpec((1, tt, CONV_W), lambda bi, i: (bi, i, col)),
                pl.BlockSpec((1, CONV_HALO, CONV_W), lambda bi, i: (bi, jnp.maximum(i * nh - 1, 0), col)),
                pl.BlockSpec((1, CONV_HALO, CONV_W), lambda bi, i: (bi, jnp.minimum((i + 1) * nh, nblk - 1), col))]

    kpad = -(-CONV_K // SUBLANES) * SUBLANES
    wpad = jnp.zeros((kpad, CONV_W), F32).at[:CONV_K].set(lp['dw_w'])
    vec = pl.BlockSpec((1, CONV_W), lambda bi, i: (0, 0))
    return pl.pallas_call(
        _conv_kernel, grid=(b, tx // tt),
        in_specs=trio(ZCA // CONV_W) + trio(ZCB // CONV_W)
                 + [pl.BlockSpec((kpad, CONV_W), lambda bi, i: (0, 0)), vec, vec, vec],
        out_specs=pl.BlockSpec((1, tt, CONV_W), lambda bi, i: (bi, i, 0)),
        out_shape=jax.ShapeDtypeStruct((b, tx, CONV_W), F32),
        scratch_shapes=[pltpu.VMEM((tt + 2 * CONV_HALO, CONV_W), F32),
                        pltpu.VMEM((CONV_ROWS + ((CONV_K - 1) // SUBLANES) * SUBLANES, CONV_W), F32)],
        compiler_params=_cparams(("parallel", "parallel")), name="conv_module",
    )(z3, z3, z3, z3, z3, z3, wpad, lp['dw_b'].reshape(1, -1), lp['cln_g'].reshape(1, -1), lp['cln_b'].reshape(1, -1))


SUBLANES = 8
DECAY_FOLD = 16


def _wkv_scan_kernel(rk_f, vk_f, w_f, an_f, rk_b, vk_b, w_b, an_b, ka_ref, yf_ref, yb_ref, s_ref, *, tt):
    @pl.when(pl.program_id(0) == 0)
    def _():
        s_ref[...] = jnp.zeros_like(s_ref)

    nk, _, lanes = s_ref.shape
    half = lanes // 2
    fwd = lax.broadcasted_iota(jnp.int32, (nk, lanes), 1) < half
    ka = ka_ref[...]
    fold = math.gcd(tt, DECAY_FOLD)

    def group(gi, carry):
        def step(ti, gam):
            t = gi * fold + ti
            tb = tt - 1 - t
            rkf, rkb, vkf, vkb = rk_f[t], rk_b[tb], vk_f[t], vk_b[tb]
            r = jnp.where(fwd, rkf, pltpu.roll(rkb, half, 1))
            k = jnp.where(fwd, pltpu.roll(rkf, half, 1), rkb)
            v = jnp.where(fwd, vkf, pltpu.roll(vkb, half, 1))
            kk = jnp.where(fwd, pltpu.roll(vkf, half, 1), vkb)
            w = jnp.where(fwd, w_f[t], w_b[tb])
            an = jnp.where(fwd, an_f[t], an_b[tb])
            gam_new = gam * w
            inv = 1.0 / gam_new
            kks = kk * gam
            kds = k * (1.0 - (an + 1.0) * ka) * inv
            bbs = kk * an * inv
            rs = r * gam_new
            row = lambda a, ki: a[ki:ki + 1, :]
            sa = [s_ref[0] * row(kks, 0), s_ref[1] * row(kks, 1)]
            for ki in range(2, nk):
                sa[ki % 2] = sa[ki % 2] + s_ref[ki] * row(kks, ki)
            sa = sa[0] + sa[1]
            ys = [None, None]
            for ki in range(nk):
                s_new = s_ref[ki] + sa * row(bbs, ki) + v * row(kds, ki)
                s_ref[ki] = s_new
                contrib = s_new * row(rs, ki)
                ys[ki % 2] = contrib if ys[ki % 2] is None else ys[ki % 2] + contrib
            y = ys[0] + ys[1]
            yf_ref[t] = y
            yb_ref[tb] = y
            return gam_new

        gam = lax.fori_loop(0, fold, step, jnp.ones((nk, lanes), F32), unroll=4)
        for ki in range(nk):
            s_ref[ki] = s_ref[ki] * gam[ki:ki + 1, :]
        return carry

    lax.fori_loop(0, tt // fold, group, 0)


def wkv_scan(rk, vk, w, an, ka, n_ctx_steps):
    ttot, n, ch = rk.shape
    tt = _pick_tile(math.gcd(n_ctx_steps, ttot - n_ctx_steps), (32, 16, 8, 4, 2, 1))
    nt, nc = ttot // tt, n_ctx_steps // tt
    fspec = pl.BlockSpec((tt, n, ch), lambda i: (i, 0, 0))
    bspec = pl.BlockSpec((tt, n, ch), lambda i: (jnp.where(i < nc, nc - 1 - i, nt - 1 - (i - nc)), 0, 0))
    return pl.pallas_call(
        functools.partial(_wkv_scan_kernel, tt=tt),
        grid=(nt,), in_specs=[fspec] * 4 + [bspec] * 4 + [pl.BlockSpec((n, ch), lambda i: (0, 0))],
        out_specs=[fspec, bspec],
        out_shape=[jax.ShapeDtypeStruct((ttot, n, ch), F32)] * 2,
        scratch_shapes=[pltpu.VMEM((n, n, ch), F32)],
        compiler_params=_cparams(("arbitrary",)), name="wkv_scan",
    )(rk, vk, w, an, rk, vk, w, an, ka)


def _merge_kernel(x_ref, att_ref, y_ref, bv_ref, rg_ref, cv_ref, g0_ref, g1_ref, g2_ref,
                  lg_ref, lb_ref, e_ref, wa_ref, wr_ref, wc_ref, wo_ref, ng_ref, gt_ref, o_ref):
    def proj(a, w_ref):
        return jnp.dot(a.astype(BF16), w_ref[...], preferred_element_type=F32)

    y = y_ref[...]
    cen = y - _head_sums(y, e_ref) * (1.0 / RWKV_HEAD)
    var = _head_sums(cen * cen, e_ref) * (1.0 / RWKV_HEAD)
    rw = (cen * lax.rsqrt(var + LNX_EPS) * lg_ref[...] + lb_ref[...] + bv_ref[...]) * rg_ref[...]
    m = (jax.nn.sigmoid(g0_ref[...]) * proj(att_ref[...], wa_ref)
         + jax.nn.sigmoid(g1_ref[...]) * proj(rw, wr_ref)
         + jax.nn.sigmoid(g2_ref[...]) * proj(cv_ref[...], wc_ref))
    ml = proj(m, wo_ref)
    ms = jnp.mean(ml * ml, axis=-1, keepdims=True)
    o_ref[...] = x_ref[...] + gt_ref[0] * (ml * lax.rsqrt(ms + NORM_EPS) * ng_ref[...])


def merge_out(x, att, y, bv, rg, cv, z, lnx_g, lnx_b, wa, wr, wc, wo, ng, gt, rows_per_group):
    m, d = x.shape
    tm = _pick_tile(rows_per_group, (512, 256, 128, 64, 32, 16, 8))
    row = lambda i: (i, 0)
    const = lambda i: (0, 0)
    gcol = ZG // d
    br = pl.BlockSpec((tm, RWKV_W), row)
    return pl.pallas_call(
        _merge_kernel, grid=(m // tm,),
        in_specs=[pl.BlockSpec((tm, d), row),
                  pl.BlockSpec((tm, Q_W), row), br, br, br, pl.BlockSpec((tm, CONV_W), row),
                  pl.BlockSpec((tm, d), lambda i: (i, gcol)),
                  pl.BlockSpec((tm, d), lambda i: (i, gcol + 1)),
                  pl.BlockSpec((tm, d), lambda i: (i, gcol + 2)),
                  pl.BlockSpec((1, RWKV_W), const), pl.BlockSpec((1, RWKV_W), const),
                  pl.BlockSpec((RWKV_W, RWKV_W), const),
                  pl.BlockSpec((Q_W, d), const), pl.BlockSpec((RWKV_W, d), const), pl.BlockSpec((CONV_W, d), const),
                  pl.BlockSpec((d, d), const), pl.BlockSpec((1, d), const),
                  pl.BlockSpec((1, 1, d), lambda i: ((i * tm) // rows_per_group, 0, 0))],
        out_specs=pl.BlockSpec((tm, d), row),
        out_shape=jax.ShapeDtypeStruct((m, d), F32),
        compiler_params=_cparams(("parallel",)), name="merge_out",
    )(x, att, y, bv, rg, cv, z, z, z, lnx_g.reshape(1, -1), lnx_b.reshape(1, -1), _head_ones(),
      wa, wr, wc, wo, ng.reshape(1, d), gt)


def _swiglu_kernel(te_ref, nu_ref, h_ref, wg_ref, wu_ref, wd_ref, o_ref):
    f = pl.program_id(1)

    used = pl.program_id(0) < nu_ref[0]

    @pl.when(jnp.logical_and(used, f == 0))
    def _():
        o_ref[...] = jnp.zeros_like(o_ref)

    @pl.when(used)
    def _():
        h = h_ref[...]
        hg = jnp.dot(h, wg_ref[0].astype(BF16), preferred_element_type=F32)
        hu = jnp.dot(h, wu_ref[0].astype(BF16), preferred_element_type=F32)
        act = (hg * jax.nn.sigmoid(hg) * hu).astype(BF16)
        o_ref[...] += jnp.dot(act, wd_ref[0].astype(BF16), preferred_element_type=F32)


def grouped_swiglu(h, tile_expert, n_used, wg, wu, wd, tm):
    p, d = h.shape
    ff = wg.shape[2]
    tf = _pick_tile(ff, (512, 1408, 256, 128))
    nf = ff // tf
    ti = lambda i, nu: jnp.minimum(i, nu[0] - 1)
    fi = lambda i, f, nu: jnp.where(i < nu[0], f, nf - 1)
    grid_spec = pltpu.PrefetchScalarGridSpec(
        num_scalar_prefetch=2, grid=(p // tm, nf),
        in_specs=[pl.BlockSpec((tm, d), lambda i, f, te, nu: (ti(i, nu), 0)),
                  pl.BlockSpec((1, d, tf), lambda i, f, te, nu: (te[ti(i, nu)], 0, fi(i, f, nu))),
                  pl.BlockSpec((1, d, tf), lambda i, f, te, nu: (te[ti(i, nu)], 0, fi(i, f, nu))),
                  pl.BlockSpec((1, tf, d), lambda i, f, te, nu: (te[ti(i, nu)], fi(i, f, nu), 0))],
        out_specs=pl.BlockSpec((tm, d), lambda i, f, te, nu: (ti(i, nu), 0)))
    return pl.pallas_call(
        _swiglu_kernel, grid_spec=grid_spec,
        out_shape=jax.ShapeDtypeStruct((p, d), F32),
        compiler_params=_cparams(("arbitrary", "arbitrary")), name="swiglu",
    )(tile_expert, n_used, h, wg, wu, wd)


def _dense_ffn_kernel(x_ref, g_ref, sc_ref, sh_ref, wg_ref, wu_ref, wd_ref, g2_ref, gt_ref, o_ref, h_ref, acc_ref):
    f = pl.program_id(1)

    @pl.when(f == 0)
    def _():
        h_ref[...] = _norm_mod(x_ref[...], g_ref[...], sc_ref[0], sh_ref[0]).astype(BF16)
        acc_ref[...] = jnp.zeros_like(acc_ref)

    h = h_ref[...]
    hg = jnp.dot(h, wg_ref[...], preferred_element_type=F32)
    hu = jnp.dot(h, wu_ref[...], preferred_element_type=F32)
    act = (hg * jax.nn.sigmoid(hg) * hu).astype(BF16)
    acc_ref[...] += jnp.dot(act, wd_ref[...], preferred_element_type=F32)

    @pl.when(f == pl.num_programs(1) - 1)
    def _():
        y = acc_ref[...]
        ms = jnp.mean(y * y, axis=-1, keepdims=True)
        o_ref[...] = x_ref[...] + gt_ref[0] * (y * lax.rsqrt(ms + NORM_EPS) * g2_ref[...])


def dense_ffn(x, g_in, sc, sh, wg, wu, wd, g_out, gt, rows_per_group):
    m, d = x.shape
    ff = wg.shape[1]
    tm = _pick_tile(rows_per_group, (512, 256, 128, 64, 32, 16, 8))
    tf = _pick_tile(ff, (1408, 512, 256, 128))
    grp = lambda i, f: ((i * tm) // rows_per_group, 0, 0)
    vec = pl.BlockSpec((1, d), lambda i, f: (0, 0))
    return pl.pallas_call(
        _dense_ffn_kernel, grid=(m // tm, ff // tf),
        in_specs=[pl.BlockSpec((tm, d), lambda i, f: (i, 0)), vec,
                  pl.BlockSpec((1, 1, d), grp), pl.BlockSpec((1, 1, d), grp),
                  pl.BlockSpec((d, tf), lambda i, f: (0, f)), pl.BlockSpec((d, tf), lambda i, f: (0, f)),
                  pl.BlockSpec((tf, d), lambda i, f: (f, 0)), vec, pl.BlockSpec((1, 1, d), grp)],
        out_specs=pl.BlockSpec((tm, d), lambda i, f: (i, 0)),
        out_shape=jax.ShapeDtypeStruct((m, d), F32),
        scratch_shapes=[pltpu.VMEM((tm, d), BF16), pltpu.VMEM((tm, d), F32)],
        compiler_params=_cparams(("parallel", "arbitrary")), name="dense_ffn",
    )(x, g_in.reshape(1, d), sc, sh, wg, wu, wd, g_out.reshape(1, d), gt)


def _rope_tables(t):
    rows = t // GRID_W
    row = jnp.broadcast_to(jnp.arange(rows)[:, None], (rows, GRID_W)).reshape(-1).astype(F32)
    col = jnp.broadcast_to(jnp.arange(GRID_W)[None, :], (rows, GRID_W)).reshape(-1).astype(F32)
    inv = ROPE_BASE ** (-jnp.arange(0, AXIS_DIM, 2, dtype=F32) / AXIS_DIM)
    ar = row[:, None] * inv[None, :]
    ac = col[:, None] * inv[None, :]
    cos64 = jnp.concatenate([jnp.cos(ar), jnp.cos(ar), jnp.cos(ac), jnp.cos(ac)], axis=-1)
    sin64 = jnp.concatenate([jnp.sin(ar), jnp.sin(ar), jnp.sin(ac), jnp.sin(ac)], axis=-1)
    return jnp.tile(cos64, (1, ATT_HEADS)), jnp.tile(sin64, (1, ATT_HEADS))


def _rope_partner(w):
    d = w.shape[0]
    q = AXIS_DIM // 2
    wh = w.reshape(d, -1, 4, q)
    return jnp.stack([-wh[:, :, 1], wh[:, :, 0], -wh[:, :, 3], wh[:, :, 2]], axis=2).reshape(w.shape)


def _build_w_ext(w_in):
    d = w_in.shape[0]
    wq = w_in[:, OFF_Q:OFF_K]
    wk = w_in[:, OFF_K:OFF_V]
    o = OFF_R + 3 * RWKV_W
    parts = [(ZQ, wq), (ZQP, _rope_partner(wq)), (ZK, wk), (ZKP, _rope_partner(wk)),
             (ZV, w_in[:, OFF_V:OFF_R]),
             (ZWA, w_in[:, o:o + 2 * DECAY_RANK + 2 * ICLR_RANK]),
             (ZCA, w_in[:, OFF_CONV:OFF_CONV + CONV_W]), (ZCB, w_in[:, OFF_CONV + CONV_W:OFF_GATE]),
             (ZR, w_in[:, OFF_R:OFF_R + RWKV_W]), (ZRK, w_in[:, OFF_R + RWKV_W:OFF_R + 2 * RWKV_W]),
             (ZRV, w_in[:, OFF_R + 2 * RWKV_W:OFF_R + 3 * RWKV_W]),
             (ZG, w_in[:, OFF_GATE:IN_COLS]),
             (ZZG, w_in[:, OFF_CONV - GATE_RANK:OFF_CONV])]
    parts.sort(key=lambda p: p[0])
    cols, pos = [], 0
    for off, wpart in parts:
        if off > pos:
            cols.append(jnp.zeros((d, off - pos), w_in.dtype))
        cols.append(wpart)
        pos = off + wpart.shape[1]
    if pos < NZ:
        cols.append(jnp.zeros((d, NZ - pos), w_in.dtype))
    return jnp.concatenate(cols, axis=1).astype(BF16)


def _chain_layout(pair):
    _, bsz, ttot, _ = pair.shape
    both = pair.reshape(2, bsz, ttot, RWKV_HEADS, RWKV_HEAD)
    return jnp.transpose(both, (2, 4, 0, 1, 3)).reshape(ttot, RWKV_HEAD, 2 * bsz * RWKV_HEADS)


def _bidirectional_wkv(pairs, k_a, l):
    bsz = pairs[0].shape[1]
    rk, vk, w, an = (_chain_layout(p) for p in pairs)
    ka = jnp.tile(k_a.reshape(RWKV_HEADS, RWKV_HEAD).T, (1, 2 * bsz))
    yf, yb = wkv_scan(rk, vk, w, an, ka, l)
    half = bsz * RWKV_HEADS
    y = (yf[:, :, :half] + yb[:, :, half:]).reshape(-1, RWKV_HEAD, bsz, RWKV_HEADS)
    y = jnp.transpose(y, (2, 0, 3, 1)).reshape(bsz, -1, RWKV_W)
    return y[:, l:], y[:, :l]


def _moe_dispatch(logits, tm):
    m = logits.shape[0]
    top_v, top_i = lax.top_k(logits, TOP_K)
    wts = jax.nn.softmax(top_v, axis=-1)
    e_flat = top_i.reshape(-1)
    n_assign = e_flat.shape[0]
    onehot = (e_flat[:, None] == jnp.arange(N_EXPERTS)[None, :]).astype(jnp.int32)
    csum = jnp.cumsum(onehot, axis=0)
    rank = jnp.sum(onehot * csum, axis=1) - 1
    cnt = csum[-1]
    pcnt = ((cnt + tm - 1) // tm) * tm
    pend = jnp.cumsum(pcnt)
    pstart = pend - pcnt
    dest = pstart[e_flat] + rank
    p_max = n_assign + N_EXPERTS * tm
    row_token = jnp.zeros((p_max,), jnp.int32).at[dest].set(jnp.arange(n_assign, dtype=jnp.int32) // TOP_K)
    tile_start = jnp.arange(p_max // tm, dtype=jnp.int32) * tm
    tile_expert = jnp.minimum(jnp.searchsorted(pend, tile_start, side='right'), N_EXPERTS - 1).astype(jnp.int32)
    n_used = (pend[-1:] // tm).astype(jnp.int32)
    return row_token, tile_expert, n_used, dest.reshape(m, TOP_K), wts


def kernel(x, c, ctx, c_ctx, mod_w, mod_b, norm_g, w_in, attn_sink, rwkv_mu, rwkv_w0, rwkv_w2, rwkv_a0, rwkv_a2, rwkv_g2, rwkv_k_k, rwkv_k_a, rwkv_r_k, rwkv_lnx_g, rwkv_lnx_b, vres_v0, vres_w1, vres_w2, conv_dw_w, conv_dw_b, conv_ln_g, conv_ln_b, w_attn_o, w_rwkv_o, w_conv_o, w_out, ffn_wg, ffn_wu, ffn_wd, moe_router, moe_wg, moe_wu, moe_wd):
    bsz, t, d = x.shape
    l = ctx.shape[1]
    depth = mod_w.shape[0]
    n_lat, n_ctx = bsz * t, bsz * l
    cos_t, sin_t = _rope_tables(t)
    xl = x.reshape(n_lat, d)
    xc = ctx.reshape(n_ctx, d)
    n_rows = -(-(bsz + 1) // SUBLANES) * SUBLANES
    cond = jnp.zeros((n_rows, d), F32).at[:bsz].set(jax.nn.silu(c)).at[bsz].set(jax.nn.silu(c_ctx))
    v_first = None
    for i in range(depth):
        need_ctx = i < depth - 1
        mod = matmul_bias_f32(cond, mod_w[i], mod_b[i])
        mod_l = [mod[:bsz, k * d:(k + 1) * d].reshape(bsz, 1, d) for k in range(6)]
        mod_c = [mod[bsz:bsz + 1, k * d:(k + 1) * d].reshape(1, 1, d) for k in range(6)]
        lp = {'mu': rwkv_mu[i], 'w0': rwkv_w0[i], 'w2': rwkv_w2[i], 'a0': rwkv_a0[i], 'a2': rwkv_a2[i],
              'g2': rwkv_g2[i], 'k_k': rwkv_k_k[i], 'k_a': rwkv_k_a[i], 'r_k': rwkv_r_k[i],
              'lnx_g': rwkv_lnx_g[i], 'lnx_b': rwkv_lnx_b[i],
              'dw_w': conv_dw_w[i], 'dw_b': conv_dw_b[i], 'cln_g': conv_ln_g[i], 'cln_b': conv_ln_b[i]}
        if i > 0:
            lp['v0'], lp['v1'], lp['v2'] = vres_v0[i - 1], vres_w1[i - 1], vres_w2[i - 1]
        w_ext = _build_w_ext(w_in[i])
        wa, wr, wc, wo = (w.astype(BF16) for w in (w_attn_o[i], w_rwkv_o[i], w_conv_o[i], w_out[i]))

        zl = nm_matmul(xl, norm_g[i, 0], mod_l[1], mod_l[0], w_ext, t)
        zc = nm_matmul(xc, norm_g[i, 0], mod_c[1], mod_c[0], w_ext, n_ctx)
        zl3 = zl.reshape(bsz, t, NZ)
        zc3 = zc.reshape(bsz, l, NZ)
        att_l = windowed_attention(zl3, zc3, cos_t, sin_t, attn_sink[i])
        p_c = rwkv_prep(zc3, lp, None if v_first is None else v_first[1], l + t, 0, None)
        p_l = rwkv_prep(zl3, lp, None if v_first is None else v_first[0], l + t, l, p_c['pairs'])
        if i == 0:
            v_first = (p_l['v_in'], p_c['v_in'])
        y_l, y_c = _bidirectional_wkv(p_l['pairs'], lp['k_a'], l)
        cv_l = conv_module(zl3, lp)
        flat = lambda a: a.reshape(-1, a.shape[-1])
        xl = merge_out(xl, flat(att_l), flat(y_l), flat(p_l['bv']), flat(p_l['g']), flat(cv_l), zl,
                       lp['lnx_g'], lp['lnx_b'], wa, wr, wc, wo, norm_g[i, 1], mod_l[2], t)
        if need_ctx:
            att_c = context_attention(zc3, attn_sink[i])
            cv_c = conv_module(zc3, lp)
            xc = merge_out(xc, flat(att_c), flat(y_c), flat(p_c['bv']), flat(p_c['g']), flat(cv_c), zc,
                           lp['lnx_g'], lp['lnx_b'], wa, wr, wc, wo, norm_g[i, 1], mod_c[2], n_ctx)

        j = i // 2
        if i % 2 == 0:
            wg, wu, wd = (w[j].astype(BF16) for w in (ffn_wg, ffn_wu, ffn_wd))
            xl = dense_ffn(xl, norm_g[i, 2], mod_l[4], mod_l[3], wg, wu, wd, norm_g[i, 3], mod_l[5], t)
            if need_ctx:
                xc = dense_ffn(xc, norm_g[i, 2], mod_c[4], mod_c[3], wg, wu, wd, norm_g[i, 3], mod_c[5], n_ctx)
        else:
            wg, wu, wd = moe_wg[j], moe_wu[j], moe_wd[j]
            streams = [(xl, mod_l, t)] + ([(xc, mod_c, n_ctx)] if need_ctx else [])
            outs = []
            for xs, ms, rpg in streams:
                n_tok = xs.shape[0]
                h, logits = prenorm(xs, norm_g[i, 2], ms[4], ms[3], rpg, router=moe_router[j])
                n_grp = MOE_GROUPS if n_tok % (MOE_GROUPS * SUBLANES) == 0 else 1
                gsz = n_tok // n_grp
                tm = _pick_tile(gsz, (1024, 512, 256, 128, 64, 32, 16, 8))
                fs = []
                for gi in range(n_grp):
                    row_token, tile_expert, n_used, dest, wts = _moe_dispatch(logits[gi * gsz:(gi + 1) * gsz], tm)
                    yg = grouped_swiglu(jnp.take(h, row_token + gi * gsz, axis=0), tile_expert, n_used,
                                        wg, wu, wd, tm)
                    fs.append(wts[:, 0:1] * jnp.take(yg, dest[:, 0], axis=0)
                              + wts[:, 1:2] * jnp.take(yg, dest[:, 1], axis=0))
                f = fs[0] if n_grp == 1 else jnp.concatenate(fs, axis=0)
                outs.append(resnorm(xs, f, norm_g[i, 3], ms[5], rpg))
            xl = outs[0]
            if need_ctx:
                xc = outs[1]
    return xl.reshape(bsz, t, d)
```

```python
import functools
import math

import jax
import jax.numpy as jnp
from jax import lax
from jax.experimental import pallas as pl
from jax.experimental.pallas import tpu as pltpu

F32 = jnp.float32
BF16 = jnp.bfloat16

D_MODEL = 1024
GRID_W = 64
ATT_HEADS = 8
ATT_KV_HEADS = 2
ATT_GROUP = ATT_HEADS // ATT_KV_HEADS
HEAD_DIM = 64
WINDOW = 128
ATT_BLOCK = 128
ATT_SCALE = 1.0 / math.sqrt(HEAD_DIM)
ROPE_BASE = 10000.0
AXIS_DIM = HEAD_DIM // 2
RWKV_HEADS = 8
RWKV_HEAD = 64
RWKV_W = RWKV_HEADS * RWKV_HEAD
DECAY_RANK = 32
ICLR_RANK = 32
GATE_RANK = 96
LNX_EPS = 64e-5
CONV_W = 512
CONV_K = 31
CONV_LN_EPS = 1e-5
N_BRANCH = 3
N_EXPERTS = 8
TOP_K = 2
NORM_EPS = 1e-6
Q_W = ATT_HEADS * HEAD_DIM
KV_W = ATT_KV_HEADS * HEAD_DIM
OFF_Q = 0
OFF_K = OFF_Q + Q_W
OFF_V = OFF_K + KV_W
OFF_R = OFF_V + KV_W
OFF_CONV = OFF_R + 3 * RWKV_W + 2 * DECAY_RANK + 2 * ICLR_RANK + GATE_RANK
OFF_GATE = OFF_CONV + 2 * CONV_W
IN_COLS = OFF_GATE + N_BRANCH * D_MODEL

ZQ = 0
ZQP = 512
ZK = 1024
ZKP = 1152
ZV = 1280
ZWA = 1408
ZCA = 1536
ZCB = 2048
ZR = 2560
ZRK = 3072
ZRV = 3584
ZG = 4096
ZZG = 7168
NZ = 7680

VMEM_LIMIT = 56 * 1024 * 1024
MOE_GROUPS = 2


def _cparams(sem):
    return pltpu.CompilerParams(dimension_semantics=sem, vmem_limit_bytes=VMEM_LIMIT)


def _pick_tile(n, candidates):
    for c in candidates:
        if n % c == 0:
            return c
    return n


def _matmul_bias_kernel(a_ref, w_ref, b_ref, o_ref):
    o_ref[...] = jnp.dot(a_ref[...], w_ref[...], preferred_element_type=F32,
                         precision=lax.Precision.HIGHEST) + b_ref[...]


def matmul_bias_f32(a, w, b):
    m, k = a.shape
    n = w.shape[1]
    tn = _pick_tile(n, (1536, 1024, 512, 256, 128))
    return pl.pallas_call(
        _matmul_bias_kernel,
        grid=(n // tn,),
        in_specs=[pl.BlockSpec((m, k), lambda j: (0, 0)),
                  pl.BlockSpec((k, tn), lambda j: (0, j)),
                  pl.BlockSpec((1, tn), lambda j: (0, j))],
        out_specs=pl.BlockSpec((m, tn), lambda j: (0, j)),
        out_shape=jax.ShapeDtypeStruct((m, n), F32),
        compiler_params=_cparams(("parallel",)),
        name="mod_matmul",
    )(a, w, b.reshape(1, n))


def _norm_mod(x, g, sc, sh):
    ms = jnp.mean(x * x, axis=-1, keepdims=True)
    return (x * lax.rsqrt(ms + NORM_EPS) * g) * (1.0 + sc) + sh


def _nm_matmul_kernel(x_ref, g_ref, sc_ref, sh_ref, w_ref, o_ref, h_ref):
    @pl.when(pl.program_id(1) == 0)
    def _():
        h_ref[...] = _norm_mod(x_ref[...], g_ref[...], sc_ref[0], sh_ref[0]).astype(BF16)

    o_ref[...] = jnp.dot(h_ref[...], w_ref[...], preferred_element_type=F32).astype(o_ref.dtype)


def nm_matmul(x, g, sc, sh, w, rows_per_group):
    m, d = x.shape
    n = w.shape[1]
    tm = _pick_tile(rows_per_group, (1024, 512, 256, 128, 64, 32, 16, 8))
    tn = _pick_tile(n, (2560, 1280, 1024, 512, 256, 128))
    grp = lambda i, j: ((i * tm) // rows_per_group, 0, 0)
    return pl.pallas_call(
        _nm_matmul_kernel,
        grid=(m // tm, n // tn),
        in_specs=[pl.BlockSpec((tm, d), lambda i, j: (i, 0)),
                  pl.BlockSpec((1, d), lambda i, j: (0, 0)),
                  pl.BlockSpec((1, 1, d), grp),
                  pl.BlockSpec((1, 1, d), grp),
                  pl.BlockSpec((d, tn), lambda i, j: (0, j))],
        out_specs=pl.BlockSpec((tm, tn), lambda i, j: (i, j)),
        out_shape=jax.ShapeDtypeStruct((m, n), F32),
        scratch_shapes=[pltpu.VMEM((tm, d), BF16)],
        compiler_params=_cparams(("parallel", "arbitrary")),
        name="in_proj",
    )(x, g.reshape(1, d), sc, sh, w)


def _prenorm_kernel(x_ref, g_ref, sc_ref, sh_ref, h_ref):
    h_ref[...] = _norm_mod(x_ref[...], g_ref[...], sc_ref[0], sh_ref[0]).astype(h_ref.dtype)


def _prenorm_router_kernel(x_ref, g_ref, sc_ref, sh_ref, r_ref, h_ref, l_ref):
    h = _norm_mod(x_ref[...], g_ref[...], sc_ref[0], sh_ref[0])
    h_ref[...] = h.astype(h_ref.dtype)
    l_ref[...] = jnp.dot(h, r_ref[...], preferred_element_type=F32, precision=lax.Precision.HIGHEST)


def prenorm(x, g, sc, sh, rows_per_group, router=None):
    m, d = x.shape
    tm = _pick_tile(rows_per_group, (512, 256, 128, 64, 32, 16, 8))
    grp = lambda i: ((i * tm) // rows_per_group, 0, 0)
    in_specs = [pl.BlockSpec((tm, d), lambda i: (i, 0)),
                pl.BlockSpec((1, d), lambda i: (0, 0)),
                pl.BlockSpec((1, 1, d), grp),
                pl.BlockSpec((1, 1, d), grp)]
    if router is None:
        return pl.pallas_call(
            _prenorm_kernel, grid=(m // tm,), in_specs=in_specs,
            out_specs=pl.BlockSpec((tm, d), lambda i: (i, 0)),
            out_shape=jax.ShapeDtypeStruct((m, d), BF16),
            compiler_params=_cparams(("parallel",)), name="prenorm",
        )(x, g.reshape(1, d), sc, sh)
    ne = router.shape[1]
    rpad = jnp.zeros((d, 128), F32).at[:, :ne].set(router)
    h, logits = pl.pallas_call(
        _prenorm_router_kernel, grid=(m // tm,),
        in_specs=in_specs + [pl.BlockSpec((d, 128), lambda i: (0, 0))],
        out_specs=[pl.BlockSpec((tm, d), lambda i: (i, 0)), pl.BlockSpec((tm, 128), lambda i: (i, 0))],
        out_shape=[jax.ShapeDtypeStruct((m, d), BF16), jax.ShapeDtypeStruct((m, 128), F32)],
        compiler_params=_cparams(("parallel",)), name="prenorm_router",
    )(x, g.reshape(1, d), sc, sh, rpad)
    return h, logits[:, :ne]


def _resnorm_kernel(x_ref, f_ref, g_ref, gt_ref, o_ref):
    f = f_ref[...]
    ms = jnp.mean(f * f, axis=-1, keepdims=True)
    o_ref[...] = x_ref[...] + gt_ref[0] * (f * lax.rsqrt(ms + NORM_EPS) * g_ref[...])


def resnorm(x, f, g, gt, rows_per_group):
    m, d = x.shape
    tm = _pick_tile(rows_per_group, (512, 256, 128, 64, 32, 16, 8))
    grp = lambda i: ((i * tm) // rows_per_group, 0, 0)
    return pl.pallas_call(
        _resnorm_kernel, grid=(m // tm,),
        in_specs=[pl.BlockSpec((tm, d), lambda i: (i, 0)),
                  pl.BlockSpec((tm, d), lambda i: (i, 0)),
                  pl.BlockSpec((1, d), lambda i: (0, 0)),
                  pl.BlockSpec((1, 1, d), grp)],
        out_specs=pl.BlockSpec((tm, d), lambda i: (i, 0)),
        out_shape=jax.ShapeDtypeStruct((m, d), F32),
        compiler_params=_cparams(("parallel",)), name="resnorm",
    )(x, f, g.reshape(1, d), gt)


LOG2E = 1.0 / math.log(2.0)


def _attn_heads(q, k, v, bias, sink_ref, o_ref):
    qs = (q * (ATT_SCALE * LOG2E)).astype(BF16)
    ones = jnp.ones((v.shape[0], HEAD_DIM), BF16)
    heads = range(ATT_HEADS)
    kh = [_hslice(k, h) for h in range(ATT_KV_HEADS)]
    v1 = [jnp.concatenate([_hslice(v, h), ones], axis=1) for h in range(ATT_KV_HEADS)]
    sink2 = [sink_ref[hd] * LOG2E for hd in heads]
    s = [lax.dot_general(_hslice(qs, hd), kh[hd // ATT_GROUP], (((1,), (1,)), ((), ())),
                         preferred_element_type=F32) for hd in heads]
    if bias is not None:
        s = [sh + bias for sh in s]
    m = [jnp.maximum(jnp.max(s[hd], axis=-1, keepdims=True), sink2[hd]) for hd in heads]
    p = [jnp.exp2(s[hd] - m[hd]).astype(BF16) for hd in heads]
    pv = [jnp.dot(p[hd], v1[hd // ATT_GROUP], preferred_element_type=F32) for hd in heads]
    outs = [pv[hd][:, :HEAD_DIM] / (pv[hd][:, HEAD_DIM:] + jnp.exp2(sink2[hd] - m[hd])) for hd in heads]
    o_ref[0] = jnp.concatenate(outs, axis=-1)


def _hslice(t, h):
    return t[:, h * HEAD_DIM:(h + 1) * HEAD_DIM]


def _win_attn_kernel(sink_ref, q_ref, qp_ref, cq_ref, sq_ref,
                     k0_ref, k1_ref, k2_ref, kp0_ref, kp1_ref, kp2_ref,
                     v0_ref, v1_ref, v2_ref, c0_ref, c1_ref, c2_ref, s0_ref, s1_ref, s2_ref,
                     kc_ref, vc_ref, o_ref, *, nb):
    j = pl.program_id(1)
    q = q_ref[0] * cq_ref[...] + qp_ref[0] * sq_ref[...]
    kb = jnp.concatenate([k0_ref[0] * c0_ref[...] + kp0_ref[0] * s0_ref[...],
                          k1_ref[0] * c1_ref[...] + kp1_ref[0] * s1_ref[...],
                          k2_ref[0] * c2_ref[...] + kp2_ref[0] * s2_ref[...]], axis=0).astype(BF16)
    vb = jnp.concatenate([v0_ref[0], v1_ref[0], v2_ref[0]], axis=0).astype(BF16)
    n_ctx = kc_ref.shape[1]
    k = jnp.concatenate([kc_ref[0].astype(BF16), kb], axis=0)
    v = jnp.concatenate([vc_ref[0].astype(BF16), vb], axis=0)
    n = n_ctx + 3 * ATT_BLOCK
    qi = lax.broadcasted_iota(jnp.int32, (ATT_BLOCK, n), 0) + ATT_BLOCK
    ki = lax.broadcasted_iota(jnp.int32, (ATT_BLOCK, n), 1) - n_ctx
    lo = jnp.where(j == 0, ATT_BLOCK, 0)
    hi = jnp.where(j == nb - 1, 2 * ATT_BLOCK, 3 * ATT_BLOCK)
    far = jnp.where(ki < lo, 4 * ATT_BLOCK, 0) + jnp.where(ki >= hi, 4 * ATT_BLOCK, 0)
    seen = jnp.where(ki < 0, 0, jnp.abs(qi - ki) + far) <= WINDOW
    bias = jnp.where(seen, 0.0, -jnp.inf).astype(F32)
    _attn_heads(q, k, v, bias, sink_ref, o_ref)


def _ctx_attn_kernel(sink_ref, q_ref, kc_ref, vc_ref, o_ref):
    _attn_heads(q_ref[0], kc_ref[0].astype(BF16), vc_ref[0].astype(BF16), None, sink_ref, o_ref)


def windowed_attention(zl3, zc3, cos_t, sin_t, sink):
    b, t, _ = zl3.shape
    l = zc3.shape[1]
    nb = t // ATT_BLOCK
    kcol, kpcol, vcol = ZK // KV_W, ZKP // KV_W, ZV // KV_W
    prev = lambda j: jnp.maximum(j - 1, 0)
    nxt = lambda j: jnp.minimum(j + 1, nb - 1)

    def zspec(width, rowf, col):
        return pl.BlockSpec((1, ATT_BLOCK, width), lambda bi, j: (bi, rowf(j), col))

    def tspec(width, rowf):
        return pl.BlockSpec((ATT_BLOCK, width), lambda bi, j: (rowf(j), 0))

    same = lambda j: j
    in_specs = [pl.BlockSpec(memory_space=pltpu.SMEM),
                zspec(Q_W, same, ZQ // Q_W), zspec(Q_W, same, ZQP // Q_W), tspec(Q_W, same), tspec(Q_W, same)]
    args = [sink, zl3, zl3, cos_t, sin_t]
    for col in (kcol, kpcol, vcol):
        in_specs += [zspec(KV_W, prev, col), zspec(KV_W, same, col), zspec(KV_W, nxt, col)]
        args += [zl3, zl3, zl3]
    for tab in (cos_t, sin_t):
        in_specs += [tspec(KV_W, prev), tspec(KV_W, same), tspec(KV_W, nxt)]
        args += [tab, tab, tab]
    in_specs += [pl.BlockSpec((1, l, KV_W), lambda bi, j: (bi, 0, kcol)),
                 pl.BlockSpec((1, l, KV_W), lambda bi, j: (bi, 0, vcol))]
    args += [zc3, zc3]
    return pl.pallas_call(
        functools.partial(_win_attn_kernel, nb=nb),
        grid=(b, nb), in_specs=in_specs,
        out_specs=pl.BlockSpec((1, ATT_BLOCK, Q_W), lambda bi, j: (bi, j, 0)),
        out_shape=jax.ShapeDtypeStruct((b, t, Q_W), F32),
        compiler_params=_cparams(("parallel", "parallel")), name="win_attn",
    )(*args)


def context_attention(zc3, sink):
    b, l, _ = zc3.shape
    return pl.pallas_call(
        _ctx_attn_kernel, grid=(b,),
        in_specs=[pl.BlockSpec(memory_space=pltpu.SMEM),
                  pl.BlockSpec((1, l, Q_W), lambda bi: (bi, 0, ZQ // Q_W)),
                  pl.BlockSpec((1, l, KV_W), lambda bi: (bi, 0, ZK // KV_W)),
                  pl.BlockSpec((1, l, KV_W), lambda bi: (bi, 0, ZV // KV_W))],
        out_specs=pl.BlockSpec((1, l, Q_W), lambda bi: (bi, 0, 0)),
        out_shape=jax.ShapeDtypeStruct((b, l, Q_W), F32),
        compiler_params=_cparams(("parallel",)), name="ctx_attn",
    )(sink, zc3, zc3, zc3)


HALO = 8


def _head_sums(x, e_ref):
    hi = x.astype(BF16)
    lo = (x - hi.astype(F32)).astype(BF16)
    return (jnp.dot(hi, e_ref[...], preferred_element_type=F32)
            + jnp.dot(lo, e_ref[...], preferred_element_type=F32))


def _shift_lerp_tile(cur_ref, prev_ref, next_ref, mu_ref, buf_ref, first, last):
    tt = cur_ref.shape[1]
    z = cur_ref[0]
    buf_ref[0:HALO] = jnp.where(first, 0.0, prev_ref[0])
    buf_ref[HALO:HALO + tt] = z
    buf_ref[HALO + tt:2 * HALO + tt] = jnp.where(last, 0.0, next_ref[0])
    nb = buf_ref[HALO - 1:HALO - 1 + tt] + buf_ref[HALO + 1:HALO + 1 + tt]
    return z + mu_ref[...] * (0.5 * nb - z)


def _split_bf16(a):
    hi = a.astype(BF16)
    return hi, (a - hi.astype(F32)).astype(BF16)


def _dot_3pass(a, b):
    a_hi, a_lo = _split_bf16(a)
    b_hi, b_lo = _split_bf16(b)
    return (jnp.dot(a_hi, b_hi, preferred_element_type=F32) + jnp.dot(a_lo, b_hi, preferred_element_type=F32)
            + jnp.dot(a_hi, b_lo, preferred_element_type=F32))


def _rwkv_prep_kernel(*refs, has_vres, n_alias):
    (wa_c, wa_p, wa_n, r_c, r_p, r_n, k_c, k_p, k_n, v_c, v_p, v_n, g_c, g_p, g_n,
     mu_wa, mu_r, mu_k, mu_v, mu_g, w0_ref, w2_ref, a0_ref, a2_ref, g2_ref, kk_ref, ka_ref, rk_ref, e_ref) = refs[:29]
    pos = 29
    if has_vres:
        vf_ref, v0_ref, v1_ref, v2_ref = refs[pos:pos + 4]
        pos += 4
    pos += n_alias
    (rk_o, vk_o, w_o, an_o, bv_o, g_o) = refs[pos:pos + 6]
    pos += 6
    if not has_vres:
        vin_o = refs[pos]
        pos += 1
    buf_wa, buf_r, buf_k, buf_v, buf_g = refs[pos:pos + 5]

    i = pl.program_id(1)
    first = i == 0
    last = i == pl.num_programs(1) - 1
    zwa = _shift_lerp_tile(wa_c, wa_p, wa_n, mu_wa, buf_wa, first, last)
    r = _shift_lerp_tile(r_c, r_p, r_n, mu_r, buf_r, first, last)
    k = _shift_lerp_tile(k_c, k_p, k_n, mu_k, buf_k, first, last)
    v = _shift_lerp_tile(v_c, v_p, v_n, mu_v, buf_v, first, last)
    zg = _shift_lerp_tile(g_c, g_p, g_n, mu_g, buf_g, first, last)

    if has_vres:
        mix = _dot_3pass(_dot_3pass(v, v1_ref[...]), v2_ref[...])
        v = v + (vf_ref[0] - v) * jax.nn.sigmoid(v0_ref[...] + mix)
    else:
        vin_o[0] = v
    kk = k * kk_ref[...]
    kk = kk / jnp.maximum(jnp.sqrt(_head_sums(kk * kk, e_ref)), 1e-12)
    tz = jnp.tanh(zwa)
    ksum = None
    for d in range(2):
        w_log = -jax.nn.softplus(-(w0_ref[d:d + 1, :] + _dot_3pass(tz, w2_ref[d]))) - 0.5
        w_o[d, 0] = (-jnp.exp(w_log)).astype(w_o.dtype)
        a = jax.nn.sigmoid(a0_ref[d:d + 1, :] + _dot_3pass(zwa, a2_ref[d]))
        an_o[d, 0] = (-a).astype(an_o.dtype)
        kd = k * (1.0 + (a - 1.0) * ka_ref[...])
        ksum = kd if ksum is None else ksum + kd
    rk_o[0, 0] = r.astype(rk_o.dtype)
    rk_o[1, 0] = k.astype(rk_o.dtype)
    vk_o[0, 0] = v.astype(vk_o.dtype)
    vk_o[1, 0] = kk.astype(vk_o.dtype)
    bv_o[0] = _head_sums(r * ksum * rk_ref[...], e_ref) * v
    g_o[0] = _dot_3pass(jax.nn.sigmoid(zg), g2_ref[...])


def _head_ones():
    idx = jnp.arange(RWKV_W) // RWKV_HEAD
    return (idx[:, None] == idx[None, :]).astype(BF16)


def rwkv_prep(z3, lp, v_first, ttot, row_off, pair_bufs):
    b, tx, _ = z3.shape
    tt = _pick_tile(math.gcd(tx, row_off) if row_off else tx, (256, 128, 64, 32, 16, 8))
    nh = tt // HALO
    nblk = tx // HALO
    has_vres = v_first is not None
    lanes = 128

    def trio(width, col):
        return [pl.BlockSpec((1, tt, width), lambda bi, i: (bi, i, col)),
                pl.BlockSpec((1, HALO, width), lambda bi, i: (bi, jnp.maximum(i * nh - 1, 0), col)),
                pl.BlockSpec((1, HALO, width), lambda bi, i: (bi, jnp.minimum((i + 1) * nh, nblk - 1), col))]

    def full(shape):
        return pl.BlockSpec(shape, lambda bi, i: (0,) * len(shape))

    mu = lp['mu']
    o = 3 * RWKV_W
    nwa = 2 * DECAY_RANK + 2 * ICLR_RANK
    mu_g = jnp.zeros((lanes,), F32).at[:GATE_RANK].set(mu[o + nwa:])
    w2f = jnp.zeros((2, lanes, RWKV_W), F32)
    a2f = jnp.zeros((2, lanes, RWKV_W), F32)
    for d in range(2):
        w2f = w2f.at[d, d * DECAY_RANK:(d + 1) * DECAY_RANK].set(lp['w2'][d])
        a2f = a2f.at[d, 2 * DECAY_RANK + d * ICLR_RANK:2 * DECAY_RANK + (d + 1) * ICLR_RANK].set(lp['a2'][d])
    g2f = jnp.zeros((lanes, RWKV_W), F32).at[:GATE_RANK].set(lp['g2'])
    row = lambda a: a.reshape(1, -1)
    in_specs = (trio(lanes, ZWA // lanes) + trio(RWKV_W, ZR // RWKV_W) + trio(RWKV_W, ZRK // RWKV_W)
                + trio(RWKV_W, ZRV // RWKV_W) + trio(lanes, ZZG // lanes)
                + [full((1, lanes)), full((1, RWKV_W)), full((1, RWKV_W)), full((1, RWKV_W)), full((1, lanes)),
                   full((2, RWKV_W)), full((2, lanes, RWKV_W)), full((2, RWKV_W)), full((2, lanes, RWKV_W)),
                   full((lanes, RWKV_W)), full((1, RWKV_W)), full((1, RWKV_W)), full((1, RWKV_W)),
                   full((RWKV_W, RWKV_W))])
    args = [z3] * 15 + [row(mu[o:o + nwa]), row(mu[:RWKV_W]), row(mu[RWKV_W:2 * RWKV_W]), row(mu[2 * RWKV_W:o]),
                        row(mu_g), lp['w0'], w2f, lp['a0'], a2f, g2f, row(lp['k_k']), row(lp['k_a']),
                        row(lp['r_k']), _head_ones()]
    tok = pl.BlockSpec((1, tt, RWKV_W), lambda bi, i: (bi, i, 0))
    if has_vres:
        rank = lp['v1'].shape[1]
        v1f = jnp.zeros((RWKV_W, lanes), F32).at[:, :rank].set(lp['v1'])
        v2f = jnp.zeros((lanes, RWKV_W), F32).at[:rank].set(lp['v2'])
        in_specs += [tok, full((1, RWKV_W)), full((RWKV_W, lanes)), full((lanes, RWKV_W))]
        args += [v_first, row(lp['v0']), v1f, v2f]
    n_alias = 0 if pair_bufs is None else len(pair_bufs)
    aliases = {}
    if pair_bufs is not None:
        aliases = {len(args) + n: n for n in range(n_alias)}
        in_specs += [pl.BlockSpec(memory_space=pl.ANY)] * n_alias
        args += list(pair_bufs)
    blk_off = row_off // tt
    pair = pl.BlockSpec((2, 1, tt, RWKV_W), lambda bi, i: (0, bi, i + blk_off, 0))
    pair_shape = jax.ShapeDtypeStruct((2, b, ttot, RWKV_W), BF16)
    tok_shape = jax.ShapeDtypeStruct((b, tx, RWKV_W), F32)
    n_tok = 2 if has_vres else 3
    outs = pl.pallas_call(
        functools.partial(_rwkv_prep_kernel, has_vres=has_vres, n_alias=n_alias),
        grid=(b, tx // tt), in_specs=in_specs,
        out_specs=[pair] * 4 + [tok] * n_tok,
        out_shape=[pair_shape] * 4 + [tok_shape] * n_tok,
        input_output_aliases=aliases,
        scratch_shapes=[pltpu.VMEM((tt + 2 * HALO, lanes), F32)] + [pltpu.VMEM((tt + 2 * HALO, RWKV_W), F32)] * 3
                       + [pltpu.VMEM((tt + 2 * HALO, lanes), F32)],
        compiler_params=_cparams(("parallel", "parallel")), name="rwkv_prep",
    )(*args)
    res = {'pairs': tuple(outs[:4]), 'bv': outs[4], 'g': outs[5]}
    if not has_vres:
        res['v_in'] = outs[6]
    return res


CONV_HALO = 16
CONV_ROWS = 32


def _conv_kernel(a_c, a_p, a_n, b_c, b_p, b_n, w_ref, bias_ref, lg_ref, lb_ref, o_ref, buf_ref, win_ref):
    i = pl.program_id(1)
    first = i == 0
    last = i == pl.num_programs(1) - 1
    tt = a_c.shape[1]
    glu = lambda a, b: a * jax.nn.sigmoid(b)
    buf_ref[0:CONV_HALO] = jnp.where(first, 0.0, glu(a_p[0], b_p[0]))
    buf_ref[CONV_HALO:CONV_HALO + tt] = glu(a_c[0], b_c[0])
    buf_ref[CONV_HALO + tt:2 * CONV_HALO + tt] = jnp.where(last, 0.0, glu(a_n[0], b_n[0]))
    half = CONV_K // 2
    span = CONV_ROWS + ((CONV_K - 1) // SUBLANES) * SUBLANES
    for r0 in range(0, tt, CONV_ROWS):
        base = CONV_HALO - half + r0
        acc = bias_ref[...]
        for ph in range(SUBLANES):
            win_ref[...] = buf_ref[base + ph:base + ph + span]
            for j in range(ph, CONV_K, SUBLANES):
                acc = acc + win_ref[j - ph:j - ph + CONV_ROWS] * w_ref[j:j + 1, :]
        m = jnp.mean(acc, axis=-1, keepdims=True)
        cen = acc - m
        var = jnp.mean(cen * cen, axis=-1, keepdims=True)
        y = cen * lax.rsqrt(var + CONV_LN_EPS) * lg_ref[...] + lb_ref[...]
        o_ref[0, r0:r0 + CONV_ROWS, :] = y * jax.nn.sigmoid(y)


def conv_module(z3, lp):
    b, tx, _ = z3.shape
    tt = _pick_tile(tx, (256, 128, 64, 32))
    nh = tt // CONV_HALO
    nblk = tx // CONV_HALO

    def trio(col):
        return [pl.BlockSpec((1, tt, CONV_W), lambda bi, i: (bi, i, col)),
                pl.BlockSpec((1, CONV_HALO, CONV_W), lambda bi, i: (bi, jnp.maximum(i * nh - 1, 0), col)),
                pl.BlockSpec((1, CONV_HALO, CONV_W), lambda bi, i: (bi, jnp.minimum((i + 1) * nh, nblk - 1), col))]

    kpad = -(-CONV_K // SUBLANES) * SUBLANES
    wpad = jnp.zeros((kpad, CONV_W), F32).at[:CONV_K].set(lp['dw_w'])
    vec = pl.BlockSpec((1, CONV_W), lambda bi, i: (0, 0))
    return pl.pallas_call(
        _conv_kernel, grid=(b, tx // tt),
        in_specs=trio(ZCA // CONV_W) + trio(ZCB // CONV_W)
                 + [pl.BlockSpec((kpad, CONV_W), lambda bi, i: (0, 0)), vec, vec, vec],
        out_specs=pl.BlockSpec((1, tt, CONV_W), lambda bi, i: (bi, i, 0)),
        out_shape=jax.ShapeDtypeStruct((b, tx, CONV_W), F32),
        scratch_shapes=[pltpu.VMEM((tt + 2 * CONV_HALO, CONV_W), F32),
                        pltpu.VMEM((CONV_ROWS + ((CONV_K - 1) // SUBLANES) * SUBLANES, CONV_W), F32)],
        compiler_params=_cparams(("parallel", "parallel")), name="conv_module",
    )(z3, z3, z3, z3, z3, z3, wpad, lp['dw_b'].reshape(1, -1), lp['cln_g'].reshape(1, -1), lp['cln_b'].reshape(1, -1))


SUBLANES = 8
DECAY_FOLD = 16


def _wkv_scan_kernel(rk_f, vk_f, w_f, an_f, rk_b, vk_b, w_b, an_b, ka_ref, yf_ref, yb_ref, s_ref, *, tt):
    @pl.when(pl.program_id(0) == 0)
    def _():
        s_ref[...] = jnp.zeros_like(s_ref)

    nk, _, lanes = s_ref.shape
    half = lanes // 2
    fwd = lax.broadcasted_iota(jnp.int32, (nk, lanes), 1) < half
    ka = ka_ref[...]
    fold = math.gcd(tt, DECAY_FOLD)

    def group(gi, carry):
        def step(ti, gam):
            t = gi * fold + ti
            tb = tt - 1 - t
            rkf, rkb, vkf, vkb = (a.astype(F32) for a in (rk_f[t], rk_b[tb], vk_f[t], vk_b[tb]))
            r = jnp.where(fwd, rkf, pltpu.roll(rkb, half, 1))
            k = jnp.where(fwd, pltpu.roll(rkf, half, 1), rkb)
            v = jnp.where(fwd, vkf, pltpu.roll(vkb, half, 1))
            kk = jnp.where(fwd, pltpu.roll(vkf, half, 1), vkb)
            w = jnp.exp(jnp.where(fwd, w_f[t].astype(F32), w_b[tb].astype(F32)))
            an = jnp.where(fwd, an_f[t].astype(F32), an_b[tb].astype(F32))
            gam_new = gam * w
            inv = 1.0 / gam_new
            kks = kk * gam
            kds = k * (1.0 - (an + 1.0) * ka) * inv
            bbs = kk * an * inv
            rs = r * gam_new
            row = lambda a, ki: a[ki:ki + 1, :]
            sa = [s_ref[0] * row(kks, 0), s_ref[1] * row(kks, 1)]
            for ki in range(2, nk):
                sa[ki % 2] = sa[ki % 2] + s_ref[ki] * row(kks, ki)
            sa = sa[0] + sa[1]
            ys = [None, None]
            for ki in range(nk):
                s_new = s_ref[ki] + sa * row(bbs, ki) + v * row(kds, ki)
                s_ref[ki] = s_new
                contrib = s_new * row(rs, ki)
                ys[ki % 2] = contrib if ys[ki % 2] is None else ys[ki % 2] + contrib
            y = ys[0] + ys[1]
            yf_ref[t] = y
            yb_ref[tb] = y
            return gam_new

        gam = lax.fori_loop(0, fold, step, jnp.ones((nk, lanes), F32), unroll=4)
        for ki in range(nk):
            s_ref[ki] = s_ref[ki] * gam[ki:ki + 1, :]
        return carry

    lax.fori_loop(0, tt // fold, group, 0)


def wkv_scan(rk, vk, w, an, ka, n_ctx_steps):
    ttot, n, ch = rk.shape
    tt = _pick_tile(math.gcd(n_ctx_steps, ttot - n_ctx_steps), (32, 16, 8, 4, 2, 1))
    nt, nc = ttot // tt, n_ctx_steps // tt
    fspec = pl.BlockSpec((tt, n, ch), lambda i: (i, 0, 0))
    bspec = pl.BlockSpec((tt, n, ch), lambda i: (jnp.where(i < nc, nc - 1 - i, nt - 1 - (i - nc)), 0, 0))
    return pl.pallas_call(
        functools.partial(_wkv_scan_kernel, tt=tt),
        grid=(nt,), in_specs=[fspec] * 4 + [bspec] * 4 + [pl.BlockSpec((n, ch), lambda i: (0, 0))],
        out_specs=[fspec, bspec],
        out_shape=[jax.ShapeDtypeStruct((ttot, n, ch), F32)] * 2,
        scratch_shapes=[pltpu.VMEM((n, n, ch), F32)],
        compiler_params=_cparams(("arbitrary",)), name="wkv_scan",
    )(rk, vk, w, an, rk, vk, w, an, ka)


def _merge_kernel(x_ref, att_ref, y_ref, bv_ref, rg_ref, cv_ref, g0_ref, g1_ref, g2_ref,
                  lg_ref, lb_ref, e_ref, wa_ref, wr_ref, wc_ref, wo_ref, ng_ref, gt_ref, o_ref):
    def proj(a, w_ref):
        return jnp.dot(a.astype(BF16), w_ref[...], preferred_element_type=F32)

    y = y_ref[...]
    cen = y - _head_sums(y, e_ref) * (1.0 / RWKV_HEAD)
    var = _head_sums(cen * cen, e_ref) * (1.0 / RWKV_HEAD)
    rw = (cen * lax.rsqrt(var + LNX_EPS) * lg_ref[...] + lb_ref[...] + bv_ref[...]) * rg_ref[...]
    m = (jax.nn.sigmoid(g0_ref[...]) * proj(att_ref[...], wa_ref)
         + jax.nn.sigmoid(g1_ref[...]) * proj(rw, wr_ref)
         + jax.nn.sigmoid(g2_ref[...]) * proj(cv_ref[...], wc_ref))
    ml = proj(m, wo_ref)
    ms = jnp.mean(ml * ml, axis=-1, keepdims=True)
    o_ref[...] = x_ref[...] + gt_ref[0] * (ml * lax.rsqrt(ms + NORM_EPS) * ng_ref[...])


def merge_out(x, att, y, bv, rg, cv, z, lnx_g, lnx_b, wa, wr, wc, wo, ng, gt, rows_per_group):
    m, d = x.shape
    tm = _pick_tile(rows_per_group, (512, 256, 128, 64, 32, 16, 8))
    row = lambda i: (i, 0)
    const = lambda i: (0, 0)
    gcol = ZG // d
    br = pl.BlockSpec((tm, RWKV_W), row)
    return pl.pallas_call(
        _merge_kernel, grid=(m // tm,),
        in_specs=[pl.BlockSpec((tm, d), row),
                  pl.BlockSpec((tm, Q_W), row), br, br, br, pl.BlockSpec((tm, CONV_W), row),
                  pl.BlockSpec((tm, d), lambda i: (i, gcol)),
                  pl.BlockSpec((tm, d), lambda i: (i, gcol + 1)),
                  pl.BlockSpec((tm, d), lambda i: (i, gcol + 2)),
                  pl.BlockSpec((1, RWKV_W), const), pl.BlockSpec((1, RWKV_W), const),
                  pl.BlockSpec((RWKV_W, RWKV_W), const),
                  pl.BlockSpec((Q_W, d), const), pl.BlockSpec((RWKV_W, d), const), pl.BlockSpec((CONV_W, d), const),
                  pl.BlockSpec((d, d), const), pl.BlockSpec((1, d), const),
                  pl.BlockSpec((1, 1, d), lambda i: ((i * tm) // rows_per_group, 0, 0))],
        out_specs=pl.BlockSpec((tm, d), row),
        out_shape=jax.ShapeDtypeStruct((m, d), F32),
        compiler_params=_cparams(("parallel",)), name="merge_out",
    )(x, att, y, bv, rg, cv, z, z, z, lnx_g.reshape(1, -1), lnx_b.reshape(1, -1), _head_ones(),
      wa, wr, wc, wo, ng.reshape(1, d), gt)


def _swiglu_kernel(te_ref, nu_ref, h_ref, wg_ref, wu_ref, wd_ref, o_ref):
    f = pl.program_id(1)

    used = pl.program_id(0) < nu_ref[0]

    @pl.when(jnp.logical_and(used, f == 0))
    def _():
        o_ref[...] = jnp.zeros_like(o_ref)

    @pl.when(used)
    def _():
        h = h_ref[...]
        hg = jnp.dot(h, wg_ref[0].astype(BF16), preferred_element_type=F32)
        hu = jnp.dot(h, wu_ref[0].astype(BF16), preferred_element_type=F32)
        act = (hg * jax.nn.sigmoid(hg) * hu).astype(BF16)
        o_ref[...] += jnp.dot(act, wd_ref[0].astype(BF16), preferred_element_type=F32)


def grouped_swiglu(h, tile_expert, n_used, wg, wu, wd, tm):
    p, d = h.shape
    ff = wg.shape[2]
    tf = _pick_tile(ff, (512, 1408, 256, 128))
    nf = ff // tf
    ti = lambda i, nu: jnp.minimum(i, nu[0] - 1)
    fi = lambda i, f, nu: jnp.where(i < nu[0], f, nf - 1)
    grid_spec = pltpu.PrefetchScalarGridSpec(
        num_scalar_prefetch=2, grid=(p // tm, nf),
        in_specs=[pl.BlockSpec((tm, d), lambda i, f, te, nu: (ti(i, nu), 0)),
                  pl.BlockSpec((1, d, tf), lambda i, f, te, nu: (te[ti(i, nu)], 0, fi(i, f, nu))),
                  pl.BlockSpec((1, d, tf), lambda i, f, te, nu: (te[ti(i, nu)], 0, fi(i, f, nu))),
                  pl.BlockSpec((1, tf, d), lambda i, f, te, nu: (te[ti(i, nu)], fi(i, f, nu), 0))],
        out_specs=pl.BlockSpec((tm, d), lambda i, f, te, nu: (ti(i, nu), 0)))
    return pl.pallas_call(
        _swiglu_kernel, grid_spec=grid_spec,
        out_shape=jax.ShapeDtypeStruct((p, d), F32),
        compiler_params=_cparams(("arbitrary", "arbitrary")), name="swiglu",
    )(tile_expert, n_used, h, wg, wu, wd)


def _dense_ffn_kernel(x_ref, g_ref, sc_ref, sh_ref, wg_ref, wu_ref, wd_ref, g2_ref, gt_ref, o_ref, h_ref, acc_ref):
    f = pl.program_id(1)

    @pl.when(f == 0)
    def _():
        h_ref[...] = _norm_mod(x_ref[...], g_ref[...], sc_ref[0], sh_ref[0]).astype(BF16)
        acc_ref[...] = jnp.zeros_like(acc_ref)

    h = h_ref[...]
    hg = jnp.dot(h, wg_ref[...], preferred_element_type=F32)
    hu = jnp.dot(h, wu_ref[...], preferred_element_type=F32)
    act = (hg * jax.nn.sigmoid(hg) * hu).astype(BF16)
    acc_ref[...] += jnp.dot(act, wd_ref[...], preferred_element_type=F32)

    @pl.when(f == pl.num_programs(1) - 1)
    def _():
        y = acc_ref[...]
        ms = jnp.mean(y * y, axis=-1, keepdims=True)
        o_ref[...] = x_ref[...] + gt_ref[0] * (y * lax.rsqrt(ms + NORM_EPS) * g2_ref[...])


def dense_ffn(x, g_in, sc, sh, wg, wu, wd, g_out, gt, rows_per_group):
    m, d = x.shape
    ff = wg.shape[1]
    tm = _pick_tile(rows_per_group, (512, 256, 128, 64, 32, 16, 8))
    tf = _pick_tile(ff, (1408, 512, 256, 128))
    grp = lambda i, f: ((i * tm) // rows_per_group, 0, 0)
    vec = pl.BlockSpec((1, d), lambda i, f: (0, 0))
    return pl.pallas_call(
        _dense_ffn_kernel, grid=(m // tm, ff // tf),
        in_specs=[pl.BlockSpec((tm, d), lambda i, f: (i, 0)), vec,
                  pl.BlockSpec((1, 1, d), grp), pl.BlockSpec((1, 1, d), grp),
                  pl.BlockSpec((d, tf), lambda i, f: (0, f)), pl.BlockSpec((d, tf), lambda i, f: (0, f)),
                  pl.BlockSpec((tf, d), lambda i, f: (f, 0)), vec, pl.BlockSpec((1, 1, d), grp)],
        out_specs=pl.BlockSpec((tm, d), lambda i, f: (i, 0)),
        out_shape=jax.ShapeDtypeStruct((m, d), F32),
        scratch_shapes=[pltpu.VMEM((tm, d), BF16), pltpu.VMEM((tm, d), F32)],
        compiler_params=_cparams(("parallel", "arbitrary")), name="dense_ffn",
    )(x, g_in.reshape(1, d), sc, sh, wg, wu, wd, g_out.reshape(1, d), gt)


def _rope_tables(t):
    rows = t // GRID_W
    row = jnp.broadcast_to(jnp.arange(rows)[:, None], (rows, GRID_W)).reshape(-1).astype(F32)
    col = jnp.broadcast_to(jnp.arange(GRID_W)[None, :], (rows, GRID_W)).reshape(-1).astype(F32)
    inv = ROPE_BASE ** (-jnp.arange(0, AXIS_DIM, 2, dtype=F32) / AXIS_DIM)
    ar = row[:, None] * inv[None, :]
    ac = col[:, None] * inv[None, :]
    cos64 = jnp.concatenate([jnp.cos(ar), jnp.cos(ar), jnp.cos(ac), jnp.cos(ac)], axis=-1)
    sin64 = jnp.concatenate([jnp.sin(ar), jnp.sin(ar), jnp.sin(ac), jnp.sin(ac)], axis=-1)
    return jnp.tile(cos64, (1, ATT_HEADS)), jnp.tile(sin64, (1, ATT_HEADS))


def _rope_partner(w):
    d = w.shape[0]
    q = AXIS_DIM // 2
    wh = w.reshape(d, -1, 4, q)
    return jnp.stack([-wh[:, :, 1], wh[:, :, 0], -wh[:, :, 3], wh[:, :, 2]], axis=2).reshape(w.shape)


def _build_w_ext(w_in):
    d = w_in.shape[0]
    wq = w_in[:, OFF_Q:OFF_K]
    wk = w_in[:, OFF_K:OFF_V]
    o = OFF_R + 3 * RWKV_W
    parts = [(ZQ, wq), (ZQP, _rope_partner(wq)), (ZK, wk), (ZKP, _rope_partner(wk)),
             (ZV, w_in[:, OFF_V:OFF_R]),
             (ZWA, w_in[:, o:o + 2 * DECAY_RANK + 2 * ICLR_RANK]),
             (ZCA, w_in[:, OFF_CONV:OFF_CONV + CONV_W]), (ZCB, w_in[:, OFF_CONV + CONV_W:OFF_GATE]),
             (ZR, w_in[:, OFF_R:OFF_R + RWKV_W]), (ZRK, w_in[:, OFF_R + RWKV_W:OFF_R + 2 * RWKV_W]),
             (ZRV, w_in[:, OFF_R + 2 * RWKV_W:OFF_R + 3 * RWKV_W]),
             (ZG, w_in[:, OFF_GATE:IN_COLS]),
             (ZZG, w_in[:, OFF_CONV - GATE_RANK:OFF_CONV])]
    parts.sort(key=lambda p: p[0])
    cols, pos = [], 0
    for off, wpart in parts:
        if off > pos:
            cols.append(jnp.zeros((d, off - pos), w_in.dtype))
        cols.append(wpart)
        pos = off + wpart.shape[1]
    if pos < NZ:
        cols.append(jnp.zeros((d, NZ - pos), w_in.dtype))
    return jnp.concatenate(cols, axis=1).astype(BF16)


def _chain_layout(pair):
    _, bsz, ttot, _ = pair.shape
    both = pair.reshape(2, bsz, ttot, RWKV_HEADS, RWKV_HEAD)
    return jnp.transpose(both, (2, 4, 0, 1, 3)).reshape(ttot, RWKV_HEAD, 2 * bsz * RWKV_HEADS)


def _bidirectional_wkv(pairs, k_a, l):
    bsz = pairs[0].shape[1]
    rk, vk, w, an = (_chain_layout(p) for p in pairs)
    ka = jnp.tile(k_a.reshape(RWKV_HEADS, RWKV_HEAD).T, (1, 2 * bsz))
    yf, yb = wkv_scan(rk, vk, w, an, ka, l)
    half = bsz * RWKV_HEADS
    y = (yf[:, :, :half] + yb[:, :, half:]).reshape(-1, RWKV_HEAD, bsz, RWKV_HEADS)
    y = jnp.transpose(y, (2, 0, 3, 1)).reshape(bsz, -1, RWKV_W)
    return y[:, l:], y[:, :l]


def _moe_dispatch(logits, tm):
    m = logits.shape[0]
    top_v, top_i = lax.top_k(logits, TOP_K)
    wts = jax.nn.softmax(top_v, axis=-1)
    e_flat = top_i.reshape(-1)
    n_assign = e_flat.shape[0]
    onehot = (e_flat[:, None] == jnp.arange(N_EXPERTS)[None, :]).astype(jnp.int32)
    csum = jnp.cumsum(onehot, axis=0)
    rank = jnp.sum(onehot * csum, axis=1) - 1
    cnt = csum[-1]
    pcnt = ((cnt + tm - 1) // tm) * tm
    pend = jnp.cumsum(pcnt)
    pstart = pend - pcnt
    dest = pstart[e_flat] + rank
    p_max = n_assign + N_EXPERTS * tm
    row_token = jnp.zeros((p_max,), jnp.int32).at[dest].set(jnp.arange(n_assign, dtype=jnp.int32) // TOP_K)
    tile_start = jnp.arange(p_max // tm, dtype=jnp.int32) * tm
    tile_expert = jnp.minimum(jnp.searchsorted(pend, tile_start, side='right'), N_EXPERTS - 1).astype(jnp.int32)
    n_used = (pend[-1:] // tm).astype(jnp.int32)
    return row_token, tile_expert, n_used, dest.reshape(m, TOP_K), wts


def kernel(x, c, ctx, c_ctx, mod_w, mod_b, norm_g, w_in, attn_sink, rwkv_mu, rwkv_w0, rwkv_w2, rwkv_a0, rwkv_a2, rwkv_g2, rwkv_k_k, rwkv_k_a, rwkv_r_k, rwkv_lnx_g, rwkv_lnx_b, vres_v0, vres_w1, vres_w2, conv_dw_w, conv_dw_b, conv_ln_g, conv_ln_b, w_attn_o, w_rwkv_o, w_conv_o, w_out, ffn_wg, ffn_wu, ffn_wd, moe_router, moe_wg, moe_wu, moe_wd):
    bsz, t, d = x.shape
    l = ctx.shape[1]
    depth = mod_w.shape[0]
    n_lat, n_ctx = bsz * t, bsz * l
    cos_t, sin_t = _rope_tables(t)
    xl = x.reshape(n_lat, d)
    xc = ctx.reshape(n_ctx, d)
    n_rows = -(-(bsz + 1) // SUBLANES) * SUBLANES
    cond = jnp.zeros((n_rows, d), F32).at[:bsz].set(jax.nn.silu(c)).at[bsz].set(jax.nn.silu(c_ctx))
    v_first = None
    for i in range(depth):
        need_ctx = i < depth - 1
        mod = matmul_bias_f32(cond, mod_w[i], mod_b[i])
        mod_l = [mod[:bsz, k * d:(k + 1) * d].reshape(bsz, 1, d) for k in range(6)]
        mod_c = [mod[bsz:bsz + 1, k * d:(k + 1) * d].reshape(1, 1, d) for k in range(6)]
        lp = {'mu': rwkv_mu[i], 'w0': rwkv_w0[i], 'w2': rwkv_w2[i], 'a0': rwkv_a0[i], 'a2': rwkv_a2[i],
              'g2': rwkv_g2[i], 'k_k': rwkv_k_k[i], 'k_a': rwkv_k_a[i], 'r_k': rwkv_r_k[i],
              'lnx_g': rwkv_lnx_g[i], 'lnx_b': rwkv_lnx_b[i],
              'dw_w': conv_dw_w[i], 'dw_b': conv_dw_b[i], 'cln_g': conv_ln_g[i], 'cln_b': conv_ln_b[i]}
        if i > 0:
            lp['v0'], lp['v1'], lp['v2'] = vres_v0[i - 1], vres_w1[i - 1], vres_w2[i - 1]
        w_ext = _build_w_ext(w_in[i])
        wa, wr, wc, wo = (w.astype(BF16) for w in (w_attn_o[i], w_rwkv_o[i], w_conv_o[i], w_out[i]))

        zl = nm_matmul(xl, norm_g[i, 0], mod_l[1], mod_l[0], w_ext, t)
        zc = nm_matmul(xc, norm_g[i, 0], mod_c[1], mod_c[0], w_ext, n_ctx)
        zl3 = zl.reshape(bsz, t, NZ)
        zc3 = zc.reshape(bsz, l, NZ)
        att_l = windowed_attention(zl3, zc3, cos_t, sin_t, attn_sink[i])
        p_c = rwkv_prep(zc3, lp, None if v_first is None else v_first[1], l + t, 0, None)
        p_l = rwkv_prep(zl3, lp, None if v_first is None else v_first[0], l + t, l, p_c['pairs'])
        if i == 0:
            v_first = (p_l['v_in'], p_c['v_in'])
        y_l, y_c = _bidirectional_wkv(p_l['pairs'], lp['k_a'], l)
        cv_l = conv_module(zl3, lp)
        flat = lambda a: a.reshape(-1, a.shape[-1])
        xl = merge_out(xl, flat(att_l), flat(y_l), flat(p_l['bv']), flat(p_l['g']), flat(cv_l), zl,
                       lp['lnx_g'], lp['lnx_b'], wa, wr, wc, wo, norm_g[i, 1], mod_l[2], t)
        if need_ctx:
            att_c = context_attention(zc3, attn_sink[i])
            cv_c = conv_module(zc3, lp)
            xc = merge_out(xc, flat(att_c), flat(y_c), flat(p_c['bv']), flat(p_c['g']), flat(cv_c), zc,
                           lp['lnx_g'], lp['lnx_b'], wa, wr, wc, wo, norm_g[i, 1], mod_c[2], n_ctx)

        j = i // 2
        if i % 2 == 0:
            wg, wu, wd = (w[j].astype(BF16) for w in (ffn_wg, ffn_wu, ffn_wd))
            xl = dense_ffn(xl, norm_g[i, 2], mod_l[4], mod_l[3], wg, wu, wd, norm_g[i, 3], mod_l[5], t)
            if need_ctx:
                xc = dense_ffn(xc, norm_g[i, 2], mod_c[4], mod_c[3], wg, wu, wd, norm_g[i, 3], mod_c[5], n_ctx)
        else:
            wg, wu, wd = moe_wg[j], moe_wu[j], moe_wd[j]
            streams = [(xl, mod_l, t)] + ([(xc, mod_c, n_ctx)] if need_ctx else [])
            outs = []
            for xs, ms, rpg in streams:
                n_tok = xs.shape[0]
                h, logits = prenorm(xs, norm_g[i, 2], ms[4], ms[3], rpg, router=moe_router[j])
                n_grp = MOE_GROUPS if n_tok % (MOE_GROUPS * SUBLANES) == 0 else 1
                gsz = n_tok // n_grp
                tm = _pick_tile(gsz, (1024, 512, 256, 128, 64, 32, 16, 8))
                fs = []
                for gi in range(n_grp):
                    row_token, tile_expert, n_used, dest, wts = _moe_dispatch(logits[gi * gsz:(gi + 1) * gsz], tm)
                    yg = grouped_swiglu(jnp.take(h, row_token + gi * gsz, axis=0), tile_expert, n_used,
                                        wg, wu, wd, tm)
                    fs.append(wts[:, 0:1] * jnp.take(yg, dest[:, 0], axis=0)
                              + wts[:, 1:2] * jnp.take(yg, dest[:, 1], axis=0))
                f = fs[0] if n_grp == 1 else jnp.concatenate(fs, axis=0)
                outs.append(resnorm(xs, f, norm_g[i, 3], ms[5], rpg))
            xl = outs[0]
            if need_ctx:
                xc = outs[1]
    return xl.reshape(bsz, t, d)
```

```python
import functools
import math

import jax
import jax.numpy as jnp
from jax import lax
from jax.experimental import pallas as pl
from jax.experimental.pallas import tpu as pltpu

F32 = jnp.float32
BF16 = jnp.bfloat16

D_MODEL = 1024
GRID_W = 64
ATT_HEADS = 8
ATT_KV_HEADS = 2
ATT_GROUP = ATT_HEADS // ATT_KV_HEADS
HEAD_DIM = 64
WINDOW = 128
ATT_BLOCK = 128
ATT_SCALE = 1.0 / math.sqrt(HEAD_DIM)
ROPE_BASE = 10000.0
AXIS_DIM = HEAD_DIM // 2
RWKV_HEADS = 8
RWKV_HEAD = 64
RWKV_W = RWKV_HEADS * RWKV_HEAD
DECAY_RANK = 32
ICLR_RANK = 32
GATE_RANK = 96
LNX_EPS = 64e-5
CONV_W = 512
CONV_K = 31
CONV_LN_EPS = 1e-5
N_BRANCH = 3
N_EXPERTS = 8
TOP_K = 2
NORM_EPS = 1e-6
Q_W = ATT_HEADS * HEAD_DIM
KV_W = ATT_KV_HEADS * HEAD_DIM
OFF_Q = 0
OFF_K = OFF_Q + Q_W
OFF_V = OFF_K + KV_W
OFF_R = OFF_V + KV_W
OFF_CONV = OFF_R + 3 * RWKV_W + 2 * DECAY_RANK + 2 * ICLR_RANK + GATE_RANK
OFF_GATE = OFF_CONV + 2 * CONV_W
IN_COLS = OFF_GATE + N_BRANCH * D_MODEL

ZQ = 0
ZQP = 512
ZK = 1024
ZKP = 1152
ZV = 1280
ZWA = 1408
ZCA = 1536
ZCB = 2048
ZR = 2560
ZRK = 3072
ZRV = 3584
ZG = 4096
ZZG = 7168
NZ = 7680

VMEM_LIMIT = 56 * 1024 * 1024
MOE_GROUPS = 2


def _cparams(sem):
    return pltpu.CompilerParams(dimension_semantics=sem, vmem_limit_bytes=VMEM_LIMIT)


def _pick_tile(n, candidates):
    for c in candidates:
        if n % c == 0:
            return c
    return n


def _matmul_bias_kernel(a_ref, w_ref, b_ref, o_ref):
    o_ref[...] = jnp.dot(a_ref[...], w_ref[...], preferred_element_type=F32,
                         precision=lax.Precision.HIGHEST) + b_ref[...]


def matmul_bias_f32(a, w, b):
    m, k = a.shape
    n = w.shape[1]
    tn = _pick_tile(n, (1536, 1024, 512, 256, 128))
    return pl.pallas_call(
        _matmul_bias_kernel,
        grid=(n // tn,),
        in_specs=[pl.BlockSpec((m, k), lambda j: (0, 0)),
                  pl.BlockSpec((k, tn), lambda j: (0, j)),
                  pl.BlockSpec((1, tn), lambda j: (0, j))],
        out_specs=pl.BlockSpec((m, tn), lambda j: (0, j)),
        out_shape=jax.ShapeDtypeStruct((m, n), F32),
        compiler_params=_cparams(("parallel",)),
        name="mod_matmul",
    )(a, w, b.reshape(1, n))


def _norm_mod(x, g, sc, sh):
    ms = jnp.mean(x * x, axis=-1, keepdims=True)
    return (x * lax.rsqrt(ms + NORM_EPS) * g) * (1.0 + sc) + sh


def _nm_matmul_kernel(x_ref, g_ref, sc_ref, sh_ref, w_ref, o_ref, h_ref):
    @pl.when(pl.program_id(1) == 0)
    def _():
        h_ref[...] = _norm_mod(x_ref[...], g_ref[...], sc_ref[0], sh_ref[0]).astype(BF16)

    o_ref[...] = jnp.dot(h_ref[...], w_ref[...], preferred_element_type=F32).astype(o_ref.dtype)


def nm_matmul(x, g, sc, sh, w, rows_per_group):
    m, d = x.shape
    n = w.shape[1]
    tm = _pick_tile(rows_per_group, (1024, 512, 256, 128, 64, 32, 16, 8))
    tn = _pick_tile(n, (2560, 1280, 1024, 512, 256, 128))
    grp = lambda i, j: ((i * tm) // rows_per_group, 0, 0)
    return pl.pallas_call(
        _nm_matmul_kernel,
        grid=(m // tm, n // tn),
        in_specs=[pl.BlockSpec((tm, d), lambda i, j: (i, 0)),
                  pl.BlockSpec((1, d), lambda i, j: (0, 0)),
                  pl.BlockSpec((1, 1, d), grp),
                  pl.BlockSpec((1, 1, d), grp),
                  pl.BlockSpec((d, tn), lambda i, j: (0, j))],
        out_specs=pl.BlockSpec((tm, tn), lambda i, j: (i, j)),
        out_shape=jax.ShapeDtypeStruct((m, n), F32),
        scratch_shapes=[pltpu.VMEM((tm, d), BF16)],
        compiler_params=_cparams(("parallel", "arbitrary")),
        name="in_proj",
    )(x, g.reshape(1, d), sc, sh, w)


def _prenorm_kernel(x_ref, g_ref, sc_ref, sh_ref, h_ref):
    h_ref[...] = _norm_mod(x_ref[...], g_ref[...], sc_ref[0], sh_ref[0]).astype(h_ref.dtype)


def _prenorm_router_kernel(x_ref, g_ref, sc_ref, sh_ref, r_ref, h_ref, l_ref):
    h = _norm_mod(x_ref[...], g_ref[...], sc_ref[0], sh_ref[0])
    h_ref[...] = h.astype(h_ref.dtype)
    l_ref[...] = jnp.dot(h, r_ref[...], preferred_element_type=F32, precision=lax.Precision.HIGHEST)


def prenorm(x, g, sc, sh, rows_per_group, router=None):
    m, d = x.shape
    tm = _pick_tile(rows_per_group, (512, 256, 128, 64, 32, 16, 8))
    grp = lambda i: ((i * tm) // rows_per_group, 0, 0)
    in_specs = [pl.BlockSpec((tm, d), lambda i: (i, 0)),
                pl.BlockSpec((1, d), lambda i: (0, 0)),
                pl.BlockSpec((1, 1, d), grp),
                pl.BlockSpec((1, 1, d), grp)]
    if router is None:
        return pl.pallas_call(
            _prenorm_kernel, grid=(m // tm,), in_specs=in_specs,
            out_specs=pl.BlockSpec((tm, d), lambda i: (i, 0)),
            out_shape=jax.ShapeDtypeStruct((m, d), BF16),
            compiler_params=_cparams(("parallel",)), name="prenorm",
        )(x, g.reshape(1, d), sc, sh)
    ne = router.shape[1]
    rpad = jnp.zeros((d, 128), F32).at[:, :ne].set(router)
    h, logits = pl.pallas_call(
        _prenorm_router_kernel, grid=(m // tm,),
        in_specs=in_specs + [pl.BlockSpec((d, 128), lambda i: (0, 0))],
        out_specs=[pl.BlockSpec((tm, d), lambda i: (i, 0)), pl.BlockSpec((tm, 128), lambda i: (i, 0))],
        out_shape=[jax.ShapeDtypeStruct((m, d), BF16), jax.ShapeDtypeStruct((m, 128), F32)],
        compiler_params=_cparams(("parallel",)), name="prenorm_router",
    )(x, g.reshape(1, d), sc, sh, rpad)
    return h, logits[:, :ne]


def _combine_resnorm_kernel(*refs):
    x_ref, y0_ref, y1_ref, w_ref, g_ref, gt_ref, o_ref = refs[0], refs[1], refs[2], refs[3], refs[4], refs[5], refs[-1]
    f = w_ref[:, 0:1] * y0_ref[...] + w_ref[:, 1:2] * y1_ref[...]
    ms = jnp.mean(f * f, axis=-1, keepdims=True)
    o_ref[...] = x_ref[...] + gt_ref[0] * (f * lax.rsqrt(ms + NORM_EPS) * g_ref[...])


def combine_resnorm(x, y0, y1, wts, g, gt, rows_per_group, row_off, out_buf):
    m, d = x.shape
    n = y0.shape[0]
    tm = _pick_tile(math.gcd(math.gcd(rows_per_group, n), row_off) if row_off else math.gcd(rows_per_group, n),
                    (512, 256, 128, 64, 32, 16, 8))
    blk = row_off // tm
    xrow = lambda i: (i + blk, 0)
    in_specs = [pl.BlockSpec((tm, d), xrow),
                pl.BlockSpec((tm, d), lambda i: (i, 0)), pl.BlockSpec((tm, d), lambda i: (i, 0)),
                pl.BlockSpec((tm, TOP_K), lambda i: (i, 0)),
                pl.BlockSpec((1, d), lambda i: (0, 0)),
                pl.BlockSpec((1, 1, d), lambda i: (((i + blk) * tm) // rows_per_group, 0, 0))]
    args = [x, y0, y1, wts, g.reshape(1, d), gt]
    aliases = {}
    if out_buf is not None:
        aliases = {len(args): 0}
        in_specs.append(pl.BlockSpec(memory_space=pl.ANY))
        args.append(out_buf)
    return pl.pallas_call(
        _combine_resnorm_kernel, grid=(n // tm,), in_specs=in_specs,
        out_specs=pl.BlockSpec((tm, d), xrow),
        out_shape=jax.ShapeDtypeStruct((m, d), F32),
        input_output_aliases=aliases,
        compiler_params=_cparams(("parallel",)), name="combine_resnorm",
    )(*args)


LOG2E = 1.0 / math.log(2.0)


def _attn_heads(q, k, v, bias, sink_ref, o_ref):
    qs = (q * (ATT_SCALE * LOG2E)).astype(BF16)
    ones = jnp.ones((v.shape[0], HEAD_DIM), BF16)
    heads = range(ATT_HEADS)
    kh = [_hslice(k, h) for h in range(ATT_KV_HEADS)]
    v1 = [jnp.concatenate([_hslice(v, h), ones], axis=1) for h in range(ATT_KV_HEADS)]
    sink2 = [sink_ref[hd] * LOG2E for hd in heads]
    s = [lax.dot_general(_hslice(qs, hd), kh[hd // ATT_GROUP], (((1,), (1,)), ((), ())),
                         preferred_element_type=F32) for hd in heads]
    if bias is not None:
        s = [sh + bias for sh in s]
    m = [jnp.maximum(jnp.max(s[hd], axis=-1, keepdims=True), sink2[hd]) for hd in heads]
    p = [jnp.exp2(s[hd] - m[hd]).astype(BF16) for hd in heads]
    pv = [jnp.dot(p[hd], v1[hd // ATT_GROUP], preferred_element_type=F32) for hd in heads]
    outs = [pv[hd][:, :HEAD_DIM] / (pv[hd][:, HEAD_DIM:] + jnp.exp2(sink2[hd] - m[hd])) for hd in heads]
    o_ref[0] = jnp.concatenate(outs, axis=-1)


def _hslice(t, h):
    return t[:, h * HEAD_DIM:(h + 1) * HEAD_DIM]


def _win_attn_kernel(sink_ref, q_ref, qp_ref, cq_ref, sq_ref,
                     k0_ref, k1_ref, k2_ref, kp0_ref, kp1_ref, kp2_ref,
                     v0_ref, v1_ref, v2_ref, c0_ref, c1_ref, c2_ref, s0_ref, s1_ref, s2_ref,
                     kc_ref, vc_ref, o_ref, *, nb):
    j = pl.program_id(1)
    q = q_ref[0] * cq_ref[...] + qp_ref[0] * sq_ref[...]
    kb = jnp.concatenate([k0_ref[0] * c0_ref[...] + kp0_ref[0] * s0_ref[...],
                          k1_ref[0] * c1_ref[...] + kp1_ref[0] * s1_ref[...],
                          k2_ref[0] * c2_ref[...] + kp2_ref[0] * s2_ref[...]], axis=0).astype(BF16)
    vb = jnp.concatenate([v0_ref[0], v1_ref[0], v2_ref[0]], axis=0).astype(BF16)
    n_ctx = kc_ref.shape[1]
    k = jnp.concatenate([kc_ref[0].astype(BF16), kb], axis=0)
    v = jnp.concatenate([vc_ref[0].astype(BF16), vb], axis=0)
    n = n_ctx + 3 * ATT_BLOCK
    qi = lax.broadcasted_iota(jnp.int32, (ATT_BLOCK, n), 0) + ATT_BLOCK
    ki = lax.broadcasted_iota(jnp.int32, (ATT_BLOCK, n), 1) - n_ctx
    lo = jnp.where(j == 0, ATT_BLOCK, 0)
    hi = jnp.where(j == nb - 1, 2 * ATT_BLOCK, 3 * ATT_BLOCK)
    far = jnp.where(ki < lo, 4 * ATT_BLOCK, 0) + jnp.where(ki >= hi, 4 * ATT_BLOCK, 0)
    seen = jnp.where(ki < 0, 0, jnp.abs(qi - ki) + far) <= WINDOW
    bias = jnp.where(seen, 0.0, -jnp.inf).astype(F32)
    _attn_heads(q, k, v, bias, sink_ref, o_ref)


def _ctx_attn_kernel(sink_ref, q_ref, kc_ref, vc_ref, o_ref):
    _attn_heads(q_ref[0], kc_ref[0].astype(BF16), vc_ref[0].astype(BF16), None, sink_ref, o_ref)


def windowed_attention(zl3, zc3, cos_t, sin_t, sink):
    b, t, _ = zl3.shape
    l = zc3.shape[1]
    nb = t // ATT_BLOCK
    kcol, kpcol, vcol = ZK // KV_W, ZKP // KV_W, ZV // KV_W
    prev = lambda j: jnp.maximum(j - 1, 0)
    nxt = lambda j: jnp.minimum(j + 1, nb - 1)

    def zspec(width, rowf, col):
        return pl.BlockSpec((1, ATT_BLOCK, width), lambda bi, j: (bi, rowf(j), col))

    def tspec(width, rowf):
        return pl.BlockSpec((ATT_BLOCK, width), lambda bi, j: (rowf(j), 0))

    same = lambda j: j
    in_specs = [pl.BlockSpec(memory_space=pltpu.SMEM),
                zspec(Q_W, same, ZQ // Q_W), zspec(Q_W, same, ZQP // Q_W), tspec(Q_W, same), tspec(Q_W, same)]
    args = [sink, zl3, zl3, cos_t, sin_t]
    for col in (kcol, kpcol, vcol):
        in_specs += [zspec(KV_W, prev, col), zspec(KV_W, same, col), zspec(KV_W, nxt, col)]
        args += [zl3, zl3, zl3]
    for tab in (cos_t, sin_t):
        in_specs += [tspec(KV_W, prev), tspec(KV_W, same), tspec(KV_W, nxt)]
        args += [tab, tab, tab]
    in_specs += [pl.BlockSpec((1, l, KV_W), lambda bi, j: (bi, 0, kcol)),
                 pl.BlockSpec((1, l, KV_W), lambda bi, j: (bi, 0, vcol))]
    args += [zc3, zc3]
    return pl.pallas_call(
        functools.partial(_win_attn_kernel, nb=nb),
        grid=(b, nb), in_specs=in_specs,
        out_specs=pl.BlockSpec((1, ATT_BLOCK, Q_W), lambda bi, j: (bi, j, 0)),
        out_shape=jax.ShapeDtypeStruct((b, t, Q_W), F32),
        compiler_params=_cparams(("parallel", "parallel")), name="win_attn",
    )(*args)


def context_attention(zc3, sink):
    b, l, _ = zc3.shape
    return pl.pallas_call(
        _ctx_attn_kernel, grid=(b,),
        in_specs=[pl.BlockSpec(memory_space=pltpu.SMEM),
                  pl.BlockSpec((1, l, Q_W), lambda bi: (bi, 0, ZQ // Q_W)),
                  pl.BlockSpec((1, l, KV_W), lambda bi: (bi, 0, ZK // KV_W)),
                  pl.BlockSpec((1, l, KV_W), lambda bi: (bi, 0, ZV // KV_W))],
        out_specs=pl.BlockSpec((1, l, Q_W), lambda bi: (bi, 0, 0)),
        out_shape=jax.ShapeDtypeStruct((b, l, Q_W), F32),
        compiler_params=_cparams(("parallel",)), name="ctx_attn",
    )(sink, zc3, zc3, zc3)


HALO = 8


def _head_sums(x, e_ref):
    hi = x.astype(BF16)
    lo = (x - hi.astype(F32)).astype(BF16)
    return (jnp.dot(hi, e_ref[...], preferred_element_type=F32)
            + jnp.dot(lo, e_ref[...], preferred_element_type=F32))


def _shift_lerp_tile(cur_ref, prev_ref, next_ref, mu_ref, buf_ref, first, last):
    tt = cur_ref.shape[1]
    z = cur_ref[0]
    buf_ref[0:HALO] = jnp.where(first, 0.0, prev_ref[0])
    buf_ref[HALO:HALO + tt] = z
    buf_ref[HALO + tt:2 * HALO + tt] = jnp.where(last, 0.0, next_ref[0])
    nb = buf_ref[HALO - 1:HALO - 1 + tt] + buf_ref[HALO + 1:HALO + 1 + tt]
    return z + mu_ref[...] * (0.5 * nb - z)


def _split_bf16(a):
    hi = a.astype(BF16)
    return hi, (a - hi.astype(F32)).astype(BF16)


def _hi_lo(w):
    return jnp.stack(_split_bf16(w))


def _dot_3pass(a, b):
    a_hi, a_lo = _split_bf16(a)
    return (jnp.dot(a_hi, b[0], preferred_element_type=F32) + jnp.dot(a_lo, b[0], preferred_element_type=F32)
            + jnp.dot(a_hi, b[1], preferred_element_type=F32))


def _rwkv_prep_kernel(*refs, has_vres, n_alias):
    (wa_c, wa_p, wa_n, r_c, r_p, r_n, k_c, k_p, k_n, v_c, v_p, v_n, g_c, g_p, g_n,
     mu_wa, mu_r, mu_k, mu_v, mu_g, w0_ref, w2_ref, a0_ref, a2_ref, g2_ref, kk_ref, ka_ref, rk_ref, e_ref) = refs[:29]
    pos = 29
    if has_vres:
        vf_ref, v0_ref, v1_ref, v2_ref = refs[pos:pos + 4]
        pos += 4
    pos += n_alias
    (rk_o, vk_o, w_o, an_o, bv_o, g_o) = refs[pos:pos + 6]
    pos += 6
    if not has_vres:
        vin_o = refs[pos]
        pos += 1
    buf_wa, buf_r, buf_k, buf_v, buf_g = refs[pos:pos + 5]

    i = pl.program_id(1)
    first = i == 0
    last = i == pl.num_programs(1) - 1
    zwa = _shift_lerp_tile(wa_c, wa_p, wa_n, mu_wa, buf_wa, first, last)
    r = _shift_lerp_tile(r_c, r_p, r_n, mu_r, buf_r, first, last)
    k = _shift_lerp_tile(k_c, k_p, k_n, mu_k, buf_k, first, last)
    v = _shift_lerp_tile(v_c, v_p, v_n, mu_v, buf_v, first, last)
    zg = _shift_lerp_tile(g_c, g_p, g_n, mu_g, buf_g, first, last)

    if has_vres:
        mix = _dot_3pass(_dot_3pass(v, v1_ref[...]), v2_ref[...])
        v = v + (vf_ref[0] - v) * jax.nn.sigmoid(v0_ref[...] + mix)
    else:
        vin_o[0] = v
    kk = k * kk_ref[...]
    kk = kk / jnp.maximum(jnp.sqrt(_head_sums(kk * kk, e_ref)), 1e-12)
    tz = jnp.tanh(zwa)
    ksum = None
    for d in range(2):
        w_log = -jax.nn.softplus(-(w0_ref[d:d + 1, :] + _dot_3pass(tz, w2_ref[d]))) - 0.5
        w_o[d, 0] = (-jnp.exp(w_log)).astype(w_o.dtype)
        a = jax.nn.sigmoid(a0_ref[d:d + 1, :] + _dot_3pass(zwa, a2_ref[d]))
        an_o[d, 0] = (-a).astype(an_o.dtype)
        kd = k * (1.0 + (a - 1.0) * ka_ref[...])
        ksum = kd if ksum is None else ksum + kd
    rk_o[0, 0] = r.astype(rk_o.dtype)
    rk_o[1, 0] = k.astype(rk_o.dtype)
    vk_o[0, 0] = v.astype(vk_o.dtype)
    vk_o[1, 0] = kk.astype(vk_o.dtype)
    bv_o[0] = _head_sums(r * ksum * rk_ref[...], e_ref) * v
    g_o[0] = _dot_3pass(jax.nn.sigmoid(zg), g2_ref[...])


def _head_ones():
    idx = jnp.arange(RWKV_W) // RWKV_HEAD
    return (idx[:, None] == idx[None, :]).astype(BF16)


def rwkv_prep(z3, lp, v_first, ttot, row_off, pair_bufs):
    b, tx, _ = z3.shape
    tt = _pick_tile(math.gcd(tx, row_off) if row_off else tx, (256, 128, 64, 32, 16, 8))
    nh = tt // HALO
    nblk = tx // HALO
    has_vres = v_first is not None
    lanes = 128

    def trio(width, col):
        return [pl.BlockSpec((1, tt, width), lambda bi, i: (bi, i, col)),
                pl.BlockSpec((1, HALO, width), lambda bi, i: (bi, jnp.maximum(i * nh - 1, 0), col)),
                pl.BlockSpec((1, HALO, width), lambda bi, i: (bi, jnp.minimum((i + 1) * nh, nblk - 1), col))]

    def full(shape):
        return pl.BlockSpec(shape, lambda bi, i: (0,) * len(shape))

    mu = lp['mu']
    o = 3 * RWKV_W
    nwa = 2 * DECAY_RANK + 2 * ICLR_RANK
    mu_g = jnp.zeros((lanes,), F32).at[:GATE_RANK].set(mu[o + nwa:])
    w2f = jnp.zeros((2, lanes, RWKV_W), F32)
    a2f = jnp.zeros((2, lanes, RWKV_W), F32)
    for d in range(2):
        w2f = w2f.at[d, d * DECAY_RANK:(d + 1) * DECAY_RANK].set(lp['w2'][d])
        a2f = a2f.at[d, 2 * DECAY_RANK + d * ICLR_RANK:2 * DECAY_RANK + (d + 1) * ICLR_RANK].set(lp['a2'][d])
    g2f = jnp.zeros((lanes, RWKV_W), F32).at[:GATE_RANK].set(lp['g2'])
    row = lambda a: a.reshape(1, -1)
    in_specs = (trio(lanes, ZWA // lanes) + trio(RWKV_W, ZR // RWKV_W) + trio(RWKV_W, ZRK // RWKV_W)
                + trio(RWKV_W, ZRV // RWKV_W) + trio(lanes, ZZG // lanes)
                + [full((1, lanes)), full((1, RWKV_W)), full((1, RWKV_W)), full((1, RWKV_W)), full((1, lanes)),
                   full((2, RWKV_W)), full((2, 2, lanes, RWKV_W)), full((2, RWKV_W)), full((2, 2, lanes, RWKV_W)),
                   full((2, lanes, RWKV_W)), full((1, RWKV_W)), full((1, RWKV_W)), full((1, RWKV_W)),
                   full((RWKV_W, RWKV_W))])
    per_dir = lambda w: jnp.stack([_hi_lo(w[0]), _hi_lo(w[1])])
    args = [z3] * 15 + [row(mu[o:o + nwa]), row(mu[:RWKV_W]), row(mu[RWKV_W:2 * RWKV_W]), row(mu[2 * RWKV_W:o]),
                        row(mu_g), lp['w0'], per_dir(w2f), lp['a0'], per_dir(a2f), _hi_lo(g2f),
                        row(lp['k_k']), row(lp['k_a']), row(lp['r_k']), _head_ones()]
    tok = pl.BlockSpec((1, tt, RWKV_W), lambda bi, i: (bi, i, 0))
    if has_vres:
        rank = lp['v1'].shape[1]
        v1f = jnp.zeros((RWKV_W, lanes), F32).at[:, :rank].set(lp['v1'])
        v2f = jnp.zeros((lanes, RWKV_W), F32).at[:rank].set(lp['v2'])
        in_specs += [tok, full((1, RWKV_W)), full((2, RWKV_W, lanes)), full((2, lanes, RWKV_W))]
        args += [v_first, row(lp['v0']), _hi_lo(v1f), _hi_lo(v2f)]
    n_alias = 0 if pair_bufs is None else len(pair_bufs)
    aliases = {}
    if pair_bufs is not None:
        aliases = {len(args) + n: n for n in range(n_alias)}
        in_specs += [pl.BlockSpec(memory_space=pl.ANY)] * n_alias
        args += list(pair_bufs)
    blk_off = row_off // tt
    pair = pl.BlockSpec((2, 1, tt, RWKV_W), lambda bi, i: (0, bi, i + blk_off, 0))
    pair_shape = jax.ShapeDtypeStruct((2, b, ttot, RWKV_W), BF16)
    tok_shape = jax.ShapeDtypeStruct((b, tx, RWKV_W), F32)
    n_tok = 2 if has_vres else 3
    outs = pl.pallas_call(
        functools.partial(_rwkv_prep_kernel, has_vres=has_vres, n_alias=n_alias),
        grid=(b, tx // tt), in_specs=in_specs,
        out_specs=[pair] * 4 + [tok] * n_tok,
        out_shape=[pair_shape] * 4 + [tok_shape] * n_tok,
        input_output_aliases=aliases,
        scratch_shapes=[pltpu.VMEM((tt + 2 * HALO, lanes), F32)] + [pltpu.VMEM((tt + 2 * HALO, RWKV_W), F32)] * 3
                       + [pltpu.VMEM((tt + 2 * HALO, lanes), F32)],
        compiler_params=_cparams(("parallel", "parallel")), name="rwkv_prep",
    )(*args)
    res = {'pairs': tuple(outs[:4]), 'bv': outs[4], 'g': outs[5]}
    if not has_vres:
        res['v_in'] = outs[6]
    return res


CONV_HALO = 16
CONV_ROWS = 32


def _conv_kernel(a_c, a_p, a_n, b_c, b_p, b_n, w_ref, bias_ref, lg_ref, lb_ref, o_ref, buf_ref, win_ref):
    i = pl.program_id(1)
    first = i == 0
    last = i == pl.num_programs(1) - 1
    tt = a_c.shape[1]
    glu = lambda a, b: a * jax.nn.sigmoid(b)
    buf_ref[0:CONV_HALO] = jnp.where(first, 0.0, glu(a_p[0], b_p[0]))
    buf_ref[CONV_HALO:CONV_HALO + tt] = glu(a_c[0], b_c[0])
    buf_ref[CONV_HALO + tt:2 * CONV_HALO + tt] = jnp.where(last, 0.0, glu(a_n[0], b_n[0]))
    half = CONV_K // 2
    span = CONV_ROWS + ((CONV_K - 1) // SUBLANES) * SUBLANES
    for r0 in range(0, tt, CONV_ROWS):
        base = CONV_HALO - half + r0
        acc = bias_ref[...]
        for ph in range(SUBLANES):
            win_ref[...] = buf_ref[base + ph:base + ph + span]
            for j in range(ph, CONV_K, SUBLANES):
                acc = acc + win_ref[j - ph:j - ph + CONV_ROWS] * w_ref[j:j + 1, :]
        m = jnp.mean(acc, axis=-1, keepdims=True)
        cen = acc - m
        var = jnp.mean(cen * cen, axis=-1, keepdims=True)
        y = cen * lax.rsqrt(var + CONV_LN_EPS) * lg_ref[...] + lb_ref[...]
        o_ref[0, r0:r0 + CONV_ROWS, :] = y * jax.nn.sigmoid(y)


def conv_module(z3, lp):
    b, tx, _ = z3.shape
    tt = _pick_tile(tx, (256, 128, 64, 32))
    nh = tt // CONV_HALO
    nblk = tx // CONV_HALO

    def trio(col):
        return [pl.BlockSpec((1, tt, CONV_W), lambda bi, i: (bi, i, col)),
                pl.BlockSpec((1, CONV_HALO, CONV_W), lambda bi, i: (bi, jnp.maximum(i * nh - 1, 0), col)),
                pl.BlockSpec((1, CONV_HALO, CONV_W), lambda bi, i: (bi, jnp.minimum((i + 1) * nh, nblk - 1), col))]

    kpad = -(-CONV_K // SUBLANES) * SUBLANES
    wpad = jnp.zeros((kpad, CONV_W), F32).at[:CONV_K].set(lp['dw_w'])
    vec = pl.BlockSpec((1, CONV_W), lambda bi, i: (0, 0))
    return pl.pallas_call(
        _conv_kernel, grid=(b, tx // tt),
        in_specs=trio(ZCA // CONV_W) + trio(ZCB // CONV_W)
                 + [pl.BlockSpec((kpad, CONV_W), lambda bi, i: (0, 0)), vec, vec, vec],
        out_specs=pl.BlockSpec((1, tt, CONV_W), lambda bi, i: (bi, i, 0)),
        out_shape=jax.ShapeDtypeStruct((b, tx, CONV_W), F32),
        scratch_shapes=[pltpu.VMEM((tt + 2 * CONV_HALO, CONV_W), F32),
                        pltpu.VMEM((CONV_ROWS + ((CONV_K - 1) // SUBLANES) * SUBLANES, CONV_W), F32)],
        compiler_params=_cparams(("parallel", "parallel")), name="conv_module",
    )(z3, z3, z3, z3, z3, z3, wpad, lp['dw_b'].reshape(1, -1), lp['cln_g'].reshape(1, -1), lp['cln_b'].reshape(1, -1))


SUBLANES = 8
DECAY_FOLD = 16


def _wkv_scan_kernel(rk_f, vk_f, w_f, an_f, rk_b, vk_b, w_b, an_b, ka_ref, yf_ref, yb_ref, s_ref, *, tt):
    @pl.when(pl.program_id(0) == 0)
    def _():
        s_ref[...] = jnp.zeros_like(s_ref)

    nk, _, lanes = s_ref.shape
    half = lanes // 2
    fwd = lax.broadcasted_iota(jnp.int32, (nk, lanes), 1) < half
    ka = ka_ref[...]
    fold = math.gcd(tt, DECAY_FOLD)

    def group(gi, carry):
        def step(ti, gam):
            t = gi * fold + ti
            tb = tt - 1 - t
            rkf, rkb, vkf, vkb = (a.astype(F32) for a in (rk_f[t], rk_b[tb], vk_f[t], vk_b[tb]))
            r = jnp.where(fwd, rkf, pltpu.roll(rkb, half, 1))
            k = jnp.where(fwd, pltpu.roll(rkf, half, 1), rkb)
            v = jnp.where(fwd, vkf, pltpu.roll(vkb, half, 1))
            kk = jnp.where(fwd, pltpu.roll(vkf, half, 1), vkb)
            w = jnp.exp(jnp.where(fwd, w_f[t].astype(F32), w_b[tb].astype(F32)))
            an = jnp.where(fwd, an_f[t].astype(F32), an_b[tb].astype(F32))
            gam_new = gam * w
            inv = 1.0 / gam_new
            kks = kk * gam
            kds = k * (1.0 - (an + 1.0) * ka) * inv
            bbs = kk * an * inv
            rs = r * gam_new
            row = lambda a, ki: a[ki:ki + 1, :]
            sa = [s_ref[0] * row(kks, 0), s_ref[1] * row(kks, 1)]
            for ki in range(2, nk):
                sa[ki % 2] = sa[ki % 2] + s_ref[ki] * row(kks, ki)
            sa = sa[0] + sa[1]
            ys = [None, None]
            for ki in range(nk):
                s_new = s_ref[ki] + sa * row(bbs, ki) + v * row(kds, ki)
                s_ref[ki] = s_new
                contrib = s_new * row(rs, ki)
                ys[ki % 2] = contrib if ys[ki % 2] is None else ys[ki % 2] + contrib
            y = ys[0] + ys[1]
            yf_ref[t] = y
            yb_ref[tb] = y
            return gam_new

        gam = lax.fori_loop(0, fold, step, jnp.ones((nk, lanes), F32), unroll=4)
        for ki in range(nk):
            s_ref[ki] = s_ref[ki] * gam[ki:ki + 1, :]
        return carry

    lax.fori_loop(0, tt // fold, group, 0)


def wkv_scan(rk, vk, w, an, ka, n_ctx_steps):
    ttot, n, ch = rk.shape
    tt = _pick_tile(math.gcd(n_ctx_steps, ttot - n_ctx_steps), (32, 16, 8, 4, 2, 1))
    nt, nc = ttot // tt, n_ctx_steps // tt
    fspec = pl.BlockSpec((tt, n, ch), lambda i: (i, 0, 0))
    bspec = pl.BlockSpec((tt, n, ch), lambda i: (jnp.where(i < nc, nc - 1 - i, nt - 1 - (i - nc)), 0, 0))
    return pl.pallas_call(
        functools.partial(_wkv_scan_kernel, tt=tt),
        grid=(nt,), in_specs=[fspec] * 4 + [bspec] * 4 + [pl.BlockSpec((n, ch), lambda i: (0, 0))],
        out_specs=[fspec, bspec],
        out_shape=[jax.ShapeDtypeStruct((ttot, n, ch), F32)] * 2,
        scratch_shapes=[pltpu.VMEM((n, n, ch), F32)],
        compiler_params=_cparams(("arbitrary",)), name="wkv_scan",
    )(rk, vk, w, an, rk, vk, w, an, ka)


def _merge_kernel(x_ref, att_ref, y_ref, bv_ref, rg_ref, cv_ref, g0_ref, g1_ref, g2_ref,
                  lg_ref, lb_ref, e_ref, wa_ref, wr_ref, wc_ref, wo_ref, ng_ref, gt_ref, o_ref):
    def proj(a, w_ref):
        return jnp.dot(a.astype(BF16), w_ref[...], preferred_element_type=F32)

    y = y_ref[...]
    cen = y - _head_sums(y, e_ref) * (1.0 / RWKV_HEAD)
    var = _head_sums(cen * cen, e_ref) * (1.0 / RWKV_HEAD)
    rw = (cen * lax.rsqrt(var + LNX_EPS) * lg_ref[...] + lb_ref[...] + bv_ref[...]) * rg_ref[...]
    m = (jax.nn.sigmoid(g0_ref[...]) * proj(att_ref[...], wa_ref)
         + jax.nn.sigmoid(g1_ref[...]) * proj(rw, wr_ref)
         + jax.nn.sigmoid(g2_ref[...]) * proj(cv_ref[...], wc_ref))
    ml = proj(m, wo_ref)
    ms = jnp.mean(ml * ml, axis=-1, keepdims=True)
    o_ref[...] = x_ref[...] + gt_ref[0] * (ml * lax.rsqrt(ms + NORM_EPS) * ng_ref[...])


def merge_out(x, att, y, bv, rg, cv, z, lnx_g, lnx_b, wa, wr, wc, wo, ng, gt, rows_per_group):
    m, d = x.shape
    tm = _pick_tile(rows_per_group, (512, 256, 128, 64, 32, 16, 8))
    row = lambda i: (i, 0)
    const = lambda i: (0, 0)
    gcol = ZG // d
    br = pl.BlockSpec((tm, RWKV_W), row)
    return pl.pallas_call(
        _merge_kernel, grid=(m // tm,),
        in_specs=[pl.BlockSpec((tm, d), row),
                  pl.BlockSpec((tm, Q_W), row), br, br, br, pl.BlockSpec((tm, CONV_W), row),
                  pl.BlockSpec((tm, d), lambda i: (i, gcol)),
                  pl.BlockSpec((tm, d), lambda i: (i, gcol + 1)),
                  pl.BlockSpec((tm, d), lambda i: (i, gcol + 2)),
                  pl.BlockSpec((1, RWKV_W), const), pl.BlockSpec((1, RWKV_W), const),
                  pl.BlockSpec((RWKV_W, RWKV_W), const),
                  pl.BlockSpec((Q_W, d), const), pl.BlockSpec((RWKV_W, d), const), pl.BlockSpec((CONV_W, d), const),
                  pl.BlockSpec((d, d), const), pl.BlockSpec((1, d), const),
                  pl.BlockSpec((1, 1, d), lambda i: ((i * tm) // rows_per_group, 0, 0))],
        out_specs=pl.BlockSpec((tm, d), row),
        out_shape=jax.ShapeDtypeStruct((m, d), F32),
        compiler_params=_cparams(("parallel",)), name="merge_out",
    )(x, att, y, bv, rg, cv, z, z, z, lnx_g.reshape(1, -1), lnx_b.reshape(1, -1), _head_ones(),
      wa, wr, wc, wo, ng.reshape(1, d), gt)


def _swiglu_kernel(te_ref, nu_ref, h_ref, wg_ref, wu_ref, wd_ref, o_ref):
    f = pl.program_id(1)

    used = pl.program_id(0) < nu_ref[0]

    @pl.when(jnp.logical_and(used, f == 0))
    def _():
        o_ref[...] = jnp.zeros_like(o_ref)

    @pl.when(used)
    def _():
        h = h_ref[...]
        hg = jnp.dot(h, wg_ref[0].astype(BF16), preferred_element_type=F32)
        hu = jnp.dot(h, wu_ref[0].astype(BF16), preferred_element_type=F32)
        act = (hg * jax.nn.sigmoid(hg) * hu).astype(BF16)
        o_ref[...] += jnp.dot(act, wd_ref[0].astype(BF16), preferred_element_type=F32)


def grouped_swiglu(h, tile_expert, n_used, wg, wu, wd, tm):
    p, d = h.shape
    ff = wg.shape[2]
    tf = _pick_tile(ff, (512, 1408, 256, 128))
    nf = ff // tf
    ti = lambda i, nu: jnp.minimum(i, nu[0] - 1)
    fi = lambda i, f, nu: jnp.where(i < nu[0], f, nf - 1)
    grid_spec = pltpu.PrefetchScalarGridSpec(
        num_scalar_prefetch=2, grid=(p // tm, nf),
        in_specs=[pl.BlockSpec((tm, d), lambda i, f, te, nu: (ti(i, nu), 0)),
                  pl.BlockSpec((1, d, tf), lambda i, f, te, nu: (te[ti(i, nu)], 0, fi(i, f, nu))),
                  pl.BlockSpec((1, d, tf), lambda i, f, te, nu: (te[ti(i, nu)], 0, fi(i, f, nu))),
                  pl.BlockSpec((1, tf, d), lambda i, f, te, nu: (te[ti(i, nu)], fi(i, f, nu), 0))],
        out_specs=pl.BlockSpec((tm, d), lambda i, f, te, nu: (ti(i, nu), 0)))
    return pl.pallas_call(
        _swiglu_kernel, grid_spec=grid_spec,
        out_shape=jax.ShapeDtypeStruct((p, d), F32),
        compiler_params=_cparams(("arbitrary", "arbitrary")), name="swiglu",
    )(tile_expert, n_used, h, wg, wu, wd)


def _dense_ffn_kernel(x_ref, g_ref, sc_ref, sh_ref, wg_ref, wu_ref, wd_ref, g2_ref, gt_ref, o_ref, h_ref, acc_ref):
    f = pl.program_id(1)

    @pl.when(f == 0)
    def _():
        h_ref[...] = _norm_mod(x_ref[...], g_ref[...], sc_ref[0], sh_ref[0]).astype(BF16)
        acc_ref[...] = jnp.zeros_like(acc_ref)

    h = h_ref[...]
    hg = jnp.dot(h, wg_ref[...], preferred_element_type=F32)
    hu = jnp.dot(h, wu_ref[...], preferred_element_type=F32)
    act = (hg * jax.nn.sigmoid(hg) * hu).astype(BF16)
    acc_ref[...] += jnp.dot(act, wd_ref[...], preferred_element_type=F32)

    @pl.when(f == pl.num_programs(1) - 1)
    def _():
        y = acc_ref[...]
        ms = jnp.mean(y * y, axis=-1, keepdims=True)
        o_ref[...] = x_ref[...] + gt_ref[0] * (y * lax.rsqrt(ms + NORM_EPS) * g2_ref[...])


def dense_ffn(x, g_in, sc, sh, wg, wu, wd, g_out, gt, rows_per_group):
    m, d = x.shape
    ff = wg.shape[1]
    tm = _pick_tile(rows_per_group, (512, 256, 128, 64, 32, 16, 8))
    tf = _pick_tile(ff, (1408, 512, 256, 128))
    grp = lambda i, f: ((i * tm) // rows_per_group, 0, 0)
    vec = pl.BlockSpec((1, d), lambda i, f: (0, 0))
    return pl.pallas_call(
        _dense_ffn_kernel, grid=(m // tm, ff // tf),
        in_specs=[pl.BlockSpec((tm, d), lambda i, f: (i, 0)), vec,
                  pl.BlockSpec((1, 1, d), grp), pl.BlockSpec((1, 1, d), grp),
                  pl.BlockSpec((d, tf), lambda i, f: (0, f)), pl.BlockSpec((d, tf), lambda i, f: (0, f)),
                  pl.BlockSpec((tf, d), lambda i, f: (f, 0)), vec, pl.BlockSpec((1, 1, d), grp)],
        out_specs=pl.BlockSpec((tm, d), lambda i, f: (i, 0)),
        out_shape=jax.ShapeDtypeStruct((m, d), F32),
        scratch_shapes=[pltpu.VMEM((tm, d), BF16), pltpu.VMEM((tm, d), F32)],
        compiler_params=_cparams(("parallel", "arbitrary")), name="dense_ffn",
    )(x, g_in.reshape(1, d), sc, sh, wg, wu, wd, g_out.reshape(1, d), gt)


def _rope_tables(t):
    rows = t // GRID_W
    row = jnp.broadcast_to(jnp.arange(rows)[:, None], (rows, GRID_W)).reshape(-1).astype(F32)
    col = jnp.broadcast_to(jnp.arange(GRID_W)[None, :], (rows, GRID_W)).reshape(-1).astype(F32)
    inv = ROPE_BASE ** (-jnp.arange(0, AXIS_DIM, 2, dtype=F32) / AXIS_DIM)
    ar = row[:, None] * inv[None, :]
    ac = col[:, None] * inv[None, :]
    cos64 = jnp.concatenate([jnp.cos(ar), jnp.cos(ar), jnp.cos(ac), jnp.cos(ac)], axis=-1)
    sin64 = jnp.concatenate([jnp.sin(ar), jnp.sin(ar), jnp.sin(ac), jnp.sin(ac)], axis=-1)
    return jnp.tile(cos64, (1, ATT_HEADS)), jnp.tile(sin64, (1, ATT_HEADS))


def _rope_partner(w):
    d = w.shape[0]
    q = AXIS_DIM // 2
    wh = w.reshape(d, -1, 4, q)
    return jnp.stack([-wh[:, :, 1], wh[:, :, 0], -wh[:, :, 3], wh[:, :, 2]], axis=2).reshape(w.shape)


def _build_w_ext(w_in):
    d = w_in.shape[0]
    wq = w_in[:, OFF_Q:OFF_K]
    wk = w_in[:, OFF_K:OFF_V]
    o = OFF_R + 3 * RWKV_W
    parts = [(ZQ, wq), (ZQP, _rope_partner(wq)), (ZK, wk), (ZKP, _rope_partner(wk)),
             (ZV, w_in[:, OFF_V:OFF_R]),
             (ZWA, w_in[:, o:o + 2 * DECAY_RANK + 2 * ICLR_RANK]),
             (ZCA, w_in[:, OFF_CONV:OFF_CONV + CONV_W]), (ZCB, w_in[:, OFF_CONV + CONV_W:OFF_GATE]),
             (ZR, w_in[:, OFF_R:OFF_R + RWKV_W]), (ZRK, w_in[:, OFF_R + RWKV_W:OFF_R + 2 * RWKV_W]),
             (ZRV, w_in[:, OFF_R + 2 * RWKV_W:OFF_R + 3 * RWKV_W]),
             (ZG, w_in[:, OFF_GATE:IN_COLS]),
             (ZZG, w_in[:, OFF_CONV - GATE_RANK:OFF_CONV])]
    parts.sort(key=lambda p: p[0])
    cols, pos = [], 0
    for off, wpart in parts:
        if off > pos:
            cols.append(jnp.zeros((d, off - pos), w_in.dtype))
        cols.append(wpart)
        pos = off + wpart.shape[1]
    if pos < NZ:
        cols.append(jnp.zeros((d, NZ - pos), w_in.dtype))
    return jnp.concatenate(cols, axis=1).astype(BF16)


def _chain_layout(pair):
    _, bsz, ttot, _ = pair.shape
    both = pair.reshape(2, bsz, ttot, RWKV_HEADS, RWKV_HEAD)
    return jnp.transpose(both, (2, 4, 0, 1, 3)).reshape(ttot, RWKV_HEAD, 2 * bsz * RWKV_HEADS)


def _bidirectional_wkv(pairs, k_a, l):
    bsz = pairs[0].shape[1]
    rk, vk, w, an = (_chain_layout(p) for p in pairs)
    ka = jnp.tile(k_a.reshape(RWKV_HEADS, RWKV_HEAD).T, (1, 2 * bsz))
    yf, yb = wkv_scan(rk, vk, w, an, ka, l)
    half = bsz * RWKV_HEADS
    y = (yf[:, :, :half] + yb[:, :, half:]).reshape(-1, RWKV_HEAD, bsz, RWKV_HEADS)
    y = jnp.transpose(y, (2, 0, 3, 1)).reshape(bsz, -1, RWKV_W)
    return y[:, l:], y[:, :l]


def _moe_dispatch(logits, tm):
    m = logits.shape[0]
    top_v, top_i = lax.top_k(logits, TOP_K)
    wts = jax.nn.softmax(top_v, axis=-1)
    e_flat = top_i.reshape(-1)
    n_assign = e_flat.shape[0]
    onehot = (e_flat[:, None] == jnp.arange(N_EXPERTS)[None, :]).astype(jnp.int32)
    csum = jnp.cumsum(onehot, axis=0)
    rank = jnp.sum(onehot * csum, axis=1) - 1
    cnt = csum[-1]
    pcnt = ((cnt + tm - 1) // tm) * tm
    pend = jnp.cumsum(pcnt)
    pstart = pend - pcnt
    dest = pstart[e_flat] + rank
    p_max = n_assign + N_EXPERTS * tm
    row_token = jnp.zeros((p_max,), jnp.int32).at[dest].set(jnp.arange(n_assign, dtype=jnp.int32) // TOP_K)
    tile_start = jnp.arange(p_max // tm, dtype=jnp.int32) * tm
    tile_expert = jnp.minimum(jnp.searchsorted(pend, tile_start, side='right'), N_EXPERTS - 1).astype(jnp.int32)
    n_used = (pend[-1:] // tm).astype(jnp.int32)
    return row_token, tile_expert, n_used, dest.reshape(m, TOP_K), wts


def kernel(x, c, ctx, c_ctx, mod_w, mod_b, norm_g, w_in, attn_sink, rwkv_mu, rwkv_w0, rwkv_w2, rwkv_a0, rwkv_a2, rwkv_g2, rwkv_k_k, rwkv_k_a, rwkv_r_k, rwkv_lnx_g, rwkv_lnx_b, vres_v0, vres_w1, vres_w2, conv_dw_w, conv_dw_b, conv_ln_g, conv_ln_b, w_attn_o, w_rwkv_o, w_conv_o, w_out, ffn_wg, ffn_wu, ffn_wd, moe_router, moe_wg, moe_wu, moe_wd):
    bsz, t, d = x.shape
    l = ctx.shape[1]
    depth = mod_w.shape[0]
    n_lat, n_ctx = bsz * t, bsz * l
    cos_t, sin_t = _rope_tables(t)
    xl = x.reshape(n_lat, d)
    xc = ctx.reshape(n_ctx, d)
    n_rows = -(-(bsz + 1) // SUBLANES) * SUBLANES
    cond = jnp.zeros((n_rows, d), F32).at[:bsz].set(jax.nn.silu(c)).at[bsz].set(jax.nn.silu(c_ctx))
    v_first = None
    for i in range(depth):
        need_ctx = i < depth - 1
        mod = matmul_bias_f32(cond, mod_w[i], mod_b[i])
        mod_l = [mod[:bsz, k * d:(k + 1) * d].reshape(bsz, 1, d) for k in range(6)]
        mod_c = [mod[bsz:bsz + 1, k * d:(k + 1) * d].reshape(1, 1, d) for k in range(6)]
        lp = {'mu': rwkv_mu[i], 'w0': rwkv_w0[i], 'w2': rwkv_w2[i], 'a0': rwkv_a0[i], 'a2': rwkv_a2[i],
              'g2': rwkv_g2[i], 'k_k': rwkv_k_k[i], 'k_a': rwkv_k_a[i], 'r_k': rwkv_r_k[i],
              'lnx_g': rwkv_lnx_g[i], 'lnx_b': rwkv_lnx_b[i],
              'dw_w': conv_dw_w[i], 'dw_b': conv_dw_b[i], 'cln_g': conv_ln_g[i], 'cln_b': conv_ln_b[i]}
        if i > 0:
            lp['v0'], lp['v1'], lp['v2'] = vres_v0[i - 1], vres_w1[i - 1], vres_w2[i - 1]
        w_ext = _build_w_ext(w_in[i])
        wa, wr, wc, wo = (w.astype(BF16) for w in (w_attn_o[i], w_rwkv_o[i], w_conv_o[i], w_out[i]))

        zl = nm_matmul(xl, norm_g[i, 0], mod_l[1], mod_l[0], w_ext, t)
        zc = nm_matmul(xc, norm_g[i, 0], mod_c[1], mod_c[0], w_ext, n_ctx)
        zl3 = zl.reshape(bsz, t, NZ)
        zc3 = zc.reshape(bsz, l, NZ)
        att_l = windowed_attention(zl3, zc3, cos_t, sin_t, attn_sink[i])
        p_c = rwkv_prep(zc3, lp, None if v_first is None else v_first[1], l + t, 0, None)
        p_l = rwkv_prep(zl3, lp, None if v_first is None else v_first[0], l + t, l, p_c['pairs'])
        if i == 0:
            v_first = (p_l['v_in'], p_c['v_in'])
        y_l, y_c = _bidirectional_wkv(p_l['pairs'], lp['k_a'], l)
        cv_l = conv_module(zl3, lp)
        flat = lambda a: a.reshape(-1, a.shape[-1])
        xl = merge_out(xl, flat(att_l), flat(y_l), flat(p_l['bv']), flat(p_l['g']), flat(cv_l), zl,
                       lp['lnx_g'], lp['lnx_b'], wa, wr, wc, wo, norm_g[i, 1], mod_l[2], t)
        if need_ctx:
            att_c = context_attention(zc3, attn_sink[i])
            cv_c = conv_module(zc3, lp)
            xc = merge_out(xc, flat(att_c), flat(y_c), flat(p_c['bv']), flat(p_c['g']), flat(cv_c), zc,
                           lp['lnx_g'], lp['lnx_b'], wa, wr, wc, wo, norm_g[i, 1], mod_c[2], n_ctx)

        j = i // 2
        if i % 2 == 0:
            wg, wu, wd = (w[j].astype(BF16) for w in (ffn_wg, ffn_wu, ffn_wd))
            xl = dense_ffn(xl, norm_g[i, 2], mod_l[4], mod_l[3], wg, wu, wd, norm_g[i, 3], mod_l[5], t)
            if need_ctx:
                xc = dense_ffn(xc, norm_g[i, 2], mod_c[4], mod_c[3], wg, wu, wd, norm_g[i, 3], mod_c[5], n_ctx)
        else:
            wg, wu, wd = moe_wg[j], moe_wu[j], moe_wd[j]
            streams = [(xl, mod_l, t)] + ([(xc, mod_c, n_ctx)] if need_ctx else [])
            outs = []
            for xs, ms, rpg in streams:
                n_tok = xs.shape[0]
                h, logits = prenorm(xs, norm_g[i, 2], ms[4], ms[3], rpg, router=moe_router[j])
                n_grp = MOE_GROUPS if n_tok % (MOE_GROUPS * SUBLANES) == 0 else 1
                gsz = n_tok // n_grp
                tm = _pick_tile(gsz, (1024, 512, 256, 128, 64, 32, 16, 8))
                x_new = None
                for gi in range(n_grp):
                    row_token, tile_expert, n_used, dest, wts = _moe_dispatch(logits[gi * gsz:(gi + 1) * gsz], tm)
                    yg = grouped_swiglu(jnp.take(h, row_token + gi * gsz, axis=0), tile_expert, n_used,
                                        wg, wu, wd, tm)
                    x_new = combine_resnorm(xs, jnp.take(yg, dest[:, 0], axis=0), jnp.take(yg, dest[:, 1], axis=0),
                                            wts, norm_g[i, 3], ms[5], rpg, gi * gsz, x_new)
                outs.append(x_new)
            xl = outs[0]
            if need_ctx:
                xc = outs[1]
    return xl.reshape(bsz, t, d)
```

```python
import functools
import math

import jax
import jax.numpy as jnp
from jax import lax
from jax.experimental import pallas as pl
from jax.experimental.pallas import tpu as pltpu

F32 = jnp.float32
BF16 = jnp.bfloat16

D_MODEL = 1024
GRID_W = 64
ATT_HEADS = 8
ATT_KV_HEADS = 2
ATT_GROUP = ATT_HEADS // ATT_KV_HEADS
HEAD_DIM = 64
WINDOW = 128
ATT_BLOCK = 128
ATT_SCALE = 1.0 / math.sqrt(HEAD_DIM)
ROPE_BASE = 10000.0
AXIS_DIM = HEAD_DIM // 2
RWKV_HEADS = 8
RWKV_HEAD = 64
RWKV_W = RWKV_HEADS * RWKV_HEAD
DECAY_RANK = 32
ICLR_RANK = 32
GATE_RANK = 96
LNX_EPS = 64e-5
CONV_W = 512
CONV_K = 31
CONV_LN_EPS = 1e-5
N_BRANCH = 3
N_EXPERTS = 8
TOP_K = 2
NORM_EPS = 1e-6
Q_W = ATT_HEADS * HEAD_DIM
KV_W = ATT_KV_HEADS * HEAD_DIM
OFF_Q = 0
OFF_K = OFF_Q + Q_W
OFF_V = OFF_K + KV_W
OFF_R = OFF_V + KV_W
OFF_CONV = OFF_R + 3 * RWKV_W + 2 * DECAY_RANK + 2 * ICLR_RANK + GATE_RANK
OFF_GATE = OFF_CONV + 2 * CONV_W
IN_COLS = OFF_GATE + N_BRANCH * D_MODEL

ZQ = 0
ZQP = 512
ZK = 1024
ZKP = 1152
ZV = 1280
ZWA = 1408
ZCA = 1536
ZCB = 2048
ZR = 2560
ZRK = 3072
ZRV = 3584
ZG = 4096
ZZG = 7168
NZ = 7680

VMEM_LIMIT = 56 * 1024 * 1024
MOE_GROUPS = 2


def _cparams(sem):
    return pltpu.CompilerParams(dimension_semantics=sem, vmem_limit_bytes=VMEM_LIMIT)


def _pick_tile(n, candidates):
    for c in candidates:
        if n % c == 0:
            return c
    return n


def _matmul_bias_kernel(a_ref, w_ref, b_ref, o_ref):
    o_ref[...] = jnp.dot(a_ref[...], w_ref[...], preferred_element_type=F32,
                         precision=lax.Precision.HIGHEST) + b_ref[...]


def matmul_bias_f32(a, w, b):
    m, k = a.shape
    n = w.shape[1]
    tn = _pick_tile(n, (1536, 1024, 512, 256, 128))
    return pl.pallas_call(
        _matmul_bias_kernel,
        grid=(n // tn,),
        in_specs=[pl.BlockSpec((m, k), lambda j: (0, 0)),
                  pl.BlockSpec((k, tn), lambda j: (0, j)),
                  pl.BlockSpec((1, tn), lambda j: (0, j))],
        out_specs=pl.BlockSpec((m, tn), lambda j: (0, j)),
        out_shape=jax.ShapeDtypeStruct((m, n), F32),
        compiler_params=_cparams(("parallel",)),
        name="mod_matmul",
    )(a, w, b.reshape(1, n))


def _norm_mod(x, g, sc, sh):
    ms = jnp.mean(x * x, axis=-1, keepdims=True)
    return (x * lax.rsqrt(ms + NORM_EPS) * g) * (1.0 + sc) + sh


def _nm_matmul_kernel(x_ref, g_ref, sc_ref, sh_ref, w_ref, o_ref, h_ref):
    @pl.when(pl.program_id(1) == 0)
    def _():
        h_ref[...] = _norm_mod(x_ref[...], g_ref[...], sc_ref[0], sh_ref[0]).astype(BF16)

    o_ref[...] = jnp.dot(h_ref[...], w_ref[...], preferred_element_type=F32).astype(o_ref.dtype)


def nm_matmul(x, g, sc, sh, w, rows_per_group):
    m, d = x.shape
    n = w.shape[1]
    tm = _pick_tile(rows_per_group, (1024, 512, 256, 128, 64, 32, 16, 8))
    tn = _pick_tile(n, (2560, 1280, 1024, 512, 256, 128))
    grp = lambda i, j: ((i * tm) // rows_per_group, 0, 0)
    return pl.pallas_call(
        _nm_matmul_kernel,
        grid=(m // tm, n // tn),
        in_specs=[pl.BlockSpec((tm, d), lambda i, j: (i, 0)),
                  pl.BlockSpec((1, d), lambda i, j: (0, 0)),
                  pl.BlockSpec((1, 1, d), grp),
                  pl.BlockSpec((1, 1, d), grp),
                  pl.BlockSpec((d, tn), lambda i, j: (0, j))],
        out_specs=pl.BlockSpec((tm, tn), lambda i, j: (i, j)),
        out_shape=jax.ShapeDtypeStruct((m, n), F32),
        scratch_shapes=[pltpu.VMEM((tm, d), BF16)],
        compiler_params=_cparams(("parallel", "arbitrary")),
        name="in_proj",
    )(x, g.reshape(1, d), sc, sh, w)


def _prenorm_kernel(x_ref, g_ref, sc_ref, sh_ref, h_ref):
    h_ref[...] = _norm_mod(x_ref[...], g_ref[...], sc_ref[0], sh_ref[0]).astype(h_ref.dtype)


def _prenorm_router_kernel(x_ref, g_ref, sc_ref, sh_ref, r_ref, h_ref, l_ref):
    h = _norm_mod(x_ref[...], g_ref[...], sc_ref[0], sh_ref[0])
    h_ref[...] = h.astype(h_ref.dtype)
    l_ref[...] = jnp.dot(h, r_ref[...], preferred_element_type=F32, precision=lax.Precision.HIGHEST)


def prenorm(x, g, sc, sh, rows_per_group, router=None):
    m, d = x.shape
    tm = _pick_tile(rows_per_group, (512, 256, 128, 64, 32, 16, 8))
    grp = lambda i: ((i * tm) // rows_per_group, 0, 0)
    in_specs = [pl.BlockSpec((tm, d), lambda i: (i, 0)),
                pl.BlockSpec((1, d), lambda i: (0, 0)),
                pl.BlockSpec((1, 1, d), grp),
                pl.BlockSpec((1, 1, d), grp)]
    if router is None:
        return pl.pallas_call(
            _prenorm_kernel, grid=(m // tm,), in_specs=in_specs,
            out_specs=pl.BlockSpec((tm, d), lambda i: (i, 0)),
            out_shape=jax.ShapeDtypeStruct((m, d), BF16),
            compiler_params=_cparams(("parallel",)), name="prenorm",
        )(x, g.reshape(1, d), sc, sh)
    ne = router.shape[1]
    rpad = jnp.zeros((d, 128), F32).at[:, :ne].set(router)
    h, logits = pl.pallas_call(
        _prenorm_router_kernel, grid=(m // tm,),
        in_specs=in_specs + [pl.BlockSpec((d, 128), lambda i: (0, 0))],
        out_specs=[pl.BlockSpec((tm, d), lambda i: (i, 0)), pl.BlockSpec((tm, 128), lambda i: (i, 0))],
        out_shape=[jax.ShapeDtypeStruct((m, d), BF16), jax.ShapeDtypeStruct((m, 128), F32)],
        compiler_params=_cparams(("parallel",)), name="prenorm_router",
    )(x, g.reshape(1, d), sc, sh, rpad)
    return h, logits[:, :ne]


def _combine_resnorm_kernel(*refs):
    x_ref, y0_ref, y1_ref, w_ref, g_ref, gt_ref, o_ref = refs[0], refs[1], refs[2], refs[3], refs[4], refs[5], refs[-1]
    w = w_ref[...].T
    f = w[:, 0:1] * y0_ref[...] + w[:, 1:2] * y1_ref[...]
    ms = jnp.mean(f * f, axis=-1, keepdims=True)
    o_ref[...] = x_ref[...] + gt_ref[0] * (f * lax.rsqrt(ms + NORM_EPS) * g_ref[...])


def combine_resnorm(x, y0, y1, wts, g, gt, rows_per_group, row_off, out_buf):
    m, d = x.shape
    n = y0.shape[0]
    tm = _pick_tile(math.gcd(math.gcd(rows_per_group, n), row_off) if row_off else math.gcd(rows_per_group, n),
                    (512, 256, 128, 64, 32, 16, 8))
    blk = row_off // tm
    xrow = lambda i: (i + blk, 0)
    in_specs = [pl.BlockSpec((tm, d), xrow),
                pl.BlockSpec((tm, d), lambda i: (i, 0)), pl.BlockSpec((tm, d), lambda i: (i, 0)),
                pl.BlockSpec((SUBLANES, tm), lambda i: (0, i)),
                pl.BlockSpec((1, d), lambda i: (0, 0)),
                pl.BlockSpec((1, 1, d), lambda i: (((i + blk) * tm) // rows_per_group, 0, 0))]
    wts_t = jnp.zeros((SUBLANES, n), F32).at[:TOP_K].set(wts.T)
    args = [x, y0, y1, wts_t, g.reshape(1, d), gt]
    aliases = {}
    if out_buf is not None:
        aliases = {len(args): 0}
        in_specs.append(pl.BlockSpec(memory_space=pl.ANY))
        args.append(out_buf)
    return pl.pallas_call(
        _combine_resnorm_kernel, grid=(n // tm,), in_specs=in_specs,
        out_specs=pl.BlockSpec((tm, d), xrow),
        out_shape=jax.ShapeDtypeStruct((m, d), F32),
        input_output_aliases=aliases,
        compiler_params=_cparams(("parallel",)), name="combine_resnorm",
    )(*args)


LOG2E = 1.0 / math.log(2.0)


def _attn_heads(q, k, v, bias, sink_ref, o_ref):
    qs = (q * (ATT_SCALE * LOG2E)).astype(BF16)
    ones = jnp.ones((v.shape[0], HEAD_DIM), BF16)
    heads = range(ATT_HEADS)
    kh = [_hslice(k, h) for h in range(ATT_KV_HEADS)]
    v1 = [jnp.concatenate([_hslice(v, h), ones], axis=1) for h in range(ATT_KV_HEADS)]
    sink2 = [sink_ref[hd] * LOG2E for hd in heads]
    s = [lax.dot_general(_hslice(qs, hd), kh[hd // ATT_GROUP], (((1,), (1,)), ((), ())),
                         preferred_element_type=F32) for hd in heads]
    if bias is not None:
        s = [sh + bias for sh in s]
    m = [jnp.maximum(jnp.max(s[hd], axis=-1, keepdims=True), sink2[hd]) for hd in heads]
    p = [jnp.exp2(s[hd] - m[hd]).astype(BF16) for hd in heads]
    pv = [jnp.dot(p[hd], v1[hd // ATT_GROUP], preferred_element_type=F32) for hd in heads]
    outs = [pv[hd][:, :HEAD_DIM] / (pv[hd][:, HEAD_DIM:] + jnp.exp2(sink2[hd] - m[hd])) for hd in heads]
    o_ref[0] = jnp.concatenate(outs, axis=-1)


def _hslice(t, h):
    return t[:, h * HEAD_DIM:(h + 1) * HEAD_DIM]


def _win_attn_kernel(sink_ref, q_ref, qp_ref, cq_ref, sq_ref,
                     k0_ref, k1_ref, k2_ref, kp0_ref, kp1_ref, kp2_ref,
                     v0_ref, v1_ref, v2_ref, c0_ref, c1_ref, c2_ref, s0_ref, s1_ref, s2_ref,
                     kc_ref, vc_ref, o_ref, *, nb):
    j = pl.program_id(1)
    q = q_ref[0] * cq_ref[...] + qp_ref[0] * sq_ref[...]
    kb = jnp.concatenate([k0_ref[0] * c0_ref[...] + kp0_ref[0] * s0_ref[...],
                          k1_ref[0] * c1_ref[...] + kp1_ref[0] * s1_ref[...],
                          k2_ref[0] * c2_ref[...] + kp2_ref[0] * s2_ref[...]], axis=0).astype(BF16)
    vb = jnp.concatenate([v0_ref[0], v1_ref[0], v2_ref[0]], axis=0).astype(BF16)
    n_ctx = kc_ref.shape[1]
    k = jnp.concatenate([kc_ref[0].astype(BF16), kb], axis=0)
    v = jnp.concatenate([vc_ref[0].astype(BF16), vb], axis=0)
    n = n_ctx + 3 * ATT_BLOCK
    qi = lax.broadcasted_iota(jnp.int32, (ATT_BLOCK, n), 0) + ATT_BLOCK
    ki = lax.broadcasted_iota(jnp.int32, (ATT_BLOCK, n), 1) - n_ctx
    lo = jnp.where(j == 0, ATT_BLOCK, 0)
    hi = jnp.where(j == nb - 1, 2 * ATT_BLOCK, 3 * ATT_BLOCK)
    far = jnp.where(ki < lo, 4 * ATT_BLOCK, 0) + jnp.where(ki >= hi, 4 * ATT_BLOCK, 0)
    seen = jnp.where(ki < 0, 0, jnp.abs(qi - ki) + far) <= WINDOW
    bias = jnp.where(seen, 0.0, -jnp.inf).astype(F32)
    _attn_heads(q, k, v, bias, sink_ref, o_ref)


def _ctx_attn_kernel(sink_ref, q_ref, kc_ref, vc_ref, o_ref):
    _attn_heads(q_ref[0], kc_ref[0].astype(BF16), vc_ref[0].astype(BF16), None, sink_ref, o_ref)


def windowed_attention(zl3, zc3, cos_t, sin_t, sink):
    b, t, _ = zl3.shape
    l = zc3.shape[1]
    nb = t // ATT_BLOCK
    kcol, kpcol, vcol = ZK // KV_W, ZKP // KV_W, ZV // KV_W
    prev = lambda j: jnp.maximum(j - 1, 0)
    nxt = lambda j: jnp.minimum(j + 1, nb - 1)

    def zspec(width, rowf, col):
        return pl.BlockSpec((1, ATT_BLOCK, width), lambda bi, j: (bi, rowf(j), col))

    def tspec(width, rowf):
        return pl.BlockSpec((ATT_BLOCK, width), lambda bi, j: (rowf(j), 0))

    same = lambda j: j
    in_specs = [pl.BlockSpec(memory_space=pltpu.SMEM),
                zspec(Q_W, same, ZQ // Q_W), zspec(Q_W, same, ZQP // Q_W), tspec(Q_W, same), tspec(Q_W, same)]
    args = [sink, zl3, zl3, cos_t, sin_t]
    for col in (kcol, kpcol, vcol):
        in_specs += [zspec(KV_W, prev, col), zspec(KV_W, same, col), zspec(KV_W, nxt, col)]
        args += [zl3, zl3, zl3]
    for tab in (cos_t, sin_t):
        in_specs += [tspec(KV_W, prev), tspec(KV_W, same), tspec(KV_W, nxt)]
        args += [tab, tab, tab]
    in_specs += [pl.BlockSpec((1, l, KV_W), lambda bi, j: (bi, 0, kcol)),
                 pl.BlockSpec((1, l, KV_W), lambda bi, j: (bi, 0, vcol))]
    args += [zc3, zc3]
    return pl.pallas_call(
        functools.partial(_win_attn_kernel, nb=nb),
        grid=(b, nb), in_specs=in_specs,
        out_specs=pl.BlockSpec((1, ATT_BLOCK, Q_W), lambda bi, j: (bi, j, 0)),
        out_shape=jax.ShapeDtypeStruct((b, t, Q_W), F32),
        compiler_params=_cparams(("parallel", "parallel")), name="win_attn",
    )(*args)


def context_attention(zc3, sink):
    b, l, _ = zc3.shape
    return pl.pallas_call(
        _ctx_attn_kernel, grid=(b,),
        in_specs=[pl.BlockSpec(memory_space=pltpu.SMEM),
                  pl.BlockSpec((1, l, Q_W), lambda bi: (bi, 0, ZQ // Q_W)),
                  pl.BlockSpec((1, l, KV_W), lambda bi: (bi, 0, ZK // KV_W)),
                  pl.BlockSpec((1, l, KV_W), lambda bi: (bi, 0, ZV // KV_W))],
        out_specs=pl.BlockSpec((1, l, Q_W), lambda bi: (bi, 0, 0)),
        out_shape=jax.ShapeDtypeStruct((b, l, Q_W), F32),
        compiler_params=_cparams(("parallel",)), name="ctx_attn",
    )(sink, zc3, zc3, zc3)


HALO = 8


def _head_sums(x, e_ref):
    hi = x.astype(BF16)
    lo = (x - hi.astype(F32)).astype(BF16)
    return (jnp.dot(hi, e_ref[...], preferred_element_type=F32)
            + jnp.dot(lo, e_ref[...], preferred_element_type=F32))


def _shift_lerp_tile(cur_ref, prev_ref, next_ref, mu_ref, buf_ref, first, last):
    tt = cur_ref.shape[1]
    z = cur_ref[0]
    buf_ref[0:HALO] = jnp.where(first, 0.0, prev_ref[0])
    buf_ref[HALO:HALO + tt] = z
    buf_ref[HALO + tt:2 * HALO + tt] = jnp.where(last, 0.0, next_ref[0])
    nb = buf_ref[HALO - 1:HALO - 1 + tt] + buf_ref[HALO + 1:HALO + 1 + tt]
    return z + mu_ref[...] * (0.5 * nb - z)


def _split_bf16(a):
    hi = a.astype(BF16)
    return hi, (a - hi.astype(F32)).astype(BF16)


def _hi_lo(w):
    return jnp.stack(_split_bf16(w))


def _dot_3pass(a, b):
    a_hi, a_lo = _split_bf16(a)
    return (jnp.dot(a_hi, b[0], preferred_element_type=F32) + jnp.dot(a_lo, b[0], preferred_element_type=F32)
            + jnp.dot(a_hi, b[1], preferred_element_type=F32))


def _rwkv_prep_kernel(*refs, has_vres, n_alias):
    (wa_c, wa_p, wa_n, r_c, r_p, r_n, k_c, k_p, k_n, v_c, v_p, v_n, g_c, g_p, g_n,
     mu_wa, mu_r, mu_k, mu_v, mu_g, w0_ref, w2_ref, a0_ref, a2_ref, g2_ref, kk_ref, ka_ref, rk_ref, e_ref) = refs[:29]
    pos = 29
    if has_vres:
        vf_ref, v0_ref, v1_ref, v2_ref = refs[pos:pos + 4]
        pos += 4
    pos += n_alias
    (rk_o, vk_o, w_o, an_o, bv_o, g_o) = refs[pos:pos + 6]
    pos += 6
    if not has_vres:
        vin_o = refs[pos]
        pos += 1
    buf_wa, buf_r, buf_k, buf_v, buf_g = refs[pos:pos + 5]

    i = pl.program_id(1)
    first = i == 0
    last = i == pl.num_programs(1) - 1
    zwa = _shift_lerp_tile(wa_c, wa_p, wa_n, mu_wa, buf_wa, first, last)
    r = _shift_lerp_tile(r_c, r_p, r_n, mu_r, buf_r, first, last)
    k = _shift_lerp_tile(k_c, k_p, k_n, mu_k, buf_k, first, last)
    v = _shift_lerp_tile(v_c, v_p, v_n, mu_v, buf_v, first, last)
    zg = _shift_lerp_tile(g_c, g_p, g_n, mu_g, buf_g, first, last)

    if has_vres:
        mix = _dot_3pass(_dot_3pass(v, v1_ref[...]), v2_ref[...])
        v = v + (vf_ref[0] - v) * jax.nn.sigmoid(v0_ref[...] + mix)
    else:
        vin_o[0] = v
    kk = k * kk_ref[...]
    kk = kk / jnp.maximum(jnp.sqrt(_head_sums(kk * kk, e_ref)), 1e-12)
    tz = jnp.tanh(zwa)
    ksum = None
    for d in range(2):
        w_log = -jax.nn.softplus(-(w0_ref[d:d + 1, :] + _dot_3pass(tz, w2_ref[d]))) - 0.5
        w_o[d, 0] = (-jnp.exp(w_log)).astype(w_o.dtype)
        a = jax.nn.sigmoid(a0_ref[d:d + 1, :] + _dot_3pass(zwa, a2_ref[d]))
        an_o[d, 0] = (-a).astype(an_o.dtype)
        kd = k * (1.0 + (a - 1.0) * ka_ref[...])
        ksum = kd if ksum is None else ksum + kd
    rk_o[0, 0] = r.astype(rk_o.dtype)
    rk_o[1, 0] = k.astype(rk_o.dtype)
    vk_o[0, 0] = v.astype(vk_o.dtype)
    vk_o[1, 0] = kk.astype(vk_o.dtype)
    bv_o[0] = _head_sums(r * ksum * rk_ref[...], e_ref) * v
    g_o[0] = _dot_3pass(jax.nn.sigmoid(zg), g2_ref[...])


def _head_ones():
    idx = jnp.arange(RWKV_W) // RWKV_HEAD
    return (idx[:, None] == idx[None, :]).astype(BF16)


def rwkv_prep(z3, lp, v_first, ttot, row_off, pair_bufs):
    b, tx, _ = z3.shape
    tt = _pick_tile(math.gcd(tx, row_off) if row_off else tx, (256, 128, 64, 32, 16, 8))
    nh = tt // HALO
    nblk = tx // HALO
    has_vres = v_first is not None
    lanes = 128

    def trio(width, col):
        return [pl.BlockSpec((1, tt, width), lambda bi, i: (bi, i, col)),
                pl.BlockSpec((1, HALO, width), lambda bi, i: (bi, jnp.maximum(i * nh - 1, 0), col)),
                pl.BlockSpec((1, HALO, width), lambda bi, i: (bi, jnp.minimum((i + 1) * nh, nblk - 1), col))]

    def full(shape):
        return pl.BlockSpec(shape, lambda bi, i: (0,) * len(shape))

    mu = lp['mu']
    o = 3 * RWKV_W
    nwa = 2 * DECAY_RANK + 2 * ICLR_RANK
    mu_g = jnp.zeros((lanes,), F32).at[:GATE_RANK].set(mu[o + nwa:])
    w2f = jnp.zeros((2, lanes, RWKV_W), F32)
    a2f = jnp.zeros((2, lanes, RWKV_W), F32)
    for d in range(2):
        w2f = w2f.at[d, d * DECAY_RANK:(d + 1) * DECAY_RANK].set(lp['w2'][d])
        a2f = a2f.at[d, 2 * DECAY_RANK + d * ICLR_RANK:2 * DECAY_RANK + (d + 1) * ICLR_RANK].set(lp['a2'][d])
    g2f = jnp.zeros((lanes, RWKV_W), F32).at[:GATE_RANK].set(lp['g2'])
    row = lambda a: a.reshape(1, -1)
    in_specs = (trio(lanes, ZWA // lanes) + trio(RWKV_W, ZR // RWKV_W) + trio(RWKV_W, ZRK // RWKV_W)
                + trio(RWKV_W, ZRV // RWKV_W) + trio(lanes, ZZG // lanes)
                + [full((1, lanes)), full((1, RWKV_W)), full((1, RWKV_W)), full((1, RWKV_W)), full((1, lanes)),
                   full((2, RWKV_W)), full((2, 2, lanes, RWKV_W)), full((2, RWKV_W)), full((2, 2, lanes, RWKV_W)),
                   full((2, lanes, RWKV_W)), full((1, RWKV_W)), full((1, RWKV_W)), full((1, RWKV_W)),
                   full((RWKV_W, RWKV_W))])
    per_dir = lambda w: jnp.stack([_hi_lo(w[0]), _hi_lo(w[1])])
    args = [z3] * 15 + [row(mu[o:o + nwa]), row(mu[:RWKV_W]), row(mu[RWKV_W:2 * RWKV_W]), row(mu[2 * RWKV_W:o]),
                        row(mu_g), lp['w0'], per_dir(w2f), lp['a0'], per_dir(a2f), _hi_lo(g2f),
                        row(lp['k_k']), row(lp['k_a']), row(lp['r_k']), _head_ones()]
    tok = pl.BlockSpec((1, tt, RWKV_W), lambda bi, i: (bi, i, 0))
    if has_vres:
        rank = lp['v1'].shape[1]
        v1f = jnp.zeros((RWKV_W, lanes), F32).at[:, :rank].set(lp['v1'])
        v2f = jnp.zeros((lanes, RWKV_W), F32).at[:rank].set(lp['v2'])
        in_specs += [tok, full((1, RWKV_W)), full((2, RWKV_W, lanes)), full((2, lanes, RWKV_W))]
        args += [v_first, row(lp['v0']), _hi_lo(v1f), _hi_lo(v2f)]
    n_alias = 0 if pair_bufs is None else len(pair_bufs)
    aliases = {}
    if pair_bufs is not None:
        aliases = {len(args) + n: n for n in range(n_alias)}
        in_specs += [pl.BlockSpec(memory_space=pl.ANY)] * n_alias
        args += list(pair_bufs)
    blk_off = row_off // tt
    pair = pl.BlockSpec((2, 1, tt, RWKV_W), lambda bi, i: (0, bi, i + blk_off, 0))
    pair_shape = jax.ShapeDtypeStruct((2, b, ttot, RWKV_W), BF16)
    tok_shape = jax.ShapeDtypeStruct((b, tx, RWKV_W), F32)
    n_tok = 2 if has_vres else 3
    outs = pl.pallas_call(
        functools.partial(_rwkv_prep_kernel, has_vres=has_vres, n_alias=n_alias),
        grid=(b, tx // tt), in_specs=in_specs,
        out_specs=[pair] * 4 + [tok] * n_tok,
        out_shape=[pair_shape] * 4 + [tok_shape] * n_tok,
        input_output_aliases=aliases,
        scratch_shapes=[pltpu.VMEM((tt + 2 * HALO, lanes), F32)] + [pltpu.VMEM((tt + 2 * HALO, RWKV_W), F32)] * 3
                       + [pltpu.VMEM((tt + 2 * HALO, lanes), F32)],
        compiler_params=_cparams(("parallel", "parallel")), name="rwkv_prep",
    )(*args)
    res = {'pairs': tuple(outs[:4]), 'bv': outs[4], 'g': outs[5]}
    if not has_vres:
        res['v_in'] = outs[6]
    return res


CONV_HALO = 16
CONV_ROWS = 32


def _conv_kernel(a_c, a_p, a_n, b_c, b_p, b_n, w_ref, bias_ref, lg_ref, lb_ref, o_ref, buf_ref, win_ref):
    i = pl.program_id(1)
    first = i == 0
    last = i == pl.num_programs(1) - 1
    tt = a_c.shape[1]
    glu = lambda a, b: a * jax.nn.sigmoid(b)
    buf_ref[0:CONV_HALO] = jnp.where(first, 0.0, glu(a_p[0], b_p[0]))
    buf_ref[CONV_HALO:CONV_HALO + tt] = glu(a_c[0], b_c[0])
    buf_ref[CONV_HALO + tt:2 * CONV_HALO + tt] = jnp.where(last, 0.0, glu(a_n[0], b_n[0]))
    half = CONV_K // 2
    span = CONV_ROWS + ((CONV_K - 1) // SUBLANES) * SUBLANES
    for r0 in range(0, tt, CONV_ROWS):
        base = CONV_HALO - half + r0
        acc = bias_ref[...]
        for ph in range(SUBLANES):
            win_ref[...] = buf_ref[base + ph:base + ph + span]
            for j in range(ph, CONV_K, SUBLANES):
                acc = acc + win_ref[j - ph:j - ph + CONV_ROWS] * w_ref[j:j + 1, :]
        m = jnp.mean(acc, axis=-1, keepdims=True)
        cen = acc - m
        var = jnp.mean(cen * cen, axis=-1, keepdims=True)
        y = cen * lax.rsqrt(var + CONV_LN_EPS) * lg_ref[...] + lb_ref[...]
        o_ref[0, r0:r0 + CONV_ROWS, :] = y * jax.nn.sigmoid(y)


def conv_module(z3, lp):
    b, tx, _ = z3.shape
    tt = _pick_tile(tx, (256, 128, 64, 32))
    nh = tt // CONV_HALO
    nblk = tx // CONV_HALO

    def trio(col):
        return [pl.BlockSpec((1, tt, CONV_W), lambda bi, i: (bi, i, col)),
                pl.BlockSpec((1, CONV_HALO, CONV_W), lambda bi, i: (bi, jnp.maximum(i * nh - 1, 0), col)),
                pl.BlockSpec((1, CONV_HALO, CONV_W), lambda bi, i: (bi, jnp.minimum((i + 1) * nh, nblk - 1), col))]

    kpad = -(-CONV_K // SUBLANES) * SUBLANES
    wpad = jnp.zeros((kpad, CONV_W), F32).at[:CONV_K].set(lp['dw_w'])
    vec = pl.BlockSpec((1, CONV_W), lambda bi, i: (0, 0))
    return pl.pallas_call(
        _conv_kernel, grid=(b, tx // tt),
        in_specs=trio(ZCA // CONV_W) + trio(ZCB // CONV_W)
                 + [pl.BlockSpec((kpad, CONV_W), lambda bi, i: (0, 0)), vec, vec, vec],
        out_specs=pl.BlockSpec((1, tt, CONV_W), lambda bi, i: (bi, i, 0)),
        out_shape=jax.ShapeDtypeStruct((b, tx, CONV_W), F32),
        scratch_shapes=[pltpu.VMEM((tt + 2 * CONV_HALO, CONV_W), F32),
                        pltpu.VMEM((CONV_ROWS + ((CONV_K - 1) // SUBLANES) * SUBLANES, CONV_W), F32)],
        compiler_params=_cparams(("parallel", "parallel")), name="conv_module",
    )(z3, z3, z3, z3, z3, z3, wpad, lp['dw_b'].reshape(1, -1), lp['cln_g'].reshape(1, -1), lp['cln_b'].reshape(1, -1))


SUBLANES = 8
DECAY_FOLD = 16


def _wkv_scan_kernel(rk_f, vk_f, w_f, an_f, rk_b, vk_b, w_b, an_b, ka_ref, yf_ref, yb_ref, s_ref, *, tt):
    @pl.when(pl.program_id(0) == 0)
    def _():
        s_ref[...] = jnp.zeros_like(s_ref)

    nk, _, lanes = s_ref.shape
    half = lanes // 2
    fwd = lax.broadcasted_iota(jnp.int32, (nk, lanes), 1) < half
    ka = ka_ref[...]
    fold = math.gcd(tt, DECAY_FOLD)

    def group(gi, carry):
        def step(ti, gam):
            t = gi * fold + ti
            tb = tt - 1 - t
            rkf, rkb, vkf, vkb = (a.astype(F32) for a in (rk_f[t], rk_b[tb], vk_f[t], vk_b[tb]))
            r = jnp.where(fwd, rkf, pltpu.roll(rkb, half, 1))
            k = jnp.where(fwd, pltpu.roll(rkf, half, 1), rkb)
            v = jnp.where(fwd, vkf, pltpu.roll(vkb, half, 1))
            kk = jnp.where(fwd, pltpu.roll(vkf, half, 1), vkb)
            w = jnp.exp(jnp.where(fwd, w_f[t].astype(F32), w_b[tb].astype(F32)))
            an = jnp.where(fwd, an_f[t].astype(F32), an_b[tb].astype(F32))
            gam_new = gam * w
            inv = 1.0 / gam_new
            kks = kk * gam
            kds = k * (1.0 - (an + 1.0) * ka) * inv
            bbs = kk * an * inv
            rs = r * gam_new
            row = lambda a, ki: a[ki:ki + 1, :]
            sa = [s_ref[0] * row(kks, 0), s_ref[1] * row(kks, 1)]
            for ki in range(2, nk):
                sa[ki % 2] = sa[ki % 2] + s_ref[ki] * row(kks, ki)
            sa = sa[0] + sa[1]
            ys = [None, None]
            for ki in range(nk):
                s_new = s_ref[ki] + sa * row(bbs, ki) + v * row(kds, ki)
                s_ref[ki] = s_new
                contrib = s_new * row(rs, ki)
                ys[ki % 2] = contrib if ys[ki % 2] is None else ys[ki % 2] + contrib
            y = ys[0] + ys[1]
            yf_ref[t] = y
            yb_ref[tb] = y
            return gam_new

        gam = lax.fori_loop(0, fold, step, jnp.ones((nk, lanes), F32), unroll=4)
        for ki in range(nk):
            s_ref[ki] = s_ref[ki] * gam[ki:ki + 1, :]
        return carry

    lax.fori_loop(0, tt // fold, group, 0)


def wkv_scan(rk, vk, w, an, ka, n_ctx_steps):
    ttot, n, ch = rk.shape
    tt = _pick_tile(math.gcd(n_ctx_steps, ttot - n_ctx_steps), (32, 16, 8, 4, 2, 1))
    nt, nc = ttot // tt, n_ctx_steps // tt
    fspec = pl.BlockSpec((tt, n, ch), lambda i: (i, 0, 0))
    bspec = pl.BlockSpec((tt, n, ch), lambda i: (jnp.where(i < nc, nc - 1 - i, nt - 1 - (i - nc)), 0, 0))
    return pl.pallas_call(
        functools.partial(_wkv_scan_kernel, tt=tt),
        grid=(nt,), in_specs=[fspec] * 4 + [bspec] * 4 + [pl.BlockSpec((n, ch), lambda i: (0, 0))],
        out_specs=[fspec, bspec],
        out_shape=[jax.ShapeDtypeStruct((ttot, n, ch), F32)] * 2,
        scratch_shapes=[pltpu.VMEM((n, n, ch), F32)],
        compiler_params=_cparams(("arbitrary",)), name="wkv_scan",
    )(rk, vk, w, an, rk, vk, w, an, ka)


def _merge_kernel(x_ref, att_ref, y_ref, bv_ref, rg_ref, cv_ref, g0_ref, g1_ref, g2_ref,
                  lg_ref, lb_ref, e_ref, wa_ref, wr_ref, wc_ref, wo_ref, ng_ref, gt_ref, o_ref):
    def proj(a, w_ref):
        return jnp.dot(a.astype(BF16), w_ref[...], preferred_element_type=F32)

    y = y_ref[...]
    cen = y - _head_sums(y, e_ref) * (1.0 / RWKV_HEAD)
    var = _head_sums(cen * cen, e_ref) * (1.0 / RWKV_HEAD)
    rw = (cen * lax.rsqrt(var + LNX_EPS) * lg_ref[...] + lb_ref[...] + bv_ref[...]) * rg_ref[...]
    m = (jax.nn.sigmoid(g0_ref[...]) * proj(att_ref[...], wa_ref)
         + jax.nn.sigmoid(g1_ref[...]) * proj(rw, wr_ref)
         + jax.nn.sigmoid(g2_ref[...]) * proj(cv_ref[...], wc_ref))
    ml = proj(m, wo_ref)
    ms = jnp.mean(ml * ml, axis=-1, keepdims=True)
    o_ref[...] = x_ref[...] + gt_ref[0] * (ml * lax.rsqrt(ms + NORM_EPS) * ng_ref[...])


def merge_out(x, att, y, bv, rg, cv, z, lnx_g, lnx_b, wa, wr, wc, wo, ng, gt, rows_per_group):
    m, d = x.shape
    tm = _pick_tile(rows_per_group, (512, 256, 128, 64, 32, 16, 8))
    row = lambda i: (i, 0)
    const = lambda i: (0, 0)
    gcol = ZG // d
    br = pl.BlockSpec((tm, RWKV_W), row)
    return pl.pallas_call(
        _merge_kernel, grid=(m // tm,),
        in_specs=[pl.BlockSpec((tm, d), row),
                  pl.BlockSpec((tm, Q_W), row), br, br, br, pl.BlockSpec((tm, CONV_W), row),
                  pl.BlockSpec((tm, d), lambda i: (i, gcol)),
                  pl.BlockSpec((tm, d), lambda i: (i, gcol + 1)),
                  pl.BlockSpec((tm, d), lambda i: (i, gcol + 2)),
                  pl.BlockSpec((1, RWKV_W), const), pl.BlockSpec((1, RWKV_W), const),
                  pl.BlockSpec((RWKV_W, RWKV_W), const),
                  pl.BlockSpec((Q_W, d), const), pl.BlockSpec((RWKV_W, d), const), pl.BlockSpec((CONV_W, d), const),
                  pl.BlockSpec((d, d), const), pl.BlockSpec((1, d), const),
                  pl.BlockSpec((1, 1, d), lambda i: ((i * tm) // rows_per_group, 0, 0))],
        out_specs=pl.BlockSpec((tm, d), row),
        out_shape=jax.ShapeDtypeStruct((m, d), F32),
        compiler_params=_cparams(("parallel",)), name="merge_out",
    )(x, att, y, bv, rg, cv, z, z, z, lnx_g.reshape(1, -1), lnx_b.reshape(1, -1), _head_ones(),
      wa, wr, wc, wo, ng.reshape(1, d), gt)


def _swiglu_kernel(te_ref, nu_ref, h_ref, wg_ref, wu_ref, wd_ref, o_ref):
    f = pl.program_id(1)

    used = pl.program_id(0) < nu_ref[0]

    @pl.when(jnp.logical_and(used, f == 0))
    def _():
        o_ref[...] = jnp.zeros_like(o_ref)

    @pl.when(used)
    def _():
        h = h_ref[...]
        hg = jnp.dot(h, wg_ref[0].astype(BF16), preferred_element_type=F32)
        hu = jnp.dot(h, wu_ref[0].astype(BF16), preferred_element_type=F32)
        act = (hg * jax.nn.sigmoid(hg) * hu).astype(BF16)
        o_ref[...] += jnp.dot(act, wd_ref[0].astype(BF16), preferred_element_type=F32)


def grouped_swiglu(h, tile_expert, n_used, wg, wu, wd, tm):
    p, d = h.shape
    ff = wg.shape[2]
    tf = _pick_tile(ff, (512, 1408, 256, 128))
    nf = ff // tf
    ti = lambda i, nu: jnp.minimum(i, nu[0] - 1)
    fi = lambda i, f, nu: jnp.where(i < nu[0], f, nf - 1)
    grid_spec = pltpu.PrefetchScalarGridSpec(
        num_scalar_prefetch=2, grid=(p // tm, nf),
        in_specs=[pl.BlockSpec((tm, d), lambda i, f, te, nu: (ti(i, nu), 0)),
                  pl.BlockSpec((1, d, tf), lambda i, f, te, nu: (te[ti(i, nu)], 0, fi(i, f, nu))),
                  pl.BlockSpec((1, d, tf), lambda i, f, te, nu: (te[ti(i, nu)], 0, fi(i, f, nu))),
                  pl.BlockSpec((1, tf, d), lambda i, f, te, nu: (te[ti(i, nu)], fi(i, f, nu), 0))],
        out_specs=pl.BlockSpec((tm, d), lambda i, f, te, nu: (ti(i, nu), 0)))
    return pl.pallas_call(
        _swiglu_kernel, grid_spec=grid_spec,
        out_shape=jax.ShapeDtypeStruct((p, d), F32),
        compiler_params=_cparams(("arbitrary", "arbitrary")), name="swiglu",
    )(tile_expert, n_used, h, wg, wu, wd)


def _dense_ffn_kernel(x_ref, g_ref, sc_ref, sh_ref, wg_ref, wu_ref, wd_ref, g2_ref, gt_ref, o_ref, h_ref, acc_ref):
    f = pl.program_id(1)

    @pl.when(f == 0)
    def _():
        h_ref[...] = _norm_mod(x_ref[...], g_ref[...], sc_ref[0], sh_ref[0]).astype(BF16)
        acc_ref[...] = jnp.zeros_like(acc_ref)

    h = h_ref[...]
    hg = jnp.dot(h, wg_ref[...], preferred_element_type=F32)
    hu = jnp.dot(h, wu_ref[...], preferred_element_type=F32)
    act = (hg * jax.nn.sigmoid(hg) * hu).astype(BF16)
    acc_ref[...] += jnp.dot(act, wd_ref[...], preferred_element_type=F32)

    @pl.when(f == pl.num_programs(1) - 1)
    def _():
        y = acc_ref[...]
        ms = jnp.mean(y * y, axis=-1, keepdims=True)
        o_ref[...] = x_ref[...] + gt_ref[0] * (y * lax.rsqrt(ms + NORM_EPS) * g2_ref[...])


def dense_ffn(x, g_in, sc, sh, wg, wu, wd, g_out, gt, rows_per_group):
    m, d = x.shape
    ff = wg.shape[1]
    tm = _pick_tile(rows_per_group, (512, 256, 128, 64, 32, 16, 8))
    tf = _pick_tile(ff, (1408, 512, 256, 128))
    grp = lambda i, f: ((i * tm) // rows_per_group, 0, 0)
    vec = pl.BlockSpec((1, d), lambda i, f: (0, 0))
    return pl.pallas_call(
        _dense_ffn_kernel, grid=(m // tm, ff // tf),
        in_specs=[pl.BlockSpec((tm, d), lambda i, f: (i, 0)), vec,
                  pl.BlockSpec((1, 1, d), grp), pl.BlockSpec((1, 1, d), grp),
                  pl.BlockSpec((d, tf), lambda i, f: (0, f)), pl.BlockSpec((d, tf), lambda i, f: (0, f)),
                  pl.BlockSpec((tf, d), lambda i, f: (f, 0)), vec, pl.BlockSpec((1, 1, d), grp)],
        out_specs=pl.BlockSpec((tm, d), lambda i, f: (i, 0)),
        out_shape=jax.ShapeDtypeStruct((m, d), F32),
        scratch_shapes=[pltpu.VMEM((tm, d), BF16), pltpu.VMEM((tm, d), F32)],
        compiler_params=_cparams(("parallel", "arbitrary")), name="dense_ffn",
    )(x, g_in.reshape(1, d), sc, sh, wg, wu, wd, g_out.reshape(1, d), gt)


def _rope_tables(t):
    rows = t // GRID_W
    row = jnp.broadcast_to(jnp.arange(rows)[:, None], (rows, GRID_W)).reshape(-1).astype(F32)
    col = jnp.broadcast_to(jnp.arange(GRID_W)[None, :], (rows, GRID_W)).reshape(-1).astype(F32)
    inv = ROPE_BASE ** (-jnp.arange(0, AXIS_DIM, 2, dtype=F32) / AXIS_DIM)
    ar = row[:, None] * inv[None, :]
    ac = col[:, None] * inv[None, :]
    cos64 = jnp.concatenate([jnp.cos(ar), jnp.cos(ar), jnp.cos(ac), jnp.cos(ac)], axis=-1)
    sin64 = jnp.concatenate([jnp.sin(ar), jnp.sin(ar), jnp.sin(ac), jnp.sin(ac)], axis=-1)
    return jnp.tile(cos64, (1, ATT_HEADS)), jnp.tile(sin64, (1, ATT_HEADS))


def _rope_partner(w):
    d = w.shape[0]
    q = AXIS_DIM // 2
    wh = w.reshape(d, -1, 4, q)
    return jnp.stack([-wh[:, :, 1], wh[:, :, 0], -wh[:, :, 3], wh[:, :, 2]], axis=2).reshape(w.shape)


def _build_w_ext(w_in):
    d = w_in.shape[0]
    wq = w_in[:, OFF_Q:OFF_K]
    wk = w_in[:, OFF_K:OFF_V]
    o = OFF_R + 3 * RWKV_W
    parts = [(ZQ, wq), (ZQP, _rope_partner(wq)), (ZK, wk), (ZKP, _rope_partner(wk)),
             (ZV, w_in[:, OFF_V:OFF_R]),
             (ZWA, w_in[:, o:o + 2 * DECAY_RANK + 2 * ICLR_RANK]),
             (ZCA, w_in[:, OFF_CONV:OFF_CONV + CONV_W]), (ZCB, w_in[:, OFF_CONV + CONV_W:OFF_GATE]),
             (ZR, w_in[:, OFF_R:OFF_R + RWKV_W]), (ZRK, w_in[:, OFF_R + RWKV_W:OFF_R + 2 * RWKV_W]),
             (ZRV, w_in[:, OFF_R + 2 * RWKV_W:OFF_R + 3 * RWKV_W]),
             (ZG, w_in[:, OFF_GATE:IN_COLS]),
             (ZZG, w_in[:, OFF_CONV - GATE_RANK:OFF_CONV])]
    parts.sort(key=lambda p: p[0])
    cols, pos = [], 0
    for off, wpart in parts:
        if off > pos:
            cols.append(jnp.zeros((d, off - pos), w_in.dtype))
        cols.append(wpart)
        pos = off + wpart.shape[1]
    if pos < NZ:
        cols.append(jnp.zeros((d, NZ - pos), w_in.dtype))
    return jnp.concatenate(cols, axis=1).astype(BF16)


def _chain_layout(pair):
    _, bsz, ttot, _ = pair.shape
    both = pair.reshape(2, bsz, ttot, RWKV_HEADS, RWKV_HEAD)
    return jnp.transpose(both, (2, 4, 0, 1, 3)).reshape(ttot, RWKV_HEAD, 2 * bsz * RWKV_HEADS)


def _bidirectional_wkv(pairs, k_a, l):
    bsz = pairs[0].shape[1]
    rk, vk, w, an = (_chain_layout(p) for p in pairs)
    ka = jnp.tile(k_a.reshape(RWKV_HEADS, RWKV_HEAD).T, (1, 2 * bsz))
    yf, yb = wkv_scan(rk, vk, w, an, ka, l)
    half = bsz * RWKV_HEADS
    y = (yf[:, :, :half] + yb[:, :, half:]).reshape(-1, RWKV_HEAD, bsz, RWKV_HEADS)
    y = jnp.transpose(y, (2, 0, 3, 1)).reshape(bsz, -1, RWKV_W)
    return y[:, l:], y[:, :l]


def _moe_dispatch(logits, tm):
    m = logits.shape[0]
    top_v, top_i = lax.top_k(logits, TOP_K)
    wts = jax.nn.softmax(top_v, axis=-1)
    e_flat = top_i.reshape(-1)
    n_assign = e_flat.shape[0]
    onehot = (e_flat[:, None] == jnp.arange(N_EXPERTS)[None, :]).astype(jnp.int32)
    csum = jnp.cumsum(onehot, axis=0)
    rank = jnp.sum(onehot * csum, axis=1) - 1
    cnt = csum[-1]
    pcnt = ((cnt + tm - 1) // tm) * tm
    pend = jnp.cumsum(pcnt)
    pstart = pend - pcnt
    dest = pstart[e_flat] + rank
    p_max = n_assign + N_EXPERTS * tm
    tile_start = jnp.arange(p_max // tm, dtype=jnp.int32) * tm
    tile_expert = jnp.minimum(jnp.searchsorted(pend, tile_start, side='right'), N_EXPERTS - 1).astype(jnp.int32)
    order = jnp.argsort(e_flat, stable=True).astype(jnp.int32)
    rows = jnp.arange(p_max, dtype=jnp.int32)
    row_expert = tile_expert[rows // tm]
    src = order[jnp.clip((jnp.cumsum(cnt) - cnt)[row_expert] + rows - pstart[row_expert], 0, n_assign - 1)]
    row_token = src // TOP_K
    n_used = (pend[-1:] // tm).astype(jnp.int32)
    return row_token, tile_expert, n_used, dest.reshape(m, TOP_K), wts


def kernel(x, c, ctx, c_ctx, mod_w, mod_b, norm_g, w_in, attn_sink, rwkv_mu, rwkv_w0, rwkv_w2, rwkv_a0, rwkv_a2, rwkv_g2, rwkv_k_k, rwkv_k_a, rwkv_r_k, rwkv_lnx_g, rwkv_lnx_b, vres_v0, vres_w1, vres_w2, conv_dw_w, conv_dw_b, conv_ln_g, conv_ln_b, w_attn_o, w_rwkv_o, w_conv_o, w_out, ffn_wg, ffn_wu, ffn_wd, moe_router, moe_wg, moe_wu, moe_wd):
    bsz, t, d = x.shape
    l = ctx.shape[1]
    depth = mod_w.shape[0]
    n_lat, n_ctx = bsz * t, bsz * l
    cos_t, sin_t = _rope_tables(t)
    xl = x.reshape(n_lat, d)
    xc = ctx.reshape(n_ctx, d)
    n_rows = -(-(bsz + 1) // SUBLANES) * SUBLANES
    cond = jnp.zeros((n_rows, d), F32).at[:bsz].set(jax.nn.silu(c)).at[bsz].set(jax.nn.silu(c_ctx))
    v_first = None
    for i in range(depth):
        need_ctx = i < depth - 1
        mod = matmul_bias_f32(cond, mod_w[i], mod_b[i])
        mod_l = [mod[:bsz, k * d:(k + 1) * d].reshape(bsz, 1, d) for k in range(6)]
        mod_c = [mod[bsz:bsz + 1, k * d:(k + 1) * d].reshape(1, 1, d) for k in range(6)]
        lp = {'mu': rwkv_mu[i], 'w0': rwkv_w0[i], 'w2': rwkv_w2[i], 'a0': rwkv_a0[i], 'a2': rwkv_a2[i],
              'g2': rwkv_g2[i], 'k_k': rwkv_k_k[i], 'k_a': rwkv_k_a[i], 'r_k': rwkv_r_k[i],
              'lnx_g': rwkv_lnx_g[i], 'lnx_b': rwkv_lnx_b[i],
              'dw_w': conv_dw_w[i], 'dw_b': conv_dw_b[i], 'cln_g': conv_ln_g[i], 'cln_b': conv_ln_b[i]}
        if i > 0:
            lp['v0'], lp['v1'], lp['v2'] = vres_v0[i - 1], vres_w1[i - 1], vres_w2[i - 1]
        w_ext = _build_w_ext(w_in[i])
        wa, wr, wc, wo = (w.astype(BF16) for w in (w_attn_o[i], w_rwkv_o[i], w_conv_o[i], w_out[i]))

        zl = nm_matmul(xl, norm_g[i, 0], mod_l[1], mod_l[0], w_ext, t)
        zc = nm_matmul(xc, norm_g[i, 0], mod_c[1], mod_c[0], w_ext, n_ctx)
        zl3 = zl.reshape(bsz, t, NZ)
        zc3 = zc.reshape(bsz, l, NZ)
        att_l = windowed_attention(zl3, zc3, cos_t, sin_t, attn_sink[i])
        p_c = rwkv_prep(zc3, lp, None if v_first is None else v_first[1], l + t, 0, None)
        p_l = rwkv_prep(zl3, lp, None if v_first is None else v_first[0], l + t, l, p_c['pairs'])
        if i == 0:
            v_first = (p_l['v_in'], p_c['v_in'])
        y_l, y_c = _bidirectional_wkv(p_l['pairs'], lp['k_a'], l)
        cv_l = conv_module(zl3, lp)
        flat = lambda a: a.reshape(-1, a.shape[-1])
        xl = merge_out(xl, flat(att_l), flat(y_l), flat(p_l['bv']), flat(p_l['g']), flat(cv_l), zl,
                       lp['lnx_g'], lp['lnx_b'], wa, wr, wc, wo, norm_g[i, 1], mod_l[2], t)
        if need_ctx:
            att_c = context_attention(zc3, attn_sink[i])
            cv_c = conv_module(zc3, lp)
            xc = merge_out(xc, flat(att_c), flat(y_c), flat(p_c['bv']), flat(p_c['g']), flat(cv_c), zc,
                           lp['lnx_g'], lp['lnx_b'], wa, wr, wc, wo, norm_g[i, 1], mod_c[2], n_ctx)

        j = i // 2
        if i % 2 == 0:
            wg, wu, wd = (w[j].astype(BF16) for w in (ffn_wg, ffn_wu, ffn_wd))
            xl = dense_ffn(xl, norm_g[i, 2], mod_l[4], mod_l[3], wg, wu, wd, norm_g[i, 3], mod_l[5], t)
            if need_ctx:
                xc = dense_ffn(xc, norm_g[i, 2], mod_c[4], mod_c[3], wg, wu, wd, norm_g[i, 3], mod_c[5], n_ctx)
        else:
            wg, wu, wd = moe_wg[j], moe_wu[j], moe_wd[j]
            streams = [(xl, mod_l, t)] + ([(xc, mod_c, n_ctx)] if need_ctx else [])
            outs = []
            for xs, ms, rpg in streams:
                n_tok = xs.shape[0]
                h, logits = prenorm(xs, norm_g[i, 2], ms[4], ms[3], rpg, router=moe_router[j])
                n_grp = MOE_GROUPS if n_tok % (MOE_GROUPS * SUBLANES) == 0 else 1
                gsz = n_tok // n_grp
                tm = _pick_tile(gsz, (1024, 512, 256, 128, 64, 32, 16, 8))
                x_new = None
                for gi in range(n_grp):
                    row_token, tile_expert, n_used, dest, wts = _moe_dispatch(logits[gi * gsz:(gi + 1) * gsz], tm)
                    yg = grouped_swiglu(jnp.take(h, row_token + gi * gsz, axis=0), tile_expert, n_used,
                                        wg, wu, wd, tm)
                    x_new = combine_resnorm(xs, jnp.take(yg, dest[:, 0], axis=0), jnp.take(yg, dest[:, 1], axis=0),
                                            wts, norm_g[i, 3], ms[5], rpg, gi * gsz, x_new)
                outs.append(x_new)
            xl = outs[0]
            if need_ctx:
                xc = outs[1]
    return xl.reshape(bsz, t, d)
```

```python
import functools
import math

import jax
import jax.numpy as jnp
from jax import lax
from jax.experimental import pallas as pl
from jax.experimental.pallas import tpu as pltpu

F32 = jnp.float32
BF16 = jnp.bfloat16

D_MODEL = 1024
GRID_W = 64
ATT_HEADS = 8
ATT_KV_HEADS = 2
ATT_GROUP = ATT_HEADS // ATT_KV_HEADS
HEAD_DIM = 64
WINDOW = 128
ATT_BLOCK = 128
ATT_SCALE = 1.0 / math.sqrt(HEAD_DIM)
ROPE_BASE = 10000.0
AXIS_DIM = HEAD_DIM // 2
RWKV_HEADS = 8
RWKV_HEAD = 64
RWKV_W = RWKV_HEADS * RWKV_HEAD
DECAY_RANK = 32
ICLR_RANK = 32
GATE_RANK = 96
LNX_EPS = 64e-5
CONV_W = 512
CONV_K = 31
CONV_LN_EPS = 1e-5
N_BRANCH = 3
N_EXPERTS = 8
TOP_K = 2
NORM_EPS = 1e-6
Q_W = ATT_HEADS * HEAD_DIM
KV_W = ATT_KV_HEADS * HEAD_DIM
OFF_Q = 0
OFF_K = OFF_Q + Q_W
OFF_V = OFF_K + KV_W
OFF_R = OFF_V + KV_W
OFF_CONV = OFF_R + 3 * RWKV_W + 2 * DECAY_RANK + 2 * ICLR_RANK + GATE_RANK
OFF_GATE = OFF_CONV + 2 * CONV_W
IN_COLS = OFF_GATE + N_BRANCH * D_MODEL

ZQ = 0
ZQP = 512
ZK = 1024
ZKP = 1152
ZV = 1280
ZWA = 1408
ZCA = 1536
ZCB = 2048
ZR = 2560
ZRK = 3072
ZRV = 3584
ZG = 4096
ZZG = 7168
NZ = 7680

VMEM_LIMIT = 56 * 1024 * 1024
MOE_GROUPS = 2


def _cparams(sem):
    return pltpu.CompilerParams(dimension_semantics=sem, vmem_limit_bytes=VMEM_LIMIT)


def _pick_tile(n, candidates):
    for c in candidates:
        if n % c == 0:
            return c
    return n


def _matmul_bias_kernel(a_ref, w_ref, b_ref, o_ref):
    o_ref[...] = jnp.dot(a_ref[...], w_ref[...], preferred_element_type=F32,
                         precision=lax.Precision.HIGHEST) + b_ref[...]


def matmul_bias_f32(a, w, b):
    m, k = a.shape
    n = w.shape[1]
    tn = _pick_tile(n, (1536, 1024, 512, 256, 128))
    return pl.pallas_call(
        _matmul_bias_kernel,
        grid=(n // tn,),
        in_specs=[pl.BlockSpec((m, k), lambda j: (0, 0)),
                  pl.BlockSpec((k, tn), lambda j: (0, j)),
                  pl.BlockSpec((1, tn), lambda j: (0, j))],
        out_specs=pl.BlockSpec((m, tn), lambda j: (0, j)),
        out_shape=jax.ShapeDtypeStruct((m, n), F32),
        compiler_params=_cparams(("parallel",)),
        name="mod_matmul",
    )(a, w, b.reshape(1, n))


def _norm_mod(x, g, sc, sh):
    ms = jnp.mean(x * x, axis=-1, keepdims=True)
    return (x * lax.rsqrt(ms + NORM_EPS) * g) * (1.0 + sc) + sh


def _nm_matmul_kernel(x_ref, g_ref, sc_ref, sh_ref, w_ref, o_ref, h_ref):
    @pl.when(pl.program_id(1) == 0)
    def _():
        h_ref[...] = _norm_mod(x_ref[...], g_ref[...], sc_ref[0], sh_ref[0]).astype(BF16)

    o_ref[...] = jnp.dot(h_ref[...], w_ref[...], preferred_element_type=F32).astype(o_ref.dtype)


def nm_matmul(x, g, sc, sh, w, rows_per_group):
    m, d = x.shape
    n = w.shape[1]
    tm = _pick_tile(rows_per_group, (1024, 512, 256, 128, 64, 32, 16, 8))
    tn = _pick_tile(n, (2560, 1280, 1024, 512, 256, 128))
    grp = lambda i, j: ((i * tm) // rows_per_group, 0, 0)
    return pl.pallas_call(
        _nm_matmul_kernel,
        grid=(m // tm, n // tn),
        in_specs=[pl.BlockSpec((tm, d), lambda i, j: (i, 0)),
                  pl.BlockSpec((1, d), lambda i, j: (0, 0)),
                  pl.BlockSpec((1, 1, d), grp),
                  pl.BlockSpec((1, 1, d), grp),
                  pl.BlockSpec((d, tn), lambda i, j: (0, j))],
        out_specs=pl.BlockSpec((tm, tn), lambda i, j: (i, j)),
        out_shape=jax.ShapeDtypeStruct((m, n), F32),
        scratch_shapes=[pltpu.VMEM((tm, d), BF16)],
        compiler_params=_cparams(("parallel", "arbitrary")),
        name="in_proj",
    )(x, g.reshape(1, d), sc, sh, w)


def _prenorm_kernel(x_ref, g_ref, sc_ref, sh_ref, h_ref):
    h_ref[...] = _norm_mod(x_ref[...], g_ref[...], sc_ref[0], sh_ref[0]).astype(h_ref.dtype)


def _prenorm_router_kernel(x_ref, g_ref, sc_ref, sh_ref, r_ref, h_ref, l_ref):
    h = _norm_mod(x_ref[...], g_ref[...], sc_ref[0], sh_ref[0])
    h_ref[...] = h.astype(h_ref.dtype)
    l_ref[...] = jnp.dot(h, r_ref[...], preferred_element_type=F32, precision=lax.Precision.HIGHEST)


def prenorm(x, g, sc, sh, rows_per_group, router=None):
    m, d = x.shape
    tm = _pick_tile(rows_per_group, (512, 256, 128, 64, 32, 16, 8))
    grp = lambda i: ((i * tm) // rows_per_group, 0, 0)
    in_specs = [pl.BlockSpec((tm, d), lambda i: (i, 0)),
                pl.BlockSpec((1, d), lambda i: (0, 0)),
                pl.BlockSpec((1, 1, d), grp),
                pl.BlockSpec((1, 1, d), grp)]
    if router is None:
        return pl.pallas_call(
            _prenorm_kernel, grid=(m // tm,), in_specs=in_specs,
            out_specs=pl.BlockSpec((tm, d), lambda i: (i, 0)),
            out_shape=jax.ShapeDtypeStruct((m, d), BF16),
            compiler_params=_cparams(("parallel",)), name="prenorm",
        )(x, g.reshape(1, d), sc, sh)
    ne = router.shape[1]
    rpad = jnp.zeros((d, 128), F32).at[:, :ne].set(router)
    h, logits = pl.pallas_call(
        _prenorm_router_kernel, grid=(m // tm,),
        in_specs=in_specs + [pl.BlockSpec((d, 128), lambda i: (0, 0))],
        out_specs=[pl.BlockSpec((tm, d), lambda i: (i, 0)), pl.BlockSpec((tm, 128), lambda i: (i, 0))],
        out_shape=[jax.ShapeDtypeStruct((m, d), BF16), jax.ShapeDtypeStruct((m, 128), F32)],
        compiler_params=_cparams(("parallel",)), name="prenorm_router",
    )(x, g.reshape(1, d), sc, sh, rpad)
    return h, logits[:, :ne]


def _combine_resnorm_kernel(*refs):
    x_ref, y0_ref, y1_ref, w_ref, g_ref, gt_ref, o_ref = refs[0], refs[1], refs[2], refs[3], refs[4], refs[5], refs[-1]
    w = w_ref[...].T
    f = w[:, 0:1] * y0_ref[...] + w[:, 1:2] * y1_ref[...]
    ms = jnp.mean(f * f, axis=-1, keepdims=True)
    o_ref[...] = x_ref[...] + gt_ref[0] * (f * lax.rsqrt(ms + NORM_EPS) * g_ref[...])


def combine_resnorm(x, y0, y1, wts, g, gt, rows_per_group, row_off, out_buf):
    m, d = x.shape
    n = y0.shape[0]
    tm = _pick_tile(math.gcd(math.gcd(rows_per_group, n), row_off) if row_off else math.gcd(rows_per_group, n),
                    (512, 256, 128, 64, 32, 16, 8))
    blk = row_off // tm
    xrow = lambda i: (i + blk, 0)
    in_specs = [pl.BlockSpec((tm, d), xrow),
                pl.BlockSpec((tm, d), lambda i: (i, 0)), pl.BlockSpec((tm, d), lambda i: (i, 0)),
                pl.BlockSpec((SUBLANES, tm), lambda i: (0, i)),
                pl.BlockSpec((1, d), lambda i: (0, 0)),
                pl.BlockSpec((1, 1, d), lambda i: (((i + blk) * tm) // rows_per_group, 0, 0))]
    wts_t = jnp.zeros((SUBLANES, n), F32).at[:TOP_K].set(wts.T)
    args = [x, y0, y1, wts_t, g.reshape(1, d), gt]
    aliases = {}
    if out_buf is not None:
        aliases = {len(args): 0}
        in_specs.append(pl.BlockSpec(memory_space=pl.ANY))
        args.append(out_buf)
    return pl.pallas_call(
        _combine_resnorm_kernel, grid=(n // tm,), in_specs=in_specs,
        out_specs=pl.BlockSpec((tm, d), xrow),
        out_shape=jax.ShapeDtypeStruct((m, d), F32),
        input_output_aliases=aliases,
        compiler_params=_cparams(("parallel",)), name="combine_resnorm",
    )(*args)


LOG2E = 1.0 / math.log(2.0)


def _attn_heads(q, k, v, bias, sink_ref, o_ref):
    qs = (q * (ATT_SCALE * LOG2E)).astype(BF16)
    ones = jnp.ones((v.shape[0], HEAD_DIM), BF16)
    heads = range(ATT_HEADS)
    kh = [_hslice(k, h) for h in range(ATT_KV_HEADS)]
    v1 = [jnp.concatenate([_hslice(v, h), ones], axis=1) for h in range(ATT_KV_HEADS)]
    sink2 = [sink_ref[hd] * LOG2E for hd in heads]
    s = [lax.dot_general(_hslice(qs, hd), kh[hd // ATT_GROUP], (((1,), (1,)), ((), ())),
                         preferred_element_type=F32) for hd in heads]
    if bias is not None:
        s = [sh + bias for sh in s]
    m = [jnp.maximum(jnp.max(s[hd], axis=-1, keepdims=True), sink2[hd]) for hd in heads]
    p = [jnp.exp2(s[hd] - m[hd]).astype(BF16) for hd in heads]
    pv = [jnp.dot(p[hd], v1[hd // ATT_GROUP], preferred_element_type=F32) for hd in heads]
    outs = [pv[hd][:, :HEAD_DIM] / (pv[hd][:, HEAD_DIM:] + jnp.exp2(sink2[hd] - m[hd])) for hd in heads]
    o_ref[0] = jnp.concatenate(outs, axis=-1)


def _hslice(t, h):
    return t[:, h * HEAD_DIM:(h + 1) * HEAD_DIM]


def _win_attn_kernel(sink_ref, q_ref, qp_ref, cq_ref, sq_ref,
                     k0_ref, k1_ref, k2_ref, kp0_ref, kp1_ref, kp2_ref,
                     v0_ref, v1_ref, v2_ref, c0_ref, c1_ref, c2_ref, s0_ref, s1_ref, s2_ref,
                     kc_ref, vc_ref, o_ref, *, nb):
    j = pl.program_id(1)
    q = q_ref[0] * cq_ref[...] + qp_ref[0] * sq_ref[...]
    kb = jnp.concatenate([k0_ref[0] * c0_ref[...] + kp0_ref[0] * s0_ref[...],
                          k1_ref[0] * c1_ref[...] + kp1_ref[0] * s1_ref[...],
                          k2_ref[0] * c2_ref[...] + kp2_ref[0] * s2_ref[...]], axis=0).astype(BF16)
    vb = jnp.concatenate([v0_ref[0], v1_ref[0], v2_ref[0]], axis=0).astype(BF16)
    n_ctx = kc_ref.shape[1]
    k = jnp.concatenate([kc_ref[0].astype(BF16), kb], axis=0)
    v = jnp.concatenate([vc_ref[0].astype(BF16), vb], axis=0)
    n = n_ctx + 3 * ATT_BLOCK
    qi = lax.broadcasted_iota(jnp.int32, (ATT_BLOCK, n), 0) + ATT_BLOCK
    ki = lax.broadcasted_iota(jnp.int32, (ATT_BLOCK, n), 1) - n_ctx
    lo = jnp.where(j == 0, ATT_BLOCK, 0)
    hi = jnp.where(j == nb - 1, 2 * ATT_BLOCK, 3 * ATT_BLOCK)
    far = jnp.where(ki < lo, 4 * ATT_BLOCK, 0) + jnp.where(ki >= hi, 4 * ATT_BLOCK, 0)
    seen = jnp.where(ki < 0, 0, jnp.abs(qi - ki) + far) <= WINDOW
    bias = jnp.where(seen, 0.0, -jnp.inf).astype(F32)
    _attn_heads(q, k, v, bias, sink_ref, o_ref)


def _ctx_attn_kernel(sink_ref, q_ref, kc_ref, vc_ref, o_ref):
    _attn_heads(q_ref[0], kc_ref[0].astype(BF16), vc_ref[0].astype(BF16), None, sink_ref, o_ref)


def windowed_attention(zl3, zc3, cos_t, sin_t, sink):
    b, t, _ = zl3.shape
    l = zc3.shape[1]
    nb = t // ATT_BLOCK
    kcol, kpcol, vcol = ZK // KV_W, ZKP // KV_W, ZV // KV_W
    prev = lambda j: jnp.maximum(j - 1, 0)
    nxt = lambda j: jnp.minimum(j + 1, nb - 1)

    def zspec(width, rowf, col):
        return pl.BlockSpec((1, ATT_BLOCK, width), lambda bi, j: (bi, rowf(j), col))

    def tspec(width, rowf):
        return pl.BlockSpec((ATT_BLOCK, width), lambda bi, j: (rowf(j), 0))

    same = lambda j: j
    in_specs = [pl.BlockSpec(memory_space=pltpu.SMEM),
                zspec(Q_W, same, ZQ // Q_W), zspec(Q_W, same, ZQP // Q_W), tspec(Q_W, same), tspec(Q_W, same)]
    args = [sink, zl3, zl3, cos_t, sin_t]
    for col in (kcol, kpcol, vcol):
        in_specs += [zspec(KV_W, prev, col), zspec(KV_W, same, col), zspec(KV_W, nxt, col)]
        args += [zl3, zl3, zl3]
    for tab in (cos_t, sin_t):
        in_specs += [tspec(KV_W, prev), tspec(KV_W, same), tspec(KV_W, nxt)]
        args += [tab, tab, tab]
    in_specs += [pl.BlockSpec((1, l, KV_W), lambda bi, j: (bi, 0, kcol)),
                 pl.BlockSpec((1, l, KV_W), lambda bi, j: (bi, 0, vcol))]
    args += [zc3, zc3]
    return pl.pallas_call(
        functools.partial(_win_attn_kernel, nb=nb),
        grid=(b, nb), in_specs=in_specs,
        out_specs=pl.BlockSpec((1, ATT_BLOCK, Q_W), lambda bi, j: (bi, j, 0)),
        out_shape=jax.ShapeDtypeStruct((b, t, Q_W), F32),
        compiler_params=_cparams(("parallel", "parallel")), name="win_attn",
    )(*args)


def context_attention(zc3, sink):
    b, l, _ = zc3.shape
    return pl.pallas_call(
        _ctx_attn_kernel, grid=(b,),
        in_specs=[pl.BlockSpec(memory_space=pltpu.SMEM),
                  pl.BlockSpec((1, l, Q_W), lambda bi: (bi, 0, ZQ // Q_W)),
                  pl.BlockSpec((1, l, KV_W), lambda bi: (bi, 0, ZK // KV_W)),
                  pl.BlockSpec((1, l, KV_W), lambda bi: (bi, 0, ZV // KV_W))],
        out_specs=pl.BlockSpec((1, l, Q_W), lambda bi: (bi, 0, 0)),
        out_shape=jax.ShapeDtypeStruct((b, l, Q_W), F32),
        compiler_params=_cparams(("parallel",)), name="ctx_attn",
    )(sink, zc3, zc3, zc3)


HALO = 8


def _head_sums(x, e_ref):
    hi = x.astype(BF16)
    lo = (x - hi.astype(F32)).astype(BF16)
    return (jnp.dot(hi, e_ref[...], preferred_element_type=F32)
            + jnp.dot(lo, e_ref[...], preferred_element_type=F32))


def _shift_lerp_tile(cur_ref, prev_ref, next_ref, mu_ref, buf_ref, first, last):
    tt = cur_ref.shape[1]
    z = cur_ref[0]
    buf_ref[0:HALO] = jnp.where(first, 0.0, prev_ref[0])
    buf_ref[HALO:HALO + tt] = z
    buf_ref[HALO + tt:2 * HALO + tt] = jnp.where(last, 0.0, next_ref[0])
    nb = buf_ref[HALO - 1:HALO - 1 + tt] + buf_ref[HALO + 1:HALO + 1 + tt]
    return z + mu_ref[...] * (0.5 * nb - z)


def _split_bf16(a):
    hi = a.astype(BF16)
    return hi, (a - hi.astype(F32)).astype(BF16)


def _hi_lo(w):
    return jnp.stack(_split_bf16(w))


def _dot_3pass(a, b):
    a_hi, a_lo = _split_bf16(a)
    return (jnp.dot(a_hi, b[0], preferred_element_type=F32) + jnp.dot(a_lo, b[0], preferred_element_type=F32)
            + jnp.dot(a_hi, b[1], preferred_element_type=F32))


def _rwkv_prep_kernel(*refs, has_vres, n_alias):
    (wa_c, wa_p, wa_n, r_c, r_p, r_n, k_c, k_p, k_n, v_c, v_p, v_n, g_c, g_p, g_n,
     mu_wa, mu_r, mu_k, mu_v, mu_g, w0_ref, w2_ref, a0_ref, a2_ref, g2_ref, kk_ref, ka_ref, rk_ref, e_ref) = refs[:29]
    pos = 29
    if has_vres:
        vf_ref, v0_ref, v1_ref, v2_ref = refs[pos:pos + 4]
        pos += 4
    pos += n_alias
    (rk_o, vk_o, w_o, an_o, bv_o, g_o) = refs[pos:pos + 6]
    pos += 6
    if not has_vres:
        vin_o = refs[pos]
        pos += 1
    buf_wa, buf_r, buf_k, buf_v, buf_g = refs[pos:pos + 5]

    i = pl.program_id(1)
    first = i == 0
    last = i == pl.num_programs(1) - 1
    zwa = _shift_lerp_tile(wa_c, wa_p, wa_n, mu_wa, buf_wa, first, last)
    r = _shift_lerp_tile(r_c, r_p, r_n, mu_r, buf_r, first, last)
    k = _shift_lerp_tile(k_c, k_p, k_n, mu_k, buf_k, first, last)
    v = _shift_lerp_tile(v_c, v_p, v_n, mu_v, buf_v, first, last)
    zg = _shift_lerp_tile(g_c, g_p, g_n, mu_g, buf_g, first, last)

    if has_vres:
        mix = _dot_3pass(_dot_3pass(v, v1_ref[...]), v2_ref[...])
        v = v + (vf_ref[0] - v) * jax.nn.sigmoid(v0_ref[...] + mix)
    else:
        vin_o[0] = v
    kk = k * kk_ref[...]
    kk = kk / jnp.maximum(jnp.sqrt(_head_sums(kk * kk, e_ref)), 1e-12)
    tz = jnp.tanh(zwa)
    ksum = None
    for d in range(2):
        w_log = -jax.nn.softplus(-(w0_ref[d:d + 1, :] + _dot_3pass(tz, w2_ref[d]))) - 0.5
        w_o[d, 0] = (-jnp.exp(w_log)).astype(w_o.dtype)
        a = jax.nn.sigmoid(a0_ref[d:d + 1, :] + _dot_3pass(zwa, a2_ref[d]))
        an_o[d, 0] = (-a).astype(an_o.dtype)
        kd = k * (1.0 + (a - 1.0) * ka_ref[...])
        ksum = kd if ksum is None else ksum + kd
    rk_o[0, 0] = r.astype(rk_o.dtype)
    rk_o[1, 0] = k.astype(rk_o.dtype)
    vk_o[0, 0] = v.astype(vk_o.dtype)
    vk_o[1, 0] = kk.astype(vk_o.dtype)
    bv_o[0] = _head_sums(r * ksum * rk_ref[...], e_ref) * v
    g_o[0] = _dot_3pass(jax.nn.sigmoid(zg), g2_ref[...])


def _head_ones():
    idx = jnp.arange(RWKV_W) // RWKV_HEAD
    return (idx[:, None] == idx[None, :]).astype(BF16)


def rwkv_prep(z3, lp, v_first, ttot, row_off, pair_bufs):
    b, tx, _ = z3.shape
    tt = _pick_tile(math.gcd(tx, row_off) if row_off else tx, (256, 128, 64, 32, 16, 8))
    nh = tt // HALO
    nblk = tx // HALO
    has_vres = v_first is not None
    lanes = 128

    def trio(width, col):
        return [pl.BlockSpec((1, tt, width), lambda bi, i: (bi, i, col)),
                pl.BlockSpec((1, HALO, width), lambda bi, i: (bi, jnp.maximum(i * nh - 1, 0), col)),
                pl.BlockSpec((1, HALO, width), lambda bi, i: (bi, jnp.minimum((i + 1) * nh, nblk - 1), col))]

    def full(shape):
        return pl.BlockSpec(shape, lambda bi, i: (0,) * len(shape))

    mu = lp['mu']
    o = 3 * RWKV_W
    nwa = 2 * DECAY_RANK + 2 * ICLR_RANK
    mu_g = jnp.zeros((lanes,), F32).at[:GATE_RANK].set(mu[o + nwa:])
    w2f = jnp.zeros((2, lanes, RWKV_W), F32)
    a2f = jnp.zeros((2, lanes, RWKV_W), F32)
    for d in range(2):
        w2f = w2f.at[d, d * DECAY_RANK:(d + 1) * DECAY_RANK].set(lp['w2'][d])
        a2f = a2f.at[d, 2 * DECAY_RANK + d * ICLR_RANK:2 * DECAY_RANK + (d + 1) * ICLR_RANK].set(lp['a2'][d])
    g2f = jnp.zeros((lanes, RWKV_W), F32).at[:GATE_RANK].set(lp['g2'])
    row = lambda a: a.reshape(1, -1)
    in_specs = (trio(lanes, ZWA // lanes) + trio(RWKV_W, ZR // RWKV_W) + trio(RWKV_W, ZRK // RWKV_W)
                + trio(RWKV_W, ZRV // RWKV_W) + trio(lanes, ZZG // lanes)
                + [full((1, lanes)), full((1, RWKV_W)), full((1, RWKV_W)), full((1, RWKV_W)), full((1, lanes)),
                   full((2, RWKV_W)), full((2, 2, lanes, RWKV_W)), full((2, RWKV_W)), full((2, 2, lanes, RWKV_W)),
                   full((2, lanes, RWKV_W)), full((1, RWKV_W)), full((1, RWKV_W)), full((1, RWKV_W)),
                   full((RWKV_W, RWKV_W))])
    per_dir = lambda w: jnp.stack([_hi_lo(w[0]), _hi_lo(w[1])])
    args = [z3] * 15 + [row(mu[o:o + nwa]), row(mu[:RWKV_W]), row(mu[RWKV_W:2 * RWKV_W]), row(mu[2 * RWKV_W:o]),
                        row(mu_g), lp['w0'], per_dir(w2f), lp['a0'], per_dir(a2f), _hi_lo(g2f),
                        row(lp['k_k']), row(lp['k_a']), row(lp['r_k']), _head_ones()]
    tok = pl.BlockSpec((1, tt, RWKV_W), lambda bi, i: (bi, i, 0))
    if has_vres:
        rank = lp['v1'].shape[1]
        v1f = jnp.zeros((RWKV_W, lanes), F32).at[:, :rank].set(lp['v1'])
        v2f = jnp.zeros((lanes, RWKV_W), F32).at[:rank].set(lp['v2'])
        in_specs += [tok, full((1, RWKV_W)), full((2, RWKV_W, lanes)), full((2, lanes, RWKV_W))]
        args += [v_first, row(lp['v0']), _hi_lo(v1f), _hi_lo(v2f)]
    n_alias = 0 if pair_bufs is None else len(pair_bufs)
    aliases = {}
    if pair_bufs is not None:
        aliases = {len(args) + n: n for n in range(n_alias)}
        in_specs += [pl.BlockSpec(memory_space=pl.ANY)] * n_alias
        args += list(pair_bufs)
    blk_off = row_off // tt
    pair = pl.BlockSpec((2, 1, tt, RWKV_W), lambda bi, i: (0, bi, i + blk_off, 0))
    pair_shape = jax.ShapeDtypeStruct((2, b, ttot, RWKV_W), BF16)
    tok_shape = jax.ShapeDtypeStruct((b, tx, RWKV_W), F32)
    n_tok = 2 if has_vres else 3
    outs = pl.pallas_call(
        functools.partial(_rwkv_prep_kernel, has_vres=has_vres, n_alias=n_alias),
        grid=(b, tx // tt), in_specs=in_specs,
        out_specs=[pair] * 4 + [tok] * n_tok,
        out_shape=[pair_shape] * 4 + [tok_shape] * n_tok,
        input_output_aliases=aliases,
        scratch_shapes=[pltpu.VMEM((tt + 2 * HALO, lanes), F32)] + [pltpu.VMEM((tt + 2 * HALO, RWKV_W), F32)] * 3
                       + [pltpu.VMEM((tt + 2 * HALO, lanes), F32)],
        compiler_params=_cparams(("parallel", "parallel")), name="rwkv_prep",
    )(*args)
    res = {'pairs': tuple(outs[:4]), 'bv': outs[4], 'g': outs[5]}
    if not has_vres:
        res['v_in'] = outs[6]
    return res


CONV_HALO = 16
CONV_ROWS = 32


def _conv_kernel(a_c, a_p, a_n, b_c, b_p, b_n, w_ref, bias_ref, lg_ref, lb_ref, o_ref, buf_ref, win_ref):
    i = pl.program_id(1)
    first = i == 0
    last = i == pl.num_programs(1) - 1
    tt = a_c.shape[1]
    glu = lambda a, b: a * jax.nn.sigmoid(b)
    buf_ref[0:CONV_HALO] = jnp.where(first, 0.0, glu(a_p[0], b_p[0]))
    buf_ref[CONV_HALO:CONV_HALO + tt] = glu(a_c[0], b_c[0])
    buf_ref[CONV_HALO + tt:2 * CONV_HALO + tt] = jnp.where(last, 0.0, glu(a_n[0], b_n[0]))
    half = CONV_K // 2
    span = CONV_ROWS + ((CONV_K - 1) // SUBLANES) * SUBLANES
    for r0 in range(0, tt, CONV_ROWS):
        base = CONV_HALO - half + r0
        acc = bias_ref[...]
        for ph in range(SUBLANES):
            win_ref[...] = buf_ref[base + ph:base + ph + span]
            for j in range(ph, CONV_K, SUBLANES):
                acc = acc + win_ref[j - ph:j - ph + CONV_ROWS] * w_ref[j:j + 1, :]
        m = jnp.mean(acc, axis=-1, keepdims=True)
        cen = acc - m
        var = jnp.mean(cen * cen, axis=-1, keepdims=True)
        y = cen * lax.rsqrt(var + CONV_LN_EPS) * lg_ref[...] + lb_ref[...]
        o_ref[0, r0:r0 + CONV_ROWS, :] = y * jax.nn.sigmoid(y)


def conv_module(z3, lp):
    b, tx, _ = z3.shape
    tt = _pick_tile(tx, (256, 128, 64, 32))
    nh = tt // CONV_HALO
    nblk = tx // CONV_HALO

    def trio(col):
        return [pl.BlockSpec((1, tt, CONV_W), lambda bi, i: (bi, i, col)),
                pl.BlockSpec((1, CONV_HALO, CONV_W), lambda bi, i: (bi, jnp.maximum(i * nh - 1, 0), col)),
                pl.BlockSpec((1, CONV_HALO, CONV_W), lambda bi, i: (bi, jnp.minimum((i + 1) * nh, nblk - 1), col))]

    kpad = -(-CONV_K // SUBLANES) * SUBLANES
    wpad = jnp.zeros((kpad, CONV_W), F32).at[:CONV_K].set(lp['dw_w'])
    vec = pl.BlockSpec((1, CONV_W), lambda bi, i: (0, 0))
    return pl.pallas_call(
        _conv_kernel, grid=(b, tx // tt),
        in_specs=trio(ZCA // CONV_W) + trio(ZCB // CONV_W)
                 + [pl.BlockSpec((kpad, CONV_W), lambda bi, i: (0, 0)), vec, vec, vec],
        out_specs=pl.BlockSpec((1, tt, CONV_W), lambda bi, i: (bi, i, 0)),
        out_shape=jax.ShapeDtypeStruct((b, tx, CONV_W), F32),
        scratch_shapes=[pltpu.VMEM((tt + 2 * CONV_HALO, CONV_W), F32),
                        pltpu.VMEM((CONV_ROWS + ((CONV_K - 1) // SUBLANES) * SUBLANES, CONV_W), F32)],
        compiler_params=_cparams(("parallel", "parallel")), name="conv_module",
    )(z3, z3, z3, z3, z3, z3, wpad, lp['dw_b'].reshape(1, -1), lp['cln_g'].reshape(1, -1), lp['cln_b'].reshape(1, -1))


SUBLANES = 8
DECAY_FOLD = 16


def _wkv_scan_kernel(rk_f, vk_f, w_f, an_f, rk_b, vk_b, w_b, an_b, ka_ref, yf_ref, yb_ref, s_ref, *, tt):
    @pl.when(pl.program_id(0) == 0)
    def _():
        s_ref[...] = jnp.zeros_like(s_ref)

    nk, _, lanes = s_ref.shape
    half = lanes // 2
    fwd = lax.broadcasted_iota(jnp.int32, (nk, lanes), 1) < half
    ka = ka_ref[...]
    fold = math.gcd(tt, DECAY_FOLD)

    def group(gi, carry):
        def step(ti, gam):
            t = gi * fold + ti
            tb = tt - 1 - t
            rkf, rkb, vkf, vkb = (a.astype(F32) for a in (rk_f[t], rk_b[tb], vk_f[t], vk_b[tb]))
            r = jnp.where(fwd, rkf, pltpu.roll(rkb, half, 1))
            k = jnp.where(fwd, pltpu.roll(rkf, half, 1), rkb)
            v = jnp.where(fwd, vkf, pltpu.roll(vkb, half, 1))
            kk = jnp.where(fwd, pltpu.roll(vkf, half, 1), vkb)
            w = jnp.exp(jnp.where(fwd, w_f[t].astype(F32), w_b[tb].astype(F32)))
            an = jnp.where(fwd, an_f[t].astype(F32), an_b[tb].astype(F32))
            gam_new = gam * w
            inv = 1.0 / gam_new
            kks = kk * gam
            kds = k * (1.0 - (an + 1.0) * ka) * inv
            bbs = kk * an * inv
            rs = r * gam_new
            row = lambda a, ki: a[ki:ki + 1, :]
            sa = [s_ref[0] * row(kks, 0), s_ref[1] * row(kks, 1)]
            for ki in range(2, nk):
                sa[ki % 2] = sa[ki % 2] + s_ref[ki] * row(kks, ki)
            sa = sa[0] + sa[1]
            ys = [None, None]
            for ki in range(nk):
                s_new = s_ref[ki] + sa * row(bbs, ki) + v * row(kds, ki)
                s_ref[ki] = s_new
                contrib = s_new * row(rs, ki)
                ys[ki % 2] = contrib if ys[ki % 2] is None else ys[ki % 2] + contrib
            y = ys[0] + ys[1]
            yf_ref[t] = y
            yb_ref[tb] = y
            return gam_new

        gam = lax.fori_loop(0, fold, step, jnp.ones((nk, lanes), F32), unroll=4)
        for ki in range(nk):
            s_ref[ki] = s_ref[ki] * gam[ki:ki + 1, :]
        return carry

    lax.fori_loop(0, tt // fold, group, 0)


def wkv_scan(rk, vk, w, an, ka, n_ctx_steps):
    ttot, n, ch = rk.shape
    tt = _pick_tile(math.gcd(n_ctx_steps, ttot - n_ctx_steps), (32, 16, 8, 4, 2, 1))
    nt, nc = ttot // tt, n_ctx_steps // tt
    fspec = pl.BlockSpec((tt, n, ch), lambda i: (i, 0, 0))
    bspec = pl.BlockSpec((tt, n, ch), lambda i: (jnp.where(i < nc, nc - 1 - i, nt - 1 - (i - nc)), 0, 0))
    return pl.pallas_call(
        functools.partial(_wkv_scan_kernel, tt=tt),
        grid=(nt,), in_specs=[fspec] * 4 + [bspec] * 4 + [pl.BlockSpec((n, ch), lambda i: (0, 0))],
        out_specs=[fspec, bspec],
        out_shape=[jax.ShapeDtypeStruct((ttot, n, ch), F32)] * 2,
        scratch_shapes=[pltpu.VMEM((n, n, ch), F32)],
        compiler_params=_cparams(("arbitrary",)), name="wkv_scan",
    )(rk, vk, w, an, rk, vk, w, an, ka)


def _merge_kernel(x_ref, att_ref, y_ref, bv_ref, rg_ref, cv_ref, g0_ref, g1_ref, g2_ref,
                  lg_ref, lb_ref, e_ref, wa_ref, wr_ref, wc_ref, wo_ref, ng_ref, gt_ref, o_ref):
    def proj(a, w_ref):
        return jnp.dot(a.astype(BF16), w_ref[...], preferred_element_type=F32)

    y = y_ref[...]
    cen = y - _head_sums(y, e_ref) * (1.0 / RWKV_HEAD)
    var = _head_sums(cen * cen, e_ref) * (1.0 / RWKV_HEAD)
    rw = (cen * lax.rsqrt(var + LNX_EPS) * lg_ref[...] + lb_ref[...] + bv_ref[...]) * rg_ref[...]
    m = (jax.nn.sigmoid(g0_ref[...]) * proj(att_ref[...], wa_ref)
         + jax.nn.sigmoid(g1_ref[...]) * proj(rw, wr_ref)
         + jax.nn.sigmoid(g2_ref[...]) * proj(cv_ref[...], wc_ref))
    ml = proj(m, wo_ref)
    ms = jnp.mean(ml * ml, axis=-1, keepdims=True)
    o_ref[...] = x_ref[...] + gt_ref[0] * (ml * lax.rsqrt(ms + NORM_EPS) * ng_ref[...])


def merge_out(x, att, y, bv, rg, cv, z, lnx_g, lnx_b, wa, wr, wc, wo, ng, gt, rows_per_group):
    m, d = x.shape
    tm = _pick_tile(rows_per_group, (512, 256, 128, 64, 32, 16, 8))
    row = lambda i: (i, 0)
    const = lambda i: (0, 0)
    gcol = ZG // d
    br = pl.BlockSpec((tm, RWKV_W), row)
    return pl.pallas_call(
        _merge_kernel, grid=(m // tm,),
        in_specs=[pl.BlockSpec((tm, d), row),
                  pl.BlockSpec((tm, Q_W), row), br, br, br, pl.BlockSpec((tm, CONV_W), row),
                  pl.BlockSpec((tm, d), lambda i: (i, gcol)),
                  pl.BlockSpec((tm, d), lambda i: (i, gcol + 1)),
                  pl.BlockSpec((tm, d), lambda i: (i, gcol + 2)),
                  pl.BlockSpec((1, RWKV_W), const), pl.BlockSpec((1, RWKV_W), const),
                  pl.BlockSpec((RWKV_W, RWKV_W), const),
                  pl.BlockSpec((Q_W, d), const), pl.BlockSpec((RWKV_W, d), const), pl.BlockSpec((CONV_W, d), const),
                  pl.BlockSpec((d, d), const), pl.BlockSpec((1, d), const),
                  pl.BlockSpec((1, 1, d), lambda i: ((i * tm) // rows_per_group, 0, 0))],
        out_specs=pl.BlockSpec((tm, d), row),
        out_shape=jax.ShapeDtypeStruct((m, d), F32),
        compiler_params=_cparams(("parallel",)), name="merge_out",
    )(x, att, y, bv, rg, cv, z, z, z, lnx_g.reshape(1, -1), lnx_b.reshape(1, -1), _head_ones(),
      wa, wr, wc, wo, ng.reshape(1, d), gt)


def _swiglu_kernel(te_ref, nu_ref, h_ref, wg_ref, wu_ref, wd_ref, o_ref):
    f = pl.program_id(1)

    used = pl.program_id(0) < nu_ref[0]

    @pl.when(jnp.logical_and(used, f == 0))
    def _():
        o_ref[...] = jnp.zeros_like(o_ref)

    @pl.when(used)
    def _():
        h = h_ref[...]
        hg = jnp.dot(h, wg_ref[0].astype(BF16), preferred_element_type=F32)
        hu = jnp.dot(h, wu_ref[0].astype(BF16), preferred_element_type=F32)
        act = (hg * jax.nn.sigmoid(hg) * hu).astype(BF16)
        o_ref[...] += jnp.dot(act, wd_ref[0].astype(BF16), preferred_element_type=F32)


def grouped_swiglu(h, tile_expert, n_used, wg, wu, wd, tm):
    p, d = h.shape
    ff = wg.shape[2]
    tf = _pick_tile(ff, (512, 1408, 256, 128))
    nf = ff // tf
    ti = lambda i, nu: jnp.minimum(i, nu[0] - 1)
    fi = lambda i, f, nu: jnp.where(i < nu[0], f, nf - 1)
    grid_spec = pltpu.PrefetchScalarGridSpec(
        num_scalar_prefetch=2, grid=(p // tm, nf),
        in_specs=[pl.BlockSpec((tm, d), lambda i, f, te, nu: (ti(i, nu), 0)),
                  pl.BlockSpec((1, d, tf), lambda i, f, te, nu: (te[ti(i, nu)], 0, fi(i, f, nu))),
                  pl.BlockSpec((1, d, tf), lambda i, f, te, nu: (te[ti(i, nu)], 0, fi(i, f, nu))),
                  pl.BlockSpec((1, tf, d), lambda i, f, te, nu: (te[ti(i, nu)], fi(i, f, nu), 0))],
        out_specs=pl.BlockSpec((tm, d), lambda i, f, te, nu: (ti(i, nu), 0)))
    return pl.pallas_call(
        _swiglu_kernel, grid_spec=grid_spec,
        out_shape=jax.ShapeDtypeStruct((p, d), F32),
        compiler_params=_cparams(("arbitrary", "arbitrary")), name="swiglu",
    )(tile_expert, n_used, h, wg, wu, wd)


def _dense_ffn_kernel(x_ref, g_ref, sc_ref, sh_ref, wg_ref, wu_ref, wd_ref, g2_ref, gt_ref, o_ref, h_ref, acc_ref):
    f = pl.program_id(1)

    @pl.when(f == 0)
    def _():
        h_ref[...] = _norm_mod(x_ref[...], g_ref[...], sc_ref[0], sh_ref[0]).astype(BF16)
        acc_ref[...] = jnp.zeros_like(acc_ref)

    h = h_ref[...]
    hg = jnp.dot(h, wg_ref[...], preferred_element_type=F32)
    hu = jnp.dot(h, wu_ref[...], preferred_element_type=F32)
    act = (hg * jax.nn.sigmoid(hg) * hu).astype(BF16)
    acc_ref[...] += jnp.dot(act, wd_ref[...], preferred_element_type=F32)

    @pl.when(f == pl.num_programs(1) - 1)
    def _():
        y = acc_ref[...]
        ms = jnp.mean(y * y, axis=-1, keepdims=True)
        o_ref[...] = x_ref[...] + gt_ref[0] * (y * lax.rsqrt(ms + NORM_EPS) * g2_ref[...])


def dense_ffn(x, g_in, sc, sh, wg, wu, wd, g_out, gt, rows_per_group):
    m, d = x.shape
    ff = wg.shape[1]
    tm = _pick_tile(rows_per_group, (512, 256, 128, 64, 32, 16, 8))
    tf = _pick_tile(ff, (1408, 512, 256, 128))
    grp = lambda i, f: ((i * tm) // rows_per_group, 0, 0)
    vec = pl.BlockSpec((1, d), lambda i, f: (0, 0))
    return pl.pallas_call(
        _dense_ffn_kernel, grid=(m // tm, ff // tf),
        in_specs=[pl.BlockSpec((tm, d), lambda i, f: (i, 0)), vec,
                  pl.BlockSpec((1, 1, d), grp), pl.BlockSpec((1, 1, d), grp),
                  pl.BlockSpec((d, tf), lambda i, f: (0, f)), pl.BlockSpec((d, tf), lambda i, f: (0, f)),
                  pl.BlockSpec((tf, d), lambda i, f: (f, 0)), vec, pl.BlockSpec((1, 1, d), grp)],
        out_specs=pl.BlockSpec((tm, d), lambda i, f: (i, 0)),
        out_shape=jax.ShapeDtypeStruct((m, d), F32),
        scratch_shapes=[pltpu.VMEM((tm, d), BF16), pltpu.VMEM((tm, d), F32)],
        compiler_params=_cparams(("parallel", "arbitrary")), name="dense_ffn",
    )(x, g_in.reshape(1, d), sc, sh, wg, wu, wd, g_out.reshape(1, d), gt)


def _rope_tables(t):
    rows = t // GRID_W
    row = jnp.broadcast_to(jnp.arange(rows)[:, None], (rows, GRID_W)).reshape(-1).astype(F32)
    col = jnp.broadcast_to(jnp.arange(GRID_W)[None, :], (rows, GRID_W)).reshape(-1).astype(F32)
    inv = ROPE_BASE ** (-jnp.arange(0, AXIS_DIM, 2, dtype=F32) / AXIS_DIM)
    ar = row[:, None] * inv[None, :]
    ac = col[:, None] * inv[None, :]
    cos64 = jnp.concatenate([jnp.cos(ar), jnp.cos(ar), jnp.cos(ac), jnp.cos(ac)], axis=-1)
    sin64 = jnp.concatenate([jnp.sin(ar), jnp.sin(ar), jnp.sin(ac), jnp.sin(ac)], axis=-1)
    return jnp.tile(cos64, (1, ATT_HEADS)), jnp.tile(sin64, (1, ATT_HEADS))


def _rope_partner(w):
    d = w.shape[0]
    q = AXIS_DIM // 2
    wh = w.reshape(d, -1, 4, q)
    return jnp.stack([-wh[:, :, 1], wh[:, :, 0], -wh[:, :, 3], wh[:, :, 2]], axis=2).reshape(w.shape)


def _build_w_ext(w_in):
    d = w_in.shape[0]
    wq = w_in[:, OFF_Q:OFF_K]
    wk = w_in[:, OFF_K:OFF_V]
    o = OFF_R + 3 * RWKV_W
    parts = [(ZQ, wq), (ZQP, _rope_partner(wq)), (ZK, wk), (ZKP, _rope_partner(wk)),
             (ZV, w_in[:, OFF_V:OFF_R]),
             (ZWA, w_in[:, o:o + 2 * DECAY_RANK + 2 * ICLR_RANK]),
             (ZCA, w_in[:, OFF_CONV:OFF_CONV + CONV_W]), (ZCB, w_in[:, OFF_CONV + CONV_W:OFF_GATE]),
             (ZR, w_in[:, OFF_R:OFF_R + RWKV_W]), (ZRK, w_in[:, OFF_R + RWKV_W:OFF_R + 2 * RWKV_W]),
             (ZRV, w_in[:, OFF_R + 2 * RWKV_W:OFF_R + 3 * RWKV_W]),
             (ZG, w_in[:, OFF_GATE:IN_COLS]),
             (ZZG, w_in[:, OFF_CONV - GATE_RANK:OFF_CONV])]
    parts.sort(key=lambda p: p[0])
    cols, pos = [], 0
    for off, wpart in parts:
        if off > pos:
            cols.append(jnp.zeros((d, off - pos), w_in.dtype))
        cols.append(wpart)
        pos = off + wpart.shape[1]
    if pos < NZ:
        cols.append(jnp.zeros((d, NZ - pos), w_in.dtype))
    return jnp.concatenate(cols, axis=1).astype(BF16)


def _chain_layout(pair):
    _, bsz, ttot, _ = pair.shape
    both = pair.reshape(2, bsz, ttot, RWKV_HEADS, RWKV_HEAD)
    return jnp.transpose(both, (2, 4, 0, 1, 3)).reshape(ttot, RWKV_HEAD, 2 * bsz * RWKV_HEADS)


def _bidirectional_wkv(pairs, k_a, l):
    bsz = pairs[0].shape[1]
    rk, vk, w, an = (_chain_layout(p) for p in pairs)
    ka = jnp.tile(k_a.reshape(RWKV_HEADS, RWKV_HEAD).T, (1, 2 * bsz))
    yf, yb = wkv_scan(rk, vk, w, an, ka, l)
    half = bsz * RWKV_HEADS
    y = (yf[:, :, :half] + yb[:, :, half:]).reshape(-1, RWKV_HEAD, bsz, RWKV_HEADS)
    y = jnp.transpose(y, (2, 0, 3, 1)).reshape(bsz, -1, RWKV_W)
    return y[:, l:], y[:, :l]


def _moe_dispatch(logits, tm):
    m = logits.shape[0]
    top_v, top_i = lax.top_k(logits, TOP_K)
    wts = jax.nn.softmax(top_v, axis=-1)
    e_flat = top_i.reshape(-1)
    n_assign = e_flat.shape[0]
    order = jnp.argsort(e_flat, stable=True).astype(jnp.int32)
    pos = jnp.argsort(order).astype(jnp.int32)
    cnt = jnp.sum((e_flat[:, None] == jnp.arange(N_EXPERTS)[None, :]).astype(jnp.int32), axis=0)
    start = jnp.cumsum(cnt) - cnt
    pcnt = ((cnt + tm - 1) // tm) * tm
    pend = jnp.cumsum(pcnt)
    pstart = pend - pcnt
    dest = (pstart - start)[e_flat] + pos
    p_max = n_assign + N_EXPERTS * tm
    tile_start = jnp.arange(p_max // tm, dtype=jnp.int32) * tm
    tile_expert = jnp.minimum(jnp.searchsorted(pend, tile_start, side='right'), N_EXPERTS - 1).astype(jnp.int32)
    rows = jnp.arange(p_max, dtype=jnp.int32)
    row_expert = tile_expert[rows // tm]
    src = order[jnp.clip(rows - (pstart - start)[row_expert], 0, n_assign - 1)]
    row_token = src // TOP_K
    n_used = (pend[-1:] // tm).astype(jnp.int32)
    return row_token, tile_expert, n_used, dest.reshape(m, TOP_K), wts


def kernel(x, c, ctx, c_ctx, mod_w, mod_b, norm_g, w_in, attn_sink, rwkv_mu, rwkv_w0, rwkv_w2, rwkv_a0, rwkv_a2, rwkv_g2, rwkv_k_k, rwkv_k_a, rwkv_r_k, rwkv_lnx_g, rwkv_lnx_b, vres_v0, vres_w1, vres_w2, conv_dw_w, conv_dw_b, conv_ln_g, conv_ln_b, w_attn_o, w_rwkv_o, w_conv_o, w_out, ffn_wg, ffn_wu, ffn_wd, moe_router, moe_wg, moe_wu, moe_wd):
    bsz, t, d = x.shape
    l = ctx.shape[1]
    depth = mod_w.shape[0]
    n_lat, n_ctx = bsz * t, bsz * l
    cos_t, sin_t = _rope_tables(t)
    xl = x.reshape(n_lat, d)
    xc = ctx.reshape(n_ctx, d)
    n_rows = -(-(bsz + 1) // SUBLANES) * SUBLANES
    cond = jnp.zeros((n_rows, d), F32).at[:bsz].set(jax.nn.silu(c)).at[bsz].set(jax.nn.silu(c_ctx))
    v_first = None
    for i in range(depth):
        need_ctx = i < depth - 1
        mod = matmul_bias_f32(cond, mod_w[i], mod_b[i])
        mod_l = [mod[:bsz, k * d:(k + 1) * d].reshape(bsz, 1, d) for k in range(6)]
        mod_c = [mod[bsz:bsz + 1, k * d:(k + 1) * d].reshape(1, 1, d) for k in range(6)]
        lp = {'mu': rwkv_mu[i], 'w0': rwkv_w0[i], 'w2': rwkv_w2[i], 'a0': rwkv_a0[i], 'a2': rwkv_a2[i],
              'g2': rwkv_g2[i], 'k_k': rwkv_k_k[i], 'k_a': rwkv_k_a[i], 'r_k': rwkv_r_k[i],
              'lnx_g': rwkv_lnx_g[i], 'lnx_b': rwkv_lnx_b[i],
              'dw_w': conv_dw_w[i], 'dw_b': conv_dw_b[i], 'cln_g': conv_ln_g[i], 'cln_b': conv_ln_b[i]}
        if i > 0:
            lp['v0'], lp['v1'], lp['v2'] = vres_v0[i - 1], vres_w1[i - 1], vres_w2[i - 1]
        w_ext = _build_w_ext(w_in[i])
        wa, wr, wc, wo = (w.astype(BF16) for w in (w_attn_o[i], w_rwkv_o[i], w_conv_o[i], w_out[i]))

        zl = nm_matmul(xl, norm_g[i, 0], mod_l[1], mod_l[0], w_ext, t)
        zc = nm_matmul(xc, norm_g[i, 0], mod_c[1], mod_c[0], w_ext, n_ctx)
        zl3 = zl.reshape(bsz, t, NZ)
        zc3 = zc.reshape(bsz, l, NZ)
        att_l = windowed_attention(zl3, zc3, cos_t, sin_t, attn_sink[i])
        p_c = rwkv_prep(zc3, lp, None if v_first is None else v_first[1], l + t, 0, None)
        p_l = rwkv_prep(zl3, lp, None if v_first is None else v_first[0], l + t, l, p_c['pairs'])
        if i == 0:
            v_first = (p_l['v_in'], p_c['v_in'])
        y_l, y_c = _bidirectional_wkv(p_l['pairs'], lp['k_a'], l)
        cv_l = conv_module(zl3, lp)
        flat = lambda a: a.reshape(-1, a.shape[-1])
        xl = merge_out(xl, flat(att_l), flat(y_l), flat(p_l['bv']), flat(p_l['g']), flat(cv_l), zl,
                       lp['lnx_g'], lp['lnx_b'], wa, wr, wc, wo, norm_g[i, 1], mod_l[2], t)
        if need_ctx:
            att_c = context_attention(zc3, attn_sink[i])
            cv_c = conv_module(zc3, lp)
            xc = merge_out(xc, flat(att_c), flat(y_c), flat(p_c['bv']), flat(p_c['g']), flat(cv_c), zc,
                           lp['lnx_g'], lp['lnx_b'], wa, wr, wc, wo, norm_g[i, 1], mod_c[2], n_ctx)

        j = i // 2
        if i % 2 == 0:
            wg, wu, wd = (w[j].astype(BF16) for w in (ffn_wg, ffn_wu, ffn_wd))
            xl = dense_ffn(xl, norm_g[i, 2], mod_l[4], mod_l[3], wg, wu, wd, norm_g[i, 3], mod_l[5], t)
            if need_ctx:
                xc = dense_ffn(xc, norm_g[i, 2], mod_c[4], mod_c[3], wg, wu, wd, norm_g[i, 3], mod_c[5], n_ctx)
        else:
            wg, wu, wd = moe_wg[j], moe_wu[j], moe_wd[j]
            streams = [(xl, mod_l, t)] + ([(xc, mod_c, n_ctx)] if need_ctx else [])
            outs = []
            for xs, ms, rpg in streams:
                n_tok = xs.shape[0]
                h, logits = prenorm(xs, norm_g[i, 2], ms[4], ms[3], rpg, router=moe_router[j])
                n_grp = MOE_GROUPS if n_tok % (MOE_GROUPS * SUBLANES) == 0 else 1
                gsz = n_tok // n_grp
                tm = _pick_tile(gsz, (1024, 512, 256, 128, 64, 32, 16, 8))
                x_new = None
                for gi in range(n_grp):
                    row_token, tile_expert, n_used, dest, wts = _moe_dispatch(logits[gi * gsz:(gi + 1) * gsz], tm)
                    yg = grouped_swiglu(jnp.take(h, row_token + gi * gsz, axis=0), tile_expert, n_used,
                                        wg, wu, wd, tm)
                    x_new = combine_resnorm(xs, jnp.take(yg, dest[:, 0], axis=0), jnp.take(yg, dest[:, 1], axis=0),
                                            wts, norm_g[i, 3], ms[5], rpg, gi * gsz, x_new)
                outs.append(x_new)
            xl = outs[0]
            if need_ctx:
                xc = outs[1]
    return xl.reshape(bsz, t, d)
```

```python
import functools
import math

import jax
import jax.numpy as jnp
from jax import lax
from jax.experimental import pallas as pl
from jax.experimental.pallas import tpu as pltpu

F32 = jnp.float32
BF16 = jnp.bfloat16

D_MODEL = 1024
GRID_W = 64
ATT_HEADS = 8
ATT_KV_HEADS = 2
ATT_GROUP = ATT_HEADS // ATT_KV_HEADS
HEAD_DIM = 64
WINDOW = 128
ATT_BLOCK = 128
ATT_SCALE = 1.0 / math.sqrt(HEAD_DIM)
ROPE_BASE = 10000.0
AXIS_DIM = HEAD_DIM // 2
RWKV_HEADS = 8
RWKV_HEAD = 64
RWKV_W = RWKV_HEADS * RWKV_HEAD
DECAY_RANK = 32
ICLR_RANK = 32
GATE_RANK = 96
LNX_EPS = 64e-5
CONV_W = 512
CONV_K = 31
CONV_LN_EPS = 1e-5
N_BRANCH = 3
N_EXPERTS = 8
TOP_K = 2
NORM_EPS = 1e-6
Q_W = ATT_HEADS * HEAD_DIM
KV_W = ATT_KV_HEADS * HEAD_DIM
OFF_Q = 0
OFF_K = OFF_Q + Q_W
OFF_V = OFF_K + KV_W
OFF_R = OFF_V + KV_W
OFF_CONV = OFF_R + 3 * RWKV_W + 2 * DECAY_RANK + 2 * ICLR_RANK + GATE_RANK
OFF_GATE = OFF_CONV + 2 * CONV_W
IN_COLS = OFF_GATE + N_BRANCH * D_MODEL

ZQ = 0
ZQP = 512
ZK = 1024
ZKP = 1152
ZV = 1280
ZWA = 1408
ZCA = 1536
ZCB = 2048
ZR = 2560
ZRK = 3072
ZRV = 3584
ZG = 4096
ZZG = 7168
NZ = 7680

VMEM_LIMIT = 56 * 1024 * 1024
LANES = 128
MOE_GROUPS = 2


def _cparams(sem):
    return pltpu.CompilerParams(dimension_semantics=sem, vmem_limit_bytes=VMEM_LIMIT)


def _pick_tile(n, candidates):
    for c in candidates:
        if n % c == 0:
            return c
    return n


def _matmul_bias_kernel(a_ref, w_ref, b_ref, o_ref):
    o_ref[...] = jnp.dot(a_ref[...], w_ref[...], preferred_element_type=F32,
                         precision=lax.Precision.HIGHEST) + b_ref[...]


def matmul_bias_f32(a, w, b):
    m, k = a.shape
    n = w.shape[1]
    tn = _pick_tile(n, (1536, 1024, 512, 256, 128))
    return pl.pallas_call(
        _matmul_bias_kernel,
        grid=(n // tn,),
        in_specs=[pl.BlockSpec((m, k), lambda j: (0, 0)),
                  pl.BlockSpec((k, tn), lambda j: (0, j)),
                  pl.BlockSpec((1, tn), lambda j: (0, j))],
        out_specs=pl.BlockSpec((m, tn), lambda j: (0, j)),
        out_shape=jax.ShapeDtypeStruct((m, n), F32),
        compiler_params=_cparams(("parallel",)),
        name="mod_matmul",
    )(a, w, b.reshape(1, n))


def _norm_mod(x, g, sc, sh):
    ms = jnp.mean(x * x, axis=-1, keepdims=True)
    return (x * lax.rsqrt(ms + NORM_EPS) * g) * (1.0 + sc) + sh


def _nm_matmul_kernel(x_ref, g_ref, sc_ref, sh_ref, w_ref, o_ref, h_ref):
    @pl.when(pl.program_id(1) == 0)
    def _():
        h_ref[...] = _norm_mod(x_ref[...], g_ref[...], sc_ref[0], sh_ref[0]).astype(BF16)

    o_ref[...] = jnp.dot(h_ref[...], w_ref[...], preferred_element_type=F32).astype(o_ref.dtype)


def nm_matmul(x, g, sc, sh, w, rows_per_group):
    m, d = x.shape
    n = w.shape[1]
    tm = _pick_tile(rows_per_group, (1024, 512, 256, 128, 64, 32, 16, 8))
    tn = _pick_tile(n, (2560, 1280, 1024, 512, 256, 128))
    grp = lambda i, j: ((i * tm) // rows_per_group, 0, 0)
    return pl.pallas_call(
        _nm_matmul_kernel,
        grid=(m // tm, n // tn),
        in_specs=[pl.BlockSpec((tm, d), lambda i, j: (i, 0)),
                  pl.BlockSpec((1, d), lambda i, j: (0, 0)),
                  pl.BlockSpec((1, 1, d), grp),
                  pl.BlockSpec((1, 1, d), grp),
                  pl.BlockSpec((d, tn), lambda i, j: (0, j))],
        out_specs=pl.BlockSpec((tm, tn), lambda i, j: (i, j)),
        out_shape=jax.ShapeDtypeStruct((m, n), F32),
        scratch_shapes=[pltpu.VMEM((tm, d), BF16)],
        compiler_params=_cparams(("parallel", "arbitrary")),
        name="in_proj",
    )(x, g.reshape(1, d), sc, sh, w)


def _prenorm_router_kernel(x_ref, g_ref, sc_ref, sh_ref, r_ref, h_ref, l_ref):
    h = _norm_mod(x_ref[...], g_ref[...], sc_ref[0], sh_ref[0])
    h_ref[...] = h.astype(h_ref.dtype)
    l_ref[...] = jnp.dot(h, r_ref[...], preferred_element_type=F32, precision=lax.Precision.HIGHEST)


def prenorm(x, g, sc, sh, rows_per_group, router):
    m, d = x.shape
    tm = _pick_tile(rows_per_group, (512, 256, 128, 64, 32, 16, 8))
    grp = lambda i: ((i * tm) // rows_per_group, 0, 0)
    ne = router.shape[1]
    rpad = jnp.zeros((d, LANES), F32).at[:, :ne].set(router)
    h, logits = pl.pallas_call(
        _prenorm_router_kernel, grid=(m // tm,),
        in_specs=[pl.BlockSpec((tm, d), lambda i: (i, 0)),
                  pl.BlockSpec((1, d), lambda i: (0, 0)),
                  pl.BlockSpec((1, 1, d), grp),
                  pl.BlockSpec((1, 1, d), grp),
                  pl.BlockSpec((d, LANES), lambda i: (0, 0))],
        out_specs=[pl.BlockSpec((tm, d), lambda i: (i, 0)), pl.BlockSpec((tm, LANES), lambda i: (i, 0))],
        out_shape=[jax.ShapeDtypeStruct((m, d), BF16), jax.ShapeDtypeStruct((m, LANES), F32)],
        compiler_params=_cparams(("parallel",)), name="prenorm_router",
    )(x, g.reshape(1, d), sc, sh, rpad)
    return h, logits[:, :ne]


def _combine_resnorm_kernel(x_ref, y0_ref, y1_ref, w_ref, g_ref, gt_ref, *rest):
    o_ref = rest[-1]
    w = w_ref[...].T
    f = w[:, 0:1] * y0_ref[...] + w[:, 1:2] * y1_ref[...]
    ms = jnp.mean(f * f, axis=-1, keepdims=True)
    o_ref[...] = x_ref[...] + gt_ref[0] * (f * lax.rsqrt(ms + NORM_EPS) * g_ref[...])


def combine_resnorm(x, y0, y1, wts, g, gt, rows_per_group, row_off, out_buf):
    m, d = x.shape
    n = y0.shape[0]
    tm = _pick_tile(math.gcd(math.gcd(rows_per_group, n), row_off) if row_off else math.gcd(rows_per_group, n),
                    (512, 256, 128, 64, 32, 16, 8))
    blk = row_off // tm
    xrow = lambda i: (i + blk, 0)
    in_specs = [pl.BlockSpec((tm, d), xrow),
                pl.BlockSpec((tm, d), lambda i: (i, 0)), pl.BlockSpec((tm, d), lambda i: (i, 0)),
                pl.BlockSpec((SUBLANES, tm), lambda i: (0, i)),
                pl.BlockSpec((1, d), lambda i: (0, 0)),
                pl.BlockSpec((1, 1, d), lambda i: (((i + blk) * tm) // rows_per_group, 0, 0))]
    wts_t = jnp.zeros((SUBLANES, n), F32).at[:TOP_K].set(wts.T)
    args = [x, y0, y1, wts_t, g.reshape(1, d), gt]
    aliases = {}
    if out_buf is not None:
        aliases = {len(args): 0}
        in_specs.append(pl.BlockSpec(memory_space=pl.ANY))
        args.append(out_buf)
    return pl.pallas_call(
        _combine_resnorm_kernel, grid=(n // tm,), in_specs=in_specs,
        out_specs=pl.BlockSpec((tm, d), xrow),
        out_shape=jax.ShapeDtypeStruct((m, d), F32),
        input_output_aliases=aliases,
        compiler_params=_cparams(("parallel",)), name="combine_resnorm",
    )(*args)


LOG2E = 1.0 / math.log(2.0)


def _attn_heads(q, k, v, bias, sink_ref, o_ref):
    qs = (q * (ATT_SCALE * LOG2E)).astype(BF16)
    ones = jnp.ones((v.shape[0], HEAD_DIM), BF16)
    heads = range(ATT_HEADS)
    kh = [_hslice(k, h) for h in range(ATT_KV_HEADS)]
    v1 = [jnp.concatenate([_hslice(v, h), ones], axis=1) for h in range(ATT_KV_HEADS)]
    sink2 = [sink_ref[hd] * LOG2E for hd in heads]
    s = [lax.dot_general(_hslice(qs, hd), kh[hd // ATT_GROUP], (((1,), (1,)), ((), ())),
                         preferred_element_type=F32) for hd in heads]
    if bias is not None:
        s = [sh + bias for sh in s]
    m = [jnp.maximum(jnp.max(s[hd], axis=-1, keepdims=True), sink2[hd]) for hd in heads]
    p = [jnp.exp2(s[hd] - m[hd]).astype(BF16) for hd in heads]
    pv = [jnp.dot(p[hd], v1[hd // ATT_GROUP], preferred_element_type=F32) for hd in heads]
    outs = [pv[hd][:, :HEAD_DIM] / (pv[hd][:, HEAD_DIM:] + jnp.exp2(sink2[hd] - m[hd])) for hd in heads]
    o_ref[0] = jnp.concatenate(outs, axis=-1)


def _hslice(t, h):
    return t[:, h * HEAD_DIM:(h + 1) * HEAD_DIM]


def _win_attn_kernel(sink_ref, q_ref, qp_ref, cq_ref, sq_ref,
                     k0_ref, k1_ref, k2_ref, kp0_ref, kp1_ref, kp2_ref,
                     v0_ref, v1_ref, v2_ref, c0_ref, c1_ref, c2_ref, s0_ref, s1_ref, s2_ref,
                     kc_ref, vc_ref, o_ref, *, nb):
    j = pl.program_id(1)
    q = q_ref[0] * cq_ref[...] + qp_ref[0] * sq_ref[...]
    kb = jnp.concatenate([k0_ref[0] * c0_ref[...] + kp0_ref[0] * s0_ref[...],
                          k1_ref[0] * c1_ref[...] + kp1_ref[0] * s1_ref[...],
                          k2_ref[0] * c2_ref[...] + kp2_ref[0] * s2_ref[...]], axis=0).astype(BF16)
    vb = jnp.concatenate([v0_ref[0], v1_ref[0], v2_ref[0]], axis=0).astype(BF16)
    n_ctx = kc_ref.shape[1]
    k = jnp.concatenate([kc_ref[0].astype(BF16), kb], axis=0)
    v = jnp.concatenate([vc_ref[0].astype(BF16), vb], axis=0)
    n = n_ctx + 3 * ATT_BLOCK
    qi = lax.broadcasted_iota(jnp.int32, (ATT_BLOCK, n), 0) + ATT_BLOCK
    ki = lax.broadcasted_iota(jnp.int32, (ATT_BLOCK, n), 1) - n_ctx
    lo = jnp.where(j == 0, ATT_BLOCK, 0)
    hi = jnp.where(j == nb - 1, 2 * ATT_BLOCK, 3 * ATT_BLOCK)
    far = jnp.where(ki < lo, 4 * ATT_BLOCK, 0) + jnp.where(ki >= hi, 4 * ATT_BLOCK, 0)
    seen = jnp.where(ki < 0, 0, jnp.abs(qi - ki) + far) <= WINDOW
    bias = jnp.where(seen, 0.0, -jnp.inf).astype(F32)
    _attn_heads(q, k, v, bias, sink_ref, o_ref)


def _ctx_attn_kernel(sink_ref, q_ref, kc_ref, vc_ref, o_ref):
    _attn_heads(q_ref[0], kc_ref[0].astype(BF16), vc_ref[0].astype(BF16), None, sink_ref, o_ref)


def windowed_attention(zl3, zc3, cos_t, sin_t, sink):
    b, t, _ = zl3.shape
    l = zc3.shape[1]
    nb = t // ATT_BLOCK
    kcol, kpcol, vcol = ZK // KV_W, ZKP // KV_W, ZV // KV_W
    prev = lambda j: jnp.maximum(j - 1, 0)
    nxt = lambda j: jnp.minimum(j + 1, nb - 1)

    def zspec(width, rowf, col):
        return pl.BlockSpec((1, ATT_BLOCK, width), lambda bi, j: (bi, rowf(j), col))

    def tspec(width, rowf):
        return pl.BlockSpec((ATT_BLOCK, width), lambda bi, j: (rowf(j), 0))

    same = lambda j: j
    in_specs = [pl.BlockSpec(memory_space=pltpu.SMEM),
                zspec(Q_W, same, ZQ // Q_W), zspec(Q_W, same, ZQP // Q_W), tspec(Q_W, same), tspec(Q_W, same)]
    args = [sink, zl3, zl3, cos_t, sin_t]
    for col in (kcol, kpcol, vcol):
        in_specs += [zspec(KV_W, prev, col), zspec(KV_W, same, col), zspec(KV_W, nxt, col)]
        args += [zl3, zl3, zl3]
    for tab in (cos_t, sin_t):
        in_specs += [tspec(KV_W, prev), tspec(KV_W, same), tspec(KV_W, nxt)]
        args += [tab, tab, tab]
    in_specs += [pl.BlockSpec((1, l, KV_W), lambda bi, j: (bi, 0, kcol)),
                 pl.BlockSpec((1, l, KV_W), lambda bi, j: (bi, 0, vcol))]
    args += [zc3, zc3]
    return pl.pallas_call(
        functools.partial(_win_attn_kernel, nb=nb),
        grid=(b, nb), in_specs=in_specs,
        out_specs=pl.BlockSpec((1, ATT_BLOCK, Q_W), lambda bi, j: (bi, j, 0)),
        out_shape=jax.ShapeDtypeStruct((b, t, Q_W), F32),
        compiler_params=_cparams(("parallel", "parallel")), name="win_attn",
    )(*args)


def context_attention(zc3, sink):
    b, l, _ = zc3.shape
    return pl.pallas_call(
        _ctx_attn_kernel, grid=(b,),
        in_specs=[pl.BlockSpec(memory_space=pltpu.SMEM),
                  pl.BlockSpec((1, l, Q_W), lambda bi: (bi, 0, ZQ // Q_W)),
                  pl.BlockSpec((1, l, KV_W), lambda bi: (bi, 0, ZK // KV_W)),
                  pl.BlockSpec((1, l, KV_W), lambda bi: (bi, 0, ZV // KV_W))],
        out_specs=pl.BlockSpec((1, l, Q_W), lambda bi: (bi, 0, 0)),
        out_shape=jax.ShapeDtypeStruct((b, l, Q_W), F32),
        compiler_params=_cparams(("parallel",)), name="ctx_attn",
    )(sink, zc3, zc3, zc3)


HALO = 8


def _head_sums(x, e_ref):
    hi = x.astype(BF16)
    lo = (x - hi.astype(F32)).astype(BF16)
    return (jnp.dot(hi, e_ref[...], preferred_element_type=F32)
            + jnp.dot(lo, e_ref[...], preferred_element_type=F32))


def _shift_lerp_tile(cur_ref, prev_ref, next_ref, mu_ref, buf_ref, first, last):
    tt = cur_ref.shape[1]
    z = cur_ref[0]
    buf_ref[0:HALO] = jnp.where(first, 0.0, prev_ref[0])
    buf_ref[HALO:HALO + tt] = z
    buf_ref[HALO + tt:2 * HALO + tt] = jnp.where(last, 0.0, next_ref[0])
    nb = buf_ref[HALO - 1:HALO - 1 + tt] + buf_ref[HALO + 1:HALO + 1 + tt]
    return z + mu_ref[...] * (0.5 * nb - z)


def _split_bf16(a):
    hi = a.astype(BF16)
    return hi, (a - hi.astype(F32)).astype(BF16)


def _hi_lo(w):
    return jnp.stack(_split_bf16(w))


def _dot_3pass(a, b):
    a_hi, a_lo = _split_bf16(a)
    return (jnp.dot(a_hi, b[0], preferred_element_type=F32) + jnp.dot(a_lo, b[0], preferred_element_type=F32)
            + jnp.dot(a_hi, b[1], preferred_element_type=F32))


def _rwkv_prep_kernel(*refs, has_vres, n_alias):
    (wa_c, wa_p, wa_n, r_c, r_p, r_n, k_c, k_p, k_n, v_c, v_p, v_n, g_c, g_p, g_n,
     mu_wa, mu_r, mu_k, mu_v, mu_g, w0_ref, w2_ref, a0_ref, a2_ref, g2_ref, kk_ref, ka_ref, rk_ref, e_ref) = refs[:29]
    pos = 29
    if has_vres:
        vf_ref, v0_ref, v1_ref, v2_ref = refs[pos:pos + 4]
        pos += 4
    pos += n_alias
    (rk_o, vk_o, w_o, an_o, bv_o, g_o) = refs[pos:pos + 6]
    pos += 6
    if not has_vres:
        vin_o = refs[pos]
        pos += 1
    buf_wa, buf_r, buf_k, buf_v, buf_g = refs[pos:pos + 5]

    i = pl.program_id(1)
    first = i == 0
    last = i == pl.num_programs(1) - 1
    zwa = _shift_lerp_tile(wa_c, wa_p, wa_n, mu_wa, buf_wa, first, last)
    r = _shift_lerp_tile(r_c, r_p, r_n, mu_r, buf_r, first, last)
    k = _shift_lerp_tile(k_c, k_p, k_n, mu_k, buf_k, first, last)
    v = _shift_lerp_tile(v_c, v_p, v_n, mu_v, buf_v, first, last)
    zg = _shift_lerp_tile(g_c, g_p, g_n, mu_g, buf_g, first, last)

    if has_vres:
        mix = _dot_3pass(_dot_3pass(v, v1_ref[...]), v2_ref[...])
        v = v + (vf_ref[0] - v) * jax.nn.sigmoid(v0_ref[...] + mix)
    else:
        vin_o[0] = v
    kk = k * kk_ref[...]
    kk = kk / jnp.maximum(jnp.sqrt(_head_sums(kk * kk, e_ref)), 1e-12)
    tz = jnp.tanh(zwa)
    ksum = None
    for d in range(2):
        w_log = -jax.nn.softplus(-(w0_ref[d:d + 1, :] + _dot_3pass(tz, w2_ref[d]))) - 0.5
        w_o[d, 0] = (-jnp.exp(w_log)).astype(w_o.dtype)
        a = jax.nn.sigmoid(a0_ref[d:d + 1, :] + _dot_3pass(zwa, a2_ref[d]))
        an_o[d, 0] = (-a).astype(an_o.dtype)
        kd = k * (1.0 + (a - 1.0) * ka_ref[...])
        ksum = kd if ksum is None else ksum + kd
    rk_o[0, 0] = r.astype(rk_o.dtype)
    rk_o[1, 0] = k.astype(rk_o.dtype)
    vk_o[0, 0] = v.astype(vk_o.dtype)
    vk_o[1, 0] = kk.astype(vk_o.dtype)
    bv_o[0] = _head_sums(r * ksum * rk_ref[...], e_ref) * v
    g_o[0] = _dot_3pass(jax.nn.sigmoid(zg), g2_ref[...])


def _head_ones():
    idx = jnp.arange(RWKV_W) // RWKV_HEAD
    return (idx[:, None] == idx[None, :]).astype(BF16)


def rwkv_prep(z3, lp, v_first, ttot, row_off, pair_bufs):
    b, tx, _ = z3.shape
    tt = _pick_tile(math.gcd(tx, row_off) if row_off else tx, (256, 128, 64, 32, 16, 8))
    nh = tt // HALO
    nblk = tx // HALO
    has_vres = v_first is not None
    lanes = LANES

    def trio(width, col):
        return [pl.BlockSpec((1, tt, width), lambda bi, i: (bi, i, col)),
                pl.BlockSpec((1, HALO, width), lambda bi, i: (bi, jnp.maximum(i * nh - 1, 0), col)),
                pl.BlockSpec((1, HALO, width), lambda bi, i: (bi, jnp.minimum((i + 1) * nh, nblk - 1), col))]

    def full(shape):
        return pl.BlockSpec(shape, lambda bi, i: (0,) * len(shape))

    mu = lp['mu']
    o = 3 * RWKV_W
    nwa = 2 * DECAY_RANK + 2 * ICLR_RANK
    mu_g = jnp.zeros((lanes,), F32).at[:GATE_RANK].set(mu[o + nwa:])
    w2f = jnp.zeros((2, lanes, RWKV_W), F32)
    a2f = jnp.zeros((2, lanes, RWKV_W), F32)
    for d in range(2):
        w2f = w2f.at[d, d * DECAY_RANK:(d + 1) * DECAY_RANK].set(lp['w2'][d])
        a2f = a2f.at[d, 2 * DECAY_RANK + d * ICLR_RANK:2 * DECAY_RANK + (d + 1) * ICLR_RANK].set(lp['a2'][d])
    g2f = jnp.zeros((lanes, RWKV_W), F32).at[:GATE_RANK].set(lp['g2'])
    row = lambda a: a.reshape(1, -1)
    in_specs = (trio(lanes, ZWA // lanes) + trio(RWKV_W, ZR // RWKV_W) + trio(RWKV_W, ZRK // RWKV_W)
                + trio(RWKV_W, ZRV // RWKV_W) + trio(lanes, ZZG // lanes)
                + [full((1, lanes)), full((1, RWKV_W)), full((1, RWKV_W)), full((1, RWKV_W)), full((1, lanes)),
                   full((2, RWKV_W)), full((2, 2, lanes, RWKV_W)), full((2, RWKV_W)), full((2, 2, lanes, RWKV_W)),
                   full((2, lanes, RWKV_W)), full((1, RWKV_W)), full((1, RWKV_W)), full((1, RWKV_W)),
                   full((RWKV_W, RWKV_W))])
    per_dir = lambda w: jnp.stack([_hi_lo(w[0]), _hi_lo(w[1])])
    args = [z3] * 15 + [row(mu[o:o + nwa]), row(mu[:RWKV_W]), row(mu[RWKV_W:2 * RWKV_W]), row(mu[2 * RWKV_W:o]),
                        row(mu_g), lp['w0'], per_dir(w2f), lp['a0'], per_dir(a2f), _hi_lo(g2f),
                        row(lp['k_k']), row(lp['k_a']), row(lp['r_k']), _head_ones()]
    tok = pl.BlockSpec((1, tt, RWKV_W), lambda bi, i: (bi, i, 0))
    if has_vres:
        rank = lp['v1'].shape[1]
        v1f = jnp.zeros((RWKV_W, lanes), F32).at[:, :rank].set(lp['v1'])
        v2f = jnp.zeros((lanes, RWKV_W), F32).at[:rank].set(lp['v2'])
        in_specs += [tok, full((1, RWKV_W)), full((2, RWKV_W, lanes)), full((2, lanes, RWKV_W))]
        args += [v_first, row(lp['v0']), _hi_lo(v1f), _hi_lo(v2f)]
    n_alias = 0 if pair_bufs is None else len(pair_bufs)
    aliases = {}
    if pair_bufs is not None:
        aliases = {len(args) + n: n for n in range(n_alias)}
        in_specs += [pl.BlockSpec(memory_space=pl.ANY)] * n_alias
        args += list(pair_bufs)
    blk_off = row_off // tt
    pair = pl.BlockSpec((2, 1, tt, RWKV_W), lambda bi, i: (0, bi, i + blk_off, 0))
    pair_shape = jax.ShapeDtypeStruct((2, b, ttot, RWKV_W), BF16)
    tok_shape = jax.ShapeDtypeStruct((b, tx, RWKV_W), F32)
    n_tok = 2 if has_vres else 3
    outs = pl.pallas_call(
        functools.partial(_rwkv_prep_kernel, has_vres=has_vres, n_alias=n_alias),
        grid=(b, tx // tt), in_specs=in_specs,
        out_specs=[pair] * 4 + [tok] * n_tok,
        out_shape=[pair_shape] * 4 + [tok_shape] * n_tok,
        input_output_aliases=aliases,
        scratch_shapes=[pltpu.VMEM((tt + 2 * HALO, lanes), F32)] + [pltpu.VMEM((tt + 2 * HALO, RWKV_W), F32)] * 3
                       + [pltpu.VMEM((tt + 2 * HALO, lanes), F32)],
        compiler_params=_cparams(("parallel", "parallel")), name="rwkv_prep",
    )(*args)
    res = {'pairs': tuple(outs[:4]), 'bv': outs[4], 'g': outs[5]}
    if not has_vres:
        res['v_in'] = outs[6]
    return res


CONV_HALO = 16
CONV_ROWS = 32


def _conv_kernel(a_c, a_p, a_n, b_c, b_p, b_n, w_ref, bias_ref, lg_ref, lb_ref, o_ref, buf_ref, win_ref):
    i = pl.program_id(1)
    first = i == 0
    last = i == pl.num_programs(1) - 1
    tt = a_c.shape[1]
    glu = lambda a, b: a * jax.nn.sigmoid(b)
    buf_ref[0:CONV_HALO] = jnp.where(first, 0.0, glu(a_p[0], b_p[0]))
    buf_ref[CONV_HALO:CONV_HALO + tt] = glu(a_c[0], b_c[0])
    buf_ref[CONV_HALO + tt:2 * CONV_HALO + tt] = jnp.where(last, 0.0, glu(a_n[0], b_n[0]))
    half = CONV_K // 2
    span = CONV_ROWS + ((CONV_K - 1) // SUBLANES) * SUBLANES
    for r0 in range(0, tt, CONV_ROWS):
        base = CONV_HALO - half + r0
        acc = bias_ref[...]
        for ph in range(SUBLANES):
            win_ref[...] = buf_ref[base + ph:base + ph + span]
            for j in range(ph, CONV_K, SUBLANES):
                acc = acc + win_ref[j - ph:j - ph + CONV_ROWS] * w_ref[j:j + 1, :]
        m = jnp.mean(acc, axis=-1, keepdims=True)
        cen = acc - m
        var = jnp.mean(cen * cen, axis=-1, keepdims=True)
        y = cen * lax.rsqrt(var + CONV_LN_EPS) * lg_ref[...] + lb_ref[...]
        o_ref[0, r0:r0 + CONV_ROWS, :] = y * jax.nn.sigmoid(y)


def conv_module(z3, lp):
    b, tx, _ = z3.shape
    tt = _pick_tile(tx, (256, 128, 64, 32))
    nh = tt // CONV_HALO
    nblk = tx // CONV_HALO

    def trio(col):
        return [pl.BlockSpec((1, tt, CONV_W), lambda bi, i: (bi, i, col)),
                pl.BlockSpec((1, CONV_HALO, CONV_W), lambda bi, i: (bi, jnp.maximum(i * nh - 1, 0), col)),
                pl.BlockSpec((1, CONV_HALO, CONV_W), lambda bi, i: (bi, jnp.minimum((i + 1) * nh, nblk - 1), col))]

    kpad = -(-CONV_K // SUBLANES) * SUBLANES
    wpad = jnp.zeros((kpad, CONV_W), F32).at[:CONV_K].set(lp['dw_w'])
    vec = pl.BlockSpec((1, CONV_W), lambda bi, i: (0, 0))
    return pl.pallas_call(
        _conv_kernel, grid=(b, tx // tt),
        in_specs=trio(ZCA // CONV_W) + trio(ZCB // CONV_W)
                 + [pl.BlockSpec((kpad, CONV_W), lambda bi, i: (0, 0)), vec, vec, vec],
        out_specs=pl.BlockSpec((1, tt, CONV_W), lambda bi, i: (bi, i, 0)),
        out_shape=jax.ShapeDtypeStruct((b, tx, CONV_W), F32),
        scratch_shapes=[pltpu.VMEM((tt + 2 * CONV_HALO, CONV_W), F32),
                        pltpu.VMEM((CONV_ROWS + ((CONV_K - 1) // SUBLANES) * SUBLANES, CONV_W), F32)],
        compiler_params=_cparams(("parallel", "parallel")), name="conv_module",
    )(z3, z3, z3, z3, z3, z3, wpad, lp['dw_b'].reshape(1, -1), lp['cln_g'].reshape(1, -1), lp['cln_b'].reshape(1, -1))


SUBLANES = 8
DECAY_FOLD = 16


def _wkv_scan_kernel(rk_f, vk_f, w_f, an_f, rk_b, vk_b, w_b, an_b, ka_ref, yf_ref, yb_ref, s_ref, *, tt):
    @pl.when(pl.program_id(0) == 0)
    def _():
        s_ref[...] = jnp.zeros_like(s_ref)

    nk, _, lanes = s_ref.shape
    half = lanes // 2
    fwd = lax.broadcasted_iota(jnp.int32, (nk, lanes), 1) < half
    ka = ka_ref[...]
    fold = math.gcd(tt, DECAY_FOLD)

    def group(gi, carry):
        def step(ti, gam):
            t = gi * fold + ti
            tb = tt - 1 - t
            rkf, rkb, vkf, vkb = (a.astype(F32) for a in (rk_f[t], rk_b[tb], vk_f[t], vk_b[tb]))
            r = jnp.where(fwd, rkf, pltpu.roll(rkb, half, 1))
            k = jnp.where(fwd, pltpu.roll(rkf, half, 1), rkb)
            v = jnp.where(fwd, vkf, pltpu.roll(vkb, half, 1))
            kk = jnp.where(fwd, pltpu.roll(vkf, half, 1), vkb)
            w = jnp.exp(jnp.where(fwd, w_f[t].astype(F32), w_b[tb].astype(F32)))
            an = jnp.where(fwd, an_f[t].astype(F32), an_b[tb].astype(F32))
            gam_new = gam * w
            inv = 1.0 / gam_new
            kks = kk * gam
            kds = k * (1.0 - (an + 1.0) * ka) * inv
            bbs = kk * an * inv
            rs = r * gam_new
            row = lambda a, ki: a[ki:ki + 1, :]
            sa = [s_ref[0] * row(kks, 0), s_ref[1] * row(kks, 1)]
            for ki in range(2, nk):
                sa[ki % 2] = sa[ki % 2] + s_ref[ki] * row(kks, ki)
            sa = sa[0] + sa[1]
            ys = [None, None]
            for ki in range(nk):
                s_new = s_ref[ki] + sa * row(bbs, ki) + v * row(kds, ki)
                s_ref[ki] = s_new
                contrib = s_new * row(rs, ki)
                ys[ki % 2] = contrib if ys[ki % 2] is None else ys[ki % 2] + contrib
            y = ys[0] + ys[1]
            yf_ref[t] = y
            yb_ref[tb] = y
            return gam_new

        gam = lax.fori_loop(0, fold, step, jnp.ones((nk, lanes), F32), unroll=4)
        for ki in range(nk):
            s_ref[ki] = s_ref[ki] * gam[ki:ki + 1, :]
        return carry

    lax.fori_loop(0, tt // fold, group, 0)


def wkv_scan(rk, vk, w, an, ka, n_ctx_steps):
    ttot, n, ch = rk.shape
    tt = _pick_tile(math.gcd(n_ctx_steps, ttot - n_ctx_steps), (32, 16, 8, 4, 2, 1))
    nt, nc = ttot // tt, n_ctx_steps // tt
    fspec = pl.BlockSpec((tt, n, ch), lambda i: (i, 0, 0))
    bspec = pl.BlockSpec((tt, n, ch), lambda i: (jnp.where(i < nc, nc - 1 - i, nt - 1 - (i - nc)), 0, 0))
    return pl.pallas_call(
        functools.partial(_wkv_scan_kernel, tt=tt),
        grid=(nt,), in_specs=[fspec] * 4 + [bspec] * 4 + [pl.BlockSpec((n, ch), lambda i: (0, 0))],
        out_specs=[fspec, bspec],
        out_shape=[jax.ShapeDtypeStruct((ttot, n, ch), F32)] * 2,
        scratch_shapes=[pltpu.VMEM((n, n, ch), F32)],
        compiler_params=_cparams(("arbitrary",)), name="wkv_scan",
    )(rk, vk, w, an, rk, vk, w, an, ka)


def _merge_kernel(x_ref, att_ref, y_ref, bv_ref, rg_ref, cv_ref, g0_ref, g1_ref, g2_ref,
                  lg_ref, lb_ref, e_ref, wa_ref, wr_ref, wc_ref, wo_ref, ng_ref, gt_ref, o_ref):
    def proj(a, w_ref):
        return jnp.dot(a.astype(BF16), w_ref[...], preferred_element_type=F32)

    y = y_ref[...]
    cen = y - _head_sums(y, e_ref) * (1.0 / RWKV_HEAD)
    var = _head_sums(cen * cen, e_ref) * (1.0 / RWKV_HEAD)
    rw = (cen * lax.rsqrt(var + LNX_EPS) * lg_ref[...] + lb_ref[...] + bv_ref[...]) * rg_ref[...]
    m = (jax.nn.sigmoid(g0_ref[...]) * proj(att_ref[...], wa_ref)
         + jax.nn.sigmoid(g1_ref[...]) * proj(rw, wr_ref)
         + jax.nn.sigmoid(g2_ref[...]) * proj(cv_ref[...], wc_ref))
    ml = proj(m, wo_ref)
    ms = jnp.mean(ml * ml, axis=-1, keepdims=True)
    o_ref[...] = x_ref[...] + gt_ref[0] * (ml * lax.rsqrt(ms + NORM_EPS) * ng_ref[...])


def merge_out(x, att, y, bv, rg, cv, z, lnx_g, lnx_b, wa, wr, wc, wo, ng, gt, rows_per_group):
    m, d = x.shape
    tm = _pick_tile(rows_per_group, (512, 256, 128, 64, 32, 16, 8))
    row = lambda i: (i, 0)
    const = lambda i: (0, 0)
    gcol = ZG // d
    br = pl.BlockSpec((tm, RWKV_W), row)
    return pl.pallas_call(
        _merge_kernel, grid=(m // tm,),
        in_specs=[pl.BlockSpec((tm, d), row),
                  pl.BlockSpec((tm, Q_W), row), br, br, br, pl.BlockSpec((tm, CONV_W), row),
                  pl.BlockSpec((tm, d), lambda i: (i, gcol)),
                  pl.BlockSpec((tm, d), lambda i: (i, gcol + 1)),
                  pl.BlockSpec((tm, d), lambda i: (i, gcol + 2)),
                  pl.BlockSpec((1, RWKV_W), const), pl.BlockSpec((1, RWKV_W), const),
                  pl.BlockSpec((RWKV_W, RWKV_W), const),
                  pl.BlockSpec((Q_W, d), const), pl.BlockSpec((RWKV_W, d), const), pl.BlockSpec((CONV_W, d), const),
                  pl.BlockSpec((d, d), const), pl.BlockSpec((1, d), const),
                  pl.BlockSpec((1, 1, d), lambda i: ((i * tm) // rows_per_group, 0, 0))],
        out_specs=pl.BlockSpec((tm, d), row),
        out_shape=jax.ShapeDtypeStruct((m, d), F32),
        compiler_params=_cparams(("parallel",)), name="merge_out",
    )(x, att, y, bv, rg, cv, z, z, z, lnx_g.reshape(1, -1), lnx_b.reshape(1, -1), _head_ones(),
      wa, wr, wc, wo, ng.reshape(1, d), gt)


def _swiglu_kernel(te_ref, nu_ref, h_ref, wg_ref, wu_ref, wd_ref, o_ref):
    f = pl.program_id(1)

    used = pl.program_id(0) < nu_ref[0]

    @pl.when(jnp.logical_and(used, f == 0))
    def _():
        o_ref[...] = jnp.zeros_like(o_ref)

    @pl.when(used)
    def _():
        h = h_ref[...]
        hg = jnp.dot(h, wg_ref[0].astype(BF16), preferred_element_type=F32)
        hu = jnp.dot(h, wu_ref[0].astype(BF16), preferred_element_type=F32)
        act = (hg * jax.nn.sigmoid(hg) * hu).astype(BF16)
        o_ref[...] += jnp.dot(act, wd_ref[0].astype(BF16), preferred_element_type=F32)


def grouped_swiglu(h, tile_expert, n_used, wg, wu, wd, tm):
    p, d = h.shape
    ff = wg.shape[2]
    tf = _pick_tile(ff, (512, 1408, 256, 128))
    nf = ff // tf
    ti = lambda i, nu: jnp.minimum(i, nu[0] - 1)
    fi = lambda i, f, nu: jnp.where(i < nu[0], f, nf - 1)
    grid_spec = pltpu.PrefetchScalarGridSpec(
        num_scalar_prefetch=2, grid=(p // tm, nf),
        in_specs=[pl.BlockSpec((tm, d), lambda i, f, te, nu: (ti(i, nu), 0)),
                  pl.BlockSpec((1, d, tf), lambda i, f, te, nu: (te[ti(i, nu)], 0, fi(i, f, nu))),
                  pl.BlockSpec((1, d, tf), lambda i, f, te, nu: (te[ti(i, nu)], 0, fi(i, f, nu))),
                  pl.BlockSpec((1, tf, d), lambda i, f, te, nu: (te[ti(i, nu)], fi(i, f, nu), 0))],
        out_specs=pl.BlockSpec((tm, d), lambda i, f, te, nu: (ti(i, nu), 0)))
    return pl.pallas_call(
        _swiglu_kernel, grid_spec=grid_spec,
        out_shape=jax.ShapeDtypeStruct((p, d), F32),
        compiler_params=_cparams(("arbitrary", "arbitrary")), name="swiglu",
    )(tile_expert, n_used, h, wg, wu, wd)


def _dense_ffn_kernel(x_ref, g_ref, sc_ref, sh_ref, wg_ref, wu_ref, wd_ref, g2_ref, gt_ref, o_ref, h_ref, acc_ref):
    f = pl.program_id(1)

    @pl.when(f == 0)
    def _():
        h_ref[...] = _norm_mod(x_ref[...], g_ref[...], sc_ref[0], sh_ref[0]).astype(BF16)
        acc_ref[...] = jnp.zeros_like(acc_ref)

    h = h_ref[...]
    hg = jnp.dot(h, wg_ref[...], preferred_element_type=F32)
    hu = jnp.dot(h, wu_ref[...], preferred_element_type=F32)
    act = (hg * jax.nn.sigmoid(hg) * hu).astype(BF16)
    acc_ref[...] += jnp.dot(act, wd_ref[...], preferred_element_type=F32)

    @pl.when(f == pl.num_programs(1) - 1)
    def _():
        y = acc_ref[...]
        ms = jnp.mean(y * y, axis=-1, keepdims=True)
        o_ref[...] = x_ref[...] + gt_ref[0] * (y * lax.rsqrt(ms + NORM_EPS) * g2_ref[...])


def dense_ffn(x, g_in, sc, sh, wg, wu, wd, g_out, gt, rows_per_group):
    m, d = x.shape
    ff = wg.shape[1]
    tm = _pick_tile(rows_per_group, (512, 256, 128, 64, 32, 16, 8))
    tf = _pick_tile(ff, (1408, 512, 256, 128))
    grp = lambda i, f: ((i * tm) // rows_per_group, 0, 0)
    vec = pl.BlockSpec((1, d), lambda i, f: (0, 0))
    return pl.pallas_call(
        _dense_ffn_kernel, grid=(m // tm, ff // tf),
        in_specs=[pl.BlockSpec((tm, d), lambda i, f: (i, 0)), vec,
                  pl.BlockSpec((1, 1, d), grp), pl.BlockSpec((1, 1, d), grp),
                  pl.BlockSpec((d, tf), lambda i, f: (0, f)), pl.BlockSpec((d, tf), lambda i, f: (0, f)),
                  pl.BlockSpec((tf, d), lambda i, f: (f, 0)), vec, pl.BlockSpec((1, 1, d), grp)],
        out_specs=pl.BlockSpec((tm, d), lambda i, f: (i, 0)),
        out_shape=jax.ShapeDtypeStruct((m, d), F32),
        scratch_shapes=[pltpu.VMEM((tm, d), BF16), pltpu.VMEM((tm, d), F32)],
        compiler_params=_cparams(("parallel", "arbitrary")), name="dense_ffn",
    )(x, g_in.reshape(1, d), sc, sh, wg, wu, wd, g_out.reshape(1, d), gt)


def _rope_tables(t):
    rows = t // GRID_W
    row = jnp.broadcast_to(jnp.arange(rows)[:, None], (rows, GRID_W)).reshape(-1).astype(F32)
    col = jnp.broadcast_to(jnp.arange(GRID_W)[None, :], (rows, GRID_W)).reshape(-1).astype(F32)
    inv = ROPE_BASE ** (-jnp.arange(0, AXIS_DIM, 2, dtype=F32) / AXIS_DIM)
    ar = row[:, None] * inv[None, :]
    ac = col[:, None] * inv[None, :]
    cos64 = jnp.concatenate([jnp.cos(ar), jnp.cos(ar), jnp.cos(ac), jnp.cos(ac)], axis=-1)
    sin64 = jnp.concatenate([jnp.sin(ar), jnp.sin(ar), jnp.sin(ac), jnp.sin(ac)], axis=-1)
    return jnp.tile(cos64, (1, ATT_HEADS)), jnp.tile(sin64, (1, ATT_HEADS))


def _rope_partner(w):
    d = w.shape[0]
    q = AXIS_DIM // 2
    wh = w.reshape(d, -1, 4, q)
    return jnp.stack([-wh[:, :, 1], wh[:, :, 0], -wh[:, :, 3], wh[:, :, 2]], axis=2).reshape(w.shape)


def _build_w_ext(w_in):
    d = w_in.shape[0]
    wq = w_in[:, OFF_Q:OFF_K]
    wk = w_in[:, OFF_K:OFF_V]
    o = OFF_R + 3 * RWKV_W
    parts = [(ZQ, wq), (ZQP, _rope_partner(wq)), (ZK, wk), (ZKP, _rope_partner(wk)),
             (ZV, w_in[:, OFF_V:OFF_R]),
             (ZWA, w_in[:, o:o + 2 * DECAY_RANK + 2 * ICLR_RANK]),
             (ZCA, w_in[:, OFF_CONV:OFF_CONV + CONV_W]), (ZCB, w_in[:, OFF_CONV + CONV_W:OFF_GATE]),
             (ZR, w_in[:, OFF_R:OFF_R + RWKV_W]), (ZRK, w_in[:, OFF_R + RWKV_W:OFF_R + 2 * RWKV_W]),
             (ZRV, w_in[:, OFF_R + 2 * RWKV_W:OFF_R + 3 * RWKV_W]),
             (ZG, w_in[:, OFF_GATE:IN_COLS]),
             (ZZG, w_in[:, OFF_CONV - GATE_RANK:OFF_CONV])]
    parts.sort(key=lambda p: p[0])
    cols, pos = [], 0
    for off, wpart in parts:
        if off > pos:
            cols.append(jnp.zeros((d, off - pos), w_in.dtype))
        cols.append(wpart)
        pos = off + wpart.shape[1]
    if pos < NZ:
        cols.append(jnp.zeros((d, NZ - pos), w_in.dtype))
    return jnp.concatenate(cols, axis=1).astype(BF16)


def _chain_layout(pair):
    _, bsz, ttot, _ = pair.shape
    both = pair.reshape(2, bsz, ttot, RWKV_HEADS, RWKV_HEAD)
    return jnp.transpose(both, (2, 4, 0, 1, 3)).reshape(ttot, RWKV_HEAD, 2 * bsz * RWKV_HEADS)


def _bidirectional_wkv(pairs, k_a, l):
    bsz = pairs[0].shape[1]
    rk, vk, w, an = (_chain_layout(p) for p in pairs)
    ka = jnp.tile(k_a.reshape(RWKV_HEADS, RWKV_HEAD).T, (1, 2 * bsz))
    yf, yb = wkv_scan(rk, vk, w, an, ka, l)
    half = bsz * RWKV_HEADS
    y = (yf[:, :, :half] + yb[:, :, half:]).reshape(-1, RWKV_HEAD, bsz, RWKV_HEADS)
    y = jnp.transpose(y, (2, 0, 3, 1)).reshape(bsz, -1, RWKV_W)
    return y[:, l:], y[:, :l]


def _moe_dispatch(logits, tm):
    m = logits.shape[0]
    top_v, top_i = lax.top_k(logits, TOP_K)
    wts = jax.nn.softmax(top_v, axis=-1)
    e_flat = top_i.reshape(-1)
    n_assign = e_flat.shape[0]
    order = jnp.argsort(e_flat, stable=True).astype(jnp.int32)
    pos = jnp.argsort(order).astype(jnp.int32)
    cnt = jnp.sum((e_flat[:, None] == jnp.arange(N_EXPERTS)[None, :]).astype(jnp.int32), axis=0)
    start = jnp.cumsum(cnt) - cnt
    pcnt = ((cnt + tm - 1) // tm) * tm
    pend = jnp.cumsum(pcnt)
    pstart = pend - pcnt
    dest = (pstart - start)[e_flat] + pos
    p_max = n_assign + N_EXPERTS * tm
    tile_start = jnp.arange(p_max // tm, dtype=jnp.int32) * tm
    tile_expert = jnp.minimum(jnp.searchsorted(pend, tile_start, side='right'), N_EXPERTS - 1).astype(jnp.int32)
    rows = jnp.arange(p_max, dtype=jnp.int32)
    row_expert = tile_expert[rows // tm]
    src = order[jnp.clip(rows - (pstart - start)[row_expert], 0, n_assign - 1)]
    row_token = src // TOP_K
    n_used = (pend[-1:] // tm).astype(jnp.int32)
    return row_token, tile_expert, n_used, dest.reshape(m, TOP_K), wts


def kernel(x, c, ctx, c_ctx, mod_w, mod_b, norm_g, w_in, attn_sink, rwkv_mu, rwkv_w0, rwkv_w2, rwkv_a0, rwkv_a2, rwkv_g2, rwkv_k_k, rwkv_k_a, rwkv_r_k, rwkv_lnx_g, rwkv_lnx_b, vres_v0, vres_w1, vres_w2, conv_dw_w, conv_dw_b, conv_ln_g, conv_ln_b, w_attn_o, w_rwkv_o, w_conv_o, w_out, ffn_wg, ffn_wu, ffn_wd, moe_router, moe_wg, moe_wu, moe_wd):
    bsz, t, d = x.shape
    l = ctx.shape[1]
    depth = mod_w.shape[0]
    n_lat, n_ctx = bsz * t, bsz * l
    cos_t, sin_t = _rope_tables(t)
    xl = x.reshape(n_lat, d)
    xc = ctx.reshape(n_ctx, d)
    n_rows = -(-(bsz + 1) // SUBLANES) * SUBLANES
    cond = jnp.zeros((n_rows, d), F32).at[:bsz].set(jax.nn.silu(c)).at[bsz].set(jax.nn.silu(c_ctx))
    v_first = None
    for i in range(depth):
        need_ctx = i < depth - 1
        mod = matmul_bias_f32(cond, mod_w[i], mod_b[i])
        mod_l = [mod[:bsz, k * d:(k + 1) * d].reshape(bsz, 1, d) for k in range(6)]
        mod_c = [mod[bsz:bsz + 1, k * d:(k + 1) * d].reshape(1, 1, d) for k in range(6)]
        lp = {'mu': rwkv_mu[i], 'w0': rwkv_w0[i], 'w2': rwkv_w2[i], 'a0': rwkv_a0[i], 'a2': rwkv_a2[i],
              'g2': rwkv_g2[i], 'k_k': rwkv_k_k[i], 'k_a': rwkv_k_a[i], 'r_k': rwkv_r_k[i],
              'lnx_g': rwkv_lnx_g[i], 'lnx_b': rwkv_lnx_b[i],
              'dw_w': conv_dw_w[i], 'dw_b': conv_dw_b[i], 'cln_g': conv_ln_g[i], 'cln_b': conv_ln_b[i]}
        if i > 0:
            lp['v0'], lp['v1'], lp['v2'] = vres_v0[i - 1], vres_w1[i - 1], vres_w2[i - 1]
        w_ext = _build_w_ext(w_in[i])
        wa, wr, wc, wo = (w.astype(BF16) for w in (w_attn_o[i], w_rwkv_o[i], w_conv_o[i], w_out[i]))

        zl = nm_matmul(xl, norm_g[i, 0], mod_l[1], mod_l[0], w_ext, t)
        zc = nm_matmul(xc, norm_g[i, 0], mod_c[1], mod_c[0], w_ext, n_ctx)
        zl3 = zl.reshape(bsz, t, NZ)
        zc3 = zc.reshape(bsz, l, NZ)
        att_l = windowed_attention(zl3, zc3, cos_t, sin_t, attn_sink[i])
        p_c = rwkv_prep(zc3, lp, None if v_first is None else v_first[1], l + t, 0, None)
        p_l = rwkv_prep(zl3, lp, None if v_first is None else v_first[0], l + t, l, p_c['pairs'])
        if i == 0:
            v_first = (p_l['v_in'], p_c['v_in'])
        y_l, y_c = _bidirectional_wkv(p_l['pairs'], lp['k_a'], l)
        cv_l = conv_module(zl3, lp)
        flat = lambda a: a.reshape(-1, a.shape[-1])
        xl = merge_out(xl, flat(att_l), flat(y_l), flat(p_l['bv']), flat(p_l['g']), flat(cv_l), zl,
                       lp['lnx_g'], lp['lnx_b'], wa, wr, wc, wo, norm_g[i, 1], mod_l[2], t)
        if need_ctx:
            att_c = context_attention(zc3, attn_sink[i])
            cv_c = conv_module(zc3, lp)
            xc = merge_out(xc, flat(att_c), flat(y_c), flat(p_c['bv']), flat(p_c['g']), flat(cv_c), zc,
                           lp['lnx_g'], lp['lnx_b'], wa, wr, wc, wo, norm_g[i, 1], mod_c[2], n_ctx)

        j = i // 2
        if i % 2 == 0:
            wg, wu, wd = (w[j].astype(BF16) for w in (ffn_wg, ffn_wu, ffn_wd))
            xl = dense_ffn(xl, norm_g[i, 2], mod_l[4], mod_l[3], wg, wu, wd, norm_g[i, 3], mod_l[5], t)
            if need_ctx:
                xc = dense_ffn(xc, norm_g[i, 2], mod_c[4], mod_c[3], wg, wu, wd, norm_g[i, 3], mod_c[5], n_ctx)
        else:
            wg, wu, wd = moe_wg[j], moe_wu[j], moe_wd[j]
            streams = [(xl, mod_l, t)] + ([(xc, mod_c, n_ctx)] if need_ctx else [])
            outs = []
            for xs, ms, rpg in streams:
                n_tok = xs.shape[0]
                h, logits = prenorm(xs, norm_g[i, 2], ms[4], ms[3], rpg, router=moe_router[j])
                n_grp = MOE_GROUPS if n_tok % (MOE_GROUPS * SUBLANES) == 0 else 1
                gsz = n_tok // n_grp
                tm = _pick_tile(gsz, (1024, 512, 256, 128, 64, 32, 16, 8))
                x_new = None
                for gi in range(n_grp):
                    row_token, tile_expert, n_used, dest, wts = _moe_dispatch(logits[gi * gsz:(gi + 1) * gsz], tm)
                    rows = lambda a, idx: jnp.take(a, idx, axis=0, mode='clip')
                    yg = grouped_swiglu(rows(h, row_token + gi * gsz), tile_expert, n_used, wg, wu, wd, tm)
                    x_new = combine_resnorm(xs, rows(yg, dest[:, 0]), rows(yg, dest[:, 1]),
                                            wts, norm_g[i, 3], ms[5], rpg, gi * gsz, x_new)
                outs.append(x_new)
            xl = outs[0]
            if need_ctx:
                xc = outs[1]
    return xl.reshape(bsz, t, d)
```
